```python
import jax
import jax.numpy as jnp
from jax import lax
import numpy as np

D_MODEL = 1024
BATCH = 8
SEQ = 2048
DEPTH = 1

N_MEM = 256
D_MIX = D_MODEL
HEAD_DIM = 64
NSA_HEADS = 8
NSA_KV_HEADS = 2
NSA_WIDTH = NSA_HEADS * HEAD_DIM
NSA_KV_WIDTH = NSA_KV_HEADS * HEAD_DIM
CONV_WIDTH = D_MIX - NSA_WIDTH
CONV_K = 3
CMP_LEN = 32
CMP_STRIDE = 16
CMP_HIDDEN = 256
SEL_LEN = 64
N_SEL = 16
WINDOW = 512
Q_BLK = 128
ROPE_THETA = 10000.0
XATTN_HEADS = 4
XATTN_HEAD_DIM = D_MODEL // XATTN_HEADS
N_EXPERTS = 32
TOP_K = 4
D_FF = D_MODEL
SWIGLU_LIMIT = 7.0
SWIGLU_ALPHA = 1.702
MOE_BLK = 128
RMS_EPS = 1e-5
NEG_INF = -1e30
FORCED = 1e30
IN_SIZES = (NSA_WIDTH, NSA_KV_WIDTH, NSA_KV_WIDTH, NSA_KV_WIDTH, NSA_KV_WIDTH, NSA_KV_WIDTH, NSA_KV_WIDTH, 3 * NSA_HEADS, CONV_WIDTH, CONV_WIDTH, CONV_WIDTH)
IN_COLS = NSA_WIDTH + 6 * NSA_KV_WIDTH + 3 * NSA_HEADS + 3 * CONV_WIDTH

kernel_name = 'hymba_nsa_shortconv_moe_block'


def rmsnorm(t, g):
    tf = t.astype(jnp.float32)
    y = tf * lax.rsqrt(jnp.mean(tf * tf, axis=-1, keepdims=True) + RMS_EPS)
    return (y * g.astype(jnp.float32)).astype(t.dtype)


def rope(t, pos):
    half = HEAD_DIM // 2
    inv_freq = ROPE_THETA ** (-jnp.arange(half, dtype=jnp.float32) / half)
    ang = pos.astype(jnp.float32)[:, :, None] * inv_freq
    cos = jnp.cos(ang)[:, :, None, :]
    sin = jnp.sin(ang)[:, :, None, :]
    tf = t.astype(jnp.float32)
    t1, t2 = tf[..., :half], tf[..., half:]
    return jnp.concatenate([t1 * cos - t2 * sin, t2 * cos + t1 * sin], axis=-1).astype(t.dtype)


def masked_softmax(s, mask):
    return jax.nn.softmax(jnp.where(mask, s, NEG_INF), axis=-1)


def compress_blocks(t, pos_emb, w1, b1, w2):
    B, S, G, dh = t.shape
    nc = (S - CMP_LEN) // CMP_STRIDE + 1
    idx = jnp.arange(nc)[:, None] * CMP_STRIDE + jnp.arange(CMP_LEN)[None, :]
    blk = t[:, idx] + pos_emb[:, None, :]
    blk = jnp.moveaxis(blk, 3, 2).reshape(B, nc, G, CMP_LEN * dh)
    hid = jax.nn.gelu(blk @ w1 + b1)
    return hid @ w2


def nsa_attention(q, k_cmp, v_cmp, k_slc, v_slc, k_win, v_win, gate_logits,
                  cmp_pos_k, cmp_pos_v, cmp_w1_k, cmp_b1_k, cmp_w2_k, cmp_w1_v, cmp_b1_v, cmp_w2_v):
    B, S = q.shape[0], q.shape[1]
    G = NSA_KV_HEADS
    HPG = NSA_HEADS // NSA_KV_HEADS
    scale = HEAD_DIM ** -0.5
    f32 = jnp.float32
    qg = q.reshape(B, S, G, HPG, HEAD_DIM)
    s_idx = jnp.arange(S)

    kc = compress_blocks(k_cmp, cmp_pos_k, cmp_w1_k, cmp_b1_k, cmp_w2_k)
    vc = compress_blocks(v_cmp, cmp_pos_v, cmp_w1_v, cmp_b1_v, cmp_w2_v)
    nc = kc.shape[1]
    c_end = jnp.arange(nc) * CMP_STRIDE + CMP_LEN - 1
    cmask = c_end[None, :] <= s_idx[:, None]
    sc = jnp.einsum('bsghd,bcgd->bghsc', qg, kc, preferred_element_type=f32) * scale
    p_cmp = jnp.where(cmask, masked_softmax(sc, cmask), 0.0)
    o_cmp = jnp.einsum('bghsc,bcgd->bsghd', p_cmp.astype(vc.dtype), vc).reshape(B, S, NSA_HEADS, HEAD_DIM)

    ns = S // SEL_LEN
    cs = jnp.arange(nc) * CMP_STRIDE
    js = jnp.arange(ns) * SEL_LEN
    overlap = jnp.clip(jnp.minimum(cs[:, None] + CMP_LEN, js[None, :] + SEL_LEN)
                       - jnp.maximum(cs[:, None], js[None, :]), 0, None).astype(f32) / CMP_LEN
    imp = jnp.einsum('bghsc,cj->bgsj', p_cmp, overlap)
    cur = (s_idx // SEL_LEN)[:, None]
    jj = jnp.arange(ns)[None, :]
    forced = (jj == 0) | (jj == cur) | (jj == cur - 1)
    imp = jnp.where(forced, FORCED, jnp.where(jj > cur, NEG_INF, imp))
    n_top = min(N_SEL, ns)
    _, sel_idx = lax.top_k(imp, n_top)

    ks_blk = jnp.moveaxis(k_slc, 2, 1).reshape(B, G, ns, SEL_LEN, HEAD_DIM)
    vs_blk = jnp.moveaxis(v_slc, 2, 1).reshape(B, G, ns, SEL_LEN, HEAD_DIM)
    kw_pad = jnp.pad(k_win, ((0, 0), (WINDOW, 0), (0, 0), (0, 0)))
    vw_pad = jnp.pad(v_win, ((0, 0), (WINDOW, 0), (0, 0), (0, 0)))
    gather = jax.vmap(jax.vmap(lambda blk, ix: blk[ix]))
    offs = jnp.arange(SEL_LEN)
    kpos_rel = jnp.arange(WINDOW + Q_BLK) - WINDOW

    def query_block(ci):
        s0 = ci * Q_BLK
        qpos = s0 + jnp.arange(Q_BLK)
        qc = lax.dynamic_slice_in_dim(qg, s0, Q_BLK, axis=1)
        ix = lax.dynamic_slice_in_dim(sel_idx, s0, Q_BLK, axis=2)
        ksel = gather(ks_blk, ix).reshape(B, G, Q_BLK, n_top * SEL_LEN, HEAD_DIM)
        vsel = gather(vs_blk, ix).reshape(B, G, Q_BLK, n_top * SEL_LEN, HEAD_DIM)
        tpos = (ix[..., None] * SEL_LEN + offs).reshape(B, G, Q_BLK, n_top * SEL_LEN)
        smask = (tpos <= qpos[:, None])[:, :, None]
        ss = jnp.einsum('bqghd,bgqnd->bghqn', qc, ksel, preferred_element_type=f32) * scale
        ps = masked_softmax(ss, smask)
        o_s = jnp.einsum('bghqn,bgqnd->bqghd', ps.astype(vsel.dtype), vsel)
        kw = lax.dynamic_slice_in_dim(kw_pad, s0, WINDOW + Q_BLK, axis=1)
        vw = lax.dynamic_slice_in_dim(vw_pad, s0, WINDOW + Q_BLK, axis=1)
        kpos = s0 + kpos_rel
        wmask = ((kpos[None, :] <= qpos[:, None]) & (qpos[:, None] - kpos[None, :] < WINDOW)
                 & (kpos[None, :] >= 0))
        sw = jnp.einsum('bqghd,bkgd->bghqk', qc, kw, preferred_element_type=f32) * scale
        pw = masked_softmax(sw, wmask)
        o_w = jnp.einsum('bghqk,bkgd->bqghd', pw.astype(vw.dtype), vw)
        return o_s, o_w

    o_slc, o_win = lax.map(query_block, jnp.arange(S // Q_BLK))
    o_slc = jnp.moveaxis(o_slc, 0, 1).reshape(B, S, NSA_HEADS, HEAD_DIM)
    o_win = jnp.moveaxis(o_win, 0, 1).reshape(B, S, NSA_HEADS, HEAD_DIM)

    g = jax.nn.sigmoid(gate_logits.astype(f32)).astype(q.dtype)
    o = g[..., 0:1] * o_cmp + g[..., 1:2] * o_slc + g[..., 2:3] * o_win
    return o.reshape(B, S, NSA_WIDTH)


def short_gated_conv(h, b_gate, c_gate, conv_w):
    u = c_gate * h
    y = lax.conv_general_dilated(u, conv_w[:, None, :].astype(u.dtype), window_strides=(1,),
                                 padding=[(CONV_K - 1, 0)], dimension_numbers=('NWC', 'WIO', 'NWC'),
                                 feature_group_count=u.shape[-1])
    return b_gate * y


def hybrid_mixer(xn, positions, w_mix_in, cmp_pos_k, cmp_pos_v, cmp_w1_k, cmp_b1_k, cmp_w2_k,
                 cmp_w1_v, cmp_b1_v, cmp_w2_v, conv_w, g_nsa_out, g_conv_out, w_mix_out):
    B, S, _ = xn.shape
    proj = xn @ w_mix_in
    splits = np.cumsum(IN_SIZES)[:-1].tolist()
    q, kc, vc, ks, vs, kw, vw, gl, ch, cb, cc = jnp.split(proj, splits, axis=-1)
    kvshape = (B, S, NSA_KV_HEADS, HEAD_DIM)
    q = rope(q.reshape(B, S, NSA_HEADS, HEAD_DIM), positions)
    kc = rope(kc.reshape(kvshape), positions)
    ks = rope(ks.reshape(kvshape), positions)
    kw = rope(kw.reshape(kvshape), positions)
    o_nsa = nsa_attention(q, kc, vc.reshape(kvshape), ks, vs.reshape(kvshape), kw, vw.reshape(kvshape),
                          gl.reshape(B, S, NSA_HEADS, 3), cmp_pos_k, cmp_pos_v,
                          cmp_w1_k, cmp_b1_k, cmp_w2_k, cmp_w1_v, cmp_b1_v, cmp_w2_v)
    o_conv = short_gated_conv(ch, cb, cc, conv_w)
    o = jnp.concatenate([rmsnorm(o_nsa, g_nsa_out), rmsnorm(o_conv, g_conv_out)], axis=-1)
    return o @ w_mix_out


def memory_cross_attention(hn, memn, w_xq, w_xkv, w_xo):
    B, S, D = hn.shape
    M = memn.shape[1]
    q = (hn @ w_xq).reshape(B, S, XATTN_HEADS, XATTN_HEAD_DIM)
    k, v = jnp.split(memn @ w_xkv, 2, axis=-1)
    k = k.reshape(B, M, XATTN_HEADS, XATTN_HEAD_DIM)
    v = v.reshape(B, M, XATTN_HEADS, XATTN_HEAD_DIM)
    s = jnp.einsum('bshd,bmhd->bhsm', q, k, preferred_element_type=jnp.float32) * XATTN_HEAD_DIM ** -0.5
    p = jax.nn.softmax(s, axis=-1)
    o = jnp.einsum('bhsm,bmhd->bshd', p.astype(v.dtype), v).reshape(B, S, D)
    return o @ w_xo


def moe_ffn(xn, w_router, b_router, w_gate_up, b_gate_up, w_down, b_down):
    B, S, D = xn.shape
    T = B * S
    A = T * TOP_K
    xt = xn.reshape(T, D)
    logits = (xt @ w_router + b_router).astype(jnp.float32)
    top_v, top_e = lax.top_k(logits, TOP_K)
    gate = jax.nn.softmax(top_v, axis=-1)
    e_flat = top_e.reshape(A).astype(jnp.int32)
    tok_flat = jnp.repeat(jnp.arange(T, dtype=jnp.int32), TOP_K)
    w_flat = gate.reshape(A)
    order = jnp.argsort(e_flat)
    e_sorted = e_flat[order]
    counts = jnp.bincount(e_flat, length=N_EXPERTS).astype(jnp.int32)
    padded = (counts + MOE_BLK - 1) // MOE_BLK * MOE_BLK
    start = jnp.cumsum(counts) - counts
    pend = jnp.cumsum(padded)
    pstart = pend - padded
    dest = pstart[e_sorted] + (jnp.arange(A, dtype=jnp.int32) - start[e_sorted])
    P = A + N_EXPERTS * MOE_BLK
    nblk = P // MOE_BLK
    tok_buf = jnp.zeros((P,), jnp.int32).at[dest].set(tok_flat[order])
    w_buf = jnp.zeros((P,), jnp.float32).at[dest].set(w_flat[order])
    blk_e = jnp.minimum(jnp.searchsorted(pend, jnp.arange(nblk, dtype=jnp.int32) * MOE_BLK, side='right'),
                        N_EXPERTS - 1).astype(jnp.int32)

    def expert_block(args):
        tok, e = args
        xb = xt[tok]
        gu = xb @ w_gate_up[e] + b_gate_up[e]
        g, u = gu[:, :D_FF], gu[:, D_FF:]
        g = jnp.minimum(g, SWIGLU_LIMIT)
        u = jnp.clip(u, -SWIGLU_LIMIT, SWIGLU_LIMIT)
        hmid = (u + 1.0) * (g * jax.nn.sigmoid(SWIGLU_ALPHA * g))
        return hmid @ w_down[e] + b_down[e]

    out = lax.map(expert_block, (tok_buf.reshape(nblk, MOE_BLK), blk_e))
    y = jnp.zeros((T, D), jnp.float32).at[tok_buf].add(out.reshape(P, D).astype(jnp.float32) * w_buf[:, None])
    return y.astype(xn.dtype).reshape(B, S, D)


def setup_inputs(seed: int = 0) -> dict:
    key = jax.random.key(seed)
    ks = jax.random.split(key, 32)
    L, D = DEPTH, D_MODEL

    def nrm(k, shape, scale):
        return jax.random.normal(k, shape, jnp.float32) * scale

    def gain(k, shape):
        return 1.0 + 0.02 * jax.random.normal(k, shape, jnp.float32)

    positions = (jax.random.randint(ks[2], (BATCH, 1), 0, 4096, dtype=jnp.int32)
                 + jnp.arange(SEQ, dtype=jnp.int32)[None, :])
    return {
        'x': nrm(ks[0], (BATCH, SEQ, D), 1.0),
        'mem': nrm(ks[1], (BATCH, N_MEM, D), 1.0),
        'positions': positions,
        'g_mix_norm': gain(ks[3], (L, D)),
        'w_mix_in': nrm(ks[4], (L, D, IN_COLS), D ** -0.5),
        'cmp_pos_k': nrm(ks[5], (L, CMP_LEN, HEAD_DIM), 0.1),
        'cmp_pos_v': nrm(ks[6], (L, CMP_LEN, HEAD_DIM), 0.1),
        'cmp_w1_k': nrm(ks[7], (L, CMP_LEN * HEAD_DIM, CMP_HIDDEN), (CMP_LEN * HEAD_DIM) ** -0.5),
        'cmp_b1_k': nrm(ks[8], (L, CMP_HIDDEN), 0.01),
        'cmp_w2_k': nrm(ks[9], (L, CMP_HIDDEN, HEAD_DIM), CMP_HIDDEN ** -0.5),
        'cmp_w1_v': nrm(ks[10], (L, CMP_LEN * HEAD_DIM, CMP_HIDDEN), (CMP_LEN * HEAD_DIM) ** -0.5),
        'cmp_b1_v': nrm(ks[11], (L, CMP_HIDDEN), 0.01),
        'cmp_w2_v': nrm(ks[12], (L, CMP_HIDDEN, HEAD_DIM), CMP_HIDDEN ** -0.5),
        'conv_w': nrm(ks[13], (L, CONV_K, CONV_WIDTH), CONV_K ** -0.5),
        'g_nsa_out': gain(ks[14], (L, NSA_WIDTH)),
        'g_conv_out': gain(ks[15], (L, CONV_WIDTH)),
        'w_mix_out': nrm(ks[16], (L, D_MIX, D), D_MIX ** -0.5),
        'g_xattn_norm': gain(ks[17], (L, D)),
        'g_mem_norm': gain(ks[18], (L, D)),
        'w_xq': nrm(ks[19], (L, D, D), D ** -0.5),
        'w_xkv': nrm(ks[20], (L, D, 2 * D), D ** -0.5),
        'w_xo': nrm(ks[21], (L, D, D), D ** -0.5),
        'g_moe_norm': gain(ks[22], (L, D)),
        'w_router': nrm(ks[23], (L, D, N_EXPERTS), D ** -0.5),
        'b_router': nrm(ks[24], (L, N_EXPERTS), 0.01),
        'w_gate_up': nrm(ks[25], (L, N_EXPERTS, D, 2 * D_FF), D ** -0.5),
        'b_gate_up': nrm(ks[26], (L, N_EXPERTS, 2 * D_FF), 0.01),
        'w_down': nrm(ks[27], (L, N_EXPERTS, D_FF, D), D_FF ** -0.5),
        'b_down': nrm(ks[28], (L, N_EXPERTS, D), 0.01),
        'g_final': gain(ks[29], (D,)),
    }


def reference(x, mem, positions, g_mix_norm, w_mix_in, cmp_pos_k, cmp_pos_v, cmp_w1_k, cmp_b1_k, cmp_w2_k,
              cmp_w1_v, cmp_b1_v, cmp_w2_v, conv_w, g_nsa_out, g_conv_out, w_mix_out, g_xattn_norm,
              g_mem_norm, w_xq, w_xkv, w_xo, g_moe_norm, w_router, b_router, w_gate_up, b_gate_up,
              w_down, b_down, g_final):
    h = x
    for l in range(DEPTH):
        h = h + hybrid_mixer(rmsnorm(h, g_mix_norm[l]), positions, w_mix_in[l], cmp_pos_k[l], cmp_pos_v[l],
                             cmp_w1_k[l], cmp_b1_k[l], cmp_w2_k[l], cmp_w1_v[l], cmp_b1_v[l], cmp_w2_v[l],
                             conv_w[l], g_nsa_out[l], g_conv_out[l], w_mix_out[l])
        h = h + memory_cross_attention(rmsnorm(h, g_xattn_norm[l]), rmsnorm(mem, g_mem_norm[l]),
                                       w_xq[l], w_xkv[l], w_xo[l])
        h = h + moe_ffn(rmsnorm(h, g_moe_norm[l]), w_router[l], b_router[l], w_gate_up[l], b_gate_up[l],
                        w_down[l], b_down[l])
    return rmsnorm(h, g_final)
```

```python
import functools

import jax
import jax.numpy as jnp
from jax import lax
from jax.experimental import pallas as pl
from jax.experimental.pallas import tpu as pltpu

f32 = jnp.float32
bf16 = jnp.bfloat16
i32 = jnp.int32

D_MODEL = 1024
HEAD_DIM = 64
NSA_HEADS = 8
NSA_KV_HEADS = 2
HPG = NSA_HEADS // NSA_KV_HEADS
NSA_WIDTH = NSA_HEADS * HEAD_DIM
KV_WIDTH = NSA_KV_HEADS * HEAD_DIM
CONV_WIDTH = D_MODEL - NSA_WIDTH
CONV_K = 3
CMP_LEN = 32
CMP_STRIDE = 16
CMP_HIDDEN = 256
SEL_LEN = 64
N_SEL = 16
WINDOW = 512
ROPE_THETA = 10000.0
XATTN_HEADS = 4
XATTN_HEAD_DIM = D_MODEL // XATTN_HEADS
N_EXPERTS = 32
TOP_K = 4
D_FF = D_MODEL
SWIGLU_LIMIT = 7.0
SWIGLU_ALPHA = 1.702
RMS_EPS = 1e-5
NEG_INF = -1e30
FORCED = 1e30

LANES = 128
SUBLANES = 8
VMEM_LIMIT = 56 * 1024 * 1024

TM_IN = 512
TQ = 256
TK = 256
TM_POST = 256
TM_ROUTE = 512
TM_ROW = 128
BM = 256
ROW_SUB = D_MODEL // LANES


def _cparams(n_axes, **kw):
    return pltpu.CompilerParams(dimension_semantics=("arbitrary",) * n_axes,
                                vmem_limit_bytes=VMEM_LIMIT, **kw)


def _rms(t, gain):
    return t * lax.rsqrt(jnp.mean(t * t, axis=-1, keepdims=True) + RMS_EPS) * gain


def _inproj_kernel(x_ref, g_ref, wr_ref, wt_ref, cosr_ref, sinr_ref, cost_ref, sint_ref,
                   convw_ref, gconv_ref,
                   qT_ref, kc_ref, vc_ref, ks_ref, kw_ref, vsT_ref, vwT_ref, gT_ref, conv_ref,
                   ubuf, *, tiles_per_seq):
    i = pl.program_id(0)
    tm = x_ref.shape[0]
    xb = _rms(x_ref[...], g_ref[...]).astype(bf16)

    pr = jnp.dot(xb, wr_ref[:, 0:4 * KV_WIDTH], preferred_element_type=f32)
    cosr = cosr_ref[...]
    sinr = sinr_ref[...]
    lane = lax.broadcasted_iota(i32, (tm, KV_WIDTH), 1)
    first_half = (lane & (HEAD_DIM - 1)) < HEAD_DIM // 2

    def rope_rows(t):
        rot = jnp.where(first_half, pltpu.roll(t, KV_WIDTH - HEAD_DIM // 2, 1),
                        pltpu.roll(t, HEAD_DIM // 2, 1))
        return t * cosr + rot * sinr

    kc_ref[...] = rope_rows(pr[:, 0:KV_WIDTH]).astype(bf16)
    vc_ref[...] = pr[:, KV_WIDTH:2 * KV_WIDTH].astype(bf16)
    ks_ref[...] = rope_rows(pr[:, 2 * KV_WIDTH:3 * KV_WIDTH]).astype(bf16)
    kw_ref[...] = rope_rows(pr[:, 3 * KV_WIDTH:4 * KV_WIDTH]).astype(bf16)

    c0 = 4 * KV_WIDTH
    pc = jnp.dot(xb, wr_ref[:, c0:c0 + 3 * CONV_WIDTH], preferred_element_type=f32)
    ch = pc[:, 0:CONV_WIDTH]
    cb = pc[:, CONV_WIDTH:2 * CONV_WIDTH]
    cc = pc[:, 2 * CONV_WIDTH:3 * CONV_WIDTH]
    u = cc * ch

    @pl.when(i % tiles_per_seq == 0)
    def _():
        ubuf[0:SUBLANES, :] = jnp.zeros((SUBLANES, CONV_WIDTH), f32)

    @pl.when(i % tiles_per_seq != 0)
    def _():
        ubuf[0:SUBLANES, :] = ubuf[tm:tm + SUBLANES, :]

    ubuf[SUBLANES:SUBLANES + tm, :] = u
    u1 = ubuf[SUBLANES - 1:SUBLANES - 1 + tm, :]
    u2 = ubuf[SUBLANES - 2:SUBLANES - 2 + tm, :]
    w = convw_ref[...]
    y = cb * (w[0:1, :] * u2 + w[1:2, :] * u1 + w[2:3, :] * u)
    conv_ref[...] = _rms(y, gconv_ref[...]).astype(bf16)

    pt = lax.dot_general(wt_ref[...], xb, (((1,), (1,)), ((), ())), preferred_element_type=f32)
    cost = cost_ref[...]
    sint = sint_ref[...]
    half = HEAD_DIM // 2
    scale = HEAD_DIM ** -0.5
    for h in range(NSA_HEADS):
        t1 = pt[h * HEAD_DIM:h * HEAD_DIM + half, :]
        t2 = pt[h * HEAD_DIM + half:(h + 1) * HEAD_DIM, :]
        qT_ref[h * HEAD_DIM:h * HEAD_DIM + half, :] = ((t1 * cost - t2 * sint) * scale).astype(bf16)
        qT_ref[h * HEAD_DIM + half:(h + 1) * HEAD_DIM, :] = ((t2 * cost + t1 * sint) * scale).astype(bf16)
    r0 = NSA_WIDTH
    vsT_ref[...] = pt[r0:r0 + KV_WIDTH, :].astype(bf16)
    vwT_ref[...] = pt[r0 + KV_WIDTH:r0 + 2 * KV_WIDTH, :].astype(bf16)
    gT_ref[...] = jax.nn.sigmoid(pt[r0 + 2 * KV_WIDTH:r0 + 2 * KV_WIDTH + 32, :])


def _inproj(x2, g_mix, w_row, w_t, cosr, sinr, cost, sint, conv_w8, g_conv, seq):
    T = x2.shape[0]
    tm = TM_IN
    n_row = w_row.shape[1]
    n_t = w_t.shape[0]
    row = lambda i: (i, 0)
    col = lambda i: (0, i)
    const = lambda i: (0, 0)
    out_shape = (
        jax.ShapeDtypeStruct((NSA_WIDTH, T), bf16),
        jax.ShapeDtypeStruct((T, KV_WIDTH), bf16),
        jax.ShapeDtypeStruct((T, KV_WIDTH), bf16),
        jax.ShapeDtypeStruct((T, KV_WIDTH), bf16),
        jax.ShapeDtypeStruct((T, KV_WIDTH), bf16),
        jax.ShapeDtypeStruct((KV_WIDTH, T), bf16),
        jax.ShapeDtypeStruct((KV_WIDTH, T), bf16),
        jax.ShapeDtypeStruct((32, T), f32),
        jax.ShapeDtypeStruct((T, CONV_WIDTH), bf16),
    )
    out_specs = (
        pl.BlockSpec((NSA_WIDTH, tm), col),
        pl.BlockSpec((tm, KV_WIDTH), row), pl.BlockSpec((tm, KV_WIDTH), row),
        pl.BlockSpec((tm, KV_WIDTH), row), pl.BlockSpec((tm, KV_WIDTH), row),
        pl.BlockSpec((KV_WIDTH, tm), col), pl.BlockSpec((KV_WIDTH, tm), col),
        pl.BlockSpec((32, tm), col),
        pl.BlockSpec((tm, CONV_WIDTH), row),
    )
    in_specs = [
        pl.BlockSpec((tm, D_MODEL), row),
        pl.BlockSpec((1, D_MODEL), const),
        pl.BlockSpec((D_MODEL, n_row), const),
        pl.BlockSpec((n_t, D_MODEL), const),
        pl.BlockSpec((tm, KV_WIDTH), row), pl.BlockSpec((tm, KV_WIDTH), row),
        pl.BlockSpec((HEAD_DIM // 2, tm), col), pl.BlockSpec((HEAD_DIM // 2, tm), col),
        pl.BlockSpec((SUBLANES, CONV_WIDTH), const),
        pl.BlockSpec((1, CONV_WIDTH), const),
    ]
    return pl.pallas_call(
        functools.partial(_inproj_kernel, tiles_per_seq=seq // tm),
        grid=(T // tm,), in_specs=in_specs, out_specs=out_specs, out_shape=out_shape,
        scratch_shapes=[pltpu.VMEM((tm + 2 * SUBLANES, CONV_WIDTH), f32)],
        compiler_params=_cparams(1), name="inproj",
    )(x2, g_mix, w_row, w_t, cosr, sinr, cost, sint, conv_w8, g_conv)


def _compress_kernel(xk_ref, xv_ref, pk_ref, pv_ref, w1k_ref, w1v_ref, b1k_ref, b1v_ref,
                     w2k_ref, w2vT_ref, kcc_ref, vcT_ref):
    ncp = xk_ref.shape[1]

    def hidden(x_ref, p_ref, w1_ref, b1_ref):
        x = x_ref[0].astype(f32)
        lo = (x + p_ref[0:1, :]).astype(bf16)
        hi = (x + p_ref[1:2, :]).astype(bf16)
        a = jnp.dot(lo, w1_ref[0], preferred_element_type=f32)
        b = jnp.dot(hi, w1_ref[1], preferred_element_type=f32)
        pre = a + pltpu.roll(b, ncp - 1, 0) + b1_ref[...]
        return jax.nn.gelu(pre).astype(bf16)

    hk = hidden(xk_ref, pk_ref, w1k_ref, b1k_ref)
    hv = hidden(xv_ref, pv_ref, w1v_ref, b1v_ref)
    for g in range(NSA_KV_HEADS):
        sl = slice(g * CMP_HIDDEN, (g + 1) * CMP_HIDDEN)
        kcc_ref[0, g] = jnp.dot(hk[:, sl], w2k_ref[...], preferred_element_type=f32).astype(bf16)
        vcT_ref[0, g] = lax.dot_general(w2vT_ref[...], hv[:, sl], (((1,), (1,)), ((), ())),
                                        preferred_element_type=f32).astype(bf16)


def _compress(kc_rows, vc_rows, pk, pv, w1k, w1v, b1k, b1v, w2k, w2vT, batch, seq):
    ncp = seq // CMP_STRIDE
    wide = CMP_STRIDE * KV_WIDTH
    xk = kc_rows.reshape(batch, ncp, wide)
    xv = vc_rows.reshape(batch, ncp, wide)
    c2 = lambda b: (0, 0)
    c3 = lambda b: (0, 0, 0)
    in_specs = [
        pl.BlockSpec((1, ncp, wide), lambda b: (b, 0, 0)),
        pl.BlockSpec((1, ncp, wide), lambda b: (b, 0, 0)),
        pl.BlockSpec((SUBLANES, wide), c2), pl.BlockSpec((SUBLANES, wide), c2),
        pl.BlockSpec((2, wide, 2 * CMP_HIDDEN), c3), pl.BlockSpec((2, wide, 2 * CMP_HIDDEN), c3),
        pl.BlockSpec((1, 2 * CMP_HIDDEN), c2), pl.BlockSpec((1, 2 * CMP_HIDDEN), c2),
        pl.BlockSpec((CMP_HIDDEN, HEAD_DIM), c2), pl.BlockSpec((HEAD_DIM, CMP_HIDDEN), c2),
    ]
    out_shape = (jax.ShapeDtypeStruct((batch, NSA_KV_HEADS, ncp, HEAD_DIM), bf16),
                 jax.ShapeDtypeStruct((batch, NSA_KV_HEADS, HEAD_DIM, ncp), bf16))
    out_specs = (pl.BlockSpec((1, NSA_KV_HEADS, ncp, HEAD_DIM), lambda b: (b, 0, 0, 0)),
                 pl.BlockSpec((1, NSA_KV_HEADS, HEAD_DIM, ncp), lambda b: (b, 0, 0, 0)))
    return pl.pallas_call(
        _compress_kernel, grid=(batch,), in_specs=in_specs, out_specs=out_specs,
        out_shape=out_shape, compiler_params=_cparams(1), name="compress",
    )(xk, xv, pk, pv, w1k, w1v, b1k, b1v, w2k, w2vT)


def _nsa_kernel(qT_ref, ks_ref, kw_ref, vsT_ref, vwT_ref, kcc_ref, vcT_ref, gT_ref, ovT_ref, eT_ref,
                o_ref):
    g = pl.program_id(1)
    qi = pl.program_id(2)
    s0 = qi * TQ
    nq = HPG * TQ
    ncp = kcc_ref.shape[2]

    q4 = jnp.concatenate([qT_ref[h * HEAD_DIM:(h + 1) * HEAD_DIM, :] for h in range(HPG)], axis=1)
    zeros = jnp.zeros_like(q4)
    q4z = jnp.where(g == 0, jnp.concatenate([q4, zeros], axis=0), jnp.concatenate([zeros, q4], axis=0))
    s_lane = s0 + (lax.broadcasted_iota(i32, (1, nq), 1) & (TQ - 1))

    sc = jnp.dot(kcc_ref[0, 0], q4, preferred_element_type=f32)
    c_end = lax.broadcasted_iota(i32, (ncp, 1), 0) * CMP_STRIDE + (CMP_LEN - 1)
    cmask = c_end <= s_lane
    scm = jnp.where(cmask, sc, NEG_INF)
    m_c = jnp.max(scm, axis=0, keepdims=True)
    e_c = jnp.where(cmask, jnp.exp(scm - m_c), 0.0)
    l_c = jnp.sum(e_c, axis=0, keepdims=True)
    p_c = e_c * jnp.where(l_c > 0.0, 1.0 / l_c, 0.0)
    o_cmp = jnp.dot(vcT_ref[0, 0], p_c.astype(bf16), preferred_element_type=f32)

    ps = p_c[:, 0:TQ]
    for h in range(1, HPG):
        ps = ps + p_c[:, h * TQ:(h + 1) * TQ]
    p_hi = ps.astype(bf16)
    r1 = ps - p_hi.astype(f32)
    p_mid = r1.astype(bf16)
    p_lo = (r1 - p_mid.astype(f32)).astype(bf16)
    ov = ovT_ref[...]
    imp = (jnp.dot(ov, p_hi, preferred_element_type=f32) + jnp.dot(ov, p_mid, preferred_element_type=f32)
           + jnp.dot(ov, p_lo, preferred_element_type=f32))
    ns = imp.shape[0]
    j_blk = lax.broadcasted_iota(i32, (ns, 1), 0)
    cur = (s0 + lax.broadcasted_iota(i32, (1, TQ), 1)) // SEL_LEN
    forced = (j_blk == 0) | (j_blk == cur) | (j_blk == cur - 1)
    imp = jnp.where(forced, FORCED, jnp.where(j_blk > cur, NEG_INF, imp))
    rank = jnp.zeros((ns, TQ), f32)
    for i in range(ns):
        row = imp[i:i + 1, :]
        tie_before = jnp.where(j_blk > i, 1.0, 0.0)
        rank = rank + jnp.where(row > imp, 1.0, jnp.where(row == imp, tie_before, 0.0))
    sel_bias = jnp.where(rank < float(min(N_SEL, ns)), 0.0, NEG_INF).astype(bf16)

    def online_update(carry, sT, vT_ref, k0):
        m, l, acc = carry
        m_new = jnp.maximum(m, jnp.max(sT, axis=0, keepdims=True))
        alpha = jnp.exp(m - m_new)
        p = jnp.exp(sT - m_new)
        l = alpha * l + jnp.sum(p, axis=0, keepdims=True)
        acc = alpha * acc + jnp.dot(vT_ref[:, pl.ds(k0, TK)], p.astype(bf16), preferred_element_type=f32)
        return m_new, l, acc

    def key_pos(k0):
        return k0 + lax.broadcasted_iota(i32, (TK, 1), 0)

    def sel_scores(k0):
        sT = jnp.dot(ks_ref[pl.ds(k0, TK), :], q4z, preferred_element_type=f32)
        bias = jnp.dot(eT_ref[pl.ds(k0, TK), :], sel_bias, preferred_element_type=f32)
        return sT + jnp.concatenate([bias] * HPG, axis=1)

    init = (jnp.full((1, nq), NEG_INF, f32), jnp.zeros((1, nq), f32), jnp.zeros((HEAD_DIM, nq), f32))

    k_diag = pl.multiple_of(qi * TK, TK)
    causal = key_pos(k_diag) <= s_lane
    carry = online_update(init, jnp.where(causal, sel_scores(k_diag), NEG_INF), vsT_ref, k_diag)

    def sel_body(kt, carry):
        k0 = pl.multiple_of(kt * TK, TK)
        return online_update(carry, sel_scores(k0), vsT_ref, k0)

    _, l_s, acc_s = lax.fori_loop(0, qi, sel_body, carry)
    o_slc = acc_s * (1.0 / l_s)

    def win_scores(k0):
        sT = jnp.dot(kw_ref[pl.ds(k0, TK), :], q4z, preferred_element_type=f32)
        kp = key_pos(k0)
        return jnp.where((kp <= s_lane) & (s_lane - kp < WINDOW), sT, NEG_INF)

    carry = online_update(init, win_scores(k_diag), vwT_ref, k_diag)

    def win_body(kt, carry):
        k0 = pl.multiple_of(kt * TK, TK)
        return online_update(carry, win_scores(k0), vwT_ref, k0)

    _, l_w, acc_w = lax.fori_loop(jnp.maximum(qi - WINDOW // TK, 0), qi, win_body, carry)
    o_win = acc_w * (1.0 / l_w)

    gates = gT_ref[...]
    for h in range(HPG):
        sl = slice(h * TQ, (h + 1) * TQ)
        o = (gates[3 * h:3 * h + 1, :] * o_cmp[:, sl] + gates[3 * h + 1:3 * h + 2, :] * o_slc[:, sl]
             + gates[3 * h + 2:3 * h + 3, :] * o_win[:, sl])
        o_ref[h * HEAD_DIM:(h + 1) * HEAD_DIM, :] = o.astype(bf16)


def _nsa(qT, ks, kw, vsT, vwT, kcc, vcT, gT, ovT, eT, batch, seq):
    T = batch * seq
    nqt = seq // TQ
    ncp = kcc.shape[2]
    ns = seq // SEL_LEN
    gw = HPG * HEAD_DIM
    qmap = lambda b, g, q: (g, b * nqt + q)
    in_specs = [
        pl.BlockSpec((gw, TQ), qmap),
        pl.BlockSpec((seq, KV_WIDTH), lambda b, g, q: (b, 0)),
        pl.BlockSpec((seq, KV_WIDTH), lambda b, g, q: (b, 0)),
        pl.BlockSpec((HEAD_DIM, seq), lambda b, g, q: (g, b)),
        pl.BlockSpec((HEAD_DIM, seq), lambda b, g, q: (g, b)),
        pl.BlockSpec((1, 1, ncp, HEAD_DIM), lambda b, g, q: (b, g, 0, 0)),
        pl.BlockSpec((1, 1, HEAD_DIM, ncp), lambda b, g, q: (b, g, 0, 0)),
        pl.BlockSpec((16, TQ), qmap),
        pl.BlockSpec((ns, ncp), lambda b, g, q: (0, 0)),
        pl.BlockSpec((seq, ns), lambda b, g, q: (0, 0)),
    ]
    return pl.pallas_call(
        _nsa_kernel, grid=(batch, NSA_KV_HEADS, nqt), in_specs=in_specs,
        out_specs=pl.BlockSpec((gw, TQ), qmap),
        out_shape=jax.ShapeDtypeStruct((NSA_WIDTH, T), bf16),
        compiler_params=_cparams(3), name="nsa",
    )(qT, ks, kw, vsT, vwT, kcc, vcT, gT, ovT, eT)


def _memkv_kernel(mem_ref, g_ref, wkT_ref, wv_ref, kT_ref, v_ref):
    mb = _rms(mem_ref[0], g_ref[...]).astype(bf16)
    kT_ref[0] = lax.dot_general(wkT_ref[...], mb, (((1,), (1,)), ((), ())),
                                preferred_element_type=f32).astype(bf16)
    v_ref[0] = jnp.dot(mb, wv_ref[...], preferred_element_type=f32).astype(bf16)


def _memkv(mem, g_mem, wkT, wv):
    batch, n_mem, _ = mem.shape
    c2 = lambda b: (0, 0)
    return pl.pallas_call(
        _memkv_kernel, grid=(batch,),
        in_specs=[pl.BlockSpec((1, n_mem, D_MODEL), lambda b: (b, 0, 0)),
                  pl.BlockSpec((1, D_MODEL), c2),
                  pl.BlockSpec((D_MODEL, D_MODEL), c2), pl.BlockSpec((D_MODEL, D_MODEL), c2)],
        out_specs=(pl.BlockSpec((1, D_MODEL, n_mem), lambda b: (b, 0, 0)),
                   pl.BlockSpec((1, n_mem, D_MODEL), lambda b: (b, 0, 0))),
        out_shape=(jax.ShapeDtypeStruct((batch, D_MODEL, n_mem), bf16),
                   jax.ShapeDtypeStruct((batch, n_mem, D_MODEL), bf16)),
        compiler_params=_cparams(1), name="memkv",
    )(mem, g_mem, wkT, wv)


def _post_kernel(x_ref, oT_ref, conv_ref, gnsa_ref, wout_ref, gx_ref, wq_ref, kT_ref, v_ref, wo_ref,
                 gmoe_ref, wr_ref, br_ref,
                 h2_ref, xn2_ref, eidx_ref, gate_ref, sel_ref):
    tm = x_ref.shape[0]
    oT = oT_ref[...].astype(f32)
    onT = (oT * lax.rsqrt(jnp.mean(oT * oT, axis=0, keepdims=True) + RMS_EPS) * gnsa_ref[...]).astype(bf16)
    mix = lax.dot_general(onT, wout_ref[0:NSA_WIDTH, :], (((0,), (0,)), ((), ())),
                          preferred_element_type=f32)
    mix = mix + jnp.dot(conv_ref[...], wout_ref[NSA_WIDTH:D_MODEL, :], preferred_element_type=f32)
    h1 = x_ref[...] + mix

    hn = _rms(h1, gx_ref[...]).astype(bf16)
    q = (jnp.dot(hn, wq_ref[...], preferred_element_type=f32) * (XATTN_HEAD_DIM ** -0.5)).astype(bf16)
    heads = []
    for h in range(XATTN_HEADS):
        sl = slice(h * XATTN_HEAD_DIM, (h + 1) * XATTN_HEAD_DIM)
        s = jnp.dot(q[:, sl], kT_ref[0, sl, :], preferred_element_type=f32)
        e = jnp.exp(s - jnp.max(s, axis=-1, keepdims=True))
        p = e * (1.0 / jnp.sum(e, axis=-1, keepdims=True))
        heads.append(jnp.dot(p.astype(bf16), v_ref[0, :, sl], preferred_element_type=f32))
    o = jnp.concatenate(heads, axis=1).astype(bf16)
    h2 = h1 + jnp.dot(o, wo_ref[...], preferred_element_type=f32)
    h2_ref[...] = h2

    xn2 = _rms(h2, gmoe_ref[...])
    for s_ in range(ROW_SUB):
        xn2_ref[pl.ds(s_, tm, stride=ROW_SUB), :] = xn2[:, s_ * LANES:(s_ + 1) * LANES]

    logits = jnp.dot(xn2.astype(bf16), wr_ref[...], preferred_element_type=f32) + br_ref[...]
    lane = lax.broadcasted_iota(i32, (tm, LANES), 1)
    work = logits
    sel = jnp.zeros((tm, LANES), f32)
    eidx = jnp.zeros((tm, LANES), i32)
    vals = []
    for k in range(TOP_K):
        mk = jnp.max(work, axis=-1, keepdims=True)
        ik = jnp.min(jnp.where(work == mk, lane, LANES), axis=-1, keepdims=True)
        hit = lane == ik
        work = jnp.where(hit, -jnp.inf, work)
        sel = jnp.where(hit, 1.0, sel)
        eidx = jnp.where(lane == k, ik, eidx)
        vals.append(mk)
    es = [jnp.exp(v - vals[0]) for v in vals]
    den = es[0]
    for e in es[1:]:
        den = den + e
    gate = jnp.zeros((tm, LANES), f32)
    for k in range(TOP_K):
        gate = jnp.where(lane == k, es[k] / den, gate)
    eidx_ref[...] = eidx
    gate_ref[...] = gate
    sel_ref[...] = sel.astype(bf16)


def _post(x2, oT, conv_n, g_nsa_col, w_out, g_x, w_q, kT, v, w_o, g_moe, w_r, b_r, seq):
    T = x2.shape[0]
    tm = TM_POST
    tps = seq // tm
    n_mem = v.shape[1]
    row = lambda i: (i, 0)
    const = lambda i: (0, 0)
    in_specs = [
        pl.BlockSpec((tm, D_MODEL), row),
        pl.BlockSpec((NSA_WIDTH, tm), lambda i: (0, i)),
        pl.BlockSpec((tm, CONV_WIDTH), row),
        pl.BlockSpec((NSA_WIDTH, 1), const),
        pl.BlockSpec((D_MODEL, D_MODEL), const),
        pl.BlockSpec((1, D_MODEL), const),
        pl.BlockSpec((D_MODEL, D_MODEL), const),
        pl.BlockSpec((1, D_MODEL, n_mem), lambda i: (i // tps, 0, 0)),
        pl.BlockSpec((1, n_mem, D_MODEL), lambda i: (i // tps, 0, 0)),
        pl.BlockSpec((D_MODEL, D_MODEL), const),
        pl.BlockSpec((1, D_MODEL), const),
        pl.BlockSpec((D_MODEL, LANES), const),
        pl.BlockSpec((1, LANES), const),
    ]
    out_shape = (jax.ShapeDtypeStruct((T, D_MODEL), f32),
                 jax.ShapeDtypeStruct((T * ROW_SUB, LANES), f32),
                 jax.ShapeDtypeStruct((T, LANES), i32),
                 jax.ShapeDtypeStruct((T, LANES), f32),
                 jax.ShapeDtypeStruct((T, LANES), bf16))
    out_specs = (pl.BlockSpec((tm, D_MODEL), row),
                 pl.BlockSpec((tm * ROW_SUB, LANES), row),
                 pl.BlockSpec((tm, LANES), row), pl.BlockSpec((tm, LANES), row),
                 pl.BlockSpec((tm, LANES), row))
    return pl.pallas_call(
        _post_kernel, grid=(T // tm,), in_specs=in_specs, out_specs=out_specs, out_shape=out_shape,
        compiler_params=_cparams(1), name="post",
    )(x2, oT, conv_n, g_nsa_col, w_out, g_x, w_q, kT, v, w_o, g_moe, w_r, b_r)


def _count_kernel(sel_ref, cum_ref, cnt_ref, carry):
    i = pl.program_id(0)
    tm = sel_ref.shape[0]

    @pl.when(i == 0)
    def _():
        carry[...] = jnp.zeros_like(carry)

    sel = sel_ref[...]
    r = lax.broadcasted_iota(i32, (tm, tm), 0)
    c = lax.broadcasted_iota(i32, (tm, tm), 1)
    strict_lower = jnp.where(c < r, 1.0, 0.0).astype(bf16)
    base = carry[0:1, :]
    cum_ref[...] = jnp.dot(strict_lower, sel, preferred_element_type=f32) + base
    total = base + jnp.sum(sel.astype(f32), axis=0, keepdims=True)
    carry[...] = jnp.broadcast_to(total, carry.shape)
    cnt_ref[...] = jnp.broadcast_to(total, cnt_ref.shape)


def _count(sel):
    T = sel.shape[0]
    tm = TM_ROUTE
    return pl.pallas_call(
        _count_kernel, grid=(T // tm,),
        in_specs=[pl.BlockSpec((tm, LANES), lambda i: (i, 0))],
        out_specs=(pl.BlockSpec((tm, LANES), lambda i: (i, 0)),
                   pl.BlockSpec((SUBLANES, LANES), lambda i: (0, 0))),
        out_shape=(jax.ShapeDtypeStruct((T, LANES), f32), jax.ShapeDtypeStruct((SUBLANES, LANES), f32)),
        scratch_shapes=[pltpu.VMEM((SUBLANES, LANES), f32)],
        compiler_params=_cparams(1), name="route_count",
    )(sel)


def _dest_kernel(cum_ref, eidx_ref, pstart_ref, dest_ref):
    tm = cum_ref.shape[0]
    lane = lax.broadcasted_iota(i32, (tm, LANES), 1)
    row_of = cum_ref[...] + pstart_ref[0:1, :]
    eidx = eidx_ref[...]
    dest = jnp.zeros((tm, LANES), f32)
    for k in range(TOP_K):
        ek = eidx[:, k:k + 1]
        dk = jnp.sum(jnp.where(lane == ek, row_of, 0.0), axis=-1, keepdims=True)
        dest = jnp.where(lane == k, dk, dest)
    dest_ref[...] = dest.astype(i32)


def _dest(cum, eidx, pstart8):
    T = cum.shape[0]
    tm = TM_ROUTE
    row = lambda i: (i, 0)
    return pl.pallas_call(
        _dest_kernel, grid=(T // tm,),
        in_specs=[pl.BlockSpec((tm, LANES), row), pl.BlockSpec((tm, LANES), row),
                  pl.BlockSpec((SUBLANES, LANES), lambda i: (0, 0))],
        out_specs=pl.BlockSpec((tm, LANES), row),
        out_shape=jax.ShapeDtypeStruct((T, LANES), i32),
        compiler_params=_cparams(1), name="route_dest",
    )(cum, eidx, pstart8)


def _dispatch_kernel(dest_ref, tail_ref, x_ref, xg_ref, zbuf, zsem, sem):
    i = pl.program_id(0)
    tm = x_ref.shape[0] // ROW_SUB

    n_blk = xg_ref.shape[0] // (BM * ROW_SUB)

    def zero_copy(row_start):
        start = pl.multiple_of(row_start * ROW_SUB, SUBLANES)
        return pltpu.make_async_copy(zbuf, xg_ref.at[pl.ds(start, BM * ROW_SUB)], zsem)

    @pl.when(i == 0)
    def _():
        zbuf[...] = jnp.zeros_like(zbuf)
        nused = tail_ref[N_EXPERTS]
        for wait in (False, True):
            for e in range(N_EXPERTS):
                @pl.when(tail_ref[e] >= 0)
                def _():
                    cp = zero_copy(tail_ref[e])
                    cp.wait() if wait else cp.start()

                @pl.when(nused + e < n_blk)
                def _():
                    cp = zero_copy((nused + e) * BM)
                    cp.wait() if wait else cp.start()

    def row_copy(r, k):
        d = pl.multiple_of(dest_ref[(i * tm + r) * TOP_K + k] * ROW_SUB, SUBLANES)
        return pltpu.make_async_copy(x_ref.at[pl.ds(r * ROW_SUB, ROW_SUB)], xg_ref.at[pl.ds(d, ROW_SUB)], sem)

    for r in range(tm):
        for k in range(TOP_K):
            row_copy(r, k).start()
    for r in range(tm):
        for k in range(TOP_K):
            row_copy(r, k).wait()


def _dispatch(dest_flat, tail_start, xn2_rows, n_rows):
    T = xn2_rows.shape[0] // ROW_SUB
    tm = TM_ROW
    return pl.pallas_call(
        _dispatch_kernel, grid=(T // tm,),
        in_specs=[pl.BlockSpec(memory_space=pltpu.SMEM), pl.BlockSpec(memory_space=pltpu.SMEM),
                  pl.BlockSpec((tm * ROW_SUB, LANES), lambda i: (i, 0))],
        out_specs=pl.BlockSpec(memory_space=pl.ANY),
        out_shape=jax.ShapeDtypeStruct((n_rows * ROW_SUB, LANES), f32),
        scratch_shapes=[pltpu.VMEM((BM * ROW_SUB, LANES), f32), pltpu.SemaphoreType.DMA,
                        pltpu.SemaphoreType.DMA],
        compiler_params=_cparams(1), name="dispatch",
    )(dest_flat, tail_start, xn2_rows)


def _ffn_kernel(blk_e_ref, blk_map_ref, first_ref, nused_ref, xg_ref, wgu_ref, bgu_ref, wd_ref, bd_ref,
                og_ref, wgu_bf, wd_bf):
    i = pl.program_id(0)

    @pl.when(i < nused_ref[0])
    def _():
        @pl.when(first_ref[i] == 1)
        def _():
            wgu_bf[...] = wgu_ref[0].astype(bf16)
            wd_bf[...] = wd_ref[0].astype(bf16)

        x = jnp.concatenate([xg_ref[pl.ds(s_, BM, stride=ROW_SUB), :] for s_ in range(ROW_SUB)],
                            axis=1).astype(bf16)
        gu = jnp.dot(x, wgu_bf[...], preferred_element_type=f32) + bgu_ref[0]
        gg = jnp.minimum(gu[:, 0:D_FF], SWIGLU_LIMIT)
        uu = jnp.clip(gu[:, D_FF:2 * D_FF], -SWIGLU_LIMIT, SWIGLU_LIMIT)
        hmid = (uu + 1.0) * (gg * jax.nn.sigmoid(SWIGLU_ALPHA * gg))
        out = jnp.dot(hmid.astype(bf16), wd_bf[...], preferred_element_type=f32) + bd_ref[0]
        for s_ in range(ROW_SUB):
            og_ref[pl.ds(s_, BM, stride=ROW_SUB), :] = out[:, s_ * LANES:(s_ + 1) * LANES]


def _ffn(blk_e, blk_map, first, nused, xg, w_gu, b_gu, w_d, b_d):
    n_blk = blk_e.shape[0]
    xmap = lambda i, be, bm, fi, nu: (bm[i], 0)
    emap = lambda i, be, bm, fi, nu: (be[i], 0, 0)
    grid_spec = pltpu.PrefetchScalarGridSpec(
        num_scalar_prefetch=4, grid=(n_blk,),
        in_specs=[pl.BlockSpec((BM * ROW_SUB, LANES), xmap),
                  pl.BlockSpec((1, D_MODEL, 2 * D_FF), emap),
                  pl.BlockSpec((1, 1, 2 * D_FF), emap),
                  pl.BlockSpec((1, D_FF, D_MODEL), emap),
                  pl.BlockSpec((1, 1, D_MODEL), emap)],
        out_specs=pl.BlockSpec((BM * ROW_SUB, LANES), xmap),
        scratch_shapes=[pltpu.VMEM((D_MODEL, 2 * D_FF), bf16), pltpu.VMEM((D_FF, D_MODEL), bf16)],
    )
    return pl.pallas_call(
        _ffn_kernel, grid_spec=grid_spec,
        out_shape=jax.ShapeDtypeStruct(xg.shape, f32),
        input_output_aliases={4: 0},
        compiler_params=_cparams(1), name="ffn",
    )(blk_e, blk_map, first, nused, xg, w_gu, b_gu, w_d, b_d)


def _combine_kernel(dest_ref, og_ref, gate_ref, h2_ref, gfin_ref, o_ref, buf, sem, *, final_norm):
    i = pl.program_id(0)
    tm = h2_ref.shape[0]

    def row_copy(r, k):
        d = pl.multiple_of(dest_ref[(i * tm + r) * TOP_K + k] * ROW_SUB, SUBLANES)
        return pltpu.make_async_copy(og_ref.at[pl.ds(d, ROW_SUB)],
                                     buf.at[k, pl.ds(r * ROW_SUB, ROW_SUB)], sem)

    for r in range(tm):
        for k in range(TOP_K):
            row_copy(r, k).start()
    for r in range(tm):
        for k in range(TOP_K):
            row_copy(r, k).wait()

    gate = gate_ref[...]
    cols = []
    for s_ in range(ROW_SUB):
        acc = gate[:, 0:1] * buf[0, pl.ds(s_, tm, stride=ROW_SUB), :]
        for k in range(1, TOP_K):
            acc = acc + gate[:, k:k + 1] * buf[k, pl.ds(s_, tm, stride=ROW_SUB), :]
        cols.append(acc)
    h = h2_ref[...] + jnp.concatenate(cols, axis=1)
    if final_norm:
        h = _rms(h, gfin_ref[...])
    o_ref[...] = h


def _combine(dest_flat, og, gate, h2, g_final, final_norm):
    T = h2.shape[0]
    tm = TM_ROW
    row = lambda i: (i, 0)
    return pl.pallas_call(
        functools.partial(_combine_kernel, final_norm=final_norm), grid=(T // tm,),
        in_specs=[pl.BlockSpec(memory_space=pltpu.SMEM), pl.BlockSpec(memory_space=pl.ANY),
                  pl.BlockSpec((tm, LANES), row), pl.BlockSpec((tm, D_MODEL), row),
                  pl.BlockSpec((1, D_MODEL), lambda i: (0, 0))],
        out_specs=pl.BlockSpec((tm, D_MODEL), row),
        out_shape=jax.ShapeDtypeStruct((T, D_MODEL), f32),
        scratch_shapes=[pltpu.VMEM((TOP_K, tm * ROW_SUB, LANES), f32), pltpu.SemaphoreType.DMA],
        compiler_params=_cparams(1), name="combine",
    )(dest_flat, og, gate, h2, g_final)


def _prep_inproj_weights(w_in):
    sizes = (NSA_WIDTH,) + (KV_WIDTH,) * 6 + (3 * NSA_HEADS,) + (CONV_WIDTH,) * 3
    offs = [0]
    for s in sizes:
        offs.append(offs[-1] + s)
    seg = lambda n: w_in[:, offs[n]:offs[n + 1]]
    q, kc, vc, ks, vs, kw, vw, gl, ch, cb, cc = (seg(n) for n in range(11))
    w_row = jnp.concatenate([kc, vc, ks, kw, ch, cb, cc], axis=1).astype(bf16)
    gl_g = gl.reshape(D_MODEL, NSA_KV_HEADS, HPG * 3)
    gl_g = jnp.pad(gl_g, ((0, 0), (0, 0), (0, 16 - HPG * 3))).reshape(D_MODEL, NSA_KV_HEADS * 16)
    w_t = jnp.concatenate([q, vs, vw, gl_g], axis=1).T.astype(bf16)
    return w_row, w_t


def _rope_tables(positions):
    half = HEAD_DIM // 2
    inv_freq = ROPE_THETA ** (-jnp.arange(half, dtype=f32) / half)
    ang = positions.reshape(-1).astype(f32)[:, None] * inv_freq
    cos = jnp.cos(ang)
    sin = jnp.sin(ang)
    reps = KV_WIDTH // HEAD_DIM
    cosr = jnp.tile(cos, (1, 2 * reps))
    sinr = jnp.tile(jnp.concatenate([-sin, sin], axis=1), (1, reps))
    return cosr, sinr, cos.T, sin.T


def _cmp_weights(pos, w1, b1):
    w1r = w1.reshape(2, CMP_STRIDE, HEAD_DIM, CMP_HIDDEN)
    eye = jnp.eye(NSA_KV_HEADS, dtype=w1.dtype)
    big = jnp.einsum('aldh,gk->algdkh', w1r, eye)
    big = big.reshape(2, CMP_STRIDE * KV_WIDTH, NSA_KV_HEADS * CMP_HIDDEN).astype(bf16)
    p = jnp.broadcast_to(pos.reshape(2, CMP_STRIDE, 1, HEAD_DIM), (2, CMP_STRIDE, NSA_KV_HEADS, HEAD_DIM))
    p = jnp.pad(p.reshape(2, CMP_STRIDE * KV_WIDTH), ((0, SUBLANES - 2), (0, 0)))
    return big, p, jnp.tile(b1.reshape(1, CMP_HIDDEN), (1, NSA_KV_HEADS))


def _layer(h, memf, tables, p, final_gain, final_norm):
    batch, seq, _ = h.shape
    T = batch * seq
    x2 = h.reshape(T, D_MODEL)
    cosr, sinr, cost, sint = tables
    row1 = lambda v: v.reshape(1, -1)

    w_row, w_t = _prep_inproj_weights(p['w_mix_in'])
    conv_w8 = jnp.pad(p['conv_w'], ((0, SUBLANES - CONV_K), (0, 0)))
    qT, kc, vc, ks, kw, vsT, vwT, gT, conv_n = _inproj(
        x2, row1(p['g_mix_norm']), w_row, w_t, cosr, sinr, cost, sint, conv_w8, row1(p['g_conv_out']), seq)

    w1k, pk, b1k = _cmp_weights(p['cmp_pos_k'], p['cmp_w1_k'], p['cmp_b1_k'])
    w1v, pv, b1v = _cmp_weights(p['cmp_pos_v'], p['cmp_w1_v'], p['cmp_b1_v'])
    kcc, vcT = _compress(kc, vc, pk, pv, w1k, w1v, b1k, b1v,
                         p['cmp_w2_k'].astype(bf16), p['cmp_w2_v'].T.astype(bf16), batch, seq)

    ncp = seq // CMP_STRIDE
    ns = seq // SEL_LEN
    cs = jnp.arange(ncp) * CMP_STRIDE
    js = jnp.arange(ns) * SEL_LEN
    overlap = jnp.clip(jnp.minimum(cs[:, None] + CMP_LEN, js[None, :] + SEL_LEN)
                       - jnp.maximum(cs[:, None], js[None, :]), 0, None).astype(f32) / CMP_LEN
    ovT = overlap.T.astype(bf16)
    eT = (jnp.arange(seq)[:, None] // SEL_LEN == jnp.arange(ns)[None, :]).astype(bf16)
    oT = _nsa(qT, ks, kw, vsT, vwT, kcc, vcT, gT, ovT, eT, batch, seq)

    w_xkv = p['w_xkv']
    kT, v = _memkv(memf, row1(p['g_mem_norm']), w_xkv[:, :D_MODEL].T.astype(bf16),
                   w_xkv[:, D_MODEL:].astype(bf16))
    w_r = jnp.pad(p['w_router'], ((0, 0), (0, LANES - N_EXPERTS))).astype(bf16)
    b_r = jnp.pad(p['b_router'], (0, LANES - N_EXPERTS), constant_values=NEG_INF).reshape(1, LANES)
    h2, xn2_rows, eidx, gate, sel = _post(
        x2, oT, conv_n, p['g_nsa_out'].reshape(NSA_WIDTH, 1), p['w_mix_out'].astype(bf16),
        row1(p['g_xattn_norm']), p['w_xq'].astype(bf16), kT, v, p['w_xo'].astype(bf16),
        row1(p['g_moe_norm']), w_r, b_r, seq)

    cum, cnt = _count(sel)
    counts = cnt[0, :N_EXPERTS].astype(i32)
    padded = (counts + BM - 1) // BM * BM
    pend = jnp.cumsum(padded)
    pstart = pend - padded
    n_rows = T * TOP_K + N_EXPERTS * BM
    n_blk = n_rows // BM
    nused = (pend[-1] // BM).astype(i32)
    blk_raw = jnp.minimum(jnp.searchsorted(pend, jnp.arange(n_blk, dtype=i32) * BM, side='right'),
                          N_EXPERTS - 1).astype(i32)
    blk_map = jnp.minimum(jnp.arange(n_blk, dtype=i32), nused - 1)
    blk_e = blk_raw[blk_map]
    first = jnp.concatenate([jnp.ones((1,), i32), (blk_e[1:] != blk_e[:-1]).astype(i32)])
    tail_start = jnp.concatenate([jnp.where(padded > 0, pend - BM, -1).astype(i32), nused.reshape(1)])
    pstart8 = jnp.broadcast_to(jnp.pad(pstart.astype(f32), (0, LANES - N_EXPERTS))[None, :], (SUBLANES, LANES))

    dest = _dest(cum, eidx, pstart8)
    dest_flat = dest[:, :TOP_K].reshape(-1)

    xg = _dispatch(dest_flat, tail_start, xn2_rows, n_rows)
    og = _ffn(blk_e, blk_map, first, nused.reshape(1), xg, p['w_gate_up'],
              p['b_gate_up'].reshape(N_EXPERTS, 1, 2 * D_FF), p['w_down'],
              p['b_down'].reshape(N_EXPERTS, 1, D_MODEL))
    out = _combine(dest_flat, og, gate, h2, row1(final_gain), final_norm)
    return out.reshape(batch, seq, D_MODEL)


_LAYER_PARAMS = ('g_mix_norm', 'w_mix_in', 'cmp_pos_k', 'cmp_pos_v', 'cmp_w1_k', 'cmp_b1_k', 'cmp_w2_k',
                 'cmp_w1_v', 'cmp_b1_v', 'cmp_w2_v', 'conv_w', 'g_nsa_out', 'g_conv_out', 'w_mix_out',
                 'g_xattn_norm', 'g_mem_norm', 'w_xq', 'w_xkv', 'w_xo', 'g_moe_norm', 'w_router', 'b_router',
                 'w_gate_up', 'b_gate_up', 'w_down', 'b_down')


def kernel(x, mem, positions, g_mix_norm, w_mix_in, cmp_pos_k, cmp_pos_v, cmp_w1_k, cmp_b1_k, cmp_w2_k, cmp_w1_v, cmp_b1_v, cmp_w2_v, conv_w, g_nsa_out, g_conv_out, w_mix_out, g_xattn_norm, g_mem_norm, w_xq, w_xkv, w_xo, g_moe_norm, w_router, b_router, w_gate_up, b_gate_up, w_down, b_down, g_final):
    stacked = dict(zip(_LAYER_PARAMS, (g_mix_norm, w_mix_in, cmp_pos_k, cmp_pos_v, cmp_w1_k, cmp_b1_k, cmp_w2_k,
                                       cmp_w1_v, cmp_b1_v, cmp_w2_v, conv_w, g_nsa_out, g_conv_out, w_mix_out,
                                       g_xattn_norm, g_mem_norm, w_xq, w_xkv, w_xo, g_moe_norm, w_router,
                                       b_router, w_gate_up, b_gate_up, w_down, b_down)))
    depth = g_mix_norm.shape[0]
    tables = _rope_tables(positions)
    h = x
    for l in range(depth):
        p = {k: v[l] for k, v in stacked.items()}
        last = l == depth - 1
        h = _layer(h, mem, tables, p, g_final, final_norm=last)
    return h
```

```python
import functools

import jax
import jax.numpy as jnp
from jax import lax
from jax.experimental import pallas as pl
from jax.experimental.pallas import tpu as pltpu

f32 = jnp.float32
bf16 = jnp.bfloat16
i32 = jnp.int32

D_MODEL = 1024
HEAD_DIM = 64
NSA_HEADS = 8
NSA_KV_HEADS = 2
HPG = NSA_HEADS // NSA_KV_HEADS
NSA_WIDTH = NSA_HEADS * HEAD_DIM
KV_WIDTH = NSA_KV_HEADS * HEAD_DIM
CONV_WIDTH = D_MODEL - NSA_WIDTH
CONV_K = 3
CMP_LEN = 32
CMP_STRIDE = 16
CMP_HIDDEN = 256
SEL_LEN = 64
N_SEL = 16
WINDOW = 512
ROPE_THETA = 10000.0
XATTN_HEADS = 4
XATTN_HEAD_DIM = D_MODEL // XATTN_HEADS
N_EXPERTS = 32
TOP_K = 4
D_FF = D_MODEL
SWIGLU_LIMIT = 7.0
SWIGLU_ALPHA = 1.702
RMS_EPS = 1e-5
NEG_INF = -1e30
FORCED = 1e30

LANES = 128
SUBLANES = 8
VMEM_LIMIT = 56 * 1024 * 1024

TM_IN = 512
TQ = 256
TK = 256
TM_POST = 256
TM_ROUTE = 512
TM_ROW = 128
BM = 256
ROW_SUB = D_MODEL // LANES


def _cparams(n_axes, **kw):
    return pltpu.CompilerParams(dimension_semantics=("arbitrary",) * n_axes,
                                vmem_limit_bytes=VMEM_LIMIT, **kw)


def _rms(t, gain):
    return t * lax.rsqrt(jnp.mean(t * t, axis=-1, keepdims=True) + RMS_EPS) * gain


def _inproj_kernel(x_ref, g_ref, wr_ref, wt_ref, cosr_ref, sinr_ref, cost_ref, sint_ref,
                   convw_ref, gconv_ref,
                   qT_ref, kc_ref, vc_ref, ks_ref, kw_ref, vsT_ref, vwT_ref, gT_ref, conv_ref,
                   ubuf, *, tiles_per_seq):
    i = pl.program_id(0)
    tm = x_ref.shape[0]
    xb = _rms(x_ref[...], g_ref[...]).astype(bf16)

    pr = jnp.dot(xb, wr_ref[:, 0:4 * KV_WIDTH], preferred_element_type=f32)
    cosr = cosr_ref[...]
    sinr = sinr_ref[...]
    lane = lax.broadcasted_iota(i32, (tm, KV_WIDTH), 1)
    first_half = (lane & (HEAD_DIM - 1)) < HEAD_DIM // 2

    def rope_rows(t):
        rot = jnp.where(first_half, pltpu.roll(t, KV_WIDTH - HEAD_DIM // 2, 1),
                        pltpu.roll(t, HEAD_DIM // 2, 1))
        return t * cosr + rot * sinr

    kc_ref[...] = rope_rows(pr[:, 0:KV_WIDTH]).astype(bf16)
    vc_ref[...] = pr[:, KV_WIDTH:2 * KV_WIDTH].astype(bf16)
    ks_ref[...] = rope_rows(pr[:, 2 * KV_WIDTH:3 * KV_WIDTH]).astype(bf16)
    kw_ref[...] = rope_rows(pr[:, 3 * KV_WIDTH:4 * KV_WIDTH]).astype(bf16)

    c0 = 4 * KV_WIDTH
    pc = jnp.dot(xb, wr_ref[:, c0:c0 + 3 * CONV_WIDTH], preferred_element_type=f32)
    ch = pc[:, 0:CONV_WIDTH]
    cb = pc[:, CONV_WIDTH:2 * CONV_WIDTH]
    cc = pc[:, 2 * CONV_WIDTH:3 * CONV_WIDTH]
    u = cc * ch

    @pl.when(i % tiles_per_seq == 0)
    def _():
        ubuf[0:SUBLANES, :] = jnp.zeros((SUBLANES, CONV_WIDTH), f32)

    @pl.when(i % tiles_per_seq != 0)
    def _():
        ubuf[0:SUBLANES, :] = ubuf[tm:tm + SUBLANES, :]

    ubuf[SUBLANES:SUBLANES + tm, :] = u
    u1 = ubuf[SUBLANES - 1:SUBLANES - 1 + tm, :]
    u2 = ubuf[SUBLANES - 2:SUBLANES - 2 + tm, :]
    w = convw_ref[...]
    y = cb * (w[0:1, :] * u2 + w[1:2, :] * u1 + w[2:3, :] * u)
    conv_ref[...] = _rms(y, gconv_ref[...]).astype(bf16)

    pt = lax.dot_general(wt_ref[...], xb, (((1,), (1,)), ((), ())), preferred_element_type=f32)
    cost = cost_ref[...]
    sint = sint_ref[...]
    half = HEAD_DIM // 2
    scale = HEAD_DIM ** -0.5
    for h in range(NSA_HEADS):
        t1 = pt[h * HEAD_DIM:h * HEAD_DIM + half, :]
        t2 = pt[h * HEAD_DIM + half:(h + 1) * HEAD_DIM, :]
        qT_ref[h * HEAD_DIM:h * HEAD_DIM + half, :] = ((t1 * cost - t2 * sint) * scale).astype(bf16)
        qT_ref[h * HEAD_DIM + half:(h + 1) * HEAD_DIM, :] = ((t2 * cost + t1 * sint) * scale).astype(bf16)
    r0 = NSA_WIDTH
    vsT_ref[...] = pt[r0:r0 + KV_WIDTH, :].astype(bf16)
    vwT_ref[...] = pt[r0 + KV_WIDTH:r0 + 2 * KV_WIDTH, :].astype(bf16)
    gT_ref[...] = jax.nn.sigmoid(pt[r0 + 2 * KV_WIDTH:r0 + 2 * KV_WIDTH + 32, :])


def _inproj(x2, g_mix, w_row, w_t, cosr, sinr, cost, sint, conv_w8, g_conv, seq):
    T = x2.shape[0]
    tm = TM_IN
    n_row = w_row.shape[1]
    n_t = w_t.shape[0]
    row = lambda i: (i, 0)
    col = lambda i: (0, i)
    const = lambda i: (0, 0)
    out_shape = (
        jax.ShapeDtypeStruct((NSA_WIDTH, T), bf16),
        jax.ShapeDtypeStruct((T, KV_WIDTH), bf16),
        jax.ShapeDtypeStruct((T, KV_WIDTH), bf16),
        jax.ShapeDtypeStruct((T, KV_WIDTH), bf16),
        jax.ShapeDtypeStruct((T, KV_WIDTH), bf16),
        jax.ShapeDtypeStruct((KV_WIDTH, T), bf16),
        jax.ShapeDtypeStruct((KV_WIDTH, T), bf16),
        jax.ShapeDtypeStruct((32, T), f32),
        jax.ShapeDtypeStruct((T, CONV_WIDTH), bf16),
    )
    out_specs = (
        pl.BlockSpec((NSA_WIDTH, tm), col),
        pl.BlockSpec((tm, KV_WIDTH), row), pl.BlockSpec((tm, KV_WIDTH), row),
        pl.BlockSpec((tm, KV_WIDTH), row), pl.BlockSpec((tm, KV_WIDTH), row),
        pl.BlockSpec((KV_WIDTH, tm), col), pl.BlockSpec((KV_WIDTH, tm), col),
        pl.BlockSpec((32, tm), col),
        pl.BlockSpec((tm, CONV_WIDTH), row),
    )
    in_specs = [
        pl.BlockSpec((tm, D_MODEL), row),
        pl.BlockSpec((1, D_MODEL), const),
        pl.BlockSpec((D_MODEL, n_row), const),
        pl.BlockSpec((n_t, D_MODEL), const),
        pl.BlockSpec((tm, KV_WIDTH), row), pl.BlockSpec((tm, KV_WIDTH), row),
        pl.BlockSpec((HEAD_DIM // 2, tm), col), pl.BlockSpec((HEAD_DIM // 2, tm), col),
        pl.BlockSpec((SUBLANES, CONV_WIDTH), const),
        pl.BlockSpec((1, CONV_WIDTH), const),
    ]
    return pl.pallas_call(
        functools.partial(_inproj_kernel, tiles_per_seq=seq // tm),
        grid=(T // tm,), in_specs=in_specs, out_specs=out_specs, out_shape=out_shape,
        scratch_shapes=[pltpu.VMEM((tm + 2 * SUBLANES, CONV_WIDTH), f32)],
        compiler_params=_cparams(1), name="inproj",
    )(x2, g_mix, w_row, w_t, cosr, sinr, cost, sint, conv_w8, g_conv)


def _compress_kernel(xk_ref, xv_ref, pk_ref, pv_ref, w1k_ref, w1v_ref, b1k_ref, b1v_ref,
                     w2k_ref, w2vT_ref, kcc_ref, vcT_ref):
    ncp = xk_ref.shape[1]

    def hidden(x_ref, p_ref, w1_ref, b1_ref):
        x = x_ref[0].astype(f32)
        lo = (x + p_ref[0:1, :]).astype(bf16)
        hi = (x + p_ref[1:2, :]).astype(bf16)
        a = jnp.dot(lo, w1_ref[0], preferred_element_type=f32)
        b = jnp.dot(hi, w1_ref[1], preferred_element_type=f32)
        pre = a + pltpu.roll(b, ncp - 1, 0) + b1_ref[...]
        return jax.nn.gelu(pre).astype(bf16)

    hk = hidden(xk_ref, pk_ref, w1k_ref, b1k_ref)
    hv = hidden(xv_ref, pv_ref, w1v_ref, b1v_ref)
    for g in range(NSA_KV_HEADS):
        sl = slice(g * CMP_HIDDEN, (g + 1) * CMP_HIDDEN)
        kcc_ref[0, g] = jnp.dot(hk[:, sl], w2k_ref[...], preferred_element_type=f32).astype(bf16)
        vcT_ref[0, g] = lax.dot_general(w2vT_ref[...], hv[:, sl], (((1,), (1,)), ((), ())),
                                        preferred_element_type=f32).astype(bf16)


def _compress(kc_rows, vc_rows, pk, pv, w1k, w1v, b1k, b1v, w2k, w2vT, batch, seq):
    ncp = seq // CMP_STRIDE
    wide = CMP_STRIDE * KV_WIDTH
    xk = kc_rows.reshape(batch, ncp, wide)
    xv = vc_rows.reshape(batch, ncp, wide)
    c2 = lambda b: (0, 0)
    c3 = lambda b: (0, 0, 0)
    in_specs = [
        pl.BlockSpec((1, ncp, wide), lambda b: (b, 0, 0)),
        pl.BlockSpec((1, ncp, wide), lambda b: (b, 0, 0)),
        pl.BlockSpec((SUBLANES, wide), c2), pl.BlockSpec((SUBLANES, wide), c2),
        pl.BlockSpec((2, wide, 2 * CMP_HIDDEN), c3), pl.BlockSpec((2, wide, 2 * CMP_HIDDEN), c3),
        pl.BlockSpec((1, 2 * CMP_HIDDEN), c2), pl.BlockSpec((1, 2 * CMP_HIDDEN), c2),
        pl.BlockSpec((CMP_HIDDEN, HEAD_DIM), c2), pl.BlockSpec((HEAD_DIM, CMP_HIDDEN), c2),
    ]
    out_shape = (jax.ShapeDtypeStruct((batch, NSA_KV_HEADS, ncp, HEAD_DIM), bf16),
                 jax.ShapeDtypeStruct((batch, NSA_KV_HEADS, HEAD_DIM, ncp), bf16))
    out_specs = (pl.BlockSpec((1, NSA_KV_HEADS, ncp, HEAD_DIM), lambda b: (b, 0, 0, 0)),
                 pl.BlockSpec((1, NSA_KV_HEADS, HEAD_DIM, ncp), lambda b: (b, 0, 0, 0)))
    return pl.pallas_call(
        _compress_kernel, grid=(batch,), in_specs=in_specs, out_specs=out_specs,
        out_shape=out_shape, compiler_params=_cparams(1), name="compress",
    )(xk, xv, pk, pv, w1k, w1v, b1k, b1v, w2k, w2vT)


def _nsa_kernel(qT_ref, ks_ref, kw_ref, vsT_ref, vwT_ref, kcc_ref, vcT_ref, gT_ref, ovT_ref, eT_ref,
                o_ref):
    g = pl.program_id(1)
    qi = pl.program_id(2)
    s0 = qi * TQ
    nq = HPG * TQ
    ncp = kcc_ref.shape[2]

    q4 = jnp.concatenate([qT_ref[h * HEAD_DIM:(h + 1) * HEAD_DIM, :] for h in range(HPG)], axis=1)
    zeros = jnp.zeros_like(q4)
    q4z = jnp.where(g == 0, jnp.concatenate([q4, zeros], axis=0), jnp.concatenate([zeros, q4], axis=0))
    s_lane = s0 + (lax.broadcasted_iota(i32, (1, nq), 1) & (TQ - 1))

    sc = jnp.dot(kcc_ref[0, 0], q4, preferred_element_type=f32)
    c_end = lax.broadcasted_iota(i32, (ncp, 1), 0) * CMP_STRIDE + (CMP_LEN - 1)
    cmask = c_end <= s_lane
    scm = jnp.where(cmask, sc, NEG_INF)
    m_c = jnp.max(scm, axis=0, keepdims=True)
    e_c = jnp.where(cmask, jnp.exp(scm - m_c), 0.0)
    l_c = jnp.sum(e_c, axis=0, keepdims=True)
    p_c = e_c * jnp.where(l_c > 0.0, 1.0 / l_c, 0.0)
    o_cmp = jnp.dot(vcT_ref[0, 0], p_c.astype(bf16), preferred_element_type=f32)

    ps = p_c[:, 0:TQ]
    for h in range(1, HPG):
        ps = ps + p_c[:, h * TQ:(h + 1) * TQ]
    p_hi = ps.astype(bf16)
    r1 = ps - p_hi.astype(f32)
    p_mid = r1.astype(bf16)
    p_lo = (r1 - p_mid.astype(f32)).astype(bf16)
    ov = ovT_ref[...]
    imp = (jnp.dot(ov, p_hi, preferred_element_type=f32) + jnp.dot(ov, p_mid, preferred_element_type=f32)
           + jnp.dot(ov, p_lo, preferred_element_type=f32))
    ns = imp.shape[0]
    j_blk = lax.broadcasted_iota(i32, (ns, 1), 0)
    cur = (s0 + lax.broadcasted_iota(i32, (1, TQ), 1)) // SEL_LEN
    forced = (j_blk == 0) | (j_blk == cur) | (j_blk == cur - 1)
    imp = jnp.where(forced, FORCED, jnp.where(j_blk > cur, NEG_INF, imp))
    rank = jnp.zeros((ns, TQ), f32)
    for i in range(ns):
        row = imp[i:i + 1, :]
        tie_before = jnp.where(j_blk > i, 1.0, 0.0)
        rank = rank + jnp.where(row > imp, 1.0, jnp.where(row == imp, tie_before, 0.0))
    sel_bias = jnp.where(rank < float(min(N_SEL, ns)), 0.0, NEG_INF).astype(bf16)

    def online_update(carry, sT, vT_ref, k0):
        m, l, acc = carry
        m_new = jnp.maximum(m, jnp.max(sT, axis=0, keepdims=True))
        alpha = jnp.exp(m - m_new)
        p = jnp.exp(sT - m_new)
        l = alpha * l + jnp.sum(p, axis=0, keepdims=True)
        acc = alpha * acc + jnp.dot(vT_ref[:, pl.ds(k0, TK)], p.astype(bf16), preferred_element_type=f32)
        return m_new, l, acc

    def key_pos(k0):
        return k0 + lax.broadcasted_iota(i32, (TK, 1), 0)

    def sel_scores(k0):
        sT = jnp.dot(ks_ref[pl.ds(k0, TK), :], q4z, preferred_element_type=f32)
        bias = jnp.dot(eT_ref[pl.ds(k0, TK), :], sel_bias, preferred_element_type=f32)
        return sT + jnp.concatenate([bias] * HPG, axis=1)

    init = (jnp.full((1, nq), NEG_INF, f32), jnp.zeros((1, nq), f32), jnp.zeros((HEAD_DIM, nq), f32))

    k_diag = pl.multiple_of(qi * TK, TK)
    causal = key_pos(k_diag) <= s_lane
    carry = online_update(init, jnp.where(causal, sel_scores(k_diag), NEG_INF), vsT_ref, k_diag)

    def sel_body(kt, carry):
        k0 = pl.multiple_of(kt * TK, TK)
        return online_update(carry, sel_scores(k0), vsT_ref, k0)

    _, l_s, acc_s = lax.fori_loop(0, qi, sel_body, carry)
    o_slc = acc_s * (1.0 / l_s)

    def win_scores(k0):
        sT = jnp.dot(kw_ref[pl.ds(k0, TK), :], q4z, preferred_element_type=f32)
        kp = key_pos(k0)
        return jnp.where((kp <= s_lane) & (s_lane - kp < WINDOW), sT, NEG_INF)

    carry = online_update(init, win_scores(k_diag), vwT_ref, k_diag)

    def win_body(kt, carry):
        k0 = pl.multiple_of(kt * TK, TK)
        return online_update(carry, win_scores(k0), vwT_ref, k0)

    _, l_w, acc_w = lax.fori_loop(jnp.maximum(qi - WINDOW // TK, 0), qi, win_body, carry)
    o_win = acc_w * (1.0 / l_w)

    gates = gT_ref[...]
    for h in range(HPG):
        sl = slice(h * TQ, (h + 1) * TQ)
        o = (gates[3 * h:3 * h + 1, :] * o_cmp[:, sl] + gates[3 * h + 1:3 * h + 2, :] * o_slc[:, sl]
             + gates[3 * h + 2:3 * h + 3, :] * o_win[:, sl])
        o_ref[h * HEAD_DIM:(h + 1) * HEAD_DIM, :] = o.astype(bf16)


def _nsa(qT, ks, kw, vsT, vwT, kcc, vcT, gT, ovT, eT, batch, seq):
    T = batch * seq
    nqt = seq // TQ
    ncp = kcc.shape[2]
    ns = seq // SEL_LEN
    gw = HPG * HEAD_DIM
    qmap = lambda b, g, q: (g, b * nqt + q)
    in_specs = [
        pl.BlockSpec((gw, TQ), qmap),
        pl.BlockSpec((seq, KV_WIDTH), lambda b, g, q: (b, 0)),
        pl.BlockSpec((seq, KV_WIDTH), lambda b, g, q: (b, 0)),
        pl.BlockSpec((HEAD_DIM, seq), lambda b, g, q: (g, b)),
        pl.BlockSpec((HEAD_DIM, seq), lambda b, g, q: (g, b)),
        pl.BlockSpec((1, 1, ncp, HEAD_DIM), lambda b, g, q: (b, g, 0, 0)),
        pl.BlockSpec((1, 1, HEAD_DIM, ncp), lambda b, g, q: (b, g, 0, 0)),
        pl.BlockSpec((16, TQ), qmap),
        pl.BlockSpec((ns, ncp), lambda b, g, q: (0, 0)),
        pl.BlockSpec((seq, ns), lambda b, g, q: (0, 0)),
    ]
    return pl.pallas_call(
        _nsa_kernel, grid=(batch, NSA_KV_HEADS, nqt), in_specs=in_specs,
        out_specs=pl.BlockSpec((gw, TQ), qmap),
        out_shape=jax.ShapeDtypeStruct((NSA_WIDTH, T), bf16),
        compiler_params=_cparams(3), name="nsa",
    )(qT, ks, kw, vsT, vwT, kcc, vcT, gT, ovT, eT)


def _memkv_kernel(mem_ref, g_ref, wkT_ref, wv_ref, kT_ref, v_ref):
    mb = _rms(mem_ref[0], g_ref[...]).astype(bf16)
    kT_ref[0] = lax.dot_general(wkT_ref[...], mb, (((1,), (1,)), ((), ())),
                                preferred_element_type=f32).astype(bf16)
    v_ref[0] = jnp.dot(mb, wv_ref[...], preferred_element_type=f32).astype(bf16)


def _memkv(mem, g_mem, wkT, wv):
    batch, n_mem, _ = mem.shape
    c2 = lambda b: (0, 0)
    return pl.pallas_call(
        _memkv_kernel, grid=(batch,),
        in_specs=[pl.BlockSpec((1, n_mem, D_MODEL), lambda b: (b, 0, 0)),
                  pl.BlockSpec((1, D_MODEL), c2),
                  pl.BlockSpec((D_MODEL, D_MODEL), c2), pl.BlockSpec((D_MODEL, D_MODEL), c2)],
        out_specs=(pl.BlockSpec((1, D_MODEL, n_mem), lambda b: (b, 0, 0)),
                   pl.BlockSpec((1, n_mem, D_MODEL), lambda b: (b, 0, 0))),
        out_shape=(jax.ShapeDtypeStruct((batch, D_MODEL, n_mem), bf16),
                   jax.ShapeDtypeStruct((batch, n_mem, D_MODEL), bf16)),
        compiler_params=_cparams(1), name="memkv",
    )(mem, g_mem, wkT, wv)


def _post_kernel(x_ref, oT_ref, conv_ref, gnsa_ref, wout_ref, gx_ref, wq_ref, kT_ref, v_ref, wo_ref,
                 gmoe_ref, wr_ref, br_ref,
                 h2_ref, xn2_ref, eidx_ref, gate_ref, sel_ref):
    tm = x_ref.shape[0]
    oT = oT_ref[...].astype(f32)
    onT = (oT * lax.rsqrt(jnp.mean(oT * oT, axis=0, keepdims=True) + RMS_EPS) * gnsa_ref[...]).astype(bf16)
    mix = lax.dot_general(onT, wout_ref[0:NSA_WIDTH, :], (((0,), (0,)), ((), ())),
                          preferred_element_type=f32)
    mix = mix + jnp.dot(conv_ref[...], wout_ref[NSA_WIDTH:D_MODEL, :], preferred_element_type=f32)
    h1 = x_ref[...] + mix

    hn = _rms(h1, gx_ref[...]).astype(bf16)
    q = (jnp.dot(hn, wq_ref[...], preferred_element_type=f32) * (XATTN_HEAD_DIM ** -0.5)).astype(bf16)
    heads = []
    for h in range(XATTN_HEADS):
        sl = slice(h * XATTN_HEAD_DIM, (h + 1) * XATTN_HEAD_DIM)
        s = jnp.dot(q[:, sl], kT_ref[0, sl, :], preferred_element_type=f32)
        e = jnp.exp(s - jnp.max(s, axis=-1, keepdims=True))
        p = e * (1.0 / jnp.sum(e, axis=-1, keepdims=True))
        heads.append(jnp.dot(p.astype(bf16), v_ref[0, :, sl], preferred_element_type=f32))
    o = jnp.concatenate(heads, axis=1).astype(bf16)
    h2 = h1 + jnp.dot(o, wo_ref[...], preferred_element_type=f32)
    h2_ref[...] = h2

    xn2 = _rms(h2, gmoe_ref[...])
    for s_ in range(ROW_SUB):
        xn2_ref[pl.ds(s_, tm, stride=ROW_SUB), :] = xn2[:, s_ * LANES:(s_ + 1) * LANES]

    logits = jnp.dot(xn2.astype(bf16), wr_ref[...], preferred_element_type=f32) + br_ref[...]
    lane = lax.broadcasted_iota(i32, (tm, LANES), 1)
    work = logits
    sel = jnp.zeros((tm, LANES), f32)
    eidx = jnp.zeros((tm, LANES), i32)
    vals = []
    for k in range(TOP_K):
        mk = jnp.max(work, axis=-1, keepdims=True)
        ik = jnp.min(jnp.where(work == mk, lane, LANES), axis=-1, keepdims=True)
        hit = lane == ik
        work = jnp.where(hit, -jnp.inf, work)
        sel = jnp.where(hit, 1.0, sel)
        eidx = jnp.where(lane == k, ik, eidx)
        vals.append(mk)
    es = [jnp.exp(v - vals[0]) for v in vals]
    den = es[0]
    for e in es[1:]:
        den = den + e
    gate = jnp.zeros((tm, LANES), f32)
    for k in range(TOP_K):
        gate = jnp.where(lane == k, es[k] / den, gate)
    eidx_ref[...] = eidx
    gate_ref[...] = gate
    sel_ref[...] = sel.astype(bf16)


def _post(x2, oT, conv_n, g_nsa_col, w_out, g_x, w_q, kT, v, w_o, g_moe, w_r, b_r, seq):
    T = x2.shape[0]
    tm = TM_POST
    tps = seq // tm
    n_mem = v.shape[1]
    row = lambda i: (i, 0)
    const = lambda i: (0, 0)
    in_specs = [
        pl.BlockSpec((tm, D_MODEL), row),
        pl.BlockSpec((NSA_WIDTH, tm), lambda i: (0, i)),
        pl.BlockSpec((tm, CONV_WIDTH), row),
        pl.BlockSpec((NSA_WIDTH, 1), const),
        pl.BlockSpec((D_MODEL, D_MODEL), const),
        pl.BlockSpec((1, D_MODEL), const),
        pl.BlockSpec((D_MODEL, D_MODEL), const),
        pl.BlockSpec((1, D_MODEL, n_mem), lambda i: (i // tps, 0, 0)),
        pl.BlockSpec((1, n_mem, D_MODEL), lambda i: (i // tps, 0, 0)),
        pl.BlockSpec((D_MODEL, D_MODEL), const),
        pl.BlockSpec((1, D_MODEL), const),
        pl.BlockSpec((D_MODEL, LANES), const),
        pl.BlockSpec((1, LANES), const),
    ]
    out_shape = (jax.ShapeDtypeStruct((T, D_MODEL), f32),
                 jax.ShapeDtypeStruct((T * ROW_SUB, LANES), f32),
                 jax.ShapeDtypeStruct((T, LANES), i32),
                 jax.ShapeDtypeStruct((T, LANES), f32),
                 jax.ShapeDtypeStruct((T, LANES), bf16))
    out_specs = (pl.BlockSpec((tm, D_MODEL), row),
                 pl.BlockSpec((tm * ROW_SUB, LANES), row),
                 pl.BlockSpec((tm, LANES), row), pl.BlockSpec((tm, LANES), row),
                 pl.BlockSpec((tm, LANES), row))
    return pl.pallas_call(
        _post_kernel, grid=(T // tm,), in_specs=in_specs, out_specs=out_specs, out_shape=out_shape,
        compiler_params=_cparams(1), name="post",
    )(x2, oT, conv_n, g_nsa_col, w_out, g_x, w_q, kT, v, w_o, g_moe, w_r, b_r)


def _count_kernel(sel_ref, cum_ref, cnt_ref, carry):
    i = pl.program_id(0)
    tm = sel_ref.shape[0]

    @pl.when(i == 0)
    def _():
        carry[...] = jnp.zeros_like(carry)

    sel = sel_ref[...]
    r = lax.broadcasted_iota(i32, (tm, tm), 0)
    c = lax.broadcasted_iota(i32, (tm, tm), 1)
    strict_lower = jnp.where(c < r, 1.0, 0.0).astype(bf16)
    base = carry[0:1, :]
    cum_ref[...] = jnp.dot(strict_lower, sel, preferred_element_type=f32) + base
    total = base + jnp.sum(sel.astype(f32), axis=0, keepdims=True)
    carry[...] = jnp.broadcast_to(total, carry.shape)
    cnt_ref[...] = jnp.broadcast_to(total, cnt_ref.shape)


def _count(sel):
    T = sel.shape[0]
    tm = TM_ROUTE
    return pl.pallas_call(
        _count_kernel, grid=(T // tm,),
        in_specs=[pl.BlockSpec((tm, LANES), lambda i: (i, 0))],
        out_specs=(pl.BlockSpec((tm, LANES), lambda i: (i, 0)),
                   pl.BlockSpec((SUBLANES, LANES), lambda i: (0, 0))),
        out_shape=(jax.ShapeDtypeStruct((T, LANES), f32), jax.ShapeDtypeStruct((SUBLANES, LANES), f32)),
        scratch_shapes=[pltpu.VMEM((SUBLANES, LANES), f32)],
        compiler_params=_cparams(1), name="route_count",
    )(sel)


def _dest_kernel(cum_ref, eidx_ref, pstart_ref, dest_ref):
    tm = cum_ref.shape[0]
    lane = lax.broadcasted_iota(i32, (tm, LANES), 1)
    row_of = cum_ref[...] + pstart_ref[0:1, :]
    eidx = eidx_ref[...]
    dest = jnp.zeros((tm, LANES), f32)
    for k in range(TOP_K):
        ek = eidx[:, k:k + 1]
        dk = jnp.sum(jnp.where(lane == ek, row_of, 0.0), axis=-1, keepdims=True)
        dest = jnp.where(lane == k, dk, dest)
    dest_ref[...] = dest.astype(i32)


def _dest(cum, eidx, pstart8):
    T = cum.shape[0]
    tm = TM_ROUTE
    row = lambda i: (i, 0)
    return pl.pallas_call(
        _dest_kernel, grid=(T // tm,),
        in_specs=[pl.BlockSpec((tm, LANES), row), pl.BlockSpec((tm, LANES), row),
                  pl.BlockSpec((SUBLANES, LANES), lambda i: (0, 0))],
        out_specs=pl.BlockSpec((tm, LANES), row),
        out_shape=jax.ShapeDtypeStruct((T, LANES), i32),
        compiler_params=_cparams(1), name="route_dest",
    )(cum, eidx, pstart8)


def _dispatch_kernel(dest_ref, tail_ref, x_ref, xg_ref, zbuf, zsem, sem):
    i = pl.program_id(0)
    tm = x_ref.shape[0] // ROW_SUB

    n_blk = xg_ref.shape[0] // (BM * ROW_SUB)

    def zero_copy(row_start):
        start = pl.multiple_of(row_start * ROW_SUB, SUBLANES)
        return pltpu.make_async_copy(zbuf, xg_ref.at[pl.ds(start, BM * ROW_SUB)], zsem)

    @pl.when(i == 0)
    def _():
        zbuf[...] = jnp.zeros_like(zbuf)
        nused = tail_ref[N_EXPERTS]
        for wait in (False, True):
            for e in range(N_EXPERTS):
                @pl.when(tail_ref[e] >= 0)
                def _():
                    cp = zero_copy(tail_ref[e])
                    cp.wait() if wait else cp.start()

                @pl.when(nused + e < n_blk)
                def _():
                    cp = zero_copy((nused + e) * BM)
                    cp.wait() if wait else cp.start()

    def row_copy(r, k):
        d = pl.multiple_of(dest_ref[(i * tm + r) * TOP_K + k] * ROW_SUB, SUBLANES)
        return pltpu.make_async_copy(x_ref.at[pl.ds(r * ROW_SUB, ROW_SUB)], xg_ref.at[pl.ds(d, ROW_SUB)], sem)

    for r in range(tm):
        for k in range(TOP_K):
            row_copy(r, k).start(priority=k % 2)
    for r in range(tm):
        for k in range(TOP_K):
            row_copy(r, k).wait()


def _dispatch(dest_flat, tail_start, xn2_rows, n_rows):
    T = xn2_rows.shape[0] // ROW_SUB
    tm = TM_ROW
    return pl.pallas_call(
        _dispatch_kernel, grid=(T // tm,),
        in_specs=[pl.BlockSpec(memory_space=pltpu.SMEM), pl.BlockSpec(memory_space=pltpu.SMEM),
                  pl.BlockSpec((tm * ROW_SUB, LANES), lambda i: (i, 0))],
        out_specs=pl.BlockSpec(memory_space=pl.ANY),
        out_shape=jax.ShapeDtypeStruct((n_rows * ROW_SUB, LANES), f32),
        scratch_shapes=[pltpu.VMEM((BM * ROW_SUB, LANES), f32), pltpu.SemaphoreType.DMA,
                        pltpu.SemaphoreType.DMA],
        compiler_params=_cparams(1), name="dispatch",
    )(dest_flat, tail_start, xn2_rows)


def _ffn_kernel(blk_e_ref, blk_map_ref, first_ref, nused_ref, xg_ref, wgu_ref, bgu_ref, wd_ref, bd_ref,
                og_ref, wgu_bf, wd_bf):
    i = pl.program_id(0)

    @pl.when(i < nused_ref[0])
    def _():
        @pl.when(first_ref[i] == 1)
        def _():
            wgu_bf[...] = wgu_ref[0].astype(bf16)
            wd_bf[...] = wd_ref[0].astype(bf16)

        x = jnp.concatenate([xg_ref[pl.ds(s_, BM, stride=ROW_SUB), :] for s_ in range(ROW_SUB)],
                            axis=1).astype(bf16)
        gu = jnp.dot(x, wgu_bf[...], preferred_element_type=f32) + bgu_ref[0]
        gg = jnp.minimum(gu[:, 0:D_FF], SWIGLU_LIMIT)
        uu = jnp.clip(gu[:, D_FF:2 * D_FF], -SWIGLU_LIMIT, SWIGLU_LIMIT)
        hmid = (uu + 1.0) * (gg * jax.nn.sigmoid(SWIGLU_ALPHA * gg))
        out = jnp.dot(hmid.astype(bf16), wd_bf[...], preferred_element_type=f32) + bd_ref[0]
        for s_ in range(ROW_SUB):
            og_ref[pl.ds(s_, BM, stride=ROW_SUB), :] = out[:, s_ * LANES:(s_ + 1) * LANES]


def _ffn(blk_e, blk_map, first, nused, xg, w_gu, b_gu, w_d, b_d):
    n_blk = blk_e.shape[0]
    xmap = lambda i, be, bm, fi, nu: (bm[i], 0)
    emap = lambda i, be, bm, fi, nu: (be[i], 0, 0)
    grid_spec = pltpu.PrefetchScalarGridSpec(
        num_scalar_prefetch=4, grid=(n_blk,),
        in_specs=[pl.BlockSpec((BM * ROW_SUB, LANES), xmap),
                  pl.BlockSpec((1, D_MODEL, 2 * D_FF), emap),
                  pl.BlockSpec((1, 1, 2 * D_FF), emap),
                  pl.BlockSpec((1, D_FF, D_MODEL), emap),
                  pl.BlockSpec((1, 1, D_MODEL), emap)],
        out_specs=pl.BlockSpec((BM * ROW_SUB, LANES), xmap),
        scratch_shapes=[pltpu.VMEM((D_MODEL, 2 * D_FF), bf16), pltpu.VMEM((D_FF, D_MODEL), bf16)],
    )
    return pl.pallas_call(
        _ffn_kernel, grid_spec=grid_spec,
        out_shape=jax.ShapeDtypeStruct(xg.shape, f32),
        input_output_aliases={4: 0},
        compiler_params=_cparams(1), name="ffn",
    )(blk_e, blk_map, first, nused, xg, w_gu, b_gu, w_d, b_d)


def _combine_kernel(dest_ref, og_ref, gate_ref, h2_ref, gfin_ref, o_ref, buf, sem, *, final_norm):
    i = pl.program_id(0)
    tm = h2_ref.shape[0]

    def row_copy(r, k):
        d = pl.multiple_of(dest_ref[(i * tm + r) * TOP_K + k] * ROW_SUB, SUBLANES)
        return pltpu.make_async_copy(og_ref.at[pl.ds(d, ROW_SUB)],
                                     buf.at[k, pl.ds(r * ROW_SUB, ROW_SUB)], sem)

    for r in range(tm):
        for k in range(TOP_K):
            row_copy(r, k).start(priority=k % 2)
    for r in range(tm):
        for k in range(TOP_K):
            row_copy(r, k).wait()

    gate = gate_ref[...]
    cols = []
    for s_ in range(ROW_SUB):
        acc = gate[:, 0:1] * buf[0, pl.ds(s_, tm, stride=ROW_SUB), :]
        for k in range(1, TOP_K):
            acc = acc + gate[:, k:k + 1] * buf[k, pl.ds(s_, tm, stride=ROW_SUB), :]
        cols.append(acc)
    h = h2_ref[...] + jnp.concatenate(cols, axis=1)
    if final_norm:
        h = _rms(h, gfin_ref[...])
    o_ref[...] = h


def _combine(dest_flat, og, gate, h2, g_final, final_norm):
    T = h2.shape[0]
    tm = TM_ROW
    row = lambda i: (i, 0)
    return pl.pallas_call(
        functools.partial(_combine_kernel, final_norm=final_norm), grid=(T // tm,),
        in_specs=[pl.BlockSpec(memory_space=pltpu.SMEM), pl.BlockSpec(memory_space=pl.ANY),
                  pl.BlockSpec((tm, LANES), row), pl.BlockSpec((tm, D_MODEL), row),
                  pl.BlockSpec((1, D_MODEL), lambda i: (0, 0))],
        out_specs=pl.BlockSpec((tm, D_MODEL), row),
        out_shape=jax.ShapeDtypeStruct((T, D_MODEL), f32),
        scratch_shapes=[pltpu.VMEM((TOP_K, tm * ROW_SUB, LANES), f32), pltpu.SemaphoreType.DMA],
        compiler_params=_cparams(1), name="combine",
    )(dest_flat, og, gate, h2, g_final)


def _prep_inproj_weights(w_in):
    sizes = (NSA_WIDTH,) + (KV_WIDTH,) * 6 + (3 * NSA_HEADS,) + (CONV_WIDTH,) * 3
    offs = [0]
    for s in sizes:
        offs.append(offs[-1] + s)
    seg = lambda n: w_in[:, offs[n]:offs[n + 1]]
    q, kc, vc, ks, vs, kw, vw, gl, ch, cb, cc = (seg(n) for n in range(11))
    w_row = jnp.concatenate([kc, vc, ks, kw, ch, cb, cc], axis=1).astype(bf16)
    gl_g = gl.reshape(D_MODEL, NSA_KV_HEADS, HPG * 3)
    gl_g = jnp.pad(gl_g, ((0, 0), (0, 0), (0, 16 - HPG * 3))).reshape(D_MODEL, NSA_KV_HEADS * 16)
    w_t = jnp.concatenate([q, vs, vw, gl_g], axis=1).T.astype(bf16)
    return w_row, w_t


def _rope_tables(positions):
    half = HEAD_DIM // 2
    inv_freq = ROPE_THETA ** (-jnp.arange(half, dtype=f32) / half)
    ang = positions.reshape(-1).astype(f32)[:, None] * inv_freq
    cos = jnp.cos(ang)
    sin = jnp.sin(ang)
    reps = KV_WIDTH // HEAD_DIM
    cosr = jnp.tile(cos, (1, 2 * reps))
    sinr = jnp.tile(jnp.concatenate([-sin, sin], axis=1), (1, reps))
    return cosr, sinr, cos.T, sin.T


def _cmp_weights(pos, w1, b1):
    w1r = w1.reshape(2, CMP_STRIDE, HEAD_DIM, CMP_HIDDEN)
    eye = jnp.eye(NSA_KV_HEADS, dtype=w1.dtype)
    big = jnp.einsum('aldh,gk->algdkh', w1r, eye)
    big = big.reshape(2, CMP_STRIDE * KV_WIDTH, NSA_KV_HEADS * CMP_HIDDEN).astype(bf16)
    p = jnp.broadcast_to(pos.reshape(2, CMP_STRIDE, 1, HEAD_DIM), (2, CMP_STRIDE, NSA_KV_HEADS, HEAD_DIM))
    p = jnp.pad(p.reshape(2, CMP_STRIDE * KV_WIDTH), ((0, SUBLANES - 2), (0, 0)))
    return big, p, jnp.tile(b1.reshape(1, CMP_HIDDEN), (1, NSA_KV_HEADS))


def _layer(h, memf, tables, p, final_gain, final_norm):
    batch, seq, _ = h.shape
    T = batch * seq
    x2 = h.reshape(T, D_MODEL)
    cosr, sinr, cost, sint = tables
    row1 = lambda v: v.reshape(1, -1)

    w_row, w_t = _prep_inproj_weights(p['w_mix_in'])
    conv_w8 = jnp.pad(p['conv_w'], ((0, SUBLANES - CONV_K), (0, 0)))
    qT, kc, vc, ks, kw, vsT, vwT, gT, conv_n = _inproj(
        x2, row1(p['g_mix_norm']), w_row, w_t, cosr, sinr, cost, sint, conv_w8, row1(p['g_conv_out']), seq)

    w1k, pk, b1k = _cmp_weights(p['cmp_pos_k'], p['cmp_w1_k'], p['cmp_b1_k'])
    w1v, pv, b1v = _cmp_weights(p['cmp_pos_v'], p['cmp_w1_v'], p['cmp_b1_v'])
    kcc, vcT = _compress(kc, vc, pk, pv, w1k, w1v, b1k, b1v,
                         p['cmp_w2_k'].astype(bf16), p['cmp_w2_v'].T.astype(bf16), batch, seq)

    ncp = seq // CMP_STRIDE
    ns = seq // SEL_LEN
    cs = jnp.arange(ncp) * CMP_STRIDE
    js = jnp.arange(ns) * SEL_LEN
    overlap = jnp.clip(jnp.minimum(cs[:, None] + CMP_LEN, js[None, :] + SEL_LEN)
                       - jnp.maximum(cs[:, None], js[None, :]), 0, None).astype(f32) / CMP_LEN
    ovT = overlap.T.astype(bf16)
    eT = (jnp.arange(seq)[:, None] // SEL_LEN == jnp.arange(ns)[None, :]).astype(bf16)
    oT = _nsa(qT, ks, kw, vsT, vwT, kcc, vcT, gT, ovT, eT, batch, seq)

    w_xkv = p['w_xkv']
    kT, v = _memkv(memf, row1(p['g_mem_norm']), w_xkv[:, :D_MODEL].T.astype(bf16),
                   w_xkv[:, D_MODEL:].astype(bf16))
    w_r = jnp.pad(p['w_router'], ((0, 0), (0, LANES - N_EXPERTS))).astype(bf16)
    b_r = jnp.pad(p['b_router'], (0, LANES - N_EXPERTS), constant_values=NEG_INF).reshape(1, LANES)
    h2, xn2_rows, eidx, gate, sel = _post(
        x2, oT, conv_n, p['g_nsa_out'].reshape(NSA_WIDTH, 1), p['w_mix_out'].astype(bf16),
        row1(p['g_xattn_norm']), p['w_xq'].astype(bf16), kT, v, p['w_xo'].astype(bf16),
        row1(p['g_moe_norm']), w_r, b_r, seq)

    cum, cnt = _count(sel)
    counts = cnt[0, :N_EXPERTS].astype(i32)
    padded = (counts + BM - 1) // BM * BM
    pend = jnp.cumsum(padded)
    pstart = pend - padded
    n_rows = T * TOP_K + N_EXPERTS * BM
    n_blk = n_rows // BM
    nused = (pend[-1] // BM).astype(i32)
    blk_start = jnp.arange(n_blk, dtype=i32) * BM
    blk_raw = jnp.minimum(jnp.sum((pend[None, :] <= blk_start[:, None]).astype(i32), axis=1), N_EXPERTS - 1)
    blk_map = jnp.minimum(jnp.arange(n_blk, dtype=i32), nused - 1)
    blk_e = blk_raw[blk_map]
    first = jnp.concatenate([jnp.ones((1,), i32), (blk_e[1:] != blk_e[:-1]).astype(i32)])
    tail_start = jnp.concatenate([jnp.where(padded > 0, pend - BM, -1).astype(i32), nused.reshape(1)])
    pstart8 = jnp.broadcast_to(jnp.pad(pstart.astype(f32), (0, LANES - N_EXPERTS))[None, :], (SUBLANES, LANES))

    dest = _dest(cum, eidx, pstart8)
    dest_flat = dest[:, :TOP_K].reshape(-1)

    xg = _dispatch(dest_flat, tail_start, xn2_rows, n_rows)
    og = _ffn(blk_e, blk_map, first, nused.reshape(1), xg, p['w_gate_up'],
              p['b_gate_up'].reshape(N_EXPERTS, 1, 2 * D_FF), p['w_down'],
              p['b_down'].reshape(N_EXPERTS, 1, D_MODEL))
    out = _combine(dest_flat, og, gate, h2, row1(final_gain), final_norm)
    return out.reshape(batch, seq, D_MODEL)


_LAYER_PARAMS = ('g_mix_norm', 'w_mix_in', 'cmp_pos_k', 'cmp_pos_v', 'cmp_w1_k', 'cmp_b1_k', 'cmp_w2_k',
                 'cmp_w1_v', 'cmp_b1_v', 'cmp_w2_v', 'conv_w', 'g_nsa_out', 'g_conv_out', 'w_mix_out',
                 'g_xattn_norm', 'g_mem_norm', 'w_xq', 'w_xkv', 'w_xo', 'g_moe_norm', 'w_router', 'b_router',
                 'w_gate_up', 'b_gate_up', 'w_down', 'b_down')


def kernel(x, mem, positions, g_mix_norm, w_mix_in, cmp_pos_k, cmp_pos_v, cmp_w1_k, cmp_b1_k, cmp_w2_k, cmp_w1_v, cmp_b1_v, cmp_w2_v, conv_w, g_nsa_out, g_conv_out, w_mix_out, g_xattn_norm, g_mem_norm, w_xq, w_xkv, w_xo, g_moe_norm, w_router, b_router, w_gate_up, b_gate_up, w_down, b_down, g_final):
    stacked = dict(zip(_LAYER_PARAMS, (g_mix_norm, w_mix_in, cmp_pos_k, cmp_pos_v, cmp_w1_k, cmp_b1_k, cmp_w2_k,
                                       cmp_w1_v, cmp_b1_v, cmp_w2_v, conv_w, g_nsa_out, g_conv_out, w_mix_out,
                                       g_xattn_norm, g_mem_norm, w_xq, w_xkv, w_xo, g_moe_norm, w_router,
                                       b_router, w_gate_up, b_gate_up, w_down, b_down)))
    depth = g_mix_norm.shape[0]
    tables = _rope_tables(positions)
    h = x
    for l in range(depth):
        p = {k: v[l] for k, v in stacked.items()}
        last = l == depth - 1
        h = _layer(h, mem, tables, p, g_final, final_norm=last)
    return h
```

```python
import functools

import jax
import jax.numpy as jnp
from jax import lax
from jax.experimental import pallas as pl
from jax.experimental.pallas import tpu as pltpu

f32 = jnp.float32
bf16 = jnp.bfloat16
i32 = jnp.int32

D_MODEL = 1024
HEAD_DIM = 64
NSA_HEADS = 8
NSA_KV_HEADS = 2
HPG = NSA_HEADS // NSA_KV_HEADS
NSA_WIDTH = NSA_HEADS * HEAD_DIM
KV_WIDTH = NSA_KV_HEADS * HEAD_DIM
CONV_WIDTH = D_MODEL - NSA_WIDTH
CONV_K = 3
CMP_LEN = 32
CMP_STRIDE = 16
CMP_HIDDEN = 256
SEL_LEN = 64
N_SEL = 16
WINDOW = 512
ROPE_THETA = 10000.0
XATTN_HEADS = 4
XATTN_HEAD_DIM = D_MODEL // XATTN_HEADS
N_EXPERTS = 32
TOP_K = 4
D_FF = D_MODEL
SWIGLU_LIMIT = 7.0
SWIGLU_ALPHA = 1.702
RMS_EPS = 1e-5
NEG_INF = -1e30
FORCED = 1e30

LANES = 128
SUBLANES = 8
VMEM_LIMIT = 56 * 1024 * 1024

TM_IN = 512
TQ = 256
TK = 256
TM_POST = 256
TM_ROUTE = 512
TM_ROW = 128
V_EXT = HEAD_DIM + 16
BM = 256
ROW_SUB = D_MODEL // LANES


def _cparams(n_axes, **kw):
    return pltpu.CompilerParams(dimension_semantics=("arbitrary",) * n_axes,
                                vmem_limit_bytes=VMEM_LIMIT, **kw)


def _rms(t, gain):
    return t * lax.rsqrt(jnp.mean(t * t, axis=-1, keepdims=True) + RMS_EPS) * gain


def _inproj_kernel(x_ref, g_ref, wr_ref, wt_ref, cosr_ref, sinr_ref, cost_ref, sint_ref,
                   convw_ref, gconv_ref,
                   qT_ref, kc_ref, vc_ref, ks_ref, kw_ref, vsT_ref, vwT_ref, gT_ref, conv_ref,
                   ubuf, *, tiles_per_seq):
    i = pl.program_id(0)
    tm = x_ref.shape[0]
    xb = _rms(x_ref[...], g_ref[...]).astype(bf16)

    pr = jnp.dot(xb, wr_ref[:, 0:4 * KV_WIDTH], preferred_element_type=f32)
    cosr = cosr_ref[...]
    sinr = sinr_ref[...]
    lane = lax.broadcasted_iota(i32, (tm, KV_WIDTH), 1)
    first_half = (lane & (HEAD_DIM - 1)) < HEAD_DIM // 2

    def rope_rows(t):
        rot = jnp.where(first_half, pltpu.roll(t, KV_WIDTH - HEAD_DIM // 2, 1),
                        pltpu.roll(t, HEAD_DIM // 2, 1))
        return t * cosr + rot * sinr

    kc_ref[...] = rope_rows(pr[:, 0:KV_WIDTH]).astype(bf16)
    vc_ref[...] = pr[:, KV_WIDTH:2 * KV_WIDTH].astype(bf16)
    ks_ref[:, 0:KV_WIDTH] = rope_rows(pr[:, 2 * KV_WIDTH:3 * KV_WIDTH]).astype(bf16)
    tok = (i % tiles_per_seq) * tm + lax.broadcasted_iota(i32, (tm, KV_WIDTH), 0)
    ks_ref[:, KV_WIDTH:2 * KV_WIDTH] = jnp.where(lane == tok // SEL_LEN, 1.0, 0.0).astype(bf16)
    kw_ref[...] = rope_rows(pr[:, 3 * KV_WIDTH:4 * KV_WIDTH]).astype(bf16)

    c0 = 4 * KV_WIDTH
    pc = jnp.dot(xb, wr_ref[:, c0:c0 + 3 * CONV_WIDTH], preferred_element_type=f32)
    ch = pc[:, 0:CONV_WIDTH]
    cb = pc[:, CONV_WIDTH:2 * CONV_WIDTH]
    cc = pc[:, 2 * CONV_WIDTH:3 * CONV_WIDTH]
    u = cc * ch

    @pl.when(i % tiles_per_seq == 0)
    def _():
        ubuf[0:SUBLANES, :] = jnp.zeros((SUBLANES, CONV_WIDTH), f32)

    @pl.when(i % tiles_per_seq != 0)
    def _():
        ubuf[0:SUBLANES, :] = ubuf[tm:tm + SUBLANES, :]

    ubuf[SUBLANES:SUBLANES + tm, :] = u
    u1 = ubuf[SUBLANES - 1:SUBLANES - 1 + tm, :]
    u2 = ubuf[SUBLANES - 2:SUBLANES - 2 + tm, :]
    w = convw_ref[...]
    y = cb * (w[0:1, :] * u2 + w[1:2, :] * u1 + w[2:3, :] * u)
    conv_ref[...] = _rms(y, gconv_ref[...]).astype(bf16)

    pt = lax.dot_general(wt_ref[...], xb, (((1,), (1,)), ((), ())), preferred_element_type=f32)
    cost = cost_ref[...]
    sint = sint_ref[...]
    half = HEAD_DIM // 2
    scale = HEAD_DIM ** -0.5
    for h in range(NSA_HEADS):
        t1 = pt[h * HEAD_DIM:h * HEAD_DIM + half, :]
        t2 = pt[h * HEAD_DIM + half:(h + 1) * HEAD_DIM, :]
        qT_ref[h * HEAD_DIM:h * HEAD_DIM + half, :] = ((t1 * cost - t2 * sint) * scale).astype(bf16)
        qT_ref[h * HEAD_DIM + half:(h + 1) * HEAD_DIM, :] = ((t2 * cost + t1 * sint) * scale).astype(bf16)
    r0 = NSA_WIDTH
    ones_rows = jnp.where(lax.broadcasted_iota(i32, (V_EXT - HEAD_DIM, tm), 0) == 0, 1.0, 0.0).astype(bf16)
    for vT_ref, base in ((vsT_ref, r0), (vwT_ref, r0 + KV_WIDTH)):
        for g in range(NSA_KV_HEADS):
            vT_ref[g * V_EXT:g * V_EXT + HEAD_DIM, :] = pt[base + g * HEAD_DIM:base + (g + 1) * HEAD_DIM, :].astype(bf16)
            vT_ref[g * V_EXT + HEAD_DIM:(g + 1) * V_EXT, :] = ones_rows
    gT_ref[...] = jax.nn.sigmoid(pt[r0 + 2 * KV_WIDTH:r0 + 2 * KV_WIDTH + 32, :])


def _inproj(x2, g_mix, w_row, w_t, cosr, sinr, cost, sint, conv_w8, g_conv, seq):
    T = x2.shape[0]
    tm = TM_IN
    n_row = w_row.shape[1]
    n_t = w_t.shape[0]
    row = lambda i: (i, 0)
    col = lambda i: (0, i)
    const = lambda i: (0, 0)
    out_shape = (
        jax.ShapeDtypeStruct((NSA_WIDTH, T), bf16),
        jax.ShapeDtypeStruct((T, KV_WIDTH), bf16),
        jax.ShapeDtypeStruct((T, KV_WIDTH), bf16),
        jax.ShapeDtypeStruct((T, 2 * KV_WIDTH), bf16),
        jax.ShapeDtypeStruct((T, KV_WIDTH), bf16),
        jax.ShapeDtypeStruct((NSA_KV_HEADS * V_EXT, T), bf16),
        jax.ShapeDtypeStruct((NSA_KV_HEADS * V_EXT, T), bf16),
        jax.ShapeDtypeStruct((32, T), f32),
        jax.ShapeDtypeStruct((T, CONV_WIDTH), bf16),
    )
    out_specs = (
        pl.BlockSpec((NSA_WIDTH, tm), col),
        pl.BlockSpec((tm, KV_WIDTH), row), pl.BlockSpec((tm, KV_WIDTH), row),
        pl.BlockSpec((tm, 2 * KV_WIDTH), row), pl.BlockSpec((tm, KV_WIDTH), row),
        pl.BlockSpec((NSA_KV_HEADS * V_EXT, tm), col), pl.BlockSpec((NSA_KV_HEADS * V_EXT, tm), col),
        pl.BlockSpec((32, tm), col),
        pl.BlockSpec((tm, CONV_WIDTH), row),
    )
    in_specs = [
        pl.BlockSpec((tm, D_MODEL), row),
        pl.BlockSpec((1, D_MODEL), const),
        pl.BlockSpec((D_MODEL, n_row), const),
        pl.BlockSpec((n_t, D_MODEL), const),
        pl.BlockSpec((tm, KV_WIDTH), row), pl.BlockSpec((tm, KV_WIDTH), row),
        pl.BlockSpec((HEAD_DIM // 2, tm), col), pl.BlockSpec((HEAD_DIM // 2, tm), col),
        pl.BlockSpec((SUBLANES, CONV_WIDTH), const),
        pl.BlockSpec((1, CONV_WIDTH), const),
    ]
    return pl.pallas_call(
        functools.partial(_inproj_kernel, tiles_per_seq=seq // tm),
        grid=(T // tm,), in_specs=in_specs, out_specs=out_specs, out_shape=out_shape,
        scratch_shapes=[pltpu.VMEM((tm + 2 * SUBLANES, CONV_WIDTH), f32)],
        compiler_params=_cparams(1), name="inproj",
    )(x2, g_mix, w_row, w_t, cosr, sinr, cost, sint, conv_w8, g_conv)


def _compress_kernel(xk_ref, xv_ref, pk_ref, pv_ref, w1k_ref, w1v_ref, b1k_ref, b1v_ref,
                     w2k_ref, w2vT_ref, kcc_ref, vcT_ref):
    ncp = xk_ref.shape[1]

    def hidden(x_ref, p_ref, w1_ref, b1_ref):
        x = x_ref[0].astype(f32)
        lo = (x + p_ref[0:1, :]).astype(bf16)
        hi = (x + p_ref[1:2, :]).astype(bf16)
        a = jnp.dot(lo, w1_ref[0], preferred_element_type=f32)
        b = jnp.dot(hi, w1_ref[1], preferred_element_type=f32)
        pre = a + pltpu.roll(b, ncp - 1, 0) + b1_ref[...]
        return jax.nn.gelu(pre).astype(bf16)

    hk = hidden(xk_ref, pk_ref, w1k_ref, b1k_ref)
    hv = hidden(xv_ref, pv_ref, w1v_ref, b1v_ref)
    for g in range(NSA_KV_HEADS):
        sl = slice(g * CMP_HIDDEN, (g + 1) * CMP_HIDDEN)
        kcc_ref[0, g] = jnp.dot(hk[:, sl], w2k_ref[...], preferred_element_type=f32).astype(bf16)
        vcT_ref[0, g] = lax.dot_general(w2vT_ref[...], hv[:, sl], (((1,), (1,)), ((), ())),
                                        preferred_element_type=f32).astype(bf16)


def _compress(kc_rows, vc_rows, pk, pv, w1k, w1v, b1k, b1v, w2k, w2vT, batch, seq):
    ncp = seq // CMP_STRIDE
    wide = CMP_STRIDE * KV_WIDTH
    xk = kc_rows.reshape(batch, ncp, wide)
    xv = vc_rows.reshape(batch, ncp, wide)
    c2 = lambda b: (0, 0)
    c3 = lambda b: (0, 0, 0)
    in_specs = [
        pl.BlockSpec((1, ncp, wide), lambda b: (b, 0, 0)),
        pl.BlockSpec((1, ncp, wide), lambda b: (b, 0, 0)),
        pl.BlockSpec((SUBLANES, wide), c2), pl.BlockSpec((SUBLANES, wide), c2),
        pl.BlockSpec((2, wide, 2 * CMP_HIDDEN), c3), pl.BlockSpec((2, wide, 2 * CMP_HIDDEN), c3),
        pl.BlockSpec((1, 2 * CMP_HIDDEN), c2), pl.BlockSpec((1, 2 * CMP_HIDDEN), c2),
        pl.BlockSpec((CMP_HIDDEN, HEAD_DIM), c2), pl.BlockSpec((HEAD_DIM, CMP_HIDDEN), c2),
    ]
    out_shape = (jax.ShapeDtypeStruct((batch, NSA_KV_HEADS, ncp, HEAD_DIM), bf16),
                 jax.ShapeDtypeStruct((batch, NSA_KV_HEADS, HEAD_DIM, ncp), bf16))
    out_specs = (pl.BlockSpec((1, NSA_KV_HEADS, ncp, HEAD_DIM), lambda b: (b, 0, 0, 0)),
                 pl.BlockSpec((1, NSA_KV_HEADS, HEAD_DIM, ncp), lambda b: (b, 0, 0, 0)))
    return pl.pallas_call(
        _compress_kernel, grid=(batch,), in_specs=in_specs, out_specs=out_specs,
        out_shape=out_shape, compiler_params=_cparams(1), name="compress",
    )(xk, xv, pk, pv, w1k, w1v, b1k, b1v, w2k, w2vT)


_NQ = HPG * TQ
_COL_BLOCKS = [slice(c * LANES, (c + 1) * LANES) for c in range(_NQ // LANES)]


def _compressed_branch(q4, kcc, vcT, ov, s0):
    ncp = kcc.shape[0]
    ns = ov.shape[0]
    s_lane = s0 + (lax.broadcasted_iota(i32, (1, _NQ), 1) & (TQ - 1))
    sc = jnp.dot(kcc, q4, preferred_element_type=f32)
    c_end = lax.broadcasted_iota(i32, (ncp, 1), 0) * CMP_STRIDE + (CMP_LEN - 1)
    blocks = []
    for cs in _COL_BLOCKS:
        cmask = c_end <= s_lane[:, cs]
        scm = jnp.where(cmask, sc[:, cs], NEG_INF)
        e_c = jnp.where(cmask, jnp.exp(scm - jnp.max(scm, axis=0, keepdims=True)), 0.0)
        l_c = jnp.sum(e_c, axis=0, keepdims=True)
        blocks.append(e_c * jnp.where(l_c > 0.0, 1.0 / l_c, 0.0))
    p_c = jnp.concatenate(blocks, axis=1)
    o_cmp = jnp.dot(vcT, p_c.astype(bf16), preferred_element_type=f32)

    ps = p_c[:, 0:TQ]
    for h in range(1, HPG):
        ps = ps + p_c[:, h * TQ:(h + 1) * TQ]
    p_hi = ps.astype(bf16)
    r1 = ps - p_hi.astype(f32)
    p_mid = r1.astype(bf16)
    p_lo = (r1 - p_mid.astype(f32)).astype(bf16)
    imp = (jnp.dot(ov, p_hi, preferred_element_type=f32) + jnp.dot(ov, p_mid, preferred_element_type=f32)
           + jnp.dot(ov, p_lo, preferred_element_type=f32))
    j_blk = lax.broadcasted_iota(i32, (ns, 1), 0)
    cur = (s0 + lax.broadcasted_iota(i32, (1, TQ), 1)) // SEL_LEN
    forced = (j_blk == 0) | (j_blk == cur) | (j_blk == cur - 1)
    imp = jnp.where(forced, FORCED, jnp.where(j_blk > cur, NEG_INF, imp))
    rank = jnp.zeros((ns, TQ), f32)
    for i in range(ns):
        row = imp[i:i + 1, :]
        tie_before = jnp.where(j_blk > i, 1.0, 0.0)
        rank = rank + jnp.where(row > imp, 1.0, jnp.where(row == imp, tie_before, 0.0))
    sel_bias = jnp.where(rank < float(min(N_SEL, ns)), 0.0, NEG_INF).astype(bf16)
    return o_cmp, sel_bias


def _attend(k_tile, q_op, vT_tile, bias, m_old, acc_old):
    sT = jnp.dot(k_tile, q_op, preferred_element_type=f32)
    p_blocks, m_blocks, a_blocks = [], [], []
    for c, cs in enumerate(_COL_BLOCKS):
        s = sT[:, cs]
        if bias is not None:
            b0 = (c % (TQ // LANES)) * LANES
            s = s + bias[:, b0:b0 + LANES]
        m_o = m_old[:, cs]
        m_n = jnp.maximum(m_o, jnp.max(s, axis=0, keepdims=True))
        p_blocks.append(jnp.exp((s - m_n).astype(bf16)))
        a_blocks.append(jnp.exp(m_o - m_n))
        m_blocks.append(m_n)
    p = jnp.concatenate(p_blocks, axis=1)
    acc = acc_old * jnp.concatenate(a_blocks, axis=1) + jnp.dot(vT_tile, p, preferred_element_type=f32)
    return jnp.concatenate(m_blocks, axis=1), acc


def _nsa_pair(pi, nqt, g, qa_ref, qb_ref, ks_ref, kw_ref, vsT_ref, vwT_ref, kcc_ref, vcT_ref, ga_ref, gb_ref,
              ovT_ref, cbias_ref, wbias_ref, o_ref):
    q_tiles = (pi, nqt - 1 - pi)
    ns = ovT_ref.shape[0]
    kcc = kcc_ref[0, 0]
    vcT = vcT_ref[0, 0]
    ov = ovT_ref[...]
    cbias = cbias_ref[...]
    n_back = WINDOW // TK
    m_init = jnp.full((1, _NQ), NEG_INF, f32)
    acc_init = jnp.zeros((V_EXT, _NQ), f32)

    def keys(ref, kt):
        return ref[kt * TK:(kt + 1) * TK, :]

    def values(ref, kt):
        return ref[:, kt * TK:(kt + 1) * TK]

    for slot, (q_ref, g_ref, qt) in enumerate(zip((qa_ref, qb_ref), (ga_ref, gb_ref), q_tiles)):
        q4 = jnp.concatenate([q_ref[h * HEAD_DIM:(h + 1) * HEAD_DIM, :] for h in range(HPG)], axis=1)
        zeros = jnp.zeros_like(q4)
        q_win = jnp.where(g == 0, jnp.concatenate([q4, zeros], axis=0), jnp.concatenate([zeros, q4], axis=0))

        m, acc = m_init, acc_init
        for j in range(min(n_back, qt) + 1):
            bias = cbias if j == 0 else (wbias_ref[...] if j == n_back else None)
            m, acc = _attend(keys(kw_ref, qt - j), q_win, values(vwT_ref, qt - j), bias, m, acc)
        o_win = acc[0:HEAD_DIM, :] * (1.0 / acc[HEAD_DIM:HEAD_DIM + 1, :])

        o_cmp, sel_bias = _compressed_branch(q4, kcc, vcT, ov, qt * TQ)

        q_sel = jnp.concatenate([q_win, jnp.concatenate([sel_bias] * HPG, axis=1),
                                 jnp.zeros((KV_WIDTH - ns, _NQ), bf16)], axis=0)
        m, acc = _attend(keys(ks_ref, qt), q_sel, values(vsT_ref, qt), cbias, m_init, acc_init)
        for kt in range(qt):
            m, acc = _attend(keys(ks_ref, kt), q_sel, values(vsT_ref, kt), None, m, acc)
        o_slc = acc[0:HEAD_DIM, :] * (1.0 / acc[HEAD_DIM:HEAD_DIM + 1, :])

        gates = g_ref[...]
        for h in range(HPG):
            sl = slice(h * TQ, (h + 1) * TQ)
            o = (gates[3 * h:3 * h + 1, :] * o_cmp[:, sl] + gates[3 * h + 1:3 * h + 2, :] * o_slc[:, sl]
                 + gates[3 * h + 2:3 * h + 3, :] * o_win[:, sl])
            o_ref[h * HEAD_DIM:(h + 1) * HEAD_DIM, slot * TQ:(slot + 1) * TQ] = o.astype(bf16)


def _nsa_kernel(*refs, nqt):
    for pi in range(nqt // 2):
        @pl.when(pl.program_id(0) == pi)
        def _():
            _nsa_pair(pi, nqt, pl.program_id(2), *refs)


def _nsa_tile_position(qt, nqt):
    return jnp.where(qt < nqt // 2, 2 * qt, 2 * (nqt - 1 - qt) + 1)


def _nsa(qT, ks, kw, vsT, vwT, kcc, vcT, gT, ovT, batch, seq):
    T = batch * seq
    nqt = seq // TQ
    ncp = kcc.shape[2]
    ns = seq // SEL_LEN
    nq = HPG * TQ
    gw = HPG * HEAD_DIM
    assert TQ == TK and WINDOW % TK == 0 and ns <= KV_WIDTH
    assert nqt % 2 == 0 and nqt // 2 >= WINDOW // TK
    kl = jnp.arange(TK)[:, None]
    ql = jnp.arange(TQ)[None, :]
    cbias = jnp.where(kl <= ql, 0.0, NEG_INF).astype(f32)
    wbias = jnp.where(kl > ql, 0.0, NEG_INF).astype(f32)
    amap = lambda p, b, g: (g, b * nqt + p)
    bmap = lambda p, b, g: (g, b * nqt + nqt - 1 - p)
    const = lambda p, b, g: (0, 0)
    in_specs = [
        pl.BlockSpec((gw, TQ), amap), pl.BlockSpec((gw, TQ), bmap),
        pl.BlockSpec((seq, 2 * KV_WIDTH), lambda p, b, g: (b, 0)),
        pl.BlockSpec((seq, KV_WIDTH), lambda p, b, g: (b, 0)),
        pl.BlockSpec((V_EXT, seq), lambda p, b, g: (g, b)),
        pl.BlockSpec((V_EXT, seq), lambda p, b, g: (g, b)),
        pl.BlockSpec((1, 1, ncp, HEAD_DIM), lambda p, b, g: (b, g, 0, 0)),
        pl.BlockSpec((1, 1, HEAD_DIM, ncp), lambda p, b, g: (b, g, 0, 0)),
        pl.BlockSpec((16, TQ), amap), pl.BlockSpec((16, TQ), bmap),
        pl.BlockSpec((ns, ncp), const),
        pl.BlockSpec((TK, TQ), const),
        pl.BlockSpec((TK, TQ), const),
    ]
    return pl.pallas_call(
        functools.partial(_nsa_kernel, nqt=nqt), grid=(nqt // 2, batch, NSA_KV_HEADS), in_specs=in_specs,
        out_specs=pl.BlockSpec((gw, 2 * TQ), lambda p, b, g: (g, b * (nqt // 2) + p)),
        out_shape=jax.ShapeDtypeStruct((NSA_WIDTH, T), bf16),
        compiler_params=_cparams(3), name="nsa",
    )(qT, qT, ks, kw, vsT, vwT, kcc, vcT, gT, gT, ovT, cbias, wbias)


def _memkv_kernel(mem_ref, g_ref, wkT_ref, wv_ref, kT_ref, v_ref):
    mb = _rms(mem_ref[0], g_ref[...]).astype(bf16)
    kT_ref[0] = lax.dot_general(wkT_ref[...], mb, (((1,), (1,)), ((), ())),
                                preferred_element_type=f32).astype(bf16)
    v_ref[0] = jnp.dot(mb, wv_ref[...], preferred_element_type=f32).astype(bf16)


def _memkv(mem, g_mem, wkT, wv):
    batch, n_mem, _ = mem.shape
    c2 = lambda b: (0, 0)
    return pl.pallas_call(
        _memkv_kernel, grid=(batch,),
        in_specs=[pl.BlockSpec((1, n_mem, D_MODEL), lambda b: (b, 0, 0)),
                  pl.BlockSpec((1, D_MODEL), c2),
                  pl.BlockSpec((D_MODEL, D_MODEL), c2), pl.BlockSpec((D_MODEL, D_MODEL), c2)],
        out_specs=(pl.BlockSpec((1, D_MODEL, n_mem), lambda b: (b, 0, 0)),
                   pl.BlockSpec((1, n_mem, D_MODEL), lambda b: (b, 0, 0))),
        out_shape=(jax.ShapeDtypeStruct((batch, D_MODEL, n_mem), bf16),
                   jax.ShapeDtypeStruct((batch, n_mem, D_MODEL), bf16)),
        compiler_params=_cparams(1), name="memkv",
    )(mem, g_mem, wkT, wv)


def _post_kernel(x_ref, oT_ref, conv_ref, gnsa_ref, wout_ref, gx_ref, wq_ref, kT_ref, v_ref, wo_ref,
                 gmoe_ref, wr_ref, br_ref,
                 h2_ref, xn2_ref, eidx_ref, gate_ref, sel_ref):
    tm = x_ref.shape[0]
    oT = oT_ref[...].astype(f32)
    onT = (oT * lax.rsqrt(jnp.mean(oT * oT, axis=0, keepdims=True) + RMS_EPS) * gnsa_ref[...]).astype(bf16)
    mix = lax.dot_general(onT, wout_ref[0:NSA_WIDTH, :], (((0,), (0,)), ((), ())),
                          preferred_element_type=f32)
    mix = mix + jnp.dot(conv_ref[...], wout_ref[NSA_WIDTH:D_MODEL, :], preferred_element_type=f32)
    h1 = x_ref[...] + mix

    hn = _rms(h1, gx_ref[...]).astype(bf16)
    q = (jnp.dot(hn, wq_ref[...], preferred_element_type=f32) * (XATTN_HEAD_DIM ** -0.5)).astype(bf16)
    heads = []
    for h in range(XATTN_HEADS):
        sl = slice(h * XATTN_HEAD_DIM, (h + 1) * XATTN_HEAD_DIM)
        s = jnp.dot(q[:, sl], kT_ref[0, sl, :], preferred_element_type=f32)
        e = jnp.exp(s - jnp.max(s, axis=-1, keepdims=True))
        p = e * (1.0 / jnp.sum(e, axis=-1, keepdims=True))
        heads.append(jnp.dot(p.astype(bf16), v_ref[0, :, sl], preferred_element_type=f32))
    o = jnp.concatenate(heads, axis=1).astype(bf16)
    h2 = h1 + jnp.dot(o, wo_ref[...], preferred_element_type=f32)
    h2_ref[...] = h2

    xn2 = _rms(h2, gmoe_ref[...])
    for s_ in range(ROW_SUB):
        xn2_ref[pl.ds(s_, tm, stride=ROW_SUB), :] = xn2[:, s_ * LANES:(s_ + 1) * LANES]

    logits = jnp.dot(xn2.astype(bf16), wr_ref[...], preferred_element_type=f32) + br_ref[...]
    lane = lax.broadcasted_iota(i32, (tm, LANES), 1)
    work = logits
    sel = jnp.zeros((tm, LANES), f32)
    eidx = jnp.zeros((tm, LANES), i32)
    vals = []
    for k in range(TOP_K):
        mk = jnp.max(work, axis=-1, keepdims=True)
        ik = jnp.min(jnp.where(work == mk, lane, LANES), axis=-1, keepdims=True)
        hit = lane == ik
        work = jnp.where(hit, -jnp.inf, work)
        sel = jnp.where(hit, 1.0, sel)
        eidx = jnp.where(lane == k, ik, eidx)
        vals.append(mk)
    es = [jnp.exp(v - vals[0]) for v in vals]
    den = es[0]
    for e in es[1:]:
        den = den + e
    gate = jnp.zeros((tm, LANES), f32)
    for k in range(TOP_K):
        gate = jnp.where(lane == k, es[k] / den, gate)
    eidx_ref[...] = eidx
    gate_ref[...] = gate
    sel_ref[...] = sel.astype(bf16)


def _post(x2, oT, conv_n, g_nsa_col, w_out, g_x, w_q, kT, v, w_o, g_moe, w_r, b_r, seq):
    T = x2.shape[0]
    tm = TM_POST
    assert tm == TQ
    tps = seq // tm
    n_mem = v.shape[1]
    row = lambda i: (i, 0)
    const = lambda i: (0, 0)
    in_specs = [
        pl.BlockSpec((tm, D_MODEL), row),
        pl.BlockSpec((NSA_WIDTH, tm), lambda i: (0, (i // tps) * tps + _nsa_tile_position(i % tps, tps))),
        pl.BlockSpec((tm, CONV_WIDTH), row),
        pl.BlockSpec((NSA_WIDTH, 1), const),
        pl.BlockSpec((D_MODEL, D_MODEL), const),
        pl.BlockSpec((1, D_MODEL), const),
        pl.BlockSpec((D_MODEL, D_MODEL), const),
        pl.BlockSpec((1, D_MODEL, n_mem), lambda i: (i // tps, 0, 0)),
        pl.BlockSpec((1, n_mem, D_MODEL), lambda i: (i // tps, 0, 0)),
        pl.BlockSpec((D_MODEL, D_MODEL), const),
        pl.BlockSpec((1, D_MODEL), const),
        pl.BlockSpec((D_MODEL, LANES), const),
        pl.BlockSpec((1, LANES), const),
    ]
    out_shape = (jax.ShapeDtypeStruct((T, D_MODEL), f32),
                 jax.ShapeDtypeStruct((T * ROW_SUB, LANES), f32),
                 jax.ShapeDtypeStruct((T, LANES), i32),
                 jax.ShapeDtypeStruct((T, LANES), f32),
                 jax.ShapeDtypeStruct((T, LANES), bf16))
    out_specs = (pl.BlockSpec((tm, D_MODEL), row),
                 pl.BlockSpec((tm * ROW_SUB, LANES), row),
                 pl.BlockSpec((tm, LANES), row), pl.BlockSpec((tm, LANES), row),
                 pl.BlockSpec((tm, LANES), row))
    return pl.pallas_call(
        _post_kernel, grid=(T // tm,), in_specs=in_specs, out_specs=out_specs, out_shape=out_shape,
        compiler_params=_cparams(1), name="post",
    )(x2, oT, conv_n, g_nsa_col, w_out, g_x, w_q, kT, v, w_o, g_moe, w_r, b_r)


def _count_kernel(sel_ref, cum_ref, cnt_ref, carry):
    i = pl.program_id(0)
    tm = sel_ref.shape[0]

    @pl.when(i == 0)
    def _():
        carry[...] = jnp.zeros_like(carry)

    sel = sel_ref[...]
    r = lax.broadcasted_iota(i32, (tm, tm), 0)
    c = lax.broadcasted_iota(i32, (tm, tm), 1)
    strict_lower = jnp.where(c < r, 1.0, 0.0).astype(bf16)
    base = carry[0:1, :]
    cum_ref[...] = jnp.dot(strict_lower, sel, preferred_element_type=f32) + base
    total = base + jnp.sum(sel.astype(f32), axis=0, keepdims=True)
    carry[...] = jnp.broadcast_to(total, carry.shape)
    cnt_ref[...] = jnp.broadcast_to(total, cnt_ref.shape)


def _count(sel):
    T = sel.shape[0]
    tm = TM_ROUTE
    return pl.pallas_call(
        _count_kernel, grid=(T // tm,),
        in_specs=[pl.BlockSpec((tm, LANES), lambda i: (i, 0))],
        out_specs=(pl.BlockSpec((tm, LANES), lambda i: (i, 0)),
                   pl.BlockSpec((SUBLANES, LANES), lambda i: (0, 0))),
        out_shape=(jax.ShapeDtypeStruct((T, LANES), f32), jax.ShapeDtypeStruct((SUBLANES, LANES), f32)),
        scratch_shapes=[pltpu.VMEM((SUBLANES, LANES), f32)],
        compiler_params=_cparams(1), name="route_count",
    )(sel)


def _dest_kernel(cum_ref, eidx_ref, pstart_ref, dest_ref):
    tm = cum_ref.shape[0]
    lane = lax.broadcasted_iota(i32, (tm, LANES), 1)
    row_of = cum_ref[...] + pstart_ref[0:1, :]
    eidx = eidx_ref[...]
    dest = jnp.zeros((tm, LANES), f32)
    for k in range(TOP_K):
        ek = eidx[:, k:k + 1]
        dk = jnp.sum(jnp.where(lane == ek, row_of, 0.0), axis=-1, keepdims=True)
        dest = jnp.where(lane == k, dk, dest)
    dest_ref[...] = dest.astype(i32)


def _dest(cum, eidx, pstart8):
    T = cum.shape[0]
    tm = TM_ROUTE
    row = lambda i: (i, 0)
    return pl.pallas_call(
        _dest_kernel, grid=(T // tm,),
        in_specs=[pl.BlockSpec((tm, LANES), row), pl.BlockSpec((tm, LANES), row),
                  pl.BlockSpec((SUBLANES, LANES), lambda i: (0, 0))],
        out_specs=pl.BlockSpec((tm, LANES), row),
        out_shape=jax.ShapeDtypeStruct((T, LANES), i32),
        compiler_params=_cparams(1), name="route_dest",
    )(cum, eidx, pstart8)


def _dispatch_kernel(dest_ref, tail_ref, x_ref, xg_ref, zbuf, zsem, sem):
    i = pl.program_id(0)
    tm = x_ref.shape[0] // ROW_SUB

    n_blk = xg_ref.shape[0] // (BM * ROW_SUB)

    def zero_copy(row_start):
        start = pl.multiple_of(row_start * ROW_SUB, SUBLANES)
        return pltpu.make_async_copy(zbuf, xg_ref.at[pl.ds(start, BM * ROW_SUB)], zsem)

    @pl.when(i == 0)
    def _():
        zbuf[...] = jnp.zeros_like(zbuf)
        nused = tail_ref[N_EXPERTS]
        for wait in (False, True):
            for e in range(N_EXPERTS):
                @pl.when(tail_ref[e] >= 0)
                def _():
                    cp = zero_copy(tail_ref[e])
                    cp.wait() if wait else cp.start()

                @pl.when(nused + e < n_blk)
                def _():
                    cp = zero_copy((nused + e) * BM)
                    cp.wait() if wait else cp.start()

    def row_copy(r, k):
        d = pl.multiple_of(dest_ref[(i * tm + r) * TOP_K + k] * ROW_SUB, SUBLANES)
        return pltpu.make_async_copy(x_ref.at[pl.ds(r * ROW_SUB, ROW_SUB)], xg_ref.at[pl.ds(d, ROW_SUB)], sem)

    for r in range(tm):
        for k in range(TOP_K):
            row_copy(r, k).start(priority=k % 2)
    for r in range(tm):
        for k in range(TOP_K):
            row_copy(r, k).wait()


def _dispatch(dest_flat, tail_start, xn2_rows, n_rows):
    T = xn2_rows.shape[0] // ROW_SUB
    tm = TM_ROW
    return pl.pallas_call(
        _dispatch_kernel, grid=(T // tm,),
        in_specs=[pl.BlockSpec(memory_space=pltpu.SMEM), pl.BlockSpec(memory_space=pltpu.SMEM),
                  pl.BlockSpec((tm * ROW_SUB, LANES), lambda i: (i, 0))],
        out_specs=pl.BlockSpec(memory_space=pl.ANY),
        out_shape=jax.ShapeDtypeStruct((n_rows * ROW_SUB, LANES), f32),
        scratch_shapes=[pltpu.VMEM((BM * ROW_SUB, LANES), f32), pltpu.SemaphoreType.DMA,
                        pltpu.SemaphoreType.DMA],
        compiler_params=_cparams(1), name="dispatch",
    )(dest_flat, tail_start, xn2_rows)


def _ffn_kernel(blk_e_ref, blk_map_ref, first_ref, nused_ref, xg_ref, wgu_ref, bgu_ref, wd_ref, bd_ref,
                og_ref, wgu_bf, wd_bf):
    i = pl.program_id(0)

    @pl.when(i < nused_ref[0])
    def _():
        @pl.when(first_ref[i] == 1)
        def _():
            wgu_bf[...] = wgu_ref[0].astype(bf16)
            wd_bf[...] = wd_ref[0].astype(bf16)

        x = jnp.concatenate([xg_ref[pl.ds(s_, BM, stride=ROW_SUB), :] for s_ in range(ROW_SUB)],
                            axis=1).astype(bf16)
        gu = jnp.dot(x, wgu_bf[...], preferred_element_type=f32) + bgu_ref[0]
        gg = jnp.minimum(gu[:, 0:D_FF], SWIGLU_LIMIT)
        uu = jnp.clip(gu[:, D_FF:2 * D_FF], -SWIGLU_LIMIT, SWIGLU_LIMIT)
        hmid = (uu + 1.0) * (gg * jax.nn.sigmoid(SWIGLU_ALPHA * gg))
        out = jnp.dot(hmid.astype(bf16), wd_bf[...], preferred_element_type=f32) + bd_ref[0]
        for s_ in range(ROW_SUB):
            og_ref[pl.ds(s_, BM, stride=ROW_SUB), :] = out[:, s_ * LANES:(s_ + 1) * LANES]


def _ffn(blk_e, blk_map, first, nused, xg, w_gu, b_gu, w_d, b_d):
    n_blk = blk_e.shape[0]
    xmap = lambda i, be, bm, fi, nu: (bm[i], 0)
    emap = lambda i, be, bm, fi, nu: (be[i], 0, 0)
    grid_spec = pltpu.PrefetchScalarGridSpec(
        num_scalar_prefetch=4, grid=(n_blk,),
        in_specs=[pl.BlockSpec((BM * ROW_SUB, LANES), xmap),
                  pl.BlockSpec((1, D_MODEL, 2 * D_FF), emap),
                  pl.BlockSpec((1, 1, 2 * D_FF), emap),
                  pl.BlockSpec((1, D_FF, D_MODEL), emap),
                  pl.BlockSpec((1, 1, D_MODEL), emap)],
        out_specs=pl.BlockSpec((BM * ROW_SUB, LANES), xmap),
        scratch_shapes=[pltpu.VMEM((D_MODEL, 2 * D_FF), bf16), pltpu.VMEM((D_FF, D_MODEL), bf16)],
    )
    return pl.pallas_call(
        _ffn_kernel, grid_spec=grid_spec,
        out_shape=jax.ShapeDtypeStruct(xg.shape, f32),
        input_output_aliases={4: 0},
        compiler_params=_cparams(1), name="ffn",
    )(blk_e, blk_map, first, nused, xg, w_gu, b_gu, w_d, b_d)


def _combine_kernel(dest_ref, og_ref, gate_ref, h2_ref, gfin_ref, o_ref, buf, sem, *, final_norm):
    i = pl.program_id(0)
    tm = h2_ref.shape[0]

    def row_copy(r, k):
        d = pl.multiple_of(dest_ref[(i * tm + r) * TOP_K + k] * ROW_SUB, SUBLANES)
        return pltpu.make_async_copy(og_ref.at[pl.ds(d, ROW_SUB)],
                                     buf.at[k, pl.ds(r * ROW_SUB, ROW_SUB)], sem)

    for r in range(tm):
        for k in range(TOP_K):
            row_copy(r, k).start(priority=k % 2)
    for r in range(tm):
        for k in range(TOP_K):
            row_copy(r, k).wait()

    gate = gate_ref[...]
    cols = []
    for s_ in range(ROW_SUB):
        acc = gate[:, 0:1] * buf[0, pl.ds(s_, tm, stride=ROW_SUB), :]
        for k in range(1, TOP_K):
            acc = acc + gate[:, k:k + 1] * buf[k, pl.ds(s_, tm, stride=ROW_SUB), :]
        cols.append(acc)
    h = h2_ref[...] + jnp.concatenate(cols, axis=1)
    if final_norm:
        h = _rms(h, gfin_ref[...])
    o_ref[...] = h


def _combine(dest_flat, og, gate, h2, g_final, final_norm):
    T = h2.shape[0]
    tm = TM_ROW
    row = lambda i: (i, 0)
    return pl.pallas_call(
        functools.partial(_combine_kernel, final_norm=final_norm), grid=(T // tm,),
        in_specs=[pl.BlockSpec(memory_space=pltpu.SMEM), pl.BlockSpec(memory_space=pl.ANY),
                  pl.BlockSpec((tm, LANES), row), pl.BlockSpec((tm, D_MODEL), row),
                  pl.BlockSpec((1, D_MODEL), lambda i: (0, 0))],
        out_specs=pl.BlockSpec((tm, D_MODEL), row),
        out_shape=jax.ShapeDtypeStruct((T, D_MODEL), f32),
        scratch_shapes=[pltpu.VMEM((TOP_K, tm * ROW_SUB, LANES), f32), pltpu.SemaphoreType.DMA],
        compiler_params=_cparams(1), name="combine",
    )(dest_flat, og, gate, h2, g_final)


def _prep_inproj_weights(w_in):
    sizes = (NSA_WIDTH,) + (KV_WIDTH,) * 6 + (3 * NSA_HEADS,) + (CONV_WIDTH,) * 3
    offs = [0]
    for s in sizes:
        offs.append(offs[-1] + s)
    seg = lambda n: w_in[:, offs[n]:offs[n + 1]]
    q, kc, vc, ks, vs, kw, vw, gl, ch, cb, cc = (seg(n) for n in range(11))
    w_row = jnp.concatenate([kc, vc, ks, kw, ch, cb, cc], axis=1).astype(bf16)
    gl_g = gl.reshape(D_MODEL, NSA_KV_HEADS, HPG * 3)
    gl_g = jnp.pad(gl_g, ((0, 0), (0, 0), (0, 16 - HPG * 3))).reshape(D_MODEL, NSA_KV_HEADS * 16)
    w_t = jnp.concatenate([q, vs, vw, gl_g], axis=1).T.astype(bf16)
    return w_row, w_t


def _rope_tables(positions):
    half = HEAD_DIM // 2
    inv_freq = ROPE_THETA ** (-jnp.arange(half, dtype=f32) / half)
    ang = positions.reshape(-1).astype(f32)[:, None] * inv_freq
    cos = jnp.cos(ang)
    sin = jnp.sin(ang)
    reps = KV_WIDTH // HEAD_DIM
    cosr = jnp.tile(cos, (1, 2 * reps))
    sinr = jnp.tile(jnp.concatenate([-sin, sin], axis=1), (1, reps))
    return cosr, sinr, cos.T, sin.T


def _cmp_weights(pos, w1, b1):
    w1r = w1.reshape(2, CMP_STRIDE, HEAD_DIM, CMP_HIDDEN)
    eye = jnp.eye(NSA_KV_HEADS, dtype=w1.dtype)
    big = jnp.einsum('aldh,gk->algdkh', w1r, eye)
    big = big.reshape(2, CMP_STRIDE * KV_WIDTH, NSA_KV_HEADS * CMP_HIDDEN).astype(bf16)
    p = jnp.broadcast_to(pos.reshape(2, CMP_STRIDE, 1, HEAD_DIM), (2, CMP_STRIDE, NSA_KV_HEADS, HEAD_DIM))
    p = jnp.pad(p.reshape(2, CMP_STRIDE * KV_WIDTH), ((0, SUBLANES - 2), (0, 0)))
    return big, p, jnp.tile(b1.reshape(1, CMP_HIDDEN), (1, NSA_KV_HEADS))


def _mixer_core(x2, tables, p, batch, seq):
    cosr, sinr, cost, sint = tables
    row1 = lambda v: v.reshape(1, -1)
    w_row, w_t = _prep_inproj_weights(p['w_mix_in'])
    conv_w8 = jnp.pad(p['conv_w'], ((0, SUBLANES - CONV_K), (0, 0)))
    qT, kc, vc, ks, kw, vsT, vwT, gT, conv_n = _inproj(
        x2, row1(p['g_mix_norm']), w_row, w_t, cosr, sinr, cost, sint, conv_w8, row1(p['g_conv_out']), seq)

    w1k, pk, b1k = _cmp_weights(p['cmp_pos_k'], p['cmp_w1_k'], p['cmp_b1_k'])
    w1v, pv, b1v = _cmp_weights(p['cmp_pos_v'], p['cmp_w1_v'], p['cmp_b1_v'])
    kcc, vcT = _compress(kc, vc, pk, pv, w1k, w1v, b1k, b1v,
                         p['cmp_w2_k'].astype(bf16), p['cmp_w2_v'].T.astype(bf16), batch, seq)

    ncp = seq // CMP_STRIDE
    ns = seq // SEL_LEN
    cs = jnp.arange(ncp) * CMP_STRIDE
    js = jnp.arange(ns) * SEL_LEN
    overlap = jnp.clip(jnp.minimum(cs[:, None] + CMP_LEN, js[None, :] + SEL_LEN)
                       - jnp.maximum(cs[:, None], js[None, :]), 0, None).astype(f32) / CMP_LEN
    ovT = overlap.T.astype(bf16)
    oT = _nsa(qT, ks, kw, vsT, vwT, kcc, vcT, gT, ovT, batch, seq)
    return oT, conv_n


def _layer(h, memf, tables, p, final_gain, final_norm):
    batch, seq, _ = h.shape
    T = batch * seq
    x2 = h.reshape(T, D_MODEL)
    row1 = lambda v: v.reshape(1, -1)
    oT, conv_n = _mixer_core(x2, tables, p, batch, seq)

    w_xkv = p['w_xkv']
    kT, v = _memkv(memf, row1(p['g_mem_norm']), w_xkv[:, :D_MODEL].T.astype(bf16),
                   w_xkv[:, D_MODEL:].astype(bf16))
    w_r = jnp.pad(p['w_router'], ((0, 0), (0, LANES - N_EXPERTS))).astype(bf16)
    b_r = jnp.pad(p['b_router'], (0, LANES - N_EXPERTS), constant_values=NEG_INF).reshape(1, LANES)
    h2, xn2_rows, eidx, gate, sel = _post(
        x2, oT, conv_n, p['g_nsa_out'].reshape(NSA_WIDTH, 1), p['w_mix_out'].astype(bf16),
        row1(p['g_xattn_norm']), p['w_xq'].astype(bf16), kT, v, p['w_xo'].astype(bf16),
        row1(p['g_moe_norm']), w_r, b_r, seq)

    cum, cnt = _count(sel)
    counts = cnt[0, :N_EXPERTS].astype(i32)
    padded = (counts + BM - 1) // BM * BM
    pend = jnp.cumsum(padded)
    pstart = pend - padded
    n_rows = T * TOP_K + N_EXPERTS * BM
    n_blk = n_rows // BM
    nused = (pend[-1] // BM).astype(i32)
    blk_start = jnp.arange(n_blk, dtype=i32) * BM
    blk_raw = jnp.minimum(jnp.sum((pend[None, :] <= blk_start[:, None]).astype(i32), axis=1), N_EXPERTS - 1)
    blk_map = jnp.minimum(jnp.arange(n_blk, dtype=i32), nused - 1)
    blk_e = blk_raw[blk_map]
    first = jnp.concatenate([jnp.ones((1,), i32), (blk_e[1:] != blk_e[:-1]).astype(i32)])
    tail_start = jnp.concatenate([jnp.where(padded > 0, pend - BM, -1).astype(i32), nused.reshape(1)])
    pstart8 = jnp.broadcast_to(jnp.pad(pstart.astype(f32), (0, LANES - N_EXPERTS))[None, :], (SUBLANES, LANES))

    dest = _dest(cum, eidx, pstart8)
    dest_flat = dest[:, :TOP_K].reshape(-1)

    xg = _dispatch(dest_flat, tail_start, xn2_rows, n_rows)
    og = _ffn(blk_e, blk_map, first, nused.reshape(1), xg, p['w_gate_up'],
              p['b_gate_up'].reshape(N_EXPERTS, 1, 2 * D_FF), p['w_down'],
              p['b_down'].reshape(N_EXPERTS, 1, D_MODEL))
    out = _combine(dest_flat, og, gate, h2, row1(final_gain), final_norm)
    return out.reshape(batch, seq, D_MODEL)


_LAYER_PARAMS = ('g_mix_norm', 'w_mix_in', 'cmp_pos_k', 'cmp_pos_v', 'cmp_w1_k', 'cmp_b1_k', 'cmp_w2_k',
                 'cmp_w1_v', 'cmp_b1_v', 'cmp_w2_v', 'conv_w', 'g_nsa_out', 'g_conv_out', 'w_mix_out',
                 'g_xattn_norm', 'g_mem_norm', 'w_xq', 'w_xkv', 'w_xo', 'g_moe_norm', 'w_router', 'b_router',
                 'w_gate_up', 'b_gate_up', 'w_down', 'b_down')


def kernel(x, mem, positions, g_mix_norm, w_mix_in, cmp_pos_k, cmp_pos_v, cmp_w1_k, cmp_b1_k, cmp_w2_k, cmp_w1_v, cmp_b1_v, cmp_w2_v, conv_w, g_nsa_out, g_conv_out, w_mix_out, g_xattn_norm, g_mem_norm, w_xq, w_xkv, w_xo, g_moe_norm, w_router, b_router, w_gate_up, b_gate_up, w_down, b_down, g_final):
    stacked = dict(zip(_LAYER_PARAMS, (g_mix_norm, w_mix_in, cmp_pos_k, cmp_pos_v, cmp_w1_k, cmp_b1_k, cmp_w2_k,
                                       cmp_w1_v, cmp_b1_v, cmp_w2_v, conv_w, g_nsa_out, g_conv_out, w_mix_out,
                                       g_xattn_norm, g_mem_norm, w_xq, w_xkv, w_xo, g_moe_norm, w_router,
                                       b_router, w_gate_up, b_gate_up, w_down, b_down)))
    depth = g_mix_norm.shape[0]
    tables = _rope_tables(positions)
    h = x
    for l in range(depth):
        p = {k: v[l] for k, v in stacked.items()}
        last = l == depth - 1
        h = _layer(h, mem, tables, p, g_final, final_norm=last)
    return h
```

```python
import functools

import jax
import jax.numpy as jnp
from jax import lax
from jax.experimental import pallas as pl
from jax.experimental.pallas import tpu as pltpu

f32 = jnp.float32
bf16 = jnp.bfloat16
i32 = jnp.int32

D_MODEL = 1024
HEAD_DIM = 64
NSA_HEADS = 8
NSA_KV_HEADS = 2
HPG = NSA_HEADS // NSA_KV_HEADS
NSA_WIDTH = NSA_HEADS * HEAD_DIM
KV_WIDTH = NSA_KV_HEADS * HEAD_DIM
CONV_WIDTH = D_MODEL - NSA_WIDTH
CONV_K = 3
CMP_LEN = 32
CMP_STRIDE = 16
CMP_HIDDEN = 256
SEL_LEN = 64
N_SEL = 16
WINDOW = 512
ROPE_THETA = 10000.0
XATTN_HEADS = 4
XATTN_HEAD_DIM = D_MODEL // XATTN_HEADS
N_EXPERTS = 32
TOP_K = 4
D_FF = D_MODEL
SWIGLU_LIMIT = 7.0
SWIGLU_ALPHA = 1.702
RMS_EPS = 1e-5
NEG_INF = -1e30
FORCED = 1e30

LANES = 128
SUBLANES = 8
VMEM_LIMIT = 56 * 1024 * 1024

TM_IN = 512
TQ = 256
TK = 256
TM_POST = 256
TM_ROUTE = 512
TM_ROW = 128
V_EXT = HEAD_DIM + 16
BM = 256
ROW_SUB = D_MODEL // LANES


def _cparams(n_axes, **kw):
    return pltpu.CompilerParams(dimension_semantics=("arbitrary",) * n_axes,
                                vmem_limit_bytes=VMEM_LIMIT, **kw)


def _rms(t, gain):
    return t * lax.rsqrt(jnp.mean(t * t, axis=-1, keepdims=True) + RMS_EPS) * gain


def _inproj_kernel(x_ref, g_ref, wr_ref, wt_ref, cosr_ref, sinr_ref, cost_ref, sint_ref,
                   convw_ref, gconv_ref,
                   qT_ref, kc_ref, vc_ref, ks_ref, kw_ref, vsT_ref, vwT_ref, gT_ref, conv_ref,
                   ubuf, *, tiles_per_seq):
    i = pl.program_id(0)
    tm = x_ref.shape[0]
    xb = _rms(x_ref[...], g_ref[...]).astype(bf16)

    pr = jnp.dot(xb, wr_ref[:, 0:4 * KV_WIDTH], preferred_element_type=f32)
    cosr = cosr_ref[...]
    sinr = sinr_ref[...]
    lane = lax.broadcasted_iota(i32, (tm, KV_WIDTH), 1)
    first_half = (lane & (HEAD_DIM - 1)) < HEAD_DIM // 2

    def rope_rows(t):
        rot = jnp.where(first_half, pltpu.roll(t, KV_WIDTH - HEAD_DIM // 2, 1),
                        pltpu.roll(t, HEAD_DIM // 2, 1))
        return t * cosr + rot * sinr

    kc_ref[...] = rope_rows(pr[:, 0:KV_WIDTH]).astype(bf16)
    vc_ref[...] = pr[:, KV_WIDTH:2 * KV_WIDTH].astype(bf16)
    ks_ref[:, 0:KV_WIDTH] = rope_rows(pr[:, 2 * KV_WIDTH:3 * KV_WIDTH]).astype(bf16)
    tok = (i % tiles_per_seq) * tm + lax.broadcasted_iota(i32, (tm, KV_WIDTH), 0)
    ks_ref[:, KV_WIDTH:2 * KV_WIDTH] = jnp.where(lane == tok // SEL_LEN, 1.0, 0.0).astype(bf16)
    kw_ref[...] = rope_rows(pr[:, 3 * KV_WIDTH:4 * KV_WIDTH]).astype(bf16)

    c0 = 4 * KV_WIDTH
    pc = jnp.dot(xb, wr_ref[:, c0:c0 + 3 * CONV_WIDTH], preferred_element_type=f32)
    ch = pc[:, 0:CONV_WIDTH]
    cb = pc[:, CONV_WIDTH:2 * CONV_WIDTH]
    cc = pc[:, 2 * CONV_WIDTH:3 * CONV_WIDTH]
    u = cc * ch

    @pl.when(i % tiles_per_seq == 0)
    def _():
        ubuf[0:SUBLANES, :] = jnp.zeros((SUBLANES, CONV_WIDTH), f32)

    @pl.when(i % tiles_per_seq != 0)
    def _():
        ubuf[0:SUBLANES, :] = ubuf[tm:tm + SUBLANES, :]

    ubuf[SUBLANES:SUBLANES + tm, :] = u
    u1 = ubuf[SUBLANES - 1:SUBLANES - 1 + tm, :]
    u2 = ubuf[SUBLANES - 2:SUBLANES - 2 + tm, :]
    w = convw_ref[...]
    y = cb * (w[0:1, :] * u2 + w[1:2, :] * u1 + w[2:3, :] * u)
    conv_ref[...] = _rms(y, gconv_ref[...]).astype(bf16)

    pt = lax.dot_general(wt_ref[...], xb, (((1,), (1,)), ((), ())), preferred_element_type=f32)
    cost = cost_ref[...]
    sint = sint_ref[...]
    half = HEAD_DIM // 2
    scale = HEAD_DIM ** -0.5
    for h in range(NSA_HEADS):
        t1 = pt[h * HEAD_DIM:h * HEAD_DIM + half, :]
        t2 = pt[h * HEAD_DIM + half:(h + 1) * HEAD_DIM, :]
        qT_ref[h * HEAD_DIM:h * HEAD_DIM + half, :] = ((t1 * cost - t2 * sint) * scale).astype(bf16)
        qT_ref[h * HEAD_DIM + half:(h + 1) * HEAD_DIM, :] = ((t2 * cost + t1 * sint) * scale).astype(bf16)
    r0 = NSA_WIDTH
    ones_rows = jnp.where(lax.broadcasted_iota(i32, (V_EXT - HEAD_DIM, tm), 0) == 0, 1.0, 0.0).astype(bf16)
    for vT_ref, base in ((vsT_ref, r0), (vwT_ref, r0 + KV_WIDTH)):
        for g in range(NSA_KV_HEADS):
            vT_ref[g * V_EXT:g * V_EXT + HEAD_DIM, :] = pt[base + g * HEAD_DIM:base + (g + 1) * HEAD_DIM, :].astype(bf16)
            vT_ref[g * V_EXT + HEAD_DIM:(g + 1) * V_EXT, :] = ones_rows
    gT_ref[...] = jax.nn.sigmoid(pt[r0 + 2 * KV_WIDTH:r0 + 2 * KV_WIDTH + 32, :])


def _inproj(x2, g_mix, w_row, w_t, cosr, sinr, cost, sint, conv_w8, g_conv, seq):
    T = x2.shape[0]
    tm = TM_IN
    n_row = w_row.shape[1]
    n_t = w_t.shape[0]
    row = lambda i: (i, 0)
    col = lambda i: (0, i)
    const = lambda i: (0, 0)
    out_shape = (
        jax.ShapeDtypeStruct((NSA_WIDTH, T), bf16),
        jax.ShapeDtypeStruct((T, KV_WIDTH), bf16),
        jax.ShapeDtypeStruct((T, KV_WIDTH), bf16),
        jax.ShapeDtypeStruct((T, 2 * KV_WIDTH), bf16),
        jax.ShapeDtypeStruct((T, KV_WIDTH), bf16),
        jax.ShapeDtypeStruct((NSA_KV_HEADS * V_EXT, T), bf16),
        jax.ShapeDtypeStruct((NSA_KV_HEADS * V_EXT, T), bf16),
        jax.ShapeDtypeStruct((32, T), f32),
        jax.ShapeDtypeStruct((T, CONV_WIDTH), bf16),
    )
    out_specs = (
        pl.BlockSpec((NSA_WIDTH, tm), col),
        pl.BlockSpec((tm, KV_WIDTH), row), pl.BlockSpec((tm, KV_WIDTH), row),
        pl.BlockSpec((tm, 2 * KV_WIDTH), row), pl.BlockSpec((tm, KV_WIDTH), row),
        pl.BlockSpec((NSA_KV_HEADS * V_EXT, tm), col), pl.BlockSpec((NSA_KV_HEADS * V_EXT, tm), col),
        pl.BlockSpec((32, tm), col),
        pl.BlockSpec((tm, CONV_WIDTH), row),
    )
    in_specs = [
        pl.BlockSpec((tm, D_MODEL), row),
        pl.BlockSpec((1, D_MODEL), const),
        pl.BlockSpec((D_MODEL, n_row), const),
        pl.BlockSpec((n_t, D_MODEL), const),
        pl.BlockSpec((tm, KV_WIDTH), row), pl.BlockSpec((tm, KV_WIDTH), row),
        pl.BlockSpec((HEAD_DIM // 2, tm), col), pl.BlockSpec((HEAD_DIM // 2, tm), col),
        pl.BlockSpec((SUBLANES, CONV_WIDTH), const),
        pl.BlockSpec((1, CONV_WIDTH), const),
    ]
    return pl.pallas_call(
        functools.partial(_inproj_kernel, tiles_per_seq=seq // tm),
        grid=(T // tm,), in_specs=in_specs, out_specs=out_specs, out_shape=out_shape,
        scratch_shapes=[pltpu.VMEM((tm + 2 * SUBLANES, CONV_WIDTH), f32)],
        compiler_params=_cparams(1), name="inproj",
    )(x2, g_mix, w_row, w_t, cosr, sinr, cost, sint, conv_w8, g_conv)


def _compress_kernel(xk_ref, xv_ref, pk_ref, pv_ref, w1k_ref, w1v_ref, b1k_ref, b1v_ref,
                     w2k_ref, w2vT_ref, kcc_ref, vcT_ref):
    ncp = xk_ref.shape[1]

    def hidden(x_ref, p_ref, w1_ref, b1_ref):
        x = x_ref[0].astype(f32)
        lo = (x + p_ref[0:1, :]).astype(bf16)
        hi = (x + p_ref[1:2, :]).astype(bf16)
        a = jnp.dot(lo, w1_ref[0], preferred_element_type=f32)
        b = jnp.dot(hi, w1_ref[1], preferred_element_type=f32)
        pre = a + pltpu.roll(b, ncp - 1, 0) + b1_ref[...]
        return jax.nn.gelu(pre).astype(bf16)

    hk = hidden(xk_ref, pk_ref, w1k_ref, b1k_ref)
    hv = hidden(xv_ref, pv_ref, w1v_ref, b1v_ref)
    for g in range(NSA_KV_HEADS):
        sl = slice(g * CMP_HIDDEN, (g + 1) * CMP_HIDDEN)
        kcc_ref[0, g] = jnp.dot(hk[:, sl], w2k_ref[...], preferred_element_type=f32).astype(bf16)
        vcT_ref[0, g] = lax.dot_general(w2vT_ref[...], hv[:, sl], (((1,), (1,)), ((), ())),
                                        preferred_element_type=f32).astype(bf16)


def _compress(kc_rows, vc_rows, pk, pv, w1k, w1v, b1k, b1v, w2k, w2vT, batch, seq):
    ncp = seq // CMP_STRIDE
    wide = CMP_STRIDE * KV_WIDTH
    xk = kc_rows.reshape(batch, ncp, wide)
    xv = vc_rows.reshape(batch, ncp, wide)
    c2 = lambda b: (0, 0)
    c3 = lambda b: (0, 0, 0)
    in_specs = [
        pl.BlockSpec((1, ncp, wide), lambda b: (b, 0, 0)),
        pl.BlockSpec((1, ncp, wide), lambda b: (b, 0, 0)),
        pl.BlockSpec((SUBLANES, wide), c2), pl.BlockSpec((SUBLANES, wide), c2),
        pl.BlockSpec((2, wide, 2 * CMP_HIDDEN), c3), pl.BlockSpec((2, wide, 2 * CMP_HIDDEN), c3),
        pl.BlockSpec((1, 2 * CMP_HIDDEN), c2), pl.BlockSpec((1, 2 * CMP_HIDDEN), c2),
        pl.BlockSpec((CMP_HIDDEN, HEAD_DIM), c2), pl.BlockSpec((HEAD_DIM, CMP_HIDDEN), c2),
    ]
    out_shape = (jax.ShapeDtypeStruct((batch, NSA_KV_HEADS, ncp, HEAD_DIM), bf16),
                 jax.ShapeDtypeStruct((batch, NSA_KV_HEADS, HEAD_DIM, ncp), bf16))
    out_specs = (pl.BlockSpec((1, NSA_KV_HEADS, ncp, HEAD_DIM), lambda b: (b, 0, 0, 0)),
                 pl.BlockSpec((1, NSA_KV_HEADS, HEAD_DIM, ncp), lambda b: (b, 0, 0, 0)))
    return pl.pallas_call(
        _compress_kernel, grid=(batch,), in_specs=in_specs, out_specs=out_specs,
        out_shape=out_shape, compiler_params=_cparams(1), name="compress",
    )(xk, xv, pk, pv, w1k, w1v, b1k, b1v, w2k, w2vT)


_NQ = HPG * TQ
_COL_BLOCKS = [slice(c * LANES, (c + 1) * LANES) for c in range(_NQ // LANES)]


def _compressed_branch(q4, kcc, vcT, ov, s0):
    ncp = kcc.shape[0]
    ns = ov.shape[0]
    s_lane = s0 + (lax.broadcasted_iota(i32, (1, _NQ), 1) & (TQ - 1))
    sc = jnp.dot(kcc, q4, preferred_element_type=f32)
    c_end = lax.broadcasted_iota(i32, (ncp, 1), 0) * CMP_STRIDE + (CMP_LEN - 1)
    blocks = []
    for cs in _COL_BLOCKS:
        cmask = c_end <= s_lane[:, cs]
        scm = jnp.where(cmask, sc[:, cs], NEG_INF)
        e_c = jnp.where(cmask, jnp.exp(scm - jnp.max(scm, axis=0, keepdims=True)), 0.0)
        l_c = jnp.sum(e_c, axis=0, keepdims=True)
        blocks.append(e_c * jnp.where(l_c > 0.0, 1.0 / l_c, 0.0))
    p_c = jnp.concatenate(blocks, axis=1)
    o_cmp = jnp.dot(vcT, p_c.astype(bf16), preferred_element_type=f32)

    ps = p_c[:, 0:TQ]
    for h in range(1, HPG):
        ps = ps + p_c[:, h * TQ:(h + 1) * TQ]
    p_hi = ps.astype(bf16)
    r1 = ps - p_hi.astype(f32)
    p_mid = r1.astype(bf16)
    p_lo = (r1 - p_mid.astype(f32)).astype(bf16)
    imp = (jnp.dot(ov, p_hi, preferred_element_type=f32) + jnp.dot(ov, p_mid, preferred_element_type=f32)
           + jnp.dot(ov, p_lo, preferred_element_type=f32))
    j_blk = lax.broadcasted_iota(i32, (ns, 1), 0)
    cur = (s0 + lax.broadcasted_iota(i32, (1, TQ), 1)) // SEL_LEN
    forced = (j_blk == 0) | (j_blk == cur) | (j_blk == cur - 1)
    imp = jnp.where(forced, FORCED, jnp.where(j_blk > cur, NEG_INF, imp))
    rank = jnp.zeros((ns, TQ), f32)
    for i in range(ns):
        row = imp[i:i + 1, :]
        tie_before = jnp.where(j_blk > i, 1.0, 0.0)
        rank = rank + jnp.where(row > imp, 1.0, jnp.where(row == imp, tie_before, 0.0))
    sel_bias = jnp.where(rank < float(min(N_SEL, ns)), 0.0, NEG_INF).astype(bf16)
    return o_cmp, sel_bias


def _attend(k_tile, q_op, vT_tile, bias, m_old, acc_old):
    sT = jnp.dot(k_tile, q_op, preferred_element_type=f32)
    p_blocks, m_blocks, a_blocks = [], [], []
    for c, cs in enumerate(_COL_BLOCKS):
        s = sT[:, cs]
        if bias is not None:
            b0 = (c % (TQ // LANES)) * LANES
            s = s + bias[:, b0:b0 + LANES]
        m_o = m_old[:, cs]
        m_n = jnp.maximum(m_o, jnp.max(s, axis=0, keepdims=True))
        p_blocks.append(jnp.exp((s - m_n).astype(bf16)))
        a_blocks.append(jnp.exp(m_o - m_n))
        m_blocks.append(m_n)
    p = jnp.concatenate(p_blocks, axis=1)
    acc = acc_old * jnp.concatenate(a_blocks, axis=1) + jnp.dot(vT_tile, p, preferred_element_type=f32)
    return jnp.concatenate(m_blocks, axis=1), acc


def _nsa_pair(pi, nqt, g, qa_ref, qb_ref, ks_ref, kw_ref, vsT_ref, vwT_ref, kcc_ref, vcT_ref, ga_ref, gb_ref,
              ovT_ref, cbias_ref, wbias_ref, o_ref):
    q_tiles = (pi, nqt - 1 - pi)
    ns = ovT_ref.shape[0]
    kcc = kcc_ref[0, 0]
    vcT = vcT_ref[0, 0]
    ov = ovT_ref[...]
    cbias = cbias_ref[...]
    n_back = WINDOW // TK
    m_init = jnp.full((1, _NQ), NEG_INF, f32)
    acc_init = jnp.zeros((V_EXT, _NQ), f32)

    def keys(ref, kt):
        return ref[kt * TK:(kt + 1) * TK, :]

    def values(ref, kt):
        return ref[:, kt * TK:(kt + 1) * TK]

    for slot, (q_ref, g_ref, qt) in enumerate(zip((qa_ref, qb_ref), (ga_ref, gb_ref), q_tiles)):
        q4 = jnp.concatenate([q_ref[h * HEAD_DIM:(h + 1) * HEAD_DIM, :] for h in range(HPG)], axis=1)
        zeros = jnp.zeros_like(q4)
        q_win = jnp.where(g == 0, jnp.concatenate([q4, zeros], axis=0), jnp.concatenate([zeros, q4], axis=0))

        m, acc = m_init, acc_init
        for j in range(min(n_back, qt) + 1):
            bias = cbias if j == 0 else (wbias_ref[...] if j == n_back else None)
            m, acc = _attend(keys(kw_ref, qt - j), q_win, values(vwT_ref, qt - j), bias, m, acc)
        o_win = acc[0:HEAD_DIM, :] * (1.0 / acc[HEAD_DIM:HEAD_DIM + 1, :])

        o_cmp, sel_bias = _compressed_branch(q4, kcc, vcT, ov, qt * TQ)

        q_sel = jnp.concatenate([q_win, jnp.concatenate([sel_bias] * HPG, axis=1),
                                 jnp.zeros((KV_WIDTH - ns, _NQ), bf16)], axis=0)
        m, acc = _attend(keys(ks_ref, qt), q_sel, values(vsT_ref, qt), cbias, m_init, acc_init)
        for kt in range(qt):
            m, acc = _attend(keys(ks_ref, kt), q_sel, values(vsT_ref, kt), None, m, acc)
        o_slc = acc[0:HEAD_DIM, :] * (1.0 / acc[HEAD_DIM:HEAD_DIM + 1, :])

        gates = g_ref[...]
        for h in range(HPG):
            sl = slice(h * TQ, (h + 1) * TQ)
            o = (gates[3 * h:3 * h + 1, :] * o_cmp[:, sl] + gates[3 * h + 1:3 * h + 2, :] * o_slc[:, sl]
                 + gates[3 * h + 2:3 * h + 3, :] * o_win[:, sl])
            o_ref[h * HEAD_DIM:(h + 1) * HEAD_DIM, slot * TQ:(slot + 1) * TQ] = o.astype(bf16)


def _nsa_kernel(*refs, nqt):
    for pi in range(nqt // 2):
        @pl.when(pl.program_id(0) == pi)
        def _():
            _nsa_pair(pi, nqt, pl.program_id(2), *refs)


def _nsa_tile_position(qt, nqt):
    return jnp.where(qt < nqt // 2, 2 * qt, 2 * (nqt - 1 - qt) + 1)


def _nsa(qT, ks, kw, vsT, vwT, kcc, vcT, gT, ovT, batch, seq):
    T = batch * seq
    nqt = seq // TQ
    ncp = kcc.shape[2]
    ns = seq // SEL_LEN
    nq = HPG * TQ
    gw = HPG * HEAD_DIM
    assert TQ == TK and WINDOW % TK == 0 and ns <= KV_WIDTH
    assert nqt % 2 == 0 and nqt // 2 >= WINDOW // TK
    kl = jnp.arange(TK)[:, None]
    ql = jnp.arange(TQ)[None, :]
    cbias = jnp.where(kl <= ql, 0.0, NEG_INF).astype(f32)
    wbias = jnp.where(kl > ql, 0.0, NEG_INF).astype(f32)
    amap = lambda p, b, g: (g, b * nqt + p)
    bmap = lambda p, b, g: (g, b * nqt + nqt - 1 - p)
    const = lambda p, b, g: (0, 0)
    in_specs = [
        pl.BlockSpec((gw, TQ), amap), pl.BlockSpec((gw, TQ), bmap),
        pl.BlockSpec((seq, 2 * KV_WIDTH), lambda p, b, g: (b, 0)),
        pl.BlockSpec((seq, KV_WIDTH), lambda p, b, g: (b, 0)),
        pl.BlockSpec((V_EXT, seq), lambda p, b, g: (g, b)),
        pl.BlockSpec((V_EXT, seq), lambda p, b, g: (g, b)),
        pl.BlockSpec((1, 1, ncp, HEAD_DIM), lambda p, b, g: (b, g, 0, 0)),
        pl.BlockSpec((1, 1, HEAD_DIM, ncp), lambda p, b, g: (b, g, 0, 0)),
        pl.BlockSpec((16, TQ), amap), pl.BlockSpec((16, TQ), bmap),
        pl.BlockSpec((ns, ncp), const),
        pl.BlockSpec((TK, TQ), const),
        pl.BlockSpec((TK, TQ), const),
    ]
    return pl.pallas_call(
        functools.partial(_nsa_kernel, nqt=nqt), grid=(nqt // 2, batch, NSA_KV_HEADS), in_specs=in_specs,
        out_specs=pl.BlockSpec((gw, 2 * TQ), lambda p, b, g: (g, b * (nqt // 2) + p)),
        out_shape=jax.ShapeDtypeStruct((NSA_WIDTH, T), bf16),
        compiler_params=_cparams(3), name="nsa",
    )(qT, qT, ks, kw, vsT, vwT, kcc, vcT, gT, gT, ovT, cbias, wbias)


def _memkv_kernel(mem_ref, g_ref, wkT_ref, wv_ref, kT_ref, v_ref):
    mb = _rms(mem_ref[0], g_ref[...]).astype(bf16)
    kT_ref[0] = lax.dot_general(wkT_ref[...], mb, (((1,), (1,)), ((), ())),
                                preferred_element_type=f32).astype(bf16)
    v_ref[0] = jnp.dot(mb, wv_ref[...], preferred_element_type=f32).astype(bf16)


def _memkv(mem, g_mem, wkT, wv):
    batch, n_mem, _ = mem.shape
    c2 = lambda b: (0, 0)
    return pl.pallas_call(
        _memkv_kernel, grid=(batch,),
        in_specs=[pl.BlockSpec((1, n_mem, D_MODEL), lambda b: (b, 0, 0)),
                  pl.BlockSpec((1, D_MODEL), c2),
                  pl.BlockSpec((D_MODEL, D_MODEL), c2), pl.BlockSpec((D_MODEL, D_MODEL), c2)],
        out_specs=(pl.BlockSpec((1, D_MODEL, n_mem), lambda b: (b, 0, 0)),
                   pl.BlockSpec((1, n_mem, D_MODEL), lambda b: (b, 0, 0))),
        out_shape=(jax.ShapeDtypeStruct((batch, D_MODEL, n_mem), bf16),
                   jax.ShapeDtypeStruct((batch, n_mem, D_MODEL), bf16)),
        compiler_params=_cparams(1), name="memkv",
    )(mem, g_mem, wkT, wv)


def _post_kernel(x_ref, oT_ref, conv_ref, gnsa_ref, wout_ref, gx_ref, wq_ref, kT_ref, v_ref, wo_ref,
                 gmoe_ref, wr_ref, br_ref,
                 h2_ref, xn2_ref, eidx_ref, gate_ref, sel_ref):
    tm = x_ref.shape[0]
    oT = oT_ref[...].astype(f32)
    onT = (oT * lax.rsqrt(jnp.mean(oT * oT, axis=0, keepdims=True) + RMS_EPS) * gnsa_ref[...]).astype(bf16)
    mix = lax.dot_general(onT, wout_ref[0:NSA_WIDTH, :], (((0,), (0,)), ((), ())),
                          preferred_element_type=f32)
    mix = mix + jnp.dot(conv_ref[...], wout_ref[NSA_WIDTH:D_MODEL, :], preferred_element_type=f32)
    h1 = x_ref[...] + mix

    hn = _rms(h1, gx_ref[...]).astype(bf16)
    q = (jnp.dot(hn, wq_ref[...], preferred_element_type=f32) * (XATTN_HEAD_DIM ** -0.5)).astype(bf16)
    heads = []
    for h in range(XATTN_HEADS):
        sl = slice(h * XATTN_HEAD_DIM, (h + 1) * XATTN_HEAD_DIM)
        s = jnp.dot(q[:, sl], kT_ref[0, sl, :], preferred_element_type=f32)
        e = jnp.exp(s - jnp.max(s, axis=-1, keepdims=True))
        p = e * (1.0 / jnp.sum(e, axis=-1, keepdims=True))
        heads.append(jnp.dot(p.astype(bf16), v_ref[0, :, sl], preferred_element_type=f32))
    o = jnp.concatenate(heads, axis=1).astype(bf16)
    h2 = h1 + jnp.dot(o, wo_ref[...], preferred_element_type=f32)
    h2_ref[...] = h2

    xn2 = _rms(h2, gmoe_ref[...])
    for s_ in range(ROW_SUB):
        xn2_ref[pl.ds(s_, tm, stride=ROW_SUB), :] = xn2[:, s_ * LANES:(s_ + 1) * LANES]

    logits = jnp.dot(xn2.astype(bf16), wr_ref[...], preferred_element_type=f32) + br_ref[...]
    lane = lax.broadcasted_iota(i32, (tm, LANES), 1)
    work = logits
    sel = jnp.zeros((tm, LANES), f32)
    eidx = jnp.zeros((tm, LANES), i32)
    vals = []
    for k in range(TOP_K):
        mk = jnp.max(work, axis=-1, keepdims=True)
        ik = jnp.min(jnp.where(work == mk, lane, LANES), axis=-1, keepdims=True)
        hit = lane == ik
        work = jnp.where(hit, -jnp.inf, work)
        sel = jnp.where(hit, 1.0, sel)
        eidx = jnp.where(lane == k, ik, eidx)
        vals.append(mk)
    es = [jnp.exp(v - vals[0]) for v in vals]
    den = es[0]
    for e in es[1:]:
        den = den + e
    gate = jnp.zeros((tm, LANES), f32)
    for k in range(TOP_K):
        gate = jnp.where(lane == k, es[k] / den, gate)
    eidx_ref[...] = eidx
    gate_ref[...] = gate
    sel_ref[...] = sel.astype(bf16)


def _post(x2, oT, conv_n, g_nsa_col, w_out, g_x, w_q, kT, v, w_o, g_moe, w_r, b_r, seq):
    T = x2.shape[0]
    tm = TM_POST
    assert tm == TQ
    tps = seq // tm
    n_mem = v.shape[1]
    row = lambda i: (i, 0)
    const = lambda i: (0, 0)
    in_specs = [
        pl.BlockSpec((tm, D_MODEL), row),
        pl.BlockSpec((NSA_WIDTH, tm), lambda i: (0, (i // tps) * tps + _nsa_tile_position(i % tps, tps))),
        pl.BlockSpec((tm, CONV_WIDTH), row),
        pl.BlockSpec((NSA_WIDTH, 1), const),
        pl.BlockSpec((D_MODEL, D_MODEL), const),
        pl.BlockSpec((1, D_MODEL), const),
        pl.BlockSpec((D_MODEL, D_MODEL), const),
        pl.BlockSpec((1, D_MODEL, n_mem), lambda i: (i // tps, 0, 0)),
        pl.BlockSpec((1, n_mem, D_MODEL), lambda i: (i // tps, 0, 0)),
        pl.BlockSpec((D_MODEL, D_MODEL), const),
        pl.BlockSpec((1, D_MODEL), const),
        pl.BlockSpec((D_MODEL, LANES), const),
        pl.BlockSpec((1, LANES), const),
    ]
    out_shape = (jax.ShapeDtypeStruct((T, D_MODEL), f32),
                 jax.ShapeDtypeStruct((T * ROW_SUB, LANES), f32),
                 jax.ShapeDtypeStruct((T, LANES), i32),
                 jax.ShapeDtypeStruct((T, LANES), f32),
                 jax.ShapeDtypeStruct((T, LANES), bf16))
    out_specs = (pl.BlockSpec((tm, D_MODEL), row),
                 pl.BlockSpec((tm * ROW_SUB, LANES), row),
                 pl.BlockSpec((tm, LANES), row), pl.BlockSpec((tm, LANES), row),
                 pl.BlockSpec((tm, LANES), row))
    return pl.pallas_call(
        _post_kernel, grid=(T // tm,), in_specs=in_specs, out_specs=out_specs, out_shape=out_shape,
        compiler_params=_cparams(1), name="post",
    )(x2, oT, conv_n, g_nsa_col, w_out, g_x, w_q, kT, v, w_o, g_moe, w_r, b_r)


def _count_kernel(sel_ref, cum_ref, cnt_ref, carry):
    i = pl.program_id(0)
    tm = sel_ref.shape[0]

    @pl.when(i == 0)
    def _():
        carry[...] = jnp.zeros_like(carry)

    sel = sel_ref[...]
    r = lax.broadcasted_iota(i32, (tm, tm), 0)
    c = lax.broadcasted_iota(i32, (tm, tm), 1)
    strict_lower = jnp.where(c < r, 1.0, 0.0).astype(bf16)
    base = carry[0:1, :]
    cum_ref[...] = jnp.dot(strict_lower, sel, preferred_element_type=f32) + base
    total = base + jnp.sum(sel.astype(f32), axis=0, keepdims=True)
    carry[...] = jnp.broadcast_to(total, carry.shape)
    cnt_ref[...] = jnp.broadcast_to(total, cnt_ref.shape)


def _count(sel):
    T = sel.shape[0]
    tm = TM_ROUTE
    return pl.pallas_call(
        _count_kernel, grid=(T // tm,),
        in_specs=[pl.BlockSpec((tm, LANES), lambda i: (i, 0))],
        out_specs=(pl.BlockSpec((tm, LANES), lambda i: (i, 0)),
                   pl.BlockSpec((SUBLANES, LANES), lambda i: (0, 0))),
        out_shape=(jax.ShapeDtypeStruct((T, LANES), f32), jax.ShapeDtypeStruct((SUBLANES, LANES), f32)),
        scratch_shapes=[pltpu.VMEM((SUBLANES, LANES), f32)],
        compiler_params=_cparams(1), name="route_count",
    )(sel)


def _dest_kernel(cum_ref, eidx_ref, pstart_ref, dest_ref):
    tm = cum_ref.shape[0]
    lane = lax.broadcasted_iota(i32, (tm, LANES), 1)
    row_of = cum_ref[...] + pstart_ref[0:1, :]
    eidx = eidx_ref[...]
    dest = jnp.zeros((tm, LANES), f32)
    for k in range(TOP_K):
        ek = eidx[:, k:k + 1]
        dk = jnp.sum(jnp.where(lane == ek, row_of, 0.0), axis=-1, keepdims=True)
        dest = jnp.where(lane == k, dk, dest)
    dest_ref[...] = dest.astype(i32)


def _dest(cum, eidx, pstart8):
    T = cum.shape[0]
    tm = TM_ROUTE
    row = lambda i: (i, 0)
    return pl.pallas_call(
        _dest_kernel, grid=(T // tm,),
        in_specs=[pl.BlockSpec((tm, LANES), row), pl.BlockSpec((tm, LANES), row),
                  pl.BlockSpec((SUBLANES, LANES), lambda i: (0, 0))],
        out_specs=pl.BlockSpec((tm, LANES), row),
        out_shape=jax.ShapeDtypeStruct((T, LANES), i32),
        compiler_params=_cparams(1), name="route_dest",
    )(cum, eidx, pstart8)


def _invert_kernel(dest_ref, init_ref, slot_ref, sem):
    cp = pltpu.make_async_copy(init_ref, slot_ref, sem)
    cp.start()
    cp.wait()
    chunk = LANES

    def body(j, carry):
        base = j * chunk
        for l in range(chunk):
            slot_ref[dest_ref[base + l]] = base + l
        return carry

    lax.fori_loop(0, dest_ref.shape[0] // chunk, body, 0)


def _invert(dest_flat, n_steps):
    parity = jnp.concatenate([jnp.arange(n_steps, dtype=i32) % 2, jnp.ones((1,), i32)])
    sink = dest_flat.shape[0] + parity[:, None] * BM + jnp.arange(BM, dtype=i32)[None, :]
    return pl.pallas_call(
        _invert_kernel,
        in_specs=[pl.BlockSpec(memory_space=pltpu.SMEM), pl.BlockSpec(memory_space=pltpu.VMEM)],
        out_specs=pl.BlockSpec(memory_space=pltpu.SMEM),
        out_shape=jax.ShapeDtypeStruct(((n_steps + 1) * BM,), i32),
        scratch_shapes=[pltpu.SemaphoreType.DMA],
        compiler_params=pltpu.CompilerParams(vmem_limit_bytes=VMEM_LIMIT), name="route_invert",
    )(dest_flat, sink.reshape(-1))


def _ffn_kernel(blk_e_ref, first_ref, nxt_e_ref, wslot_ref, nused_ref, slot_ref,
                x_hbm, wgu_hbm, bgu_ref, wd_hbm, bd_ref, y_hbm,
                xbuf0, xbuf1, obuf0, obuf1, wgu_f, wd_f, wgu_bf, wd_bf, gsem, ssem, wsem, *, n_tok, sink_row):
    i = pl.program_id(0)
    nused = nused_ref[0]
    n_steps = pl.num_programs(0)
    xbuf = (xbuf0, xbuf1)
    obuf = (obuf0, obuf1)

    def row_window(ref, row):
        return ref.at[pl.ds(pl.multiple_of(row * ROW_SUB, SUBLANES), ROW_SUB)]

    def gather(blk, par, r):
        tok = slot_ref[blk * BM + r] & (n_tok - 1)
        return pltpu.make_async_copy(row_window(x_hbm, tok), xbuf[par].at[pl.ds(r * ROW_SUB, ROW_SUB)],
                                     gsem.at[par])

    def scatter(blk, par, r):
        return pltpu.make_async_copy(obuf[par].at[pl.ds(r * ROW_SUB, ROW_SUB)],
                                     row_window(y_hbm, slot_ref[blk * BM + r]), ssem.at[par])

    def wait_gather(par):
        pltpu.make_async_copy(x_hbm.at[pl.ds(0, BM * ROW_SUB)], xbuf[par], gsem.at[par]).wait()

    def wait_scatter(par):
        pltpu.make_async_copy(obuf[par], y_hbm.at[pl.ds(0, BM * ROW_SUB)], ssem.at[par]).wait()

    def weight_copies(e, ws):
        return (pltpu.make_async_copy(wgu_hbm.at[e], wgu_f.at[ws], wsem.at[ws, 0]),
                pltpu.make_async_copy(wd_hbm.at[e], wd_f.at[ws], wsem.at[ws, 1]))

    @pl.when(i == 0)
    def _():
        obuf0[...] = jnp.zeros(obuf0.shape, f32)
        obuf1[...] = jnp.zeros(obuf1.shape, f32)
        pltpu.make_async_copy(obuf0, y_hbm.at[pl.ds(sink_row * ROW_SUB, BM * ROW_SUB)], ssem.at[0]).start()
        for cp in weight_copies(blk_e_ref[0], 0):
            cp.start(priority=1)
        for r in range(BM):
            gather(0, 0, r).start()

    @pl.when((i < nused) & (first_ref[i] == 1))
    def _():
        ws = wslot_ref[i]
        for cp in weight_copies(blk_e_ref[i], ws):
            cp.wait()

        @pl.when(nxt_e_ref[i] >= 0)
        def _():
            for cp in weight_copies(nxt_e_ref[i], 1 - ws):
                cp.start(priority=1)

        wgu_bf[...] = wgu_f[ws].astype(bf16)
        wd_bf[...] = wd_f[ws].astype(bf16)

    def block(par):
        prev = jnp.where(i == 0, n_steps, i - 1)
        wait_gather(par)
        for r in range(BM):
            gather(i + 1, 1 - par, r).start()
        for r in range(BM):
            scatter(prev, 1 - par, r).start()
        x = jnp.concatenate([xbuf[par][pl.ds(s_, BM, stride=ROW_SUB), :] for s_ in range(ROW_SUB)],
                            axis=1).astype(bf16)
        gu = jnp.dot(x, wgu_bf[...], preferred_element_type=f32) + bgu_ref[0]
        gg = jnp.minimum(gu[:, 0:D_FF], SWIGLU_LIMIT)
        uu = jnp.clip(gu[:, D_FF:2 * D_FF], -SWIGLU_LIMIT, SWIGLU_LIMIT)
        hmid = (uu + 1.0) * (gg * jax.nn.sigmoid(SWIGLU_ALPHA * gg))
        out = jnp.dot(hmid.astype(bf16), wd_bf[...], preferred_element_type=f32) + bd_ref[0]
        wait_scatter(par)
        for s_ in range(ROW_SUB):
            obuf[par][pl.ds(s_, BM, stride=ROW_SUB), :] = out[:, s_ * LANES:(s_ + 1) * LANES]

    for par in range(2):
        @pl.when((i < nused) & (i % 2 == par))
        def _():
            block(par)

        @pl.when((i == nused) & (i % 2 == par))
        def _():
            for r in range(BM):
                scatter(i - 1, 1 - par, r).start()
            wait_gather(par)
            wait_scatter(par)
            wait_scatter(1 - par)


def _ffn(blk_e, first, nxt_e, wslot, nused, slots, xn2_rows, w_gu, b_gu, w_d, b_d):
    n_steps = blk_e.shape[0]
    n_tok = xn2_rows.shape[0] // ROW_SUB
    assert n_tok & (n_tok - 1) == 0
    n_tok_rows = n_tok * TOP_K
    emap = lambda i, be, *_: (be[i], 0, 0)
    grid_spec = pltpu.PrefetchScalarGridSpec(
        num_scalar_prefetch=6, grid=(n_steps,),
        in_specs=[pl.BlockSpec(memory_space=pl.ANY),
                  pl.BlockSpec(memory_space=pl.ANY),
                  pl.BlockSpec((1, 1, 2 * D_FF), emap),
                  pl.BlockSpec(memory_space=pl.ANY),
                  pl.BlockSpec((1, 1, D_MODEL), emap)],
        out_specs=pl.BlockSpec(memory_space=pl.ANY),
        scratch_shapes=[pltpu.VMEM((BM * ROW_SUB, LANES), f32), pltpu.VMEM((BM * ROW_SUB, LANES), f32),
                        pltpu.VMEM((BM * ROW_SUB, LANES), f32), pltpu.VMEM((BM * ROW_SUB, LANES), f32),
                        pltpu.VMEM((2, D_MODEL, 2 * D_FF), f32), pltpu.VMEM((2, D_FF, D_MODEL), f32),
                        pltpu.VMEM((D_MODEL, 2 * D_FF), bf16), pltpu.VMEM((D_FF, D_MODEL), bf16),
                        pltpu.SemaphoreType.DMA((2,)), pltpu.SemaphoreType.DMA((2,)),
                        pltpu.SemaphoreType.DMA((2, 2))],
    )
    return pl.pallas_call(
        functools.partial(_ffn_kernel, n_tok=n_tok, sink_row=n_tok_rows), grid_spec=grid_spec,
        out_shape=jax.ShapeDtypeStruct(((n_tok_rows + 2 * BM) * ROW_SUB, LANES), f32),
        compiler_params=_cparams(1), name="ffn",
    )(blk_e, first, nxt_e, wslot, nused, slots, xn2_rows, w_gu, b_gu, w_d, b_d)


def _combine_kernel(*refs, final_norm):
    y_refs = refs[:TOP_K]
    gate_ref, h2_ref, gfin_ref, o_ref = refs[TOP_K:]
    tm = h2_ref.shape[0]
    gate = gate_ref[...]
    cols = []
    for s_ in range(ROW_SUB):
        acc = gate[:, 0:1] * y_refs[0][pl.ds(s_, tm, stride=ROW_SUB), :]
        for k in range(1, TOP_K):
            acc = acc + gate[:, k:k + 1] * y_refs[k][pl.ds(s_, tm, stride=ROW_SUB), :]
        cols.append(acc)
    h = h2_ref[...] + jnp.concatenate(cols, axis=1)
    if final_norm:
        h = _rms(h, gfin_ref[...])
    o_ref[...] = h


def _combine(y_rows, gate, h2, g_final, final_norm):
    T = h2.shape[0]
    tm = TM_ROW
    row = lambda i: (i, 0)
    planes = [pl.BlockSpec((tm * ROW_SUB, LANES), functools.partial(lambda i, k: (k * (T // tm) + i, 0), k=k))
              for k in range(TOP_K)]
    return pl.pallas_call(
        functools.partial(_combine_kernel, final_norm=final_norm), grid=(T // tm,),
        in_specs=planes + [pl.BlockSpec((tm, LANES), row), pl.BlockSpec((tm, D_MODEL), row),
                           pl.BlockSpec((1, D_MODEL), lambda i: (0, 0))],
        out_specs=pl.BlockSpec((tm, D_MODEL), row),
        out_shape=jax.ShapeDtypeStruct((T, D_MODEL), f32),
        compiler_params=_cparams(1), name="combine",
    )(*([y_rows] * TOP_K), gate, h2, g_final)


def _prep_inproj_weights(w_in):
    sizes = (NSA_WIDTH,) + (KV_WIDTH,) * 6 + (3 * NSA_HEADS,) + (CONV_WIDTH,) * 3
    offs = [0]
    for s in sizes:
        offs.append(offs[-1] + s)
    seg = lambda n: w_in[:, offs[n]:offs[n + 1]]
    q, kc, vc, ks, vs, kw, vw, gl, ch, cb, cc = (seg(n) for n in range(11))
    w_row = jnp.concatenate([kc, vc, ks, kw, ch, cb, cc], axis=1).astype(bf16)
    gl_g = gl.reshape(D_MODEL, NSA_KV_HEADS, HPG * 3)
    gl_g = jnp.pad(gl_g, ((0, 0), (0, 0), (0, 16 - HPG * 3))).reshape(D_MODEL, NSA_KV_HEADS * 16)
    w_t = jnp.concatenate([q, vs, vw, gl_g], axis=1).T.astype(bf16)
    return w_row, w_t


def _rope_tables(positions):
    half = HEAD_DIM // 2
    inv_freq = ROPE_THETA ** (-jnp.arange(half, dtype=f32) / half)
    ang = positions.reshape(-1).astype(f32)[:, None] * inv_freq
    cos = jnp.cos(ang)
    sin = jnp.sin(ang)
    reps = KV_WIDTH // HEAD_DIM
    cosr = jnp.tile(cos, (1, 2 * reps))
    sinr = jnp.tile(jnp.concatenate([-sin, sin], axis=1), (1, reps))
    return cosr, sinr, cos.T, sin.T


def _cmp_weights(pos, w1, b1):
    w1r = w1.reshape(2, CMP_STRIDE, HEAD_DIM, CMP_HIDDEN)
    eye = jnp.eye(NSA_KV_HEADS, dtype=w1.dtype)
    big = jnp.einsum('aldh,gk->algdkh', w1r, eye)
    big = big.reshape(2, CMP_STRIDE * KV_WIDTH, NSA_KV_HEADS * CMP_HIDDEN).astype(bf16)
    p = jnp.broadcast_to(pos.reshape(2, CMP_STRIDE, 1, HEAD_DIM), (2, CMP_STRIDE, NSA_KV_HEADS, HEAD_DIM))
    p = jnp.pad(p.reshape(2, CMP_STRIDE * KV_WIDTH), ((0, SUBLANES - 2), (0, 0)))
    return big, p, jnp.tile(b1.reshape(1, CMP_HIDDEN), (1, NSA_KV_HEADS))


def _mixer_core(x2, tables, p, batch, seq):
    cosr, sinr, cost, sint = tables
    row1 = lambda v: v.reshape(1, -1)
    w_row, w_t = _prep_inproj_weights(p['w_mix_in'])
    conv_w8 = jnp.pad(p['conv_w'], ((0, SUBLANES - CONV_K), (0, 0)))
    qT, kc, vc, ks, kw, vsT, vwT, gT, conv_n = _inproj(
        x2, row1(p['g_mix_norm']), w_row, w_t, cosr, sinr, cost, sint, conv_w8, row1(p['g_conv_out']), seq)

    w1k, pk, b1k = _cmp_weights(p['cmp_pos_k'], p['cmp_w1_k'], p['cmp_b1_k'])
    w1v, pv, b1v = _cmp_weights(p['cmp_pos_v'], p['cmp_w1_v'], p['cmp_b1_v'])
    kcc, vcT = _compress(kc, vc, pk, pv, w1k, w1v, b1k, b1v,
                         p['cmp_w2_k'].astype(bf16), p['cmp_w2_v'].T.astype(bf16), batch, seq)

    ncp = seq // CMP_STRIDE
    ns = seq // SEL_LEN
    cs = jnp.arange(ncp) * CMP_STRIDE
    js = jnp.arange(ns) * SEL_LEN
    overlap = jnp.clip(jnp.minimum(cs[:, None] + CMP_LEN, js[None, :] + SEL_LEN)
                       - jnp.maximum(cs[:, None], js[None, :]), 0, None).astype(f32) / CMP_LEN
    ovT = overlap.T.astype(bf16)
    oT = _nsa(qT, ks, kw, vsT, vwT, kcc, vcT, gT, ovT, batch, seq)
    return oT, conv_n


def _layer(h, memf, tables, p, final_gain, final_norm):
    batch, seq, _ = h.shape
    T = batch * seq
    x2 = h.reshape(T, D_MODEL)
    row1 = lambda v: v.reshape(1, -1)
    oT, conv_n = _mixer_core(x2, tables, p, batch, seq)

    w_xkv = p['w_xkv']
    kT, v = _memkv(memf, row1(p['g_mem_norm']), w_xkv[:, :D_MODEL].T.astype(bf16),
                   w_xkv[:, D_MODEL:].astype(bf16))
    w_r = jnp.pad(p['w_router'], ((0, 0), (0, LANES - N_EXPERTS))).astype(bf16)
    b_r = jnp.pad(p['b_router'], (0, LANES - N_EXPERTS), constant_values=NEG_INF).reshape(1, LANES)
    h2, xn2_rows, eidx, gate, sel = _post(
        x2, oT, conv_n, p['g_nsa_out'].reshape(NSA_WIDTH, 1), p['w_mix_out'].astype(bf16),
        row1(p['g_xattn_norm']), p['w_xq'].astype(bf16), kT, v, p['w_xo'].astype(bf16),
        row1(p['g_moe_norm']), w_r, b_r, seq)

    cum, cnt = _count(sel)
    counts = cnt[0, :N_EXPERTS].astype(i32)
    padded = (counts + BM - 1) // BM * BM
    pend = jnp.cumsum(padded)
    pstart = pend - padded
    n_steps = (T * TOP_K) // BM + N_EXPERTS + 1
    nused = (pend[-1] // BM).astype(i32)
    step = jnp.arange(n_steps, dtype=i32)
    used = step < nused
    blk_raw = jnp.minimum(jnp.sum((pend[None, :] <= (step * BM)[:, None]).astype(i32), axis=1), N_EXPERTS - 1)
    blk_e = blk_raw[jnp.minimum(step, nused - 1)]
    first = (used & jnp.concatenate([jnp.ones((1,), bool), blk_e[1:] != blk_e[:-1]])).astype(i32)
    wslot = (jnp.cumsum(first) - 1) % 2
    e_ids = jnp.arange(N_EXPERTS, dtype=i32)
    later = (e_ids[None, :] > e_ids[:, None]) & (padded > 0)[None, :]
    nxt_of = jnp.min(jnp.where(later, e_ids[None, :], N_EXPERTS), axis=1)
    nxt_e = jnp.where(nxt_of < N_EXPERTS, nxt_of, -1).astype(i32)[blk_e]
    pstart8 = jnp.broadcast_to(jnp.pad(pstart.astype(f32), (0, LANES - N_EXPERTS))[None, :], (SUBLANES, LANES))

    dest = _dest(cum, eidx, pstart8)
    slots = _invert(dest[:, :TOP_K].T.reshape(-1), n_steps)
    y_rows = _ffn(blk_e, first, nxt_e, wslot.astype(i32), nused.reshape(1), slots, xn2_rows,
                  p['w_gate_up'], p['b_gate_up'].reshape(N_EXPERTS, 1, 2 * D_FF), p['w_down'],
                  p['b_down'].reshape(N_EXPERTS, 1, D_MODEL))
    out = _combine(y_rows, gate, h2, row1(final_gain), final_norm)
    return out.reshape(batch, seq, D_MODEL)


_LAYER_PARAMS = ('g_mix_norm', 'w_mix_in', 'cmp_pos_k', 'cmp_pos_v', 'cmp_w1_k', 'cmp_b1_k', 'cmp_w2_k',
                 'cmp_w1_v', 'cmp_b1_v', 'cmp_w2_v', 'conv_w', 'g_nsa_out', 'g_conv_out', 'w_mix_out',
                 'g_xattn_norm', 'g_mem_norm', 'w_xq', 'w_xkv', 'w_xo', 'g_moe_norm', 'w_router', 'b_router',
                 'w_gate_up', 'b_gate_up', 'w_down', 'b_down')


def kernel(x, mem, positions, g_mix_norm, w_mix_in, cmp_pos_k, cmp_pos_v, cmp_w1_k, cmp_b1_k, cmp_w2_k, cmp_w1_v, cmp_b1_v, cmp_w2_v, conv_w, g_nsa_out, g_conv_out, w_mix_out, g_xattn_norm, g_mem_norm, w_xq, w_xkv, w_xo, g_moe_norm, w_router, b_router, w_gate_up, b_gate_up, w_down, b_down, g_final):
    stacked = dict(zip(_LAYER_PARAMS, (g_mix_norm, w_mix_in, cmp_pos_k, cmp_pos_v, cmp_w1_k, cmp_b1_k, cmp_w2_k,
                                       cmp_w1_v, cmp_b1_v, cmp_w2_v, conv_w, g_nsa_out, g_conv_out, w_mix_out,
                                       g_xattn_norm, g_mem_norm, w_xq, w_xkv, w_xo, g_moe_norm, w_router,
                                       b_router, w_gate_up, b_gate_up, w_down, b_down)))
    depth = g_mix_norm.shape[0]
    tables = _rope_tables(positions)
    h = x
    for l in range(depth):
        p = {k: v[l] for k, v in stacked.items()}
        last = l == depth - 1
        h = _layer(h, mem, tables, p, g_final, final_norm=last)
    return h
```

```python
import functools

import jax
import jax.numpy as jnp
from jax import lax
from jax.experimental import pallas as pl
from jax.experimental.pallas import tpu as pltpu

f32 = jnp.float32
bf16 = jnp.bfloat16
i32 = jnp.int32

D_MODEL = 1024
HEAD_DIM = 64
NSA_HEADS = 8
NSA_KV_HEADS = 2
HPG = NSA_HEADS // NSA_KV_HEADS
NSA_WIDTH = NSA_HEADS * HEAD_DIM
KV_WIDTH = NSA_KV_HEADS * HEAD_DIM
CONV_WIDTH = D_MODEL - NSA_WIDTH
CONV_K = 3
CMP_LEN = 32
CMP_STRIDE = 16
CMP_HIDDEN = 256
SEL_LEN = 64
N_SEL = 16
WINDOW = 512
ROPE_THETA = 10000.0
XATTN_HEADS = 4
XATTN_HEAD_DIM = D_MODEL // XATTN_HEADS
N_EXPERTS = 32
TOP_K = 4
D_FF = D_MODEL
SWIGLU_LIMIT = 7.0
SWIGLU_ALPHA = 1.702
RMS_EPS = 1e-5
NEG_INF = -1e30
FORCED = 1e30

LANES = 128
SUBLANES = 8
VMEM_LIMIT = 56 * 1024 * 1024

TM_IN = 512
TQ = 256
TK = 256
TM_POST = 512
TM_ROUTE = 512
TM_ROW = 128
V_EXT = HEAD_DIM + 16
BM = 256
ROW_SUB = D_MODEL // LANES


def _cparams(n_axes, **kw):
    return pltpu.CompilerParams(dimension_semantics=("arbitrary",) * n_axes,
                                vmem_limit_bytes=VMEM_LIMIT, **kw)


def _rms(t, gain):
    return t * lax.rsqrt(jnp.mean(t * t, axis=-1, keepdims=True) + RMS_EPS) * gain


def _inproj_kernel(x_ref, g_ref, wr_ref, wt_ref, cosr_ref, sinr_ref, cost_ref, sint_ref,
                   convw_ref, gconv_ref,
                   qT_ref, kc_ref, vc_ref, ks_ref, kw_ref, vsT_ref, vwT_ref, gT_ref, conv_ref,
                   ubuf, *, tiles_per_seq):
    i = pl.program_id(0)
    tm = x_ref.shape[0]
    xb = _rms(x_ref[...], g_ref[...]).astype(bf16)

    pr = jnp.dot(xb, wr_ref[:, 0:4 * KV_WIDTH], preferred_element_type=f32)
    cosr = cosr_ref[...]
    sinr = sinr_ref[...]
    lane = lax.broadcasted_iota(i32, (tm, KV_WIDTH), 1)
    first_half = (lane & (HEAD_DIM - 1)) < HEAD_DIM // 2

    def rope_rows(t):
        rot = jnp.where(first_half, pltpu.roll(t, KV_WIDTH - HEAD_DIM // 2, 1),
                        pltpu.roll(t, HEAD_DIM // 2, 1))
        return t * cosr + rot * sinr

    kc_ref[...] = rope_rows(pr[:, 0:KV_WIDTH]).astype(bf16)
    vc_ref[...] = pr[:, KV_WIDTH:2 * KV_WIDTH].astype(bf16)
    ks_ref[:, 0:KV_WIDTH] = rope_rows(pr[:, 2 * KV_WIDTH:3 * KV_WIDTH]).astype(bf16)
    tok = (i % tiles_per_seq) * tm + lax.broadcasted_iota(i32, (tm, KV_WIDTH), 0)
    ks_ref[:, KV_WIDTH:2 * KV_WIDTH] = jnp.where(lane == tok // SEL_LEN, 1.0, 0.0).astype(bf16)
    kw_ref[...] = rope_rows(pr[:, 3 * KV_WIDTH:4 * KV_WIDTH]).astype(bf16)

    c0 = 4 * KV_WIDTH
    pc = jnp.dot(xb, wr_ref[:, c0:c0 + 3 * CONV_WIDTH], preferred_element_type=f32)
    ch = pc[:, 0:CONV_WIDTH]
    cb = pc[:, CONV_WIDTH:2 * CONV_WIDTH]
    cc = pc[:, 2 * CONV_WIDTH:3 * CONV_WIDTH]
    u = cc * ch

    @pl.when(i % tiles_per_seq == 0)
    def _():
        ubuf[0:SUBLANES, :] = jnp.zeros((SUBLANES, CONV_WIDTH), f32)

    @pl.when(i % tiles_per_seq != 0)
    def _():
        ubuf[0:SUBLANES, :] = ubuf[tm:tm + SUBLANES, :]

    ubuf[SUBLANES:SUBLANES + tm, :] = u
    u1 = ubuf[SUBLANES - 1:SUBLANES - 1 + tm, :]
    u2 = ubuf[SUBLANES - 2:SUBLANES - 2 + tm, :]
    w = convw_ref[...]
    y = cb * (w[0:1, :] * u2 + w[1:2, :] * u1 + w[2:3, :] * u)
    conv_ref[...] = _rms(y, gconv_ref[...]).astype(bf16)

    pt = lax.dot_general(wt_ref[...], xb, (((1,), (1,)), ((), ())), preferred_element_type=f32)
    cost = cost_ref[...]
    sint = sint_ref[...]
    half = HEAD_DIM // 2
    scale = HEAD_DIM ** -0.5
    for h in range(NSA_HEADS):
        t1 = pt[h * HEAD_DIM:h * HEAD_DIM + half, :]
        t2 = pt[h * HEAD_DIM + half:(h + 1) * HEAD_DIM, :]
        qT_ref[h * HEAD_DIM:h * HEAD_DIM + half, :] = ((t1 * cost - t2 * sint) * scale).astype(bf16)
        qT_ref[h * HEAD_DIM + half:(h + 1) * HEAD_DIM, :] = ((t2 * cost + t1 * sint) * scale).astype(bf16)
    r0 = NSA_WIDTH
    ones_rows = jnp.where(lax.broadcasted_iota(i32, (V_EXT - HEAD_DIM, tm), 0) == 0, 1.0, 0.0).astype(bf16)
    for vT_ref, base in ((vsT_ref, r0), (vwT_ref, r0 + KV_WIDTH)):
        for g in range(NSA_KV_HEADS):
            vT_ref[g * V_EXT:g * V_EXT + HEAD_DIM, :] = pt[base + g * HEAD_DIM:base + (g + 1) * HEAD_DIM, :].astype(bf16)
            vT_ref[g * V_EXT + HEAD_DIM:(g + 1) * V_EXT, :] = ones_rows
    gT_ref[...] = jax.nn.sigmoid(pt[r0 + 2 * KV_WIDTH:r0 + 2 * KV_WIDTH + 32, :])


def _inproj(x2, g_mix, w_row, w_t, cosr, sinr, cost, sint, conv_w8, g_conv, seq):
    T = x2.shape[0]
    tm = TM_IN
    n_row = w_row.shape[1]
    n_t = w_t.shape[0]
    row = lambda i: (i, 0)
    col = lambda i: (0, i)
    const = lambda i: (0, 0)
    out_shape = (
        jax.ShapeDtypeStruct((NSA_WIDTH, T), bf16),
        jax.ShapeDtypeStruct((T, KV_WIDTH), bf16),
        jax.ShapeDtypeStruct((T, KV_WIDTH), bf16),
        jax.ShapeDtypeStruct((T, 2 * KV_WIDTH), bf16),
        jax.ShapeDtypeStruct((T, KV_WIDTH), bf16),
        jax.ShapeDtypeStruct((NSA_KV_HEADS * V_EXT, T), bf16),
        jax.ShapeDtypeStruct((NSA_KV_HEADS * V_EXT, T), bf16),
        jax.ShapeDtypeStruct((32, T), f32),
        jax.ShapeDtypeStruct((T, CONV_WIDTH), bf16),
    )
    out_specs = (
        pl.BlockSpec((NSA_WIDTH, tm), col),
        pl.BlockSpec((tm, KV_WIDTH), row), pl.BlockSpec((tm, KV_WIDTH), row),
        pl.BlockSpec((tm, 2 * KV_WIDTH), row), pl.BlockSpec((tm, KV_WIDTH), row),
        pl.BlockSpec((NSA_KV_HEADS * V_EXT, tm), col), pl.BlockSpec((NSA_KV_HEADS * V_EXT, tm), col),
        pl.BlockSpec((32, tm), col),
        pl.BlockSpec((tm, CONV_WIDTH), row),
    )
    in_specs = [
        pl.BlockSpec((tm, D_MODEL), row),
        pl.BlockSpec((1, D_MODEL), const),
        pl.BlockSpec((D_MODEL, n_row), const),
        pl.BlockSpec((n_t, D_MODEL), const),
        pl.BlockSpec((tm, KV_WIDTH), row), pl.BlockSpec((tm, KV_WIDTH), row),
        pl.BlockSpec((HEAD_DIM // 2, tm), col), pl.BlockSpec((HEAD_DIM // 2, tm), col),
        pl.BlockSpec((SUBLANES, CONV_WIDTH), const),
        pl.BlockSpec((1, CONV_WIDTH), const),
    ]
    return pl.pallas_call(
        functools.partial(_inproj_kernel, tiles_per_seq=seq // tm),
        grid=(T // tm,), in_specs=in_specs, out_specs=out_specs, out_shape=out_shape,
        scratch_shapes=[pltpu.VMEM((tm + 2 * SUBLANES, CONV_WIDTH), f32)],
        compiler_params=_cparams(1), name="inproj",
    )(x2, g_mix, w_row, w_t, cosr, sinr, cost, sint, conv_w8, g_conv)


def _compress_kernel(xk_ref, xv_ref, pk_ref, pv_ref, w1k_ref, w1v_ref, b1k_ref, b1v_ref,
                     w2k_ref, w2vT_ref, kcc_ref, vcT_ref):
    ncp = xk_ref.shape[1]

    def hidden(x_ref, p_ref, w1_ref, b1_ref):
        x = x_ref[0].astype(f32)
        lo = (x + p_ref[0:1, :]).astype(bf16)
        hi = (x + p_ref[1:2, :]).astype(bf16)
        a = jnp.dot(lo, w1_ref[0], preferred_element_type=f32)
        b = jnp.dot(hi, w1_ref[1], preferred_element_type=f32)
        pre = a + pltpu.roll(b, ncp - 1, 0) + b1_ref[...]
        return jax.nn.gelu(pre).astype(bf16)

    hk = hidden(xk_ref, pk_ref, w1k_ref, b1k_ref)
    hv = hidden(xv_ref, pv_ref, w1v_ref, b1v_ref)
    for g in range(NSA_KV_HEADS):
        sl = slice(g * CMP_HIDDEN, (g + 1) * CMP_HIDDEN)
        kcc_ref[0, g] = jnp.dot(hk[:, sl], w2k_ref[...], preferred_element_type=f32).astype(bf16)
        vcT_ref[0, g] = lax.dot_general(w2vT_ref[...], hv[:, sl], (((1,), (1,)), ((), ())),
                                        preferred_element_type=f32).astype(bf16)


def _compress(kc_rows, vc_rows, pk, pv, w1k, w1v, b1k, b1v, w2k, w2vT, batch, seq):
    ncp = seq // CMP_STRIDE
    wide = CMP_STRIDE * KV_WIDTH
    xk = kc_rows.reshape(batch, ncp, wide)
    xv = vc_rows.reshape(batch, ncp, wide)
    c2 = lambda b: (0, 0)
    c3 = lambda b: (0, 0, 0)
    in_specs = [
        pl.BlockSpec((1, ncp, wide), lambda b: (b, 0, 0)),
        pl.BlockSpec((1, ncp, wide), lambda b: (b, 0, 0)),
        pl.BlockSpec((SUBLANES, wide), c2), pl.BlockSpec((SUBLANES, wide), c2),
        pl.BlockSpec((2, wide, 2 * CMP_HIDDEN), c3), pl.BlockSpec((2, wide, 2 * CMP_HIDDEN), c3),
        pl.BlockSpec((1, 2 * CMP_HIDDEN), c2), pl.BlockSpec((1, 2 * CMP_HIDDEN), c2),
        pl.BlockSpec((CMP_HIDDEN, HEAD_DIM), c2), pl.BlockSpec((HEAD_DIM, CMP_HIDDEN), c2),
    ]
    out_shape = (jax.ShapeDtypeStruct((batch, NSA_KV_HEADS, ncp, HEAD_DIM), bf16),
                 jax.ShapeDtypeStruct((batch, NSA_KV_HEADS, HEAD_DIM, ncp), bf16))
    out_specs = (pl.BlockSpec((1, NSA_KV_HEADS, ncp, HEAD_DIM), lambda b: (b, 0, 0, 0)),
                 pl.BlockSpec((1, NSA_KV_HEADS, HEAD_DIM, ncp), lambda b: (b, 0, 0, 0)))
    return pl.pallas_call(
        _compress_kernel, grid=(batch,), in_specs=in_specs, out_specs=out_specs,
        out_shape=out_shape, compiler_params=_cparams(1), name="compress",
    )(xk, xv, pk, pv, w1k, w1v, b1k, b1v, w2k, w2vT)


_NQ = HPG * TQ
_COL_BLOCKS = [slice(c * LANES, (c + 1) * LANES) for c in range(_NQ // LANES)]


def _compressed_branch(q4, kcc, vcT, ov, s0):
    ncp = kcc.shape[0]
    ns = ov.shape[0]
    s_lane = s0 + (lax.broadcasted_iota(i32, (1, _NQ), 1) & (TQ - 1))
    sc = jnp.dot(kcc, q4, preferred_element_type=f32)
    yield
    c_end = lax.broadcasted_iota(i32, (ncp, 1), 0) * CMP_STRIDE + (CMP_LEN - 1)
    blocks = []
    for cs in _COL_BLOCKS:
        cmask = c_end <= s_lane[:, cs]
        scm = jnp.where(cmask, sc[:, cs], NEG_INF)
        e_c = jnp.where(cmask, jnp.exp(scm - jnp.max(scm, axis=0, keepdims=True)), 0.0)
        l_c = jnp.sum(e_c, axis=0, keepdims=True)
        blocks.append(e_c * jnp.where(l_c > 0.0, 1.0 / l_c, 0.0))
    p_c = jnp.concatenate(blocks, axis=1)
    o_cmp = jnp.dot(vcT, p_c.astype(bf16), preferred_element_type=f32)
    yield

    ps = p_c[:, 0:TQ]
    for h in range(1, HPG):
        ps = ps + p_c[:, h * TQ:(h + 1) * TQ]
    p_hi = ps.astype(bf16)
    r1 = ps - p_hi.astype(f32)
    p_mid = r1.astype(bf16)
    p_lo = (r1 - p_mid.astype(f32)).astype(bf16)
    imp = (jnp.dot(ov, p_hi, preferred_element_type=f32) + jnp.dot(ov, p_mid, preferred_element_type=f32)
           + jnp.dot(ov, p_lo, preferred_element_type=f32))
    yield
    j_blk = lax.broadcasted_iota(i32, (ns, 1), 0)
    cur = (s0 + lax.broadcasted_iota(i32, (1, TQ), 1)) // SEL_LEN
    forced = (j_blk == 0) | (j_blk == cur) | (j_blk == cur - 1)
    imp = jnp.where(forced, FORCED, jnp.where(j_blk > cur, NEG_INF, imp))
    rank = jnp.zeros((ns, TQ), f32)
    for i in range(ns):
        row = imp[i:i + 1, :]
        tie_before = jnp.where(j_blk > i, 1.0, 0.0)
        rank = rank + jnp.where(row > imp, 1.0, jnp.where(row == imp, tie_before, 0.0))
    sel_bias = jnp.where(rank < float(min(N_SEL, ns)), 0.0, NEG_INF).astype(bf16)
    return o_cmp, sel_bias


def _attend(tiles, q_op):
    m = jnp.full((1, _NQ), NEG_INF, f32)
    acc = jnp.zeros((V_EXT, _NQ), f32)
    s_next = jnp.dot(tiles[0][0](), q_op, preferred_element_type=f32)
    for t, (_, values_t, bias) in enumerate(tiles):
        sT = s_next
        if t + 1 < len(tiles):
            s_next = jnp.dot(tiles[t + 1][0](), q_op, preferred_element_type=f32)
        yield
        p_blocks, m_blocks, a_blocks = [], [], []
        for c, cs in enumerate(_COL_BLOCKS):
            s = sT[:, cs]
            if bias is not None:
                b0 = (c % (TQ // LANES)) * LANES
                s = s + bias[:, b0:b0 + LANES]
            m_o = m[:, cs]
            m_n = jnp.maximum(m_o, jnp.max(s, axis=0, keepdims=True))
            p_blocks.append(jnp.exp((s - m_n).astype(bf16)))
            a_blocks.append(jnp.exp(m_o - m_n))
            m_blocks.append(m_n)
        m = jnp.concatenate(m_blocks, axis=1)
        pv = jnp.dot(values_t(), jnp.concatenate(p_blocks, axis=1), preferred_element_type=f32)
        yield
        acc = acc * jnp.concatenate(a_blocks, axis=1) + pv
    return acc[0:HEAD_DIM, :] * (1.0 / acc[HEAD_DIM:HEAD_DIM + 1, :])


def _nsa_pair(pi, nqt, g, qa_ref, qb_ref, ks_ref, kw_ref, vsT_ref, vwT_ref, kcc_ref, vcT_ref, ga_ref, gb_ref,
              ovT_ref, cbias_ref, wbias_ref, o_ref):
    q_tiles = (pi, nqt - 1 - pi)
    ns = ovT_ref.shape[0]
    kcc = kcc_ref[0, 0]
    vcT = vcT_ref[0, 0]
    ov = ovT_ref[...]
    cbias = cbias_ref[...]
    n_back = WINDOW // TK

    def tile(k_ref, vT_ref, kt, bias):
        return (lambda: k_ref[kt * TK:(kt + 1) * TK, :], lambda: vT_ref[:, kt * TK:(kt + 1) * TK], bias)

    def query_tile(slot, q_ref, g_ref, qt):
        q4 = jnp.concatenate([q_ref[h * HEAD_DIM:(h + 1) * HEAD_DIM, :] for h in range(HPG)], axis=1)
        zeros = jnp.zeros_like(q4)
        q_win = jnp.where(g == 0, jnp.concatenate([q4, zeros], axis=0), jnp.concatenate([zeros, q4], axis=0))

        o_win = yield from _attend(
            [tile(kw_ref, vwT_ref, qt - j, cbias if j == 0 else (wbias_ref[...] if j == n_back else None))
             for j in range(min(n_back, qt) + 1)], q_win)

        o_cmp, sel_bias = yield from _compressed_branch(q4, kcc, vcT, ov, qt * TQ)

        q_sel = jnp.concatenate([q_win, jnp.concatenate([sel_bias] * HPG, axis=1),
                                 jnp.zeros((KV_WIDTH - ns, _NQ), bf16)], axis=0)
        o_slc = yield from _attend(
            [tile(ks_ref, vsT_ref, qt, cbias)] + [tile(ks_ref, vsT_ref, kt, None) for kt in range(qt)], q_sel)

        gates = g_ref[...]
        for h in range(HPG):
            sl = slice(h * TQ, (h + 1) * TQ)
            o = (gates[3 * h:3 * h + 1, :] * o_cmp[:, sl] + gates[3 * h + 1:3 * h + 2, :] * o_slc[:, sl]
                 + gates[3 * h + 2:3 * h + 3, :] * o_win[:, sl])
            o_ref[h * HEAD_DIM:(h + 1) * HEAD_DIM, slot * TQ:(slot + 1) * TQ] = o.astype(bf16)

    _interleave([query_tile(slot, q_ref, g_ref, qt)
                 for slot, (q_ref, g_ref, qt) in enumerate(zip((qa_ref, qb_ref), (ga_ref, gb_ref), q_tiles))])


def _nsa_kernel(*refs, nqt):
    for pi in range(nqt // 2):
        @pl.when(pl.program_id(0) == pi)
        def _():
            _nsa_pair(pi, nqt, pl.program_id(2), *refs)


def _nsa_tile_position(qt, nqt):
    return jnp.where(qt < nqt // 2, 2 * qt, 2 * (nqt - 1 - qt) + 1)


def _nsa(qT, ks, kw, vsT, vwT, kcc, vcT, gT, ovT, batch, seq):
    T = batch * seq
    nqt = seq // TQ
    ncp = kcc.shape[2]
    ns = seq // SEL_LEN
    nq = HPG * TQ
    gw = HPG * HEAD_DIM
    assert TQ == TK and WINDOW % TK == 0 and ns <= KV_WIDTH
    assert nqt % 2 == 0 and nqt // 2 >= WINDOW // TK
    kl = jnp.arange(TK)[:, None]
    ql = jnp.arange(TQ)[None, :]
    cbias = jnp.where(kl <= ql, 0.0, NEG_INF).astype(f32)
    wbias = jnp.where(kl > ql, 0.0, NEG_INF).astype(f32)
    amap = lambda p, b, g: (g, b * nqt + p)
    bmap = lambda p, b, g: (g, b * nqt + nqt - 1 - p)
    const = lambda p, b, g: (0, 0)
    in_specs = [
        pl.BlockSpec((gw, TQ), amap), pl.BlockSpec((gw, TQ), bmap),
        pl.BlockSpec((seq, 2 * KV_WIDTH), lambda p, b, g: (b, 0)),
        pl.BlockSpec((seq, KV_WIDTH), lambda p, b, g: (b, 0)),
        pl.BlockSpec((V_EXT, seq), lambda p, b, g: (g, b)),
        pl.BlockSpec((V_EXT, seq), lambda p, b, g: (g, b)),
        pl.BlockSpec((1, 1, ncp, HEAD_DIM), lambda p, b, g: (b, g, 0, 0)),
        pl.BlockSpec((1, 1, HEAD_DIM, ncp), lambda p, b, g: (b, g, 0, 0)),
        pl.BlockSpec((16, TQ), amap), pl.BlockSpec((16, TQ), bmap),
        pl.BlockSpec((ns, ncp), const),
        pl.BlockSpec((TK, TQ), const),
        pl.BlockSpec((TK, TQ), const),
    ]
    return pl.pallas_call(
        functools.partial(_nsa_kernel, nqt=nqt), grid=(nqt // 2, batch, NSA_KV_HEADS), in_specs=in_specs,
        out_specs=pl.BlockSpec((gw, 2 * TQ), lambda p, b, g: (g, b * (nqt // 2) + p)),
        out_shape=jax.ShapeDtypeStruct((NSA_WIDTH, T), bf16),
        compiler_params=_cparams(3), name="nsa",
    )(qT, qT, ks, kw, vsT, vwT, kcc, vcT, gT, gT, ovT, cbias, wbias)


def _memkv_kernel(mem_ref, g_ref, wkT_ref, wv_ref, kT_ref, v_ref):
    mb = _rms(mem_ref[0], g_ref[...]).astype(bf16)
    kT_ref[0] = lax.dot_general(wkT_ref[...], mb, (((1,), (1,)), ((), ())),
                                preferred_element_type=f32).astype(bf16)
    v_ref[0] = jnp.dot(mb, wv_ref[...], preferred_element_type=f32).astype(bf16)


def _memkv(mem, g_mem, wkT, wv):
    batch, n_mem, _ = mem.shape
    c2 = lambda b: (0, 0)
    return pl.pallas_call(
        _memkv_kernel, grid=(batch,),
        in_specs=[pl.BlockSpec((1, n_mem, D_MODEL), lambda b: (b, 0, 0)),
                  pl.BlockSpec((1, D_MODEL), c2),
                  pl.BlockSpec((D_MODEL, D_MODEL), c2), pl.BlockSpec((D_MODEL, D_MODEL), c2)],
        out_specs=(pl.BlockSpec((1, D_MODEL, n_mem), lambda b: (b, 0, 0)),
                   pl.BlockSpec((1, n_mem, D_MODEL), lambda b: (b, 0, 0))),
        out_shape=(jax.ShapeDtypeStruct((batch, D_MODEL, n_mem), bf16),
                   jax.ShapeDtypeStruct((batch, n_mem, D_MODEL), bf16)),
        compiler_params=_cparams(1), name="memkv",
    )(mem, g_mem, wkT, wv)


def _post_kernel(x_ref, oTa_ref, oTb_ref, conv_ref, gnsa_ref, wout_ref, gx_ref, wq_ref, kT_ref, v_ref, wo_ref,
                 gmoe_ref, wr_ref, br_ref,
                 h2_ref, xn2_ref, eidx_ref, gate_ref, sel_ref):
    stages = []
    for sub, oT_ref in enumerate((oTa_ref, oTb_ref)):
        rows = slice(sub * TQ, (sub + 1) * TQ)
        stages.append(_post_rows(
            x_ref.at[rows], oT_ref, conv_ref.at[rows], gnsa_ref, wout_ref, gx_ref, wq_ref, kT_ref, v_ref,
            wo_ref, gmoe_ref, wr_ref, br_ref, h2_ref.at[rows],
            xn2_ref.at[sub * TQ * ROW_SUB:(sub + 1) * TQ * ROW_SUB], eidx_ref.at[rows], gate_ref.at[rows],
            sel_ref.at[rows]))
    _interleave(stages)


_DONE = object()


def _interleave(generators):
    live = list(generators)
    while live:
        live = [g for g in live if next(g, _DONE) is not _DONE]


def _post_rows(x_ref, oT_ref, conv_ref, gnsa_ref, wout_ref, gx_ref, wq_ref, kT_ref, v_ref, wo_ref,
               gmoe_ref, wr_ref, br_ref,
               h2_ref, xn2_ref, eidx_ref, gate_ref, sel_ref):
    tm = x_ref.shape[0]
    oT = oT_ref[...].astype(f32)
    onT = (oT * lax.rsqrt(jnp.mean(oT * oT, axis=0, keepdims=True) + RMS_EPS) * gnsa_ref[...]).astype(bf16)
    mix = lax.dot_general(onT, wout_ref[0:NSA_WIDTH, :], (((0,), (0,)), ((), ())),
                          preferred_element_type=f32)
    mix = mix + jnp.dot(conv_ref[...], wout_ref[NSA_WIDTH:D_MODEL, :], preferred_element_type=f32)
    yield
    h1 = x_ref[...] + mix

    hn = _rms(h1, gx_ref[...]).astype(bf16)
    q = (jnp.dot(hn, wq_ref[...], preferred_element_type=f32) * (XATTN_HEAD_DIM ** -0.5)).astype(bf16)
    yield
    head_slices = [slice(h * XATTN_HEAD_DIM, (h + 1) * XATTN_HEAD_DIM) for h in range(XATTN_HEADS)]
    scores = [jnp.dot(q[:, sl], kT_ref[0, sl, :], preferred_element_type=f32) for sl in head_slices]
    yield
    heads = []
    for s, sl in zip(scores, head_slices):
        e = jnp.exp(s - jnp.max(s, axis=-1, keepdims=True))
        p = e * (1.0 / jnp.sum(e, axis=-1, keepdims=True))
        heads.append(jnp.dot(p.astype(bf16), v_ref[0, :, sl], preferred_element_type=f32))
    yield
    o = jnp.concatenate(heads, axis=1).astype(bf16)
    h2 = h1 + jnp.dot(o, wo_ref[...], preferred_element_type=f32)
    yield
    h2_ref[...] = h2

    xn2 = _rms(h2, gmoe_ref[...])
    for s_ in range(ROW_SUB):
        xn2_ref[pl.ds(s_, tm, stride=ROW_SUB), :] = xn2[:, s_ * LANES:(s_ + 1) * LANES]

    logits = jnp.dot(xn2.astype(bf16), wr_ref[...], preferred_element_type=f32) + br_ref[...]
    yield
    lane = lax.broadcasted_iota(i32, (tm, LANES), 1)
    work = logits
    sel = jnp.zeros((tm, LANES), f32)
    eidx = jnp.zeros((tm, LANES), i32)
    vals = []
    for k in range(TOP_K):
        mk = jnp.max(work, axis=-1, keepdims=True)
        ik = jnp.min(jnp.where(work == mk, lane, LANES), axis=-1, keepdims=True)
        hit = lane == ik
        work = jnp.where(hit, -jnp.inf, work)
        sel = jnp.where(hit, 1.0, sel)
        eidx = jnp.where(lane == k, ik, eidx)
        vals.append(mk)
    es = [jnp.exp(v - vals[0]) for v in vals]
    den = es[0]
    for e in es[1:]:
        den = den + e
    gate = jnp.zeros((tm, LANES), f32)
    for k in range(TOP_K):
        gate = jnp.where(lane == k, es[k] / den, gate)
    eidx_ref[...] = eidx
    gate_ref[...] = gate
    sel_ref[...] = sel.astype(bf16)


def _post(x2, oT, conv_n, g_nsa_col, w_out, g_x, w_q, kT, v, w_o, g_moe, w_r, b_r, seq):
    T = x2.shape[0]
    tm = TM_POST
    assert tm == 2 * TQ
    tps = seq // tm
    nqt = seq // TQ
    n_mem = v.shape[1]
    row = lambda i: (i, 0)
    const = lambda i: (0, 0)

    def o_tile(sub):
        return lambda i: (0, (i // tps) * nqt + _nsa_tile_position(2 * (i % tps) + sub, nqt))

    in_specs = [
        pl.BlockSpec((tm, D_MODEL), row),
        pl.BlockSpec((NSA_WIDTH, TQ), o_tile(0)), pl.BlockSpec((NSA_WIDTH, TQ), o_tile(1)),
        pl.BlockSpec((tm, CONV_WIDTH), row),
        pl.BlockSpec((NSA_WIDTH, 1), const),
        pl.BlockSpec((D_MODEL, D_MODEL), const),
        pl.BlockSpec((1, D_MODEL), const),
        pl.BlockSpec((D_MODEL, D_MODEL), const),
        pl.BlockSpec((1, D_MODEL, n_mem), lambda i: (i // tps, 0, 0)),
        pl.BlockSpec((1, n_mem, D_MODEL), lambda i: (i // tps, 0, 0)),
        pl.BlockSpec((D_MODEL, D_MODEL), const),
        pl.BlockSpec((1, D_MODEL), const),
        pl.BlockSpec((D_MODEL, LANES), const),
        pl.BlockSpec((1, LANES), const),
    ]
    out_shape = (jax.ShapeDtypeStruct((T, D_MODEL), f32),
                 jax.ShapeDtypeStruct((T * ROW_SUB, LANES), f32),
                 jax.ShapeDtypeStruct((T, LANES), i32),
                 jax.ShapeDtypeStruct((T, LANES), f32),
                 jax.ShapeDtypeStruct((T, LANES), bf16))
    out_specs = (pl.BlockSpec((tm, D_MODEL), row),
                 pl.BlockSpec((tm * ROW_SUB, LANES), row),
                 pl.BlockSpec((tm, LANES), row), pl.BlockSpec((tm, LANES), row),
                 pl.BlockSpec((tm, LANES), row))
    return pl.pallas_call(
        _post_kernel, grid=(T // tm,), in_specs=in_specs, out_specs=out_specs, out_shape=out_shape,
        compiler_params=_cparams(1), name="post",
    )(x2, oT, oT, conv_n, g_nsa_col, w_out, g_x, w_q, kT, v, w_o, g_moe, w_r, b_r)


def _count_kernel(sel_ref, cum_ref, cnt_ref, carry):
    i = pl.program_id(0)
    tm = sel_ref.shape[0]

    @pl.when(i == 0)
    def _():
        carry[...] = jnp.zeros_like(carry)

    sel = sel_ref[...]
    r = lax.broadcasted_iota(i32, (tm, tm), 0)
    c = lax.broadcasted_iota(i32, (tm, tm), 1)
    strict_lower = jnp.where(c < r, 1.0, 0.0).astype(bf16)
    base = carry[0:1, :]
    cum_ref[...] = jnp.dot(strict_lower, sel, preferred_element_type=f32) + base
    total = base + jnp.sum(sel.astype(f32), axis=0, keepdims=True)
    carry[...] = jnp.broadcast_to(total, carry.shape)
    cnt_ref[...] = jnp.broadcast_to(total, cnt_ref.shape)


def _count(sel):
    T = sel.shape[0]
    tm = TM_ROUTE
    return pl.pallas_call(
        _count_kernel, grid=(T // tm,),
        in_specs=[pl.BlockSpec((tm, LANES), lambda i: (i, 0))],
        out_specs=(pl.BlockSpec((tm, LANES), lambda i: (i, 0)),
                   pl.BlockSpec((SUBLANES, LANES), lambda i: (0, 0))),
        out_shape=(jax.ShapeDtypeStruct((T, LANES), f32), jax.ShapeDtypeStruct((SUBLANES, LANES), f32)),
        scratch_shapes=[pltpu.VMEM((SUBLANES, LANES), f32)],
        compiler_params=_cparams(1), name="route_count",
    )(sel)


def _dest_kernel(cum_ref, eidx_ref, pstart_ref, dest_ref):
    tm = cum_ref.shape[0]
    lane = lax.broadcasted_iota(i32, (tm, LANES), 1)
    row_of = cum_ref[...] + pstart_ref[0:1, :]
    eidx = eidx_ref[...]
    dest = jnp.zeros((tm, LANES), f32)
    for k in range(TOP_K):
        ek = eidx[:, k:k + 1]
        dk = jnp.sum(jnp.where(lane == ek, row_of, 0.0), axis=-1, keepdims=True)
        dest = jnp.where(lane == k, dk, dest)
    dest_ref[...] = dest.astype(i32)


def _dest(cum, eidx, pstart8):
    T = cum.shape[0]
    tm = TM_ROUTE
    row = lambda i: (i, 0)
    return pl.pallas_call(
        _dest_kernel, grid=(T // tm,),
        in_specs=[pl.BlockSpec((tm, LANES), row), pl.BlockSpec((tm, LANES), row),
                  pl.BlockSpec((SUBLANES, LANES), lambda i: (0, 0))],
        out_specs=pl.BlockSpec((tm, LANES), row),
        out_shape=jax.ShapeDtypeStruct((T, LANES), i32),
        compiler_params=_cparams(1), name="route_dest",
    )(cum, eidx, pstart8)


def _invert_kernel(dest_ref, init_ref, slot_ref, sem):
    cp = pltpu.make_async_copy(init_ref, slot_ref, sem)
    cp.start()
    cp.wait()
    chunk = LANES

    def body(j, carry):
        base = j * chunk
        for l in range(chunk):
            slot_ref[dest_ref[base + l]] = base + l
        return carry

    lax.fori_loop(0, dest_ref.shape[0] // chunk, body, 0)


def _invert(dest_flat, n_steps):
    parity = jnp.concatenate([jnp.arange(n_steps, dtype=i32) % 2, jnp.ones((1,), i32)])
    sink = dest_flat.shape[0] + parity[:, None] * BM + jnp.arange(BM, dtype=i32)[None, :]
    return pl.pallas_call(
        _invert_kernel,
        in_specs=[pl.BlockSpec(memory_space=pltpu.SMEM), pl.BlockSpec(memory_space=pltpu.VMEM)],
        out_specs=pl.BlockSpec(memory_space=pltpu.SMEM),
        out_shape=jax.ShapeDtypeStruct(((n_steps + 1) * BM,), i32),
        scratch_shapes=[pltpu.SemaphoreType.DMA],
        compiler_params=pltpu.CompilerParams(vmem_limit_bytes=VMEM_LIMIT), name="route_invert",
    )(dest_flat, sink.reshape(-1))


def _ffn_kernel(blk_e_ref, first_ref, nxt_e_ref, wslot_ref, nused_ref, slot_ref,
                x_hbm, wgu_hbm, bgu_ref, wd_hbm, bd_ref, y_hbm,
                xbuf0, xbuf1, obuf0, obuf1, wgu_f, wd_f, wgu_bf, wd_bf, gsem, ssem, wsem, *, n_tok, sink_row):
    i = pl.program_id(0)
    nused = nused_ref[0]
    n_steps = pl.num_programs(0)
    xbuf = (xbuf0, xbuf1)
    obuf = (obuf0, obuf1)

    def row_window(ref, row):
        return ref.at[pl.ds(pl.multiple_of(row * ROW_SUB, SUBLANES), ROW_SUB)]

    def gather(blk, par, r):
        tok = slot_ref[blk * BM + r] & (n_tok - 1)
        return pltpu.make_async_copy(row_window(x_hbm, tok), xbuf[par].at[pl.ds(r * ROW_SUB, ROW_SUB)],
                                     gsem.at[par])

    def scatter(blk, par, r):
        return pltpu.make_async_copy(obuf[par].at[pl.ds(r * ROW_SUB, ROW_SUB)],
                                     row_window(y_hbm, slot_ref[blk * BM + r]), ssem.at[par])

    def wait_gather(par):
        pltpu.make_async_copy(x_hbm.at[pl.ds(0, BM * ROW_SUB)], xbuf[par], gsem.at[par]).wait()

    def wait_scatter(par):
        pltpu.make_async_copy(obuf[par], y_hbm.at[pl.ds(0, BM * ROW_SUB)], ssem.at[par]).wait()

    def weight_copies(e, ws):
        return (pltpu.make_async_copy(wgu_hbm.at[e], wgu_f.at[ws], wsem.at[ws, 0]),
                pltpu.make_async_copy(wd_hbm.at[e], wd_f.at[ws], wsem.at[ws, 1]))

    @pl.when(i == 0)
    def _():
        obuf0[...] = jnp.zeros(obuf0.shape, f32)
        obuf1[...] = jnp.zeros(obuf1.shape, f32)
        pltpu.make_async_copy(obuf0, y_hbm.at[pl.ds(sink_row * ROW_SUB, BM * ROW_SUB)], ssem.at[0]).start()
        for cp in weight_copies(blk_e_ref[0], 0):
            cp.start(priority=1)
        for r in range(BM):
            gather(0, 0, r).start()

    @pl.when((i < nused) & (first_ref[i] == 1))
    def _():
        ws = wslot_ref[i]
        for cp in weight_copies(blk_e_ref[i], ws):
            cp.wait()

        @pl.when(nxt_e_ref[i] >= 0)
        def _():
            for cp in weight_copies(nxt_e_ref[i], 1 - ws):
                cp.start(priority=1)

        wgu_bf[...] = wgu_f[ws].astype(bf16)
        wd_bf[...] = wd_f[ws].astype(bf16)

    def block(par):
        prev = jnp.where(i == 0, n_steps, i - 1)
        wait_gather(par)
        for r in range(BM):
            gather(i + 1, 1 - par, r).start()
        for r in range(BM):
            scatter(prev, 1 - par, r).start(priority=1)
        x = jnp.concatenate([xbuf[par][pl.ds(s_, BM, stride=ROW_SUB), :] for s_ in range(ROW_SUB)],
                            axis=1).astype(bf16)
        gu = jnp.dot(x, wgu_bf[...], preferred_element_type=f32) + bgu_ref[0]
        gg = jnp.minimum(gu[:, 0:D_FF], SWIGLU_LIMIT)
        uu = jnp.clip(gu[:, D_FF:2 * D_FF], -SWIGLU_LIMIT, SWIGLU_LIMIT)
        hmid = (uu + 1.0) * (gg * jax.nn.sigmoid(SWIGLU_ALPHA * gg))
        out = jnp.dot(hmid.astype(bf16), wd_bf[...], preferred_element_type=f32) + bd_ref[0]
        wait_scatter(par)
        for s_ in range(ROW_SUB):
            obuf[par][pl.ds(s_, BM, stride=ROW_SUB), :] = out[:, s_ * LANES:(s_ + 1) * LANES]

    for par in range(2):
        @pl.when((i < nused) & (i % 2 == par))
        def _():
            block(par)

        @pl.when((i == nused) & (i % 2 == par))
        def _():
            for r in range(BM):
                scatter(i - 1, 1 - par, r).start(priority=1)
            wait_gather(par)
            wait_scatter(par)
            wait_scatter(1 - par)


def _ffn(blk_e, first, nxt_e, wslot, nused, slots, xn2_rows, w_gu, b_gu, w_d, b_d):
    n_steps = blk_e.shape[0]
    n_tok = xn2_rows.shape[0] // ROW_SUB
    assert n_tok & (n_tok - 1) == 0
    n_tok_rows = n_tok * TOP_K
    emap = lambda i, be, *_: (be[i], 0, 0)
    grid_spec = pltpu.PrefetchScalarGridSpec(
        num_scalar_prefetch=6, grid=(n_steps,),
        in_specs=[pl.BlockSpec(memory_space=pl.ANY),
                  pl.BlockSpec(memory_space=pl.ANY),
                  pl.BlockSpec((1, 1, 2 * D_FF), emap),
                  pl.BlockSpec(memory_space=pl.ANY),
                  pl.BlockSpec((1, 1, D_MODEL), emap)],
        out_specs=pl.BlockSpec(memory_space=pl.ANY),
        scratch_shapes=[pltpu.VMEM((BM * ROW_SUB, LANES), f32), pltpu.VMEM((BM * ROW_SUB, LANES), f32),
                        pltpu.VMEM((BM * ROW_SUB, LANES), f32), pltpu.VMEM((BM * ROW_SUB, LANES), f32),
                        pltpu.VMEM((2, D_MODEL, 2 * D_FF), f32), pltpu.VMEM((2, D_FF, D_MODEL), f32),
                        pltpu.VMEM((D_MODEL, 2 * D_FF), bf16), pltpu.VMEM((D_FF, D_MODEL), bf16),
                        pltpu.SemaphoreType.DMA((2,)), pltpu.SemaphoreType.DMA((2,)),
                        pltpu.SemaphoreType.DMA((2, 2))],
    )
    return pl.pallas_call(
        functools.partial(_ffn_kernel, n_tok=n_tok, sink_row=n_tok_rows), grid_spec=grid_spec,
        out_shape=jax.ShapeDtypeStruct(((n_tok_rows + 2 * BM) * ROW_SUB, LANES), f32),
        compiler_params=_cparams(1), name="ffn",
    )(blk_e, first, nxt_e, wslot, nused, slots, xn2_rows, w_gu, b_gu, w_d, b_d)


def _combine_kernel(*refs, final_norm):
    y_refs = refs[:TOP_K]
    gate_ref, h2_ref, gfin_ref, o_ref = refs[TOP_K:]
    tm = h2_ref.shape[0]
    gate = gate_ref[...]
    cols = []
    for s_ in range(ROW_SUB):
        acc = gate[:, 0:1] * y_refs[0][pl.ds(s_, tm, stride=ROW_SUB), :]
        for k in range(1, TOP_K):
            acc = acc + gate[:, k:k + 1] * y_refs[k][pl.ds(s_, tm, stride=ROW_SUB), :]
        cols.append(acc)
    h = h2_ref[...] + jnp.concatenate(cols, axis=1)
    if final_norm:
        h = _rms(h, gfin_ref[...])
    o_ref[...] = h


def _combine(y_rows, gate, h2, g_final, final_norm):
    T = h2.shape[0]
    tm = TM_ROW
    row = lambda i: (i, 0)
    planes = [pl.BlockSpec((tm * ROW_SUB, LANES), functools.partial(lambda i, k: (k * (T // tm) + i, 0), k=k))
              for k in range(TOP_K)]
    return pl.pallas_call(
        functools.partial(_combine_kernel, final_norm=final_norm), grid=(T // tm,),
        in_specs=planes + [pl.BlockSpec((tm, LANES), row), pl.BlockSpec((tm, D_MODEL), row),
                           pl.BlockSpec((1, D_MODEL), lambda i: (0, 0))],
        out_specs=pl.BlockSpec((tm, D_MODEL), row),
        out_shape=jax.ShapeDtypeStruct((T, D_MODEL), f32),
        compiler_params=_cparams(1), name="combine",
    )(*([y_rows] * TOP_K), gate, h2, g_final)


def _prep_inproj_weights(w_in):
    sizes = (NSA_WIDTH,) + (KV_WIDTH,) * 6 + (3 * NSA_HEADS,) + (CONV_WIDTH,) * 3
    offs = [0]
    for s in sizes:
        offs.append(offs[-1] + s)
    seg = lambda n: w_in[:, offs[n]:offs[n + 1]]
    q, kc, vc, ks, vs, kw, vw, gl, ch, cb, cc = (seg(n) for n in range(11))
    w_row = jnp.concatenate([kc, vc, ks, kw, ch, cb, cc], axis=1).astype(bf16)
    gl_g = gl.reshape(D_MODEL, NSA_KV_HEADS, HPG * 3)
    gl_g = jnp.pad(gl_g, ((0, 0), (0, 0), (0, 16 - HPG * 3))).reshape(D_MODEL, NSA_KV_HEADS * 16)
    w_t = jnp.concatenate([q, vs, vw, gl_g], axis=1).T.astype(bf16)
    return w_row, w_t


def _rope_tables(positions):
    half = HEAD_DIM // 2
    inv_freq = ROPE_THETA ** (-jnp.arange(half, dtype=f32) / half)
    ang = positions.reshape(-1).astype(f32)[:, None] * inv_freq
    cos = jnp.cos(ang)
    sin = jnp.sin(ang)
    reps = KV_WIDTH // HEAD_DIM
    cosr = jnp.tile(cos, (1, 2 * reps))
    sinr = jnp.tile(jnp.concatenate([-sin, sin], axis=1), (1, reps))
    return cosr, sinr, cos.T, sin.T


def _cmp_weights(pos, w1, b1):
    w1r = w1.reshape(2, CMP_STRIDE, HEAD_DIM, CMP_HIDDEN)
    eye = jnp.eye(NSA_KV_HEADS, dtype=w1.dtype)
    big = jnp.einsum('aldh,gk->algdkh', w1r, eye)
    big = big.reshape(2, CMP_STRIDE * KV_WIDTH, NSA_KV_HEADS * CMP_HIDDEN).astype(bf16)
    p = jnp.broadcast_to(pos.reshape(2, CMP_STRIDE, 1, HEAD_DIM), (2, CMP_STRIDE, NSA_KV_HEADS, HEAD_DIM))
    p = jnp.pad(p.reshape(2, CMP_STRIDE * KV_WIDTH), ((0, SUBLANES - 2), (0, 0)))
    return big, p, jnp.tile(b1.reshape(1, CMP_HIDDEN), (1, NSA_KV_HEADS))


def _mixer_core(x2, tables, p, batch, seq):
    cosr, sinr, cost, sint = tables
    row1 = lambda v: v.reshape(1, -1)
    w_row, w_t = _prep_inproj_weights(p['w_mix_in'])
    conv_w8 = jnp.pad(p['conv_w'], ((0, SUBLANES - CONV_K), (0, 0)))
    qT, kc, vc, ks, kw, vsT, vwT, gT, conv_n = _inproj(
        x2, row1(p['g_mix_norm']), w_row, w_t, cosr, sinr, cost, sint, conv_w8, row1(p['g_conv_out']), seq)

    w1k, pk, b1k = _cmp_weights(p['cmp_pos_k'], p['cmp_w1_k'], p['cmp_b1_k'])
    w1v, pv, b1v = _cmp_weights(p['cmp_pos_v'], p['cmp_w1_v'], p['cmp_b1_v'])
    kcc, vcT = _compress(kc, vc, pk, pv, w1k, w1v, b1k, b1v,
                         p['cmp_w2_k'].astype(bf16), p['cmp_w2_v'].T.astype(bf16), batch, seq)

    ncp = seq // CMP_STRIDE
    ns = seq // SEL_LEN
    cs = jnp.arange(ncp) * CMP_STRIDE
    js = jnp.arange(ns) * SEL_LEN
    overlap = jnp.clip(jnp.minimum(cs[:, None] + CMP_LEN, js[None, :] + SEL_LEN)
                       - jnp.maximum(cs[:, None], js[None, :]), 0, None).astype(f32) / CMP_LEN
    ovT = overlap.T.astype(bf16)
    oT = _nsa(qT, ks, kw, vsT, vwT, kcc, vcT, gT, ovT, batch, seq)
    return oT, conv_n


def _layer(h, memf, tables, p, final_gain, final_norm):
    batch, seq, _ = h.shape
    T = batch * seq
    x2 = h.reshape(T, D_MODEL)
    row1 = lambda v: v.reshape(1, -1)
    oT, conv_n = _mixer_core(x2, tables, p, batch, seq)

    w_xkv = p['w_xkv']
    kT, v = _memkv(memf, row1(p['g_mem_norm']), w_xkv[:, :D_MODEL].T.astype(bf16),
                   w_xkv[:, D_MODEL:].astype(bf16))
    w_r = jnp.pad(p['w_router'], ((0, 0), (0, LANES - N_EXPERTS))).astype(bf16)
    b_r = jnp.pad(p['b_router'], (0, LANES - N_EXPERTS), constant_values=NEG_INF).reshape(1, LANES)
    h2, xn2_rows, eidx, gate, sel = _post(
        x2, oT, conv_n, p['g_nsa_out'].reshape(NSA_WIDTH, 1), p['w_mix_out'].astype(bf16),
        row1(p['g_xattn_norm']), p['w_xq'].astype(bf16), kT, v, p['w_xo'].astype(bf16),
        row1(p['g_moe_norm']), w_r, b_r, seq)

    cum, cnt = _count(sel)
    counts = cnt[0, :N_EXPERTS].astype(i32)
    padded = (counts + BM - 1) // BM * BM
    pend = jnp.cumsum(padded)
    pstart = pend - padded
    n_steps = (T * TOP_K) // BM + N_EXPERTS + 1
    nused = (pend[-1] // BM).astype(i32)
    step = jnp.arange(n_steps, dtype=i32)
    used = step < nused
    blk_raw = jnp.minimum(jnp.sum((pend[None, :] <= (step * BM)[:, None]).astype(i32), axis=1), N_EXPERTS - 1)
    blk_e = blk_raw[jnp.minimum(step, nused - 1)]
    first = (used & jnp.concatenate([jnp.ones((1,), bool), blk_e[1:] != blk_e[:-1]])).astype(i32)
    wslot = (jnp.cumsum(first) - 1) % 2
    e_ids = jnp.arange(N_EXPERTS, dtype=i32)
    later = (e_ids[None, :] > e_ids[:, None]) & (padded > 0)[None, :]
    nxt_of = jnp.min(jnp.where(later, e_ids[None, :], N_EXPERTS), axis=1)
    nxt_e = jnp.where(nxt_of < N_EXPERTS, nxt_of, -1).astype(i32)[blk_e]
    pstart8 = jnp.broadcast_to(jnp.pad(pstart.astype(f32), (0, LANES - N_EXPERTS))[None, :], (SUBLANES, LANES))

    dest = _dest(cum, eidx, pstart8)
    slots = _invert(dest[:, :TOP_K].T.reshape(-1), n_steps)
    y_rows = _ffn(blk_e, first, nxt_e, wslot.astype(i32), nused.reshape(1), slots, xn2_rows,
                  p['w_gate_up'], p['b_gate_up'].reshape(N_EXPERTS, 1, 2 * D_FF), p['w_down'],
                  p['b_down'].reshape(N_EXPERTS, 1, D_MODEL))
    out = _combine(y_rows, gate, h2, row1(final_gain), final_norm)
    return out.reshape(batch, seq, D_MODEL)


_LAYER_PARAMS = ('g_mix_norm', 'w_mix_in', 'cmp_pos_k', 'cmp_pos_v', 'cmp_w1_k', 'cmp_b1_k', 'cmp_w2_k',
                 'cmp_w1_v', 'cmp_b1_v', 'cmp_w2_v', 'conv_w', 'g_nsa_out', 'g_conv_out', 'w_mix_out',
                 'g_xattn_norm', 'g_mem_norm', 'w_xq', 'w_xkv', 'w_xo', 'g_moe_norm', 'w_router', 'b_router',
                 'w_gate_up', 'b_gate_up', 'w_down', 'b_down')


def kernel(x, mem, positions, g_mix_norm, w_mix_in, cmp_pos_k, cmp_pos_v, cmp_w1_k, cmp_b1_k, cmp_w2_k, cmp_w1_v, cmp_b1_v, cmp_w2_v, conv_w, g_nsa_out, g_conv_out, w_mix_out, g_xattn_norm, g_mem_norm, w_xq, w_xkv, w_xo, g_moe_norm, w_router, b_router, w_gate_up, b_gate_up, w_down, b_down, g_final):
    stacked = dict(zip(_LAYER_PARAMS, (g_mix_norm, w_mix_in, cmp_pos_k, cmp_pos_v, cmp_w1_k, cmp_b1_k, cmp_w2_k,
                                       cmp_w1_v, cmp_b1_v, cmp_w2_v, conv_w, g_nsa_out, g_conv_out, w_mix_out,
                                       g_xattn_norm, g_mem_norm, w_xq, w_xkv, w_xo, g_moe_norm, w_router,
                                       b_router, w_gate_up, b_gate_up, w_down, b_down)))
    depth = g_mix_norm.shape[0]
    tables = _rope_tables(positions)
    h = x
    for l in range(depth):
        p = {k: v[l] for k, v in stacked.items()}
        last = l == depth - 1
        h = _layer(h, mem, tables, p, g_final, final_norm=last)
    return h
```

```python
import functools

import jax
import jax.numpy as jnp
from jax import lax
from jax.experimental import pallas as pl
from jax.experimental.pallas import tpu as pltpu

f32 = jnp.float32
bf16 = jnp.bfloat16
i32 = jnp.int32

D_MODEL = 1024
HEAD_DIM = 64
NSA_HEADS = 8
NSA_KV_HEADS = 2
HPG = NSA_HEADS // NSA_KV_HEADS
NSA_WIDTH = NSA_HEADS * HEAD_DIM
KV_WIDTH = NSA_KV_HEADS * HEAD_DIM
CONV_WIDTH = D_MODEL - NSA_WIDTH
CONV_K = 3
CMP_LEN = 32
CMP_STRIDE = 16
CMP_HIDDEN = 256
SEL_LEN = 64
N_SEL = 16
WINDOW = 512
ROPE_THETA = 10000.0
XATTN_HEADS = 4
XATTN_HEAD_DIM = D_MODEL // XATTN_HEADS
N_EXPERTS = 32
TOP_K = 4
D_FF = D_MODEL
SWIGLU_LIMIT = 7.0
SWIGLU_ALPHA = 1.702
RMS_EPS = 1e-5
NEG_INF = -1e30
FORCED = 1e30

LANES = 128
SUBLANES = 8
VMEM_LIMIT = 56 * 1024 * 1024

TM_IN = 512
TQ = 256
TK = 256
TM_POST = 512
TM_ROUTE = 512
TM_ROW = 256
V_EXT = HEAD_DIM + 16
BM = 256
ROW_SUB =D_MODEL // LANES


def _cparams(n_axes, **kw):
    return pltpu.CompilerParams(dimension_semantics=("arbitrary",) * n_axes,
                                vmem_limit_bytes=VMEM_LIMIT, **kw)


def _rms(t, gain):
    return t * lax.rsqrt(jnp.mean(t * t, axis=-1, keepdims=True) + RMS_EPS) * gain


def _inproj_kernel(x_ref, g_ref, wr_ref, wt_ref, cosr_ref, sinr_ref, cost_ref, sint_ref,
                   convw_ref, gconv_ref,
                   qT_ref, kc_ref, vc_ref, ks_ref, kw_ref, vsT_ref, vwT_ref, gT_ref, conv_ref,
                   ubuf, *, tiles_per_seq):
    i = pl.program_id(0)
    tm = x_ref.shape[0]
    xb = _rms(x_ref[...], g_ref[...]).astype(bf16)

    c0 = 4 * KV_WIDTH
    pr = jnp.dot(xb, wr_ref[:, 0:c0], preferred_element_type=f32)
    pc = jnp.dot(xb, wr_ref[:, c0:c0 + 3 * CONV_WIDTH], preferred_element_type=f32)
    pt = lax.dot_general(wt_ref[...], xb, (((1,), (1,)), ((), ())), preferred_element_type=f32)
    cosr = cosr_ref[...]
    sinr = sinr_ref[...]
    lane = lax.broadcasted_iota(i32, (tm, KV_WIDTH), 1)
    first_half = (lane & (HEAD_DIM - 1)) < HEAD_DIM // 2

    def rope_rows(t):
        rot = jnp.where(first_half, pltpu.roll(t, KV_WIDTH - HEAD_DIM // 2, 1),
                        pltpu.roll(t, HEAD_DIM // 2, 1))
        return t * cosr + rot * sinr

    kc_ref[...] = rope_rows(pr[:, 0:KV_WIDTH])
    vc_ref[...] = pr[:, KV_WIDTH:2 * KV_WIDTH]
    ks_ref[:, 0:KV_WIDTH] = rope_rows(pr[:, 2 * KV_WIDTH:3 * KV_WIDTH]).astype(bf16)
    tok = (i % tiles_per_seq) * tm + lax.broadcasted_iota(i32, (tm, KV_WIDTH), 0)
    ks_ref[:, KV_WIDTH:2 * KV_WIDTH] = jnp.where(lane == tok // SEL_LEN, 1.0, 0.0).astype(bf16)
    kw_ref[...] = rope_rows(pr[:, 3 * KV_WIDTH:4 * KV_WIDTH]).astype(bf16)

    ch = pc[:, 0:CONV_WIDTH]
    cb = pc[:, CONV_WIDTH:2 * CONV_WIDTH]
    cc = pc[:, 2 * CONV_WIDTH:3 * CONV_WIDTH]
    u = cc * ch

    @pl.when(i % tiles_per_seq == 0)
    def _():
        ubuf[0:SUBLANES, :] = jnp.zeros((SUBLANES, CONV_WIDTH), f32)

    @pl.when(i % tiles_per_seq != 0)
    def _():
        ubuf[0:SUBLANES, :] = ubuf[tm:tm + SUBLANES, :]

    ubuf[SUBLANES:SUBLANES + tm, :] = u
    u1 = ubuf[SUBLANES - 1:SUBLANES - 1 + tm, :]
    u2 = ubuf[SUBLANES - 2:SUBLANES - 2 + tm, :]
    w = convw_ref[...]
    y = cb * (w[0:1, :] * u2 + w[1:2, :] * u1 + w[2:3, :] * u)
    conv_ref[...] = _rms(y, gconv_ref[...]).astype(bf16)

    cost = cost_ref[...]
    sint = sint_ref[...]
    half = HEAD_DIM // 2
    scale = HEAD_DIM ** -0.5
    for h in range(NSA_HEADS):
        t1 = pt[h * HEAD_DIM:h * HEAD_DIM + half, :]
        t2 = pt[h * HEAD_DIM + half:(h + 1) * HEAD_DIM, :]
        qT_ref[h * HEAD_DIM:h * HEAD_DIM + half, :] = ((t1 * cost - t2 * sint) * scale).astype(bf16)
        qT_ref[h * HEAD_DIM + half:(h + 1) * HEAD_DIM, :] = ((t2 * cost + t1 * sint) * scale).astype(bf16)
    r0 = NSA_WIDTH
    ones_rows = jnp.where(lax.broadcasted_iota(i32, (V_EXT - HEAD_DIM, tm), 0) == 0, 1.0, 0.0).astype(bf16)
    for vT_ref, base in ((vsT_ref, r0), (vwT_ref, r0 + KV_WIDTH)):
        for g in range(NSA_KV_HEADS):
            vT_ref[g * V_EXT:g * V_EXT + HEAD_DIM, :] = pt[base + g * HEAD_DIM:base + (g + 1) * HEAD_DIM, :].astype(bf16)
            vT_ref[g * V_EXT + HEAD_DIM:(g + 1) * V_EXT, :] = ones_rows
    gT_ref[...] = jax.nn.sigmoid(pt[r0 + 2 * KV_WIDTH:r0 + 2 * KV_WIDTH + 32, :])


def _inproj(x2, g_mix, w_row, w_t, cosr, sinr, cost, sint, conv_w8, g_conv, seq):
    T = x2.shape[0]
    tm = TM_IN
    n_row = w_row.shape[1]
    n_t = w_t.shape[0]
    row = lambda i: (i, 0)
    col = lambda i: (0, i)
    const = lambda i: (0, 0)
    out_shape = (
        jax.ShapeDtypeStruct((NSA_WIDTH, T), bf16),
        jax.ShapeDtypeStruct((T, KV_WIDTH), f32),
        jax.ShapeDtypeStruct((T, KV_WIDTH), f32),
        jax.ShapeDtypeStruct((T, 2 * KV_WIDTH), bf16),
        jax.ShapeDtypeStruct((T, KV_WIDTH), bf16),
        jax.ShapeDtypeStruct((NSA_KV_HEADS * V_EXT, T), bf16),
        jax.ShapeDtypeStruct((NSA_KV_HEADS * V_EXT, T), bf16),
        jax.ShapeDtypeStruct((32, T), f32),
        jax.ShapeDtypeStruct((T, CONV_WIDTH), bf16),
    )
    out_specs = (
        pl.BlockSpec((NSA_WIDTH, tm), col),
        pl.BlockSpec((tm, KV_WIDTH), row), pl.BlockSpec((tm, KV_WIDTH), row),
        pl.BlockSpec((tm, 2 * KV_WIDTH), row), pl.BlockSpec((tm, KV_WIDTH), row),
        pl.BlockSpec((NSA_KV_HEADS * V_EXT, tm), col), pl.BlockSpec((NSA_KV_HEADS * V_EXT, tm), col),
        pl.BlockSpec((32, tm), col),
        pl.BlockSpec((tm, CONV_WIDTH), row),
    )
    in_specs = [
        pl.BlockSpec((tm, D_MODEL), row),
        pl.BlockSpec((1, D_MODEL), const),
        pl.BlockSpec((D_MODEL, n_row), const),
        pl.BlockSpec((n_t, D_MODEL), const),
        pl.BlockSpec((tm, KV_WIDTH), row), pl.BlockSpec((tm, KV_WIDTH), row),
        pl.BlockSpec((HEAD_DIM // 2, tm), col), pl.BlockSpec((HEAD_DIM // 2, tm), col),
        pl.BlockSpec((SUBLANES, CONV_WIDTH), const),
        pl.BlockSpec((1, CONV_WIDTH), const),
    ]
    return pl.pallas_call(
        functools.partial(_inproj_kernel, tiles_per_seq=seq // tm),
        grid=(T // tm,), in_specs=in_specs, out_specs=out_specs, out_shape=out_shape,
        scratch_shapes=[pltpu.VMEM((tm + 2 * SUBLANES, CONV_WIDTH), f32)],
        compiler_params=_cparams(1), name="inproj",
    )(x2, g_mix, w_row, w_t, cosr, sinr, cost, sint, conv_w8, g_conv)


def _compress_kernel(xk_ref, xv_ref, pk_ref, pv_ref, w1k_ref, w1v_ref, b1k_ref, b1v_ref,
                     w2k_ref, w2vT_ref, kcc_ref, vcT_ref):
    ncp = xk_ref.shape[0] // CMP_STRIDE

    def hidden(x_ref, p_ref, w1_ref, b1_ref):
        x = jnp.concatenate([x_ref[pl.ds(l, ncp, stride=CMP_STRIDE), :] for l in range(CMP_STRIDE)], axis=1)
        lo = (x + p_ref[0:1, :]).astype(bf16)
        hi = (x + p_ref[1:2, :]).astype(bf16)
        a = jnp.dot(lo, w1_ref[0], preferred_element_type=f32)
        b = jnp.dot(hi, w1_ref[1], preferred_element_type=f32)
        pre = a + pltpu.roll(b, ncp - 1, 0) + b1_ref[...]
        return jax.nn.gelu(pre).astype(bf16)

    hk = hidden(xk_ref, pk_ref, w1k_ref, b1k_ref)
    hv = hidden(xv_ref, pv_ref, w1v_ref, b1v_ref)
    for g in range(NSA_KV_HEADS):
        sl = slice(g * CMP_HIDDEN, (g + 1) * CMP_HIDDEN)
        kcc_ref[0, g] = jnp.dot(hk[:, sl], w2k_ref[...], preferred_element_type=f32).astype(bf16)
        vcT_ref[0, g] = lax.dot_general(w2vT_ref[...], hv[:, sl], (((1,), (1,)), ((), ())),
                                        preferred_element_type=f32).astype(bf16)


def _compress(kc_rows, vc_rows, pk, pv, w1k, w1v, b1k, b1v, w2k, w2vT, batch, seq):
    ncp = seq // CMP_STRIDE
    wide = CMP_STRIDE * KV_WIDTH
    c2 = lambda b: (0, 0)
    c3 = lambda b: (0, 0, 0)
    in_specs = [
        pl.BlockSpec((seq, KV_WIDTH), lambda b: (b, 0)),
        pl.BlockSpec((seq, KV_WIDTH), lambda b: (b, 0)),
        pl.BlockSpec((SUBLANES, wide), c2), pl.BlockSpec((SUBLANES, wide), c2),
        pl.BlockSpec((2, wide, 2 * CMP_HIDDEN), c3), pl.BlockSpec((2, wide, 2 * CMP_HIDDEN), c3),
        pl.BlockSpec((1, 2 * CMP_HIDDEN), c2), pl.BlockSpec((1, 2 * CMP_HIDDEN), c2),
        pl.BlockSpec((CMP_HIDDEN, HEAD_DIM), c2), pl.BlockSpec((HEAD_DIM, CMP_HIDDEN), c2),
    ]
    out_shape = (jax.ShapeDtypeStruct((batch, NSA_KV_HEADS, ncp, HEAD_DIM), bf16),
                 jax.ShapeDtypeStruct((batch, NSA_KV_HEADS, HEAD_DIM, ncp), bf16))
    out_specs = (pl.BlockSpec((1, NSA_KV_HEADS, ncp, HEAD_DIM), lambda b: (b, 0, 0, 0)),
                 pl.BlockSpec((1, NSA_KV_HEADS, HEAD_DIM, ncp), lambda b: (b, 0, 0, 0)))
    return pl.pallas_call(
        _compress_kernel, grid=(batch,), in_specs=in_specs, out_specs=out_specs,
        out_shape=out_shape, compiler_params=_cparams(1), name="compress",
    )(kc_rows, vc_rows, pk, pv, w1k, w1v, b1k, b1v, w2k, w2vT)


_NQ = HPG * TQ
_COL_BLOCKS = [slice(c * LANES, (c + 1) * LANES) for c in range(_NQ // LANES)]


def _compressed_branch(q4, kcc, vcT, ov, s0):
    ncp = kcc.shape[0]
    ns = ov.shape[0]
    s_lane = s0 + (lax.broadcasted_iota(i32, (1, _NQ), 1) & (TQ - 1))
    sc = jnp.dot(kcc, q4, preferred_element_type=f32)
    yield
    c_end = lax.broadcasted_iota(i32, (ncp, 1), 0) * CMP_STRIDE + (CMP_LEN - 1)
    blocks = []
    for cs in _COL_BLOCKS:
        cmask = c_end <= s_lane[:, cs]
        scm = jnp.where(cmask, sc[:, cs], NEG_INF)
        e_c = jnp.where(cmask, jnp.exp(scm - jnp.max(scm, axis=0, keepdims=True)), 0.0)
        l_c = jnp.sum(e_c, axis=0, keepdims=True)
        blocks.append(e_c * jnp.where(l_c > 0.0, 1.0 / l_c, 0.0))
    p_c = jnp.concatenate(blocks, axis=1)
    o_cmp = jnp.dot(vcT, p_c.astype(bf16), preferred_element_type=f32)
    yield

    ps = p_c[:, 0:TQ]
    for h in range(1, HPG):
        ps = ps + p_c[:, h * TQ:(h + 1) * TQ]
    p_hi = ps.astype(bf16)
    r1 = ps - p_hi.astype(f32)
    p_mid = r1.astype(bf16)
    p_lo = (r1 - p_mid.astype(f32)).astype(bf16)
    imp = (jnp.dot(ov, p_hi, preferred_element_type=f32) + jnp.dot(ov, p_mid, preferred_element_type=f32)
           + jnp.dot(ov, p_lo, preferred_element_type=f32))
    yield
    j_blk = lax.broadcasted_iota(i32, (ns, 1), 0)
    cur = (s0 + lax.broadcasted_iota(i32, (1, TQ), 1)) // SEL_LEN
    forced = (j_blk == 0) | (j_blk == cur) | (j_blk == cur - 1)
    imp = jnp.where(forced, FORCED, jnp.where(j_blk > cur, NEG_INF, imp))
    rank = jnp.zeros((ns, TQ), f32)
    for i in range(ns):
        row = imp[i:i + 1, :]
        tie_before = jnp.where(j_blk > i, 1.0, 0.0)
        rank = rank + jnp.where(row > imp, 1.0, jnp.where(row == imp, tie_before, 0.0))
    sel_bias = jnp.where(rank < float(min(N_SEL, ns)), 0.0, NEG_INF).astype(bf16)
    return o_cmp, sel_bias


def _attend(tiles, q_op):
    m = jnp.full((1, _NQ), NEG_INF, f32)
    acc = jnp.zeros((V_EXT, _NQ), f32)
    s_next = jnp.dot(tiles[0][0](), q_op, preferred_element_type=f32)
    for t, (_, values_t, bias) in enumerate(tiles):
        sT = s_next
        if t + 1 < len(tiles):
            s_next = jnp.dot(tiles[t + 1][0](), q_op, preferred_element_type=f32)
        yield
        p_blocks, m_blocks, a_blocks = [], [], []
        for c, cs in enumerate(_COL_BLOCKS):
            s = sT[:, cs]
            if bias is not None:
                b0 = (c % (TQ // LANES)) * LANES
                s = s + bias[:, b0:b0 + LANES]
            m_o = m[:, cs]
            m_n = jnp.maximum(m_o, jnp.max(s, axis=0, keepdims=True))
            p_blocks.append(jnp.exp((s - m_n).astype(bf16)))
            a_blocks.append(jnp.exp(m_o - m_n))
            m_blocks.append(m_n)
        m = jnp.concatenate(m_blocks, axis=1)
        pv = jnp.dot(values_t(), jnp.concatenate(p_blocks, axis=1), preferred_element_type=f32)
        yield
        acc = acc * jnp.concatenate(a_blocks, axis=1) + pv
    return acc[0:HEAD_DIM, :] * (1.0 / acc[HEAD_DIM:HEAD_DIM + 1, :])


def _nsa_pair(pi, nqt, g, qa_ref, qb_ref, ks_ref, kw_ref, vsT_ref, vwT_ref, kcc_ref, vcT_ref, ga_ref, gb_ref,
              ovT_ref, cbias_ref, wbias_ref, o_ref):
    q_tiles = (pi, nqt - 1 - pi)
    ns = ovT_ref.shape[0]
    kcc = kcc_ref[0, 0]
    vcT = vcT_ref[0, 0]
    ov = ovT_ref[...]
    cbias = cbias_ref[...]
    n_back = WINDOW // TK

    def tile(k_ref, vT_ref, kt, bias):
        return (lambda: k_ref[kt * TK:(kt + 1) * TK, :], lambda: vT_ref[:, kt * TK:(kt + 1) * TK], bias)

    def query_tile(slot, q_ref, g_ref, qt):
        q4 = jnp.concatenate([q_ref[h * HEAD_DIM:(h + 1) * HEAD_DIM, :] for h in range(HPG)], axis=1)
        zeros = jnp.zeros_like(q4)
        q_win = jnp.where(g == 0, jnp.concatenate([q4, zeros], axis=0), jnp.concatenate([zeros, q4], axis=0))

        o_win = yield from _attend(
            [tile(kw_ref, vwT_ref, qt - j, cbias if j == 0 else (wbias_ref[...] if j == n_back else None))
             for j in range(min(n_back, qt) + 1)], q_win)

        o_cmp, sel_bias = yield from _compressed_branch(q4, kcc, vcT, ov, qt * TQ)

        q_sel = jnp.concatenate([q_win, jnp.concatenate([sel_bias] * HPG, axis=1),
                                 jnp.zeros((KV_WIDTH - ns, _NQ), bf16)], axis=0)
        o_slc = yield from _attend(
            [tile(ks_ref, vsT_ref, qt, cbias)] + [tile(ks_ref, vsT_ref, kt, None) for kt in range(qt)], q_sel)

        gates = g_ref[...]
        for h in range(HPG):
            sl = slice(h * TQ, (h + 1) * TQ)
            o = (gates[3 * h:3 * h + 1, :] * o_cmp[:, sl] + gates[3 * h + 1:3 * h + 2, :] * o_slc[:, sl]
                 + gates[3 * h + 2:3 * h + 3, :] * o_win[:, sl])
            o_ref[h * HEAD_DIM:(h + 1) * HEAD_DIM, slot * TQ:(slot + 1) * TQ] = o.astype(bf16)

    _interleave([query_tile(slot, q_ref, g_ref, qt)
                 for slot, (q_ref, g_ref, qt) in enumerate(zip((qa_ref, qb_ref), (ga_ref, gb_ref), q_tiles))])


def _nsa_kernel(*refs, nqt):
    for pi in range(nqt // 2):
        @pl.when(pl.program_id(0) == pi)
        def _():
            _nsa_pair(pi, nqt, pl.program_id(2), *refs)


def _nsa_tile_position(qt, nqt):
    return jnp.where(qt < nqt // 2, 2 * qt, 2 * (nqt - 1 - qt) + 1)


def _nsa(qT, ks, kw, vsT, vwT, kcc, vcT, gT, ovT, batch, seq):
    T = batch * seq
    nqt = seq // TQ
    ncp = kcc.shape[2]
    ns = seq // SEL_LEN
    nq = HPG * TQ
    gw = HPG * HEAD_DIM
    assert TQ == TK and WINDOW % TK == 0 and ns <= KV_WIDTH
    assert nqt % 2 == 0 and nqt // 2 >= WINDOW // TK
    kl = jnp.arange(TK)[:, None]
    ql = jnp.arange(TQ)[None, :]
    cbias = jnp.where(kl <= ql, 0.0, NEG_INF).astype(f32)
    wbias = jnp.where(kl > ql, 0.0, NEG_INF).astype(f32)
    amap = lambda p, b, g: (g, b * nqt + p)
    bmap = lambda p, b, g: (g, b * nqt + nqt - 1 - p)
    const = lambda p, b, g: (0, 0)
    in_specs = [
        pl.BlockSpec((gw, TQ), amap), pl.BlockSpec((gw, TQ), bmap),
        pl.BlockSpec((seq, 2 * KV_WIDTH), lambda p, b, g: (b, 0)),
        pl.BlockSpec((seq, KV_WIDTH), lambda p, b, g: (b, 0)),
        pl.BlockSpec((V_EXT, seq), lambda p, b, g: (g, b)),
        pl.BlockSpec((V_EXT, seq), lambda p, b, g: (g, b)),
        pl.BlockSpec((1, 1, ncp, HEAD_DIM), lambda p, b, g: (b, g, 0, 0)),
        pl.BlockSpec((1, 1, HEAD_DIM, ncp), lambda p, b, g: (b, g, 0, 0)),
        pl.BlockSpec((16, TQ), amap), pl.BlockSpec((16, TQ), bmap),
        pl.BlockSpec((ns, ncp), const),
        pl.BlockSpec((TK, TQ), const),
        pl.BlockSpec((TK, TQ), const),
    ]
    return pl.pallas_call(
        functools.partial(_nsa_kernel, nqt=nqt), grid=(nqt // 2, batch, NSA_KV_HEADS), in_specs=in_specs,
        out_specs=pl.BlockSpec((gw, 2 * TQ), lambda p, b, g: (g, b * (nqt // 2) + p)),
        out_shape=jax.ShapeDtypeStruct((NSA_WIDTH, T), bf16),
        compiler_params=_cparams(3), name="nsa",
    )(qT, qT, ks, kw, vsT, vwT, kcc, vcT, gT, gT, ovT, cbias, wbias)


def _memkv_kernel(mem_ref, g_ref, wkT_ref, wv_ref, kT_ref, v_ref):
    mb = _rms(mem_ref[0], g_ref[...]).astype(bf16)
    kT_ref[0] = lax.dot_general(wkT_ref[...], mb, (((1,), (1,)), ((), ())),
                                preferred_element_type=f32).astype(bf16)
    v_ref[0] = jnp.dot(mb, wv_ref[...], preferred_element_type=f32).astype(bf16)


def _memkv(mem, g_mem, wkT, wv):
    batch, n_mem, _ = mem.shape
    c2 = lambda b: (0, 0)
    return pl.pallas_call(
        _memkv_kernel, grid=(batch,),
        in_specs=[pl.BlockSpec((1, n_mem, D_MODEL), lambda b: (b, 0, 0)),
                  pl.BlockSpec((1, D_MODEL), c2),
                  pl.BlockSpec((D_MODEL, D_MODEL), c2), pl.BlockSpec((D_MODEL, D_MODEL), c2)],
        out_specs=(pl.BlockSpec((1, D_MODEL, n_mem), lambda b: (b, 0, 0)),
                   pl.BlockSpec((1, n_mem, D_MODEL), lambda b: (b, 0, 0))),
        out_shape=(jax.ShapeDtypeStruct((batch, D_MODEL, n_mem), bf16),
                   jax.ShapeDtypeStruct((batch, n_mem, D_MODEL), bf16)),
        compiler_params=_cparams(1), name="memkv",
    )(mem, g_mem, wkT, wv)


def _post_kernel(x_ref, oTa_ref, oTb_ref, conv_ref, gnsa_ref, wout_ref, gx_ref, wq_ref, kT_ref, v_ref, wo_ref,
                 gmoe_ref, wr_ref, br_ref,
                 h2_ref, xn2_ref, eidx_ref, gate_ref, sel_ref):
    stages = []
    for sub, oT_ref in enumerate((oTa_ref, oTb_ref)):
        rows = slice(sub * TQ, (sub + 1) * TQ)
        stages.append(_post_rows(
            x_ref.at[rows], oT_ref, conv_ref.at[rows], gnsa_ref, wout_ref, gx_ref, wq_ref, kT_ref, v_ref,
            wo_ref, gmoe_ref, wr_ref, br_ref, h2_ref.at[rows],
            xn2_ref.at[sub * TQ * ROW_SUB:(sub + 1) * TQ * ROW_SUB], eidx_ref.at[rows], gate_ref.at[rows],
            sel_ref.at[rows]))
    _interleave(stages)


_DONE = object()


def _interleave(generators):
    live = list(generators)
    while live:
        live = [g for g in live if next(g, _DONE) is not _DONE]


def _post_rows(x_ref, oT_ref, conv_ref, gnsa_ref, wout_ref, gx_ref, wq_ref, kT_ref, v_ref, wo_ref,
               gmoe_ref, wr_ref, br_ref,
               h2_ref, xn2_ref, eidx_ref, gate_ref, sel_ref):
    tm = x_ref.shape[0]
    oT = oT_ref[...].astype(f32)
    onT = (oT * lax.rsqrt(jnp.mean(oT * oT, axis=0, keepdims=True) + RMS_EPS) * gnsa_ref[...]).astype(bf16)
    mix = lax.dot_general(onT, wout_ref[0:NSA_WIDTH, :], (((0,), (0,)), ((), ())),
                          preferred_element_type=f32)
    mix = mix + jnp.dot(conv_ref[...], wout_ref[NSA_WIDTH:D_MODEL, :], preferred_element_type=f32)
    yield
    h1 = x_ref[...] + mix

    hn = _rms(h1, gx_ref[...]).astype(bf16)
    q = (jnp.dot(hn, wq_ref[...], preferred_element_type=f32) * (XATTN_HEAD_DIM ** -0.5)).astype(bf16)
    yield
    head_slices = [slice(h * XATTN_HEAD_DIM, (h + 1) * XATTN_HEAD_DIM) for h in range(XATTN_HEADS)]
    scores = [jnp.dot(q[:, sl], kT_ref[0, sl, :], preferred_element_type=f32) for sl in head_slices]
    yield
    heads = []
    for s, sl in zip(scores, head_slices):
        e = jnp.exp(s - jnp.max(s, axis=-1, keepdims=True))
        p = e * (1.0 / jnp.sum(e, axis=-1, keepdims=True))
        heads.append(jnp.dot(p.astype(bf16), v_ref[0, :, sl], preferred_element_type=f32))
    yield
    o = jnp.concatenate(heads, axis=1).astype(bf16)
    h2 = h1 + jnp.dot(o, wo_ref[...], preferred_element_type=f32)
    yield
    h2_ref[...] = h2

    xn2 = _rms(h2, gmoe_ref[...])
    for s_ in range(ROW_SUB):
        xn2_ref[pl.ds(s_, tm, stride=ROW_SUB), :] = xn2[:, s_ * LANES:(s_ + 1) * LANES]

    logits = jnp.dot(xn2.astype(bf16), wr_ref[...], preferred_element_type=f32) + br_ref[...]
    yield
    lane = lax.broadcasted_iota(i32, (tm, LANES), 1)
    work = logits
    sel = jnp.zeros((tm, LANES), f32)
    eidx = jnp.zeros((tm, LANES), i32)
    vals = []
    for k in range(TOP_K):
        mk = jnp.max(work, axis=-1, keepdims=True)
        ik = jnp.min(jnp.where(work == mk, lane, LANES), axis=-1, keepdims=True)
        hit = lane == ik
        work = jnp.where(hit, -jnp.inf, work)
        sel = jnp.where(hit, 1.0, sel)
        eidx = jnp.where(lane == k, ik, eidx)
        vals.append(mk)
    es = [jnp.exp(v - vals[0]) for v in vals]
    den = es[0]
    for e in es[1:]:
        den = den + e
    gate = jnp.zeros((tm, LANES), f32)
    for k in range(TOP_K):
        gate = jnp.where(lane == k, es[k] / den, gate)
    eidx_ref[...] = eidx
    gate_ref[...] = gate
    sel_ref[...] = sel.astype(bf16)


def _post(x2, oT, conv_n, g_nsa_col, w_out, g_x, w_q, kT, v, w_o, g_moe, w_r, b_r, seq):
    T = x2.shape[0]
    tm = TM_POST
    assert tm == 2 * TQ
    tps = seq // tm
    nqt = seq // TQ
    n_mem = v.shape[1]
    row = lambda i: (i, 0)
    const = lambda i: (0, 0)

    def o_tile(sub):
        return lambda i: (0, (i // tps) * nqt + _nsa_tile_position(2 * (i % tps) + sub, nqt))

    in_specs = [
        pl.BlockSpec((tm, D_MODEL), row),
        pl.BlockSpec((NSA_WIDTH, TQ), o_tile(0)), pl.BlockSpec((NSA_WIDTH, TQ), o_tile(1)),
        pl.BlockSpec((tm, CONV_WIDTH), row),
        pl.BlockSpec((NSA_WIDTH, 1), const),
        pl.BlockSpec((D_MODEL, D_MODEL), const),
        pl.BlockSpec((1, D_MODEL), const),
        pl.BlockSpec((D_MODEL, D_MODEL), const),
        pl.BlockSpec((1, D_MODEL, n_mem), lambda i: (i // tps, 0, 0)),
        pl.BlockSpec((1, n_mem, D_MODEL), lambda i: (i // tps, 0, 0)),
        pl.BlockSpec((D_MODEL, D_MODEL), const),
        pl.BlockSpec((1, D_MODEL), const),
        pl.BlockSpec((D_MODEL, LANES), const),
        pl.BlockSpec((1, LANES), const),
    ]
    out_shape = (jax.ShapeDtypeStruct((T, D_MODEL), f32),
                 jax.ShapeDtypeStruct((T * ROW_SUB, LANES), f32),
                 jax.ShapeDtypeStruct((T, LANES), i32),
                 jax.ShapeDtypeStruct((T, LANES), f32),
                 jax.ShapeDtypeStruct((T, LANES), bf16))
    out_specs = (pl.BlockSpec((tm, D_MODEL), row),
                 pl.BlockSpec((tm * ROW_SUB, LANES), row),
                 pl.BlockSpec((tm, LANES), row), pl.BlockSpec((tm, LANES), row),
                 pl.BlockSpec((tm, LANES), row))
    return pl.pallas_call(
        _post_kernel, grid=(T // tm,), in_specs=in_specs, out_specs=out_specs, out_shape=out_shape,
        compiler_params=_cparams(1), name="post",
    )(x2, oT, oT, conv_n, g_nsa_col, w_out, g_x, w_q, kT, v, w_o, g_moe, w_r, b_r)


def _count_kernel(sel_ref, cum_ref, cnt_ref, carry):
    i = pl.program_id(0)
    tm = sel_ref.shape[0]

    @pl.when(i == 0)
    def _():
        carry[...] = jnp.zeros_like(carry)

    sel = sel_ref[...]
    r = lax.broadcasted_iota(i32, (tm, tm), 0)
    c = lax.broadcasted_iota(i32, (tm, tm), 1)
    strict_lower = jnp.where(c < r, 1.0, 0.0).astype(bf16)
    base = carry[0:1, :]
    cum_ref[...] = jnp.dot(strict_lower, sel, preferred_element_type=f32) + base
    total = base + jnp.sum(sel.astype(f32), axis=0, keepdims=True)
    carry[...] = jnp.broadcast_to(total, carry.shape)
    cnt_ref[...] = jnp.broadcast_to(total, cnt_ref.shape)


def _count(sel):
    T = sel.shape[0]
    tm = TM_ROUTE
    return pl.pallas_call(
        _count_kernel, grid=(T // tm,),
        in_specs=[pl.BlockSpec((tm, LANES), lambda i: (i, 0))],
        out_specs=(pl.BlockSpec((tm, LANES), lambda i: (i, 0)),
                   pl.BlockSpec((SUBLANES, LANES), lambda i: (0, 0))),
        out_shape=(jax.ShapeDtypeStruct((T, LANES), f32), jax.ShapeDtypeStruct((SUBLANES, LANES), f32)),
        scratch_shapes=[pltpu.VMEM((SUBLANES, LANES), f32)],
        compiler_params=_cparams(1), name="route_count",
    )(sel)


def _dest_kernel(cum_ref, eidx_ref, pstart_ref, dest_ref):
    tm = cum_ref.shape[0]
    lane = lax.broadcasted_iota(i32, (tm, LANES), 1)
    row_of = cum_ref[...] + pstart_ref[0:1, :]
    eidx = eidx_ref[...]
    dest = jnp.zeros((tm, LANES), f32)
    for k in range(TOP_K):
        ek = eidx[:, k:k + 1]
        dk = jnp.sum(jnp.where(lane == ek, row_of, 0.0), axis=-1, keepdims=True)
        dest = jnp.where(lane == k, dk, dest)
    dest_ref[...] = dest.astype(i32)


def _dest(cum, eidx, pstart8):
    T = cum.shape[0]
    tm = TM_ROUTE
    row = lambda i: (i, 0)
    return pl.pallas_call(
        _dest_kernel, grid=(T // tm,),
        in_specs=[pl.BlockSpec((tm, LANES), row), pl.BlockSpec((tm, LANES), row),
                  pl.BlockSpec((SUBLANES, LANES), lambda i: (0, 0))],
        out_specs=pl.BlockSpec((tm, LANES), row),
        out_shape=jax.ShapeDtypeStruct((T, LANES), i32),
        compiler_params=_cparams(1), name="route_dest",
    )(cum, eidx, pstart8)


def _invert_kernel(dest_ref, init_ref, slot_ref, sem):
    cp = pltpu.make_async_copy(init_ref, slot_ref, sem)
    cp.start()
    cp.wait()
    chunk = LANES

    def body(j, carry):
        base = j * chunk
        for l in range(chunk):
            slot_ref[dest_ref[base + l]] = base + l
        return carry

    lax.fori_loop(0, dest_ref.shape[0] // chunk, body, 0)


def _invert(dest_flat, n_steps):
    parity = jnp.concatenate([jnp.arange(n_steps, dtype=i32) % 2, jnp.ones((1,), i32)])
    sink = dest_flat.shape[0] + parity[:, None] * BM + jnp.arange(BM, dtype=i32)[None, :]
    return pl.pallas_call(
        _invert_kernel,
        in_specs=[pl.BlockSpec(memory_space=pltpu.SMEM), pl.BlockSpec(memory_space=pltpu.VMEM)],
        out_specs=pl.BlockSpec(memory_space=pltpu.SMEM),
        out_shape=jax.ShapeDtypeStruct(((n_steps + 1) * BM,), i32),
        scratch_shapes=[pltpu.SemaphoreType.DMA],
        compiler_params=pltpu.CompilerParams(vmem_limit_bytes=VMEM_LIMIT), name="route_invert",
    )(dest_flat, sink.reshape(-1))


def _ffn_kernel(blk_e_ref, first_ref, nxt_e_ref, wslot_ref, nused_ref, slot_ref,
                x_hbm, wgu_hbm, bgu_ref, wd_hbm, bd_ref, y_hbm,
                xbuf0, obuf1, xbuf1, obuf0, wgu_f, wd_f, wgu_bf, wd_bf, gsem, ssem, wsem, *, n_tok, sink_row):
    i = pl.program_id(0)
    nused = nused_ref[0]
    n_steps = pl.num_programs(0)
    xbuf = (xbuf0, xbuf1)
    obuf = (obuf0, obuf1)

    def row_window(ref, row):
        return ref.at[pl.ds(pl.multiple_of(row * ROW_SUB, SUBLANES), ROW_SUB)]

    def gather(blk, par, r):
        tok = slot_ref[blk * BM + r] & (n_tok - 1)
        return pltpu.make_async_copy(row_window(x_hbm, tok), xbuf[par].at[pl.ds(r * ROW_SUB, ROW_SUB)],
                                     gsem.at[par])

    def scatter(blk, par, r):
        return pltpu.make_async_copy(obuf[par].at[pl.ds(r * ROW_SUB, ROW_SUB)],
                                     row_window(y_hbm, slot_ref[blk * BM + r]), ssem.at[par])

    def wait_gather(par):
        pltpu.make_async_copy(x_hbm.at[pl.ds(0, BM * ROW_SUB)], xbuf[par], gsem.at[par]).wait()

    def wait_scatter(par):
        pltpu.make_async_copy(obuf[par], y_hbm.at[pl.ds(0, BM * ROW_SUB)], ssem.at[par]).wait()

    def weight_copies(e, ws):
        return (pltpu.make_async_copy(wgu_hbm.at[e], wgu_f.at[ws], wsem.at[ws, 0]),
                pltpu.make_async_copy(wd_hbm.at[e], wd_f.at[ws], wsem.at[ws, 1]))

    @pl.when(i == 0)
    def _():
        obuf0[...] = jnp.zeros(obuf0.shape, f32)
        obuf1[...] = jnp.zeros(obuf1.shape, f32)
        pltpu.make_async_copy(obuf0, y_hbm.at[pl.ds(sink_row * ROW_SUB, BM * ROW_SUB)], ssem.at[0]).start()
        for cp in weight_copies(blk_e_ref[0], 0):
            cp.start(priority=1)
        for r in range(BM):
            gather(0, 0, r).start()

    @pl.when((i < nused) & (first_ref[i] == 1))
    def _():
        ws = wslot_ref[i]
        for cp in weight_copies(blk_e_ref[i], ws):
            cp.wait()

        @pl.when(nxt_e_ref[i] >= 0)
        def _():
            for cp in weight_copies(nxt_e_ref[i], 1 - ws):
                cp.start(priority=1)

        wgu_bf[...] = wgu_f[ws].astype(bf16)
        wd_bf[...] = wd_f[ws].astype(bf16)

    def block(par):
        prev = jnp.where(i == 0, n_steps, i - 1)
        wait_gather(par)
        x = jnp.concatenate([xbuf[par][pl.ds(s_, BM, stride=ROW_SUB), :] for s_ in range(ROW_SUB)],
                            axis=1).astype(bf16)
        for r in range(BM):
            scatter(prev, 1 - par, r).start(priority=1)
        for r in range(BM):
            gather(i + 1, 1 - par, r).start()
        gu = jnp.dot(x, wgu_bf[...], preferred_element_type=f32) + bgu_ref[0]
        gg = jnp.minimum(gu[:, 0:D_FF], SWIGLU_LIMIT)
        uu = jnp.clip(gu[:, D_FF:2 * D_FF], -SWIGLU_LIMIT, SWIGLU_LIMIT)
        hmid = (uu + 1.0) * (gg * jax.nn.sigmoid(SWIGLU_ALPHA * gg))
        out = jnp.dot(hmid.astype(bf16), wd_bf[...], preferred_element_type=f32) + bd_ref[0]
        wait_scatter(par)
        for s_ in range(ROW_SUB):
            obuf[par][pl.ds(s_, BM, stride=ROW_SUB), :] = out[:, s_ * LANES:(s_ + 1) * LANES]

    for par in range(2):
        @pl.when((i < nused) & (i % 2 == par))
        def _():
            block(par)

        @pl.when((i == nused) & (i % 2 == par))
        def _():
            for r in range(BM):
                scatter(i - 1, 1 - par, r).start(priority=1)
            wait_gather(par)
            wait_scatter(par)
            wait_scatter(1 - par)


def _ffn(blk_e, first, nxt_e, wslot, nused, slots, xn2_rows, w_gu, b_gu, w_d, b_d):
    n_steps = blk_e.shape[0]
    n_tok = xn2_rows.shape[0] // ROW_SUB
    assert n_tok & (n_tok - 1) == 0
    n_tok_rows = n_tok * TOP_K
    emap = lambda i, be, *_: (be[i], 0, 0)
    grid_spec = pltpu.PrefetchScalarGridSpec(
        num_scalar_prefetch=6, grid=(n_steps,),
        in_specs=[pl.BlockSpec(memory_space=pl.ANY),
                  pl.BlockSpec(memory_space=pl.ANY),
                  pl.BlockSpec((1, 1, 2 * D_FF), emap),
                  pl.BlockSpec(memory_space=pl.ANY),
                  pl.BlockSpec((1, 1, D_MODEL), emap)],
        out_specs=pl.BlockSpec(memory_space=pl.ANY),
        scratch_shapes=[pltpu.VMEM((BM * ROW_SUB, LANES), f32), pltpu.VMEM((BM * ROW_SUB, LANES), f32),
                        pltpu.VMEM((BM * ROW_SUB, LANES), f32), pltpu.VMEM((BM * ROW_SUB, LANES), f32),
                        pltpu.VMEM((2, D_MODEL, 2 * D_FF), f32), pltpu.VMEM((2, D_FF, D_MODEL), f32),
                        pltpu.VMEM((D_MODEL, 2 * D_FF), bf16), pltpu.VMEM((D_FF, D_MODEL), bf16),
                        pltpu.SemaphoreType.DMA((2,)), pltpu.SemaphoreType.DMA((2,)),
                        pltpu.SemaphoreType.DMA((2, 2))],
    )
    return pl.pallas_call(
        functools.partial(_ffn_kernel, n_tok=n_tok, sink_row=n_tok_rows), grid_spec=grid_spec,
        out_shape=jax.ShapeDtypeStruct(((n_tok_rows + 2 * BM) * ROW_SUB, LANES), f32),
        compiler_params=_cparams(1), name="ffn",
    )(blk_e, first, nxt_e, wslot, nused, slots, xn2_rows, w_gu, b_gu, w_d, b_d)


def _combine_kernel(*refs, final_norm):
    y_refs = refs[:TOP_K]
    gate_ref, h2_ref, gfin_ref, o_ref = refs[TOP_K:]
    tm = h2_ref.shape[0]
    gate = gate_ref[...]
    cols = []
    for s_ in range(ROW_SUB):
        acc = gate[:, 0:1] * y_refs[0][pl.ds(s_, tm, stride=ROW_SUB), :]
        for k in range(1, TOP_K):
            acc = acc + gate[:, k:k + 1] * y_refs[k][pl.ds(s_, tm, stride=ROW_SUB), :]
        cols.append(acc)
    h = h2_ref[...] + jnp.concatenate(cols, axis=1)
    if final_norm:
        h = _rms(h, gfin_ref[...])
    o_ref[...] = h


def _combine(y_rows, gate, h2, g_final, final_norm):
    T = h2.shape[0]
    tm = TM_ROW
    row = lambda i: (i, 0)
    planes = [pl.BlockSpec((tm * ROW_SUB, LANES), functools.partial(lambda i, k: (k * (T // tm) + i, 0), k=k))
              for k in range(TOP_K)]
    return pl.pallas_call(
        functools.partial(_combine_kernel, final_norm=final_norm), grid=(T // tm,),
        in_specs=planes + [pl.BlockSpec((tm, LANES), row), pl.BlockSpec((tm, D_MODEL), row),
                           pl.BlockSpec((1, D_MODEL), lambda i: (0, 0))],
        out_specs=pl.BlockSpec((tm, D_MODEL), row),
        out_shape=jax.ShapeDtypeStruct((T, D_MODEL), f32),
        compiler_params=_cparams(1), name="combine",
    )(*([y_rows] * TOP_K), gate, h2, g_final)


def _prep_inproj_weights(w_in):
    sizes = (NSA_WIDTH,) + (KV_WIDTH,) * 6 + (3 * NSA_HEADS,) + (CONV_WIDTH,) * 3
    offs = [0]
    for s in sizes:
        offs.append(offs[-1] + s)
    seg = lambda n: w_in[:, offs[n]:offs[n + 1]]
    q, kc, vc, ks, vs, kw, vw, gl, ch, cb, cc = (seg(n) for n in range(11))
    w_row = jnp.concatenate([kc, vc, ks, kw, ch, cb, cc], axis=1).astype(bf16)
    gl_g = gl.reshape(D_MODEL, NSA_KV_HEADS, HPG * 3)
    gl_g = jnp.pad(gl_g, ((0, 0), (0, 0), (0, 16 - HPG * 3))).reshape(D_MODEL, NSA_KV_HEADS * 16)
    w_t = jnp.concatenate([q, vs, vw, gl_g], axis=1).T.astype(bf16)
    return w_row, w_t


def _rope_tables(positions):
    half = HEAD_DIM // 2
    inv_freq = ROPE_THETA ** (-jnp.arange(half, dtype=f32) / half)
    ang = positions.reshape(-1).astype(f32)[:, None] * inv_freq
    cos = jnp.cos(ang)
    sin = jnp.sin(ang)
    reps = KV_WIDTH // HEAD_DIM
    cosr = jnp.tile(cos, (1, 2 * reps))
    sinr = jnp.tile(jnp.concatenate([-sin, sin], axis=1), (1, reps))
    return cosr, sinr, cos.T, sin.T


def _cmp_weights(pos, w1, b1):
    w1r = w1.reshape(2, CMP_STRIDE, HEAD_DIM, CMP_HIDDEN)
    eye = jnp.eye(NSA_KV_HEADS, dtype=w1.dtype)
    big = jnp.einsum('aldh,gk->algdkh', w1r, eye)
    big = big.reshape(2, CMP_STRIDE * KV_WIDTH, NSA_KV_HEADS * CMP_HIDDEN).astype(bf16)
    p = jnp.broadcast_to(pos.reshape(2, CMP_STRIDE, 1, HEAD_DIM), (2, CMP_STRIDE, NSA_KV_HEADS, HEAD_DIM))
    p = jnp.pad(p.reshape(2, CMP_STRIDE * KV_WIDTH), ((0, SUBLANES - 2), (0, 0)))
    return big, p, jnp.tile(b1.reshape(1, CMP_HIDDEN), (1, NSA_KV_HEADS))


def _mixer_core(x2, tables, p, batch, seq):
    cosr, sinr, cost, sint = tables
    row1 = lambda v: v.reshape(1, -1)
    w_row, w_t = _prep_inproj_weights(p['w_mix_in'])
    conv_w8 = jnp.pad(p['conv_w'], ((0, SUBLANES - CONV_K), (0, 0)))
    qT, kc, vc, ks, kw, vsT, vwT, gT, conv_n = _inproj(
        x2, row1(p['g_mix_norm']), w_row, w_t, cosr, sinr, cost, sint, conv_w8, row1(p['g_conv_out']), seq)

    w1k, pk, b1k = _cmp_weights(p['cmp_pos_k'], p['cmp_w1_k'], p['cmp_b1_k'])
    w1v, pv, b1v = _cmp_weights(p['cmp_pos_v'], p['cmp_w1_v'], p['cmp_b1_v'])
    kcc, vcT = _compress(kc, vc, pk, pv, w1k, w1v, b1k, b1v,
                         p['cmp_w2_k'].astype(bf16), p['cmp_w2_v'].T.astype(bf16), batch, seq)

    ncp = seq // CMP_STRIDE
    ns = seq // SEL_LEN
    cs = jnp.arange(ncp) * CMP_STRIDE
    js = jnp.arange(ns) * SEL_LEN
    overlap = jnp.clip(jnp.minimum(cs[:, None] + CMP_LEN, js[None, :] + SEL_LEN)
                       - jnp.maximum(cs[:, None], js[None, :]), 0, None).astype(f32) / CMP_LEN
    ovT = overlap.T.astype(bf16)
    oT = _nsa(qT, ks, kw, vsT, vwT, kcc, vcT, gT, ovT, batch, seq)
    return oT, conv_n


def _layer(h, memf, tables, p, final_gain, final_norm):
    batch, seq, _ = h.shape
    T = batch * seq
    x2 = h.reshape(T, D_MODEL)
    row1 = lambda v: v.reshape(1, -1)
    oT, conv_n = _mixer_core(x2, tables, p, batch, seq)

    w_xkv = p['w_xkv']
    kT, v = _memkv(memf, row1(p['g_mem_norm']), w_xkv[:, :D_MODEL].T.astype(bf16),
                   w_xkv[:, D_MODEL:].astype(bf16))
    w_r = jnp.pad(p['w_router'], ((0, 0), (0, LANES - N_EXPERTS))).astype(bf16)
    b_r = jnp.pad(p['b_router'], (0, LANES - N_EXPERTS), constant_values=NEG_INF).reshape(1, LANES)
    h2, xn2_rows, eidx, gate, sel = _post(
        x2, oT, conv_n, p['g_nsa_out'].reshape(NSA_WIDTH, 1), p['w_mix_out'].astype(bf16),
        row1(p['g_xattn_norm']), p['w_xq'].astype(bf16), kT, v, p['w_xo'].astype(bf16),
        row1(p['g_moe_norm']), w_r, b_r, seq)

    cum, cnt = _count(sel)
    counts = cnt[0, :N_EXPERTS].astype(i32)
    padded = (counts + BM - 1) // BM * BM
    pend = jnp.cumsum(padded)
    pstart = pend - padded
    n_steps = (T * TOP_K) // BM + N_EXPERTS + 1
    nused = (pend[-1] // BM).astype(i32)
    step = jnp.arange(n_steps, dtype=i32)
    used = step < nused
    blk_raw = jnp.minimum(jnp.sum((pend[None, :] <= (step * BM)[:, None]).astype(i32), axis=1), N_EXPERTS - 1)
    blk_e = blk_raw[jnp.minimum(step, nused - 1)]
    first = (used & jnp.concatenate([jnp.ones((1,), bool), blk_e[1:] != blk_e[:-1]])).astype(i32)
    wslot = (jnp.cumsum(first) - 1) % 2
    e_ids = jnp.arange(N_EXPERTS, dtype=i32)
    later = (e_ids[None, :] > e_ids[:, None]) & (padded > 0)[None, :]
    nxt_of = jnp.min(jnp.where(later, e_ids[None, :], N_EXPERTS), axis=1)
    nxt_e = jnp.where(nxt_of < N_EXPERTS, nxt_of, -1).astype(i32)[blk_e]
    pstart8 = jnp.broadcast_to(jnp.pad(pstart.astype(f32), (0, LANES - N_EXPERTS))[None, :], (SUBLANES, LANES))

    dest = _dest(cum, eidx, pstart8)
    slots = _invert(dest[:, :TOP_K].T.reshape(-1), n_steps)
    y_rows = _ffn(blk_e, first, nxt_e, wslot.astype(i32), nused.reshape(1), slots, xn2_rows,
                  p['w_gate_up'], p['b_gate_up'].reshape(N_EXPERTS, 1, 2 * D_FF), p['w_down'],
                  p['b_down'].reshape(N_EXPERTS, 1, D_MODEL))
    out = _combine(y_rows, gate, h2, row1(final_gain), final_norm)
    return out.reshape(batch, seq, D_MODEL)


_LAYER_PARAMS = ('g_mix_norm', 'w_mix_in', 'cmp_pos_k', 'cmp_pos_v', 'cmp_w1_k', 'cmp_b1_k', 'cmp_w2_k',
                 'cmp_w1_v', 'cmp_b1_v', 'cmp_w2_v', 'conv_w', 'g_nsa_out', 'g_conv_out', 'w_mix_out',
                 'g_xattn_norm', 'g_mem_norm', 'w_xq', 'w_xkv', 'w_xo', 'g_moe_norm', 'w_router', 'b_router',
                 'w_gate_up', 'b_gate_up', 'w_down', 'b_down')


def kernel(x, mem, positions, g_mix_norm, w_mix_in, cmp_pos_k, cmp_pos_v, cmp_w1_k, cmp_b1_k, cmp_w2_k, cmp_w1_v, cmp_b1_v, cmp_w2_v, conv_w, g_nsa_out, g_conv_out, w_mix_out, g_xattn_norm, g_mem_norm, w_xq, w_xkv, w_xo, g_moe_norm, w_router, b_router, w_gate_up, b_gate_up, w_down, b_down, g_final):
    stacked = dict(zip(_LAYER_PARAMS, (g_mix_norm, w_mix_in, cmp_pos_k, cmp_pos_v, cmp_w1_k, cmp_b1_k, cmp_w2_k,
                                       cmp_w1_v, cmp_b1_v, cmp_w2_v, conv_w, g_nsa_out, g_conv_out, w_mix_out,
                                       g_xattn_norm, g_mem_norm, w_xq, w_xkv, w_xo, g_moe_norm, w_router,
                                       b_router, w_gate_up, b_gate_up, w_down, b_down)))
    depth = g_mix_norm.shape[0]
    tables = _rope_tables(positions)
    h = x
    for l in range(depth):
        p = {k: v[l] for k, v in stacked.items()}
        last = l == depth - 1
        h = _layer(h, mem, tables, p, g_final, final_norm=last)
    return h
```

```python
import functools

import jax
import jax.numpy as jnp
from jax import lax
from jax.experimental import pallas as pl
from jax.experimental.pallas import tpu as pltpu

f32 = jnp.float32
bf16 = jnp.bfloat16
i32 = jnp.int32

D_MODEL = 1024
HEAD_DIM = 64
NSA_HEADS = 8
NSA_KV_HEADS = 2
HPG = NSA_HEADS // NSA_KV_HEADS
NSA_WIDTH = NSA_HEADS * HEAD_DIM
KV_WIDTH = NSA_KV_HEADS * HEAD_DIM
CONV_WIDTH = D_MODEL - NSA_WIDTH
CONV_K = 3
CMP_LEN = 32
CMP_STRIDE = 16
CMP_HIDDEN = 256
SEL_LEN = 64
N_SEL = 16
WINDOW = 512
ROPE_THETA = 10000.0
XATTN_HEADS = 4
XATTN_HEAD_DIM = D_MODEL // XATTN_HEADS
N_EXPERTS = 32
TOP_K = 4
D_FF = D_MODEL
SWIGLU_LIMIT = 7.0
SWIGLU_ALPHA = 1.702
RMS_EPS = 1e-5
NEG_INF = -1e30
FORCED = 1e30

LANES = 128
SUBLANES = 8
VMEM_LIMIT = 56 * 1024 * 1024

TM_IN = 512
TQ = 256
TK = 256
TM_POST = 512
TM_ROUTE = 512
TM_ROW = 256
V_EXT = HEAD_DIM + 16
BM = 256
ROW_SUB =D_MODEL // LANES


def _cparams(n_axes, **kw):
    return pltpu.CompilerParams(dimension_semantics=("arbitrary",) * n_axes,
                                vmem_limit_bytes=VMEM_LIMIT, **kw)


def _rms(t, gain):
    return t * lax.rsqrt(jnp.mean(t * t, axis=-1, keepdims=True) + RMS_EPS) * gain


def _inproj_kernel(x_ref, g_ref, wr_ref, wt_ref, cosr_ref, sinr_ref, cost_ref, sint_ref,
                   convw_ref, gconv_ref,
                   qT_ref, kc_ref, vc_ref, ks_ref, kw_ref, vsT_ref, vwT_ref, gT_ref, conv_ref,
                   ubuf, *, tiles_per_seq):
    i = pl.program_id(0)
    tm = x_ref.shape[0]
    xb = _rms(x_ref[...], g_ref[...]).astype(bf16)

    c0 = 4 * KV_WIDTH
    pr = jnp.dot(xb, wr_ref[:, 0:c0], preferred_element_type=f32)
    pc = jnp.dot(xb, wr_ref[:, c0:c0 + 3 * CONV_WIDTH], preferred_element_type=f32)
    pt = lax.dot_general(wt_ref[...], xb, (((1,), (1,)), ((), ())), preferred_element_type=f32)
    cosr = cosr_ref[...]
    sinr = sinr_ref[...]
    lane = lax.broadcasted_iota(i32, (tm, KV_WIDTH), 1)
    first_half = (lane & (HEAD_DIM - 1)) < HEAD_DIM // 2

    def rope_rows(t):
        rot = jnp.where(first_half, pltpu.roll(t, KV_WIDTH - HEAD_DIM // 2, 1),
                        pltpu.roll(t, HEAD_DIM // 2, 1))
        return t * cosr + rot * sinr

    kc_ref[...] = rope_rows(pr[:, 0:KV_WIDTH])
    vc_ref[...] = pr[:, KV_WIDTH:2 * KV_WIDTH]
    ks_ref[:, 0:KV_WIDTH] = rope_rows(pr[:, 2 * KV_WIDTH:3 * KV_WIDTH]).astype(bf16)
    tok = (i % tiles_per_seq) * tm + lax.broadcasted_iota(i32, (tm, KV_WIDTH), 0)
    ks_ref[:, KV_WIDTH:2 * KV_WIDTH] = jnp.where(lane == tok // SEL_LEN, 1.0, 0.0).astype(bf16)
    kw_ref[...] = rope_rows(pr[:, 3 * KV_WIDTH:4 * KV_WIDTH]).astype(bf16)

    ch = pc[:, 0:CONV_WIDTH]
    cb = pc[:, CONV_WIDTH:2 * CONV_WIDTH]
    cc = pc[:, 2 * CONV_WIDTH:3 * CONV_WIDTH]
    u = cc * ch

    @pl.when(i % tiles_per_seq == 0)
    def _():
        ubuf[0:SUBLANES, :] = jnp.zeros((SUBLANES, CONV_WIDTH), f32)

    @pl.when(i % tiles_per_seq != 0)
    def _():
        ubuf[0:SUBLANES, :] = ubuf[tm:tm + SUBLANES, :]

    ubuf[SUBLANES:SUBLANES + tm, :] = u
    u1 = ubuf[SUBLANES - 1:SUBLANES - 1 + tm, :]
    u2 = ubuf[SUBLANES - 2:SUBLANES - 2 + tm, :]
    w = convw_ref[...]
    y = cb * (w[0:1, :] * u2 + w[1:2, :] * u1 + w[2:3, :] * u)
    conv_ref[...] = _rms(y, gconv_ref[...]).astype(bf16)

    cost = cost_ref[...]
    sint = sint_ref[...]
    half = HEAD_DIM // 2
    scale = HEAD_DIM ** -0.5
    for h in range(NSA_HEADS):
        t1 = pt[h * HEAD_DIM:h * HEAD_DIM + half, :]
        t2 = pt[h * HEAD_DIM + half:(h + 1) * HEAD_DIM, :]
        qT_ref[h * HEAD_DIM:h * HEAD_DIM + half, :] = ((t1 * cost - t2 * sint) * scale).astype(bf16)
        qT_ref[h * HEAD_DIM + half:(h + 1) * HEAD_DIM, :] = ((t2 * cost + t1 * sint) * scale).astype(bf16)
    r0 = NSA_WIDTH
    ones_rows = jnp.where(lax.broadcasted_iota(i32, (V_EXT - HEAD_DIM, tm), 0) == 0, 1.0, 0.0).astype(bf16)
    for vT_ref, base in ((vsT_ref, r0), (vwT_ref, r0 + KV_WIDTH)):
        for g in range(NSA_KV_HEADS):
            vT_ref[g * V_EXT:g * V_EXT + HEAD_DIM, :] = pt[base + g * HEAD_DIM:base + (g + 1) * HEAD_DIM, :].astype(bf16)
            vT_ref[g * V_EXT + HEAD_DIM:(g + 1) * V_EXT, :] = ones_rows
    gT_ref[...] = jax.nn.sigmoid(pt[r0 + 2 * KV_WIDTH:r0 + 2 * KV_WIDTH + 32, :])


def _inproj(x2, g_mix, w_row, w_t, cosr, sinr, cost, sint, conv_w8, g_conv, seq):
    T = x2.shape[0]
    tm = TM_IN
    n_row = w_row.shape[1]
    n_t = w_t.shape[0]
    row = lambda i: (i, 0)
    col = lambda i: (0, i)
    const = lambda i: (0, 0)
    out_shape = (
        jax.ShapeDtypeStruct((NSA_WIDTH, T), bf16),
        jax.ShapeDtypeStruct((T, KV_WIDTH), f32),
        jax.ShapeDtypeStruct((T, KV_WIDTH), f32),
        jax.ShapeDtypeStruct((T, 2 * KV_WIDTH), bf16),
        jax.ShapeDtypeStruct((T, KV_WIDTH), bf16),
        jax.ShapeDtypeStruct((NSA_KV_HEADS * V_EXT, T), bf16),
        jax.ShapeDtypeStruct((NSA_KV_HEADS * V_EXT, T), bf16),
        jax.ShapeDtypeStruct((32, T), f32),
        jax.ShapeDtypeStruct((T, CONV_WIDTH), bf16),
    )
    out_specs = (
        pl.BlockSpec((NSA_WIDTH, tm), col),
        pl.BlockSpec((tm, KV_WIDTH), row), pl.BlockSpec((tm, KV_WIDTH), row),
        pl.BlockSpec((tm, 2 * KV_WIDTH), row), pl.BlockSpec((tm, KV_WIDTH), row),
        pl.BlockSpec((NSA_KV_HEADS * V_EXT, tm), col), pl.BlockSpec((NSA_KV_HEADS * V_EXT, tm), col),
        pl.BlockSpec((32, tm), col),
        pl.BlockSpec((tm, CONV_WIDTH), row),
    )
    in_specs = [
        pl.BlockSpec((tm, D_MODEL), row),
        pl.BlockSpec((1, D_MODEL), const),
        pl.BlockSpec((D_MODEL, n_row), const),
        pl.BlockSpec((n_t, D_MODEL), const),
        pl.BlockSpec((tm, KV_WIDTH), row), pl.BlockSpec((tm, KV_WIDTH), row),
        pl.BlockSpec((HEAD_DIM // 2, tm), col), pl.BlockSpec((HEAD_DIM // 2, tm), col),
        pl.BlockSpec((SUBLANES, CONV_WIDTH), const),
        pl.BlockSpec((1, CONV_WIDTH), const),
    ]
    return pl.pallas_call(
        functools.partial(_inproj_kernel, tiles_per_seq=seq // tm),
        grid=(T // tm,), in_specs=in_specs, out_specs=out_specs, out_shape=out_shape,
        scratch_shapes=[pltpu.VMEM((tm + 2 * SUBLANES, CONV_WIDTH), f32)],
        compiler_params=_cparams(1), name="inproj",
    )(x2, g_mix, w_row, w_t, cosr, sinr, cost, sint, conv_w8, g_conv)


def _compress_kernel(xk_ref, xv_ref, pk_ref, pv_ref, w1k_ref, w1v_ref, b1k_ref, b1v_ref,
                     w2k_ref, w2vT_ref, kcc_ref, vcT_ref):
    ncp = xk_ref.shape[0] // CMP_STRIDE

    def hidden(x_ref, p_ref, w1_ref, b1_ref):
        x = jnp.concatenate([x_ref[pl.ds(l, ncp, stride=CMP_STRIDE), :] for l in range(CMP_STRIDE)], axis=1)
        lo = (x + p_ref[0:1, :]).astype(bf16)
        hi = (x + p_ref[1:2, :]).astype(bf16)
        a = jnp.dot(lo, w1_ref[0], preferred_element_type=f32)
        b = jnp.dot(hi, w1_ref[1], preferred_element_type=f32)
        pre = a + pltpu.roll(b, ncp - 1, 0) + b1_ref[...]
        return jax.nn.gelu(pre).astype(bf16)

    hk = hidden(xk_ref, pk_ref, w1k_ref, b1k_ref)
    hv = hidden(xv_ref, pv_ref, w1v_ref, b1v_ref)
    for g in range(NSA_KV_HEADS):
        sl = slice(g * CMP_HIDDEN, (g + 1) * CMP_HIDDEN)
        kcc_ref[0, g] = jnp.dot(hk[:, sl], w2k_ref[...], preferred_element_type=f32).astype(bf16)
        vcT_ref[0, g] = lax.dot_general(w2vT_ref[...], hv[:, sl], (((1,), (1,)), ((), ())),
                                        preferred_element_type=f32).astype(bf16)


def _compress(kc_rows, vc_rows, pk, pv, w1k, w1v, b1k, b1v, w2k, w2vT, batch, seq):
    ncp = seq // CMP_STRIDE
    wide = CMP_STRIDE * KV_WIDTH
    c2 = lambda b: (0, 0)
    c3 = lambda b: (0, 0, 0)
    in_specs = [
        pl.BlockSpec((seq, KV_WIDTH), lambda b: (b, 0)),
        pl.BlockSpec((seq, KV_WIDTH), lambda b: (b, 0)),
        pl.BlockSpec((SUBLANES, wide), c2), pl.BlockSpec((SUBLANES, wide), c2),
        pl.BlockSpec((2, wide, 2 * CMP_HIDDEN), c3), pl.BlockSpec((2, wide, 2 * CMP_HIDDEN), c3),
        pl.BlockSpec((1, 2 * CMP_HIDDEN), c2), pl.BlockSpec((1, 2 * CMP_HIDDEN), c2),
        pl.BlockSpec((CMP_HIDDEN, HEAD_DIM), c2), pl.BlockSpec((HEAD_DIM, CMP_HIDDEN), c2),
    ]
    out_shape = (jax.ShapeDtypeStruct((batch, NSA_KV_HEADS, ncp, HEAD_DIM), bf16),
                 jax.ShapeDtypeStruct((batch, NSA_KV_HEADS, HEAD_DIM, ncp), bf16))
    out_specs = (pl.BlockSpec((1, NSA_KV_HEADS, ncp, HEAD_DIM), lambda b: (b, 0, 0, 0)),
                 pl.BlockSpec((1, NSA_KV_HEADS, HEAD_DIM, ncp), lambda b: (b, 0, 0, 0)))
    return pl.pallas_call(
        _compress_kernel, grid=(batch,), in_specs=in_specs, out_specs=out_specs,
        out_shape=out_shape, compiler_params=_cparams(1), name="compress",
    )(kc_rows, vc_rows, pk, pv, w1k, w1v, b1k, b1v, w2k, w2vT)


_NQ = HPG * TQ
_COL_BLOCKS = [slice(c * LANES, (c + 1) * LANES) for c in range(_NQ // LANES)]


def _compressed_branch(q4, kcc, vcT, ov, s0):
    ncp = kcc.shape[0]
    ns = ov.shape[0]
    s_lane = s0 + (lax.broadcasted_iota(i32, (1, _NQ), 1) & (TQ - 1))
    sc = jnp.dot(kcc, q4, preferred_element_type=f32)
    yield
    c_end = lax.broadcasted_iota(i32, (ncp, 1), 0) * CMP_STRIDE + (CMP_LEN - 1)
    blocks = []
    for cs in _COL_BLOCKS:
        cmask = c_end <= s_lane[:, cs]
        scm = jnp.where(cmask, sc[:, cs], NEG_INF)
        e_c = jnp.where(cmask, jnp.exp(scm - jnp.max(scm, axis=0, keepdims=True)), 0.0)
        l_c = jnp.sum(e_c, axis=0, keepdims=True)
        blocks.append(e_c * jnp.where(l_c > 0.0, 1.0 / l_c, 0.0))
    p_c = jnp.concatenate(blocks, axis=1)
    o_cmp = jnp.dot(vcT, p_c.astype(bf16), preferred_element_type=f32)
    yield

    j_blk = lax.broadcasted_iota(i32, (ns, 1), 0)
    cur = (s0 + lax.broadcasted_iota(i32, (1, TQ), 1)) // SEL_LEN
    n_live = (s0 + TQ - 1) // SEL_LEN + 1
    if n_live <= N_SEL:
        return o_cmp, jnp.where(j_blk <= cur, 0.0, NEG_INF).astype(bf16)

    ps = p_c[:, 0:TQ]
    for h in range(1, HPG):
        ps = ps + p_c[:, h * TQ:(h + 1) * TQ]
    p_hi = ps.astype(bf16)
    r1 = ps - p_hi.astype(f32)
    p_mid = r1.astype(bf16)
    p_lo = (r1 - p_mid.astype(f32)).astype(bf16)
    imp = (jnp.dot(ov, p_hi, preferred_element_type=f32) + jnp.dot(ov, p_mid, preferred_element_type=f32)
           + jnp.dot(ov, p_lo, preferred_element_type=f32))
    yield
    forced = (j_blk == 0) | (j_blk == cur) | (j_blk == cur - 1)
    imp = jnp.where(forced, FORCED, jnp.where(j_blk > cur, NEG_INF, imp))
    rank = jnp.zeros((ns, TQ), f32)
    for i in range(min(n_live, ns)):
        row = imp[i:i + 1, :]
        tie_before = jnp.where(j_blk > i, 1.0, 0.0)
        rank = rank + jnp.where(row > imp, 1.0, jnp.where(row == imp, tie_before, 0.0))
    sel_bias = jnp.where(rank < float(min(N_SEL, ns)), 0.0, NEG_INF).astype(bf16)
    return o_cmp, sel_bias


def _attend(tiles, q_op):
    m = jnp.full((1, _NQ), NEG_INF, f32)
    acc = jnp.zeros((V_EXT, _NQ), f32)
    s_next = jnp.dot(tiles[0][0](), q_op, preferred_element_type=f32)
    for t, (_, values_t, bias) in enumerate(tiles):
        sT = s_next
        if t + 1 < len(tiles):
            s_next = jnp.dot(tiles[t + 1][0](), q_op, preferred_element_type=f32)
        yield
        p_blocks, m_blocks, a_blocks = [], [], []
        for c, cs in enumerate(_COL_BLOCKS):
            s = sT[:, cs]
            if bias is not None:
                b0 = (c % (TQ // LANES)) * LANES
                s = s + bias[:, b0:b0 + LANES]
            m_o = m[:, cs]
            m_n = jnp.maximum(m_o, jnp.max(s, axis=0, keepdims=True))
            p_blocks.append(jnp.exp((s - m_n).astype(bf16)))
            a_blocks.append(jnp.exp(m_o - m_n))
            m_blocks.append(m_n)
        m = jnp.concatenate(m_blocks, axis=1)
        pv = jnp.dot(values_t(), jnp.concatenate(p_blocks, axis=1), preferred_element_type=f32)
        yield
        acc = acc * jnp.concatenate(a_blocks, axis=1) + pv
    return acc[0:HEAD_DIM, :] * (1.0 / acc[HEAD_DIM:HEAD_DIM + 1, :])


def _nsa_pair(pi, nqt, g, qa_ref, qb_ref, ks_ref, kw_ref, vsT_ref, vwT_ref, kcc_ref, vcT_ref, ga_ref, gb_ref,
              ovT_ref, cbias_ref, wbias_ref, o_ref):
    q_tiles = (pi, nqt - 1 - pi)
    ns = ovT_ref.shape[0]
    kcc = kcc_ref[0, 0]
    vcT = vcT_ref[0, 0]
    ov = ovT_ref[...]
    cbias = cbias_ref[...]
    n_back = WINDOW // TK

    def tile(k_ref, vT_ref, kt, bias):
        return (lambda: k_ref[kt * TK:(kt + 1) * TK, :], lambda: vT_ref[:, kt * TK:(kt + 1) * TK], bias)

    def query_tile(slot, q_ref, g_ref, qt):
        q4 = jnp.concatenate([q_ref[h * HEAD_DIM:(h + 1) * HEAD_DIM, :] for h in range(HPG)], axis=1)
        zeros = jnp.zeros_like(q4)
        q_win = jnp.where(g == 0, jnp.concatenate([q4, zeros], axis=0), jnp.concatenate([zeros, q4], axis=0))

        o_win = yield from _attend(
            [tile(kw_ref, vwT_ref, qt - j, cbias if j == 0 else (wbias_ref[...] if j == n_back else None))
             for j in range(min(n_back, qt) + 1)], q_win)

        o_cmp, sel_bias = yield from _compressed_branch(q4, kcc, vcT, ov, qt * TQ)

        q_sel = jnp.concatenate([q_win, jnp.concatenate([sel_bias] * HPG, axis=1),
                                 jnp.zeros((KV_WIDTH - ns, _NQ), bf16)], axis=0)
        o_slc = yield from _attend(
            [tile(ks_ref, vsT_ref, qt, cbias)] + [tile(ks_ref, vsT_ref, kt, None) for kt in range(qt)], q_sel)

        gates = g_ref[...]
        for h in range(HPG):
            sl = slice(h * TQ, (h + 1) * TQ)
            o = (gates[3 * h:3 * h + 1, :] * o_cmp[:, sl] + gates[3 * h + 1:3 * h + 2, :] * o_slc[:, sl]
                 + gates[3 * h + 2:3 * h + 3, :] * o_win[:, sl])
            o_ref[h * HEAD_DIM:(h + 1) * HEAD_DIM, slot * TQ:(slot + 1) * TQ] = o.astype(bf16)

    _interleave([query_tile(slot, q_ref, g_ref, qt)
                 for slot, (q_ref, g_ref, qt) in enumerate(zip((qa_ref, qb_ref), (ga_ref, gb_ref), q_tiles))])


def _nsa_kernel(*refs, nqt):
    for pi in range(nqt // 2):
        @pl.when(pl.program_id(0) == pi)
        def _():
            _nsa_pair(pi, nqt, pl.program_id(2), *refs)


def _nsa_tile_position(qt, nqt):
    return jnp.where(qt < nqt // 2, 2 * qt, 2 * (nqt - 1 - qt) + 1)


def _nsa(qT, ks, kw, vsT, vwT, kcc, vcT, gT, ovT, batch, seq):
    T = batch * seq
    nqt = seq // TQ
    ncp = kcc.shape[2]
    ns = seq // SEL_LEN
    nq = HPG * TQ
    gw = HPG * HEAD_DIM
    assert TQ == TK and WINDOW % TK == 0 and ns <= KV_WIDTH
    assert nqt % 2 == 0 and nqt // 2 >= WINDOW // TK
    kl = jnp.arange(TK)[:, None]
    ql = jnp.arange(TQ)[None, :]
    cbias = jnp.where(kl <= ql, 0.0, NEG_INF).astype(f32)
    wbias = jnp.where(kl > ql, 0.0, NEG_INF).astype(f32)
    amap = lambda p, b, g: (g, b * nqt + p)
    bmap = lambda p, b, g: (g, b * nqt + nqt - 1 - p)
    const = lambda p, b, g: (0, 0)
    in_specs = [
        pl.BlockSpec((gw, TQ), amap), pl.BlockSpec((gw, TQ), bmap),
        pl.BlockSpec((seq, 2 * KV_WIDTH), lambda p, b, g: (b, 0)),
        pl.BlockSpec((seq, KV_WIDTH), lambda p, b, g: (b, 0)),
        pl.BlockSpec((V_EXT, seq), lambda p, b, g: (g, b)),
        pl.BlockSpec((V_EXT, seq), lambda p, b, g: (g, b)),
        pl.BlockSpec((1, 1, ncp, HEAD_DIM), lambda p, b, g: (b, g, 0, 0)),
        pl.BlockSpec((1, 1, HEAD_DIM, ncp), lambda p, b, g: (b, g, 0, 0)),
        pl.BlockSpec((16, TQ), amap), pl.BlockSpec((16, TQ), bmap),
        pl.BlockSpec((ns, ncp), const),
        pl.BlockSpec((TK, TQ), const),
        pl.BlockSpec((TK, TQ), const),
    ]
    return pl.pallas_call(
        functools.partial(_nsa_kernel, nqt=nqt), grid=(nqt // 2, batch, NSA_KV_HEADS), in_specs=in_specs,
        out_specs=pl.BlockSpec((gw, 2 * TQ), lambda p, b, g: (g, b * (nqt // 2) + p)),
        out_shape=jax.ShapeDtypeStruct((NSA_WIDTH, T), bf16),
        compiler_params=_cparams(3), name="nsa",
    )(qT, qT, ks, kw, vsT, vwT, kcc, vcT, gT, gT, ovT, cbias, wbias)


def _memkv_kernel(mem_ref, g_ref, wkT_ref, wv_ref, kT_ref, v_ref):
    mb = _rms(mem_ref[0], g_ref[...]).astype(bf16)
    kT_ref[0] = lax.dot_general(wkT_ref[...], mb, (((1,), (1,)), ((), ())),
                                preferred_element_type=f32).astype(bf16)
    v_ref[0] = jnp.dot(mb, wv_ref[...], preferred_element_type=f32).astype(bf16)


def _memkv(mem, g_mem, wkT, wv):
    batch, n_mem, _ = mem.shape
    c2 = lambda b: (0, 0)
    return pl.pallas_call(
        _memkv_kernel, grid=(batch,),
        in_specs=[pl.BlockSpec((1, n_mem, D_MODEL), lambda b: (b, 0, 0)),
                  pl.BlockSpec((1, D_MODEL), c2),
                  pl.BlockSpec((D_MODEL, D_MODEL), c2), pl.BlockSpec((D_MODEL, D_MODEL), c2)],
        out_specs=(pl.BlockSpec((1, D_MODEL, n_mem), lambda b: (b, 0, 0)),
                   pl.BlockSpec((1, n_mem, D_MODEL), lambda b: (b, 0, 0))),
        out_shape=(jax.ShapeDtypeStruct((batch, D_MODEL, n_mem), bf16),
                   jax.ShapeDtypeStruct((batch, n_mem, D_MODEL), bf16)),
        compiler_params=_cparams(1), name="memkv",
    )(mem, g_mem, wkT, wv)


def _post_kernel(x_ref, oTa_ref, oTb_ref, conv_ref, gnsa_ref, wout_ref, gx_ref, wq_ref, kT_ref, v_ref, wo_ref,
                 gmoe_ref, wr_ref, br_ref,
                 h2_ref, xn2_ref, eidx_ref, gate_ref, sel_ref):
    stages = []
    for sub, oT_ref in enumerate((oTa_ref, oTb_ref)):
        rows = slice(sub * TQ, (sub + 1) * TQ)
        stages.append(_post_rows(
            x_ref.at[rows], oT_ref, conv_ref.at[rows], gnsa_ref, wout_ref, gx_ref, wq_ref, kT_ref, v_ref,
            wo_ref, gmoe_ref, wr_ref, br_ref, h2_ref.at[rows],
            xn2_ref.at[sub * TQ * ROW_SUB:(sub + 1) * TQ * ROW_SUB], eidx_ref.at[rows], gate_ref.at[rows],
            sel_ref.at[rows]))
    _interleave(stages)


_DONE = object()


def _interleave(generators):
    live = list(generators)
    while live:
        live = [g for g in live if next(g, _DONE) is not _DONE]


def _post_rows(x_ref, oT_ref, conv_ref, gnsa_ref, wout_ref, gx_ref, wq_ref, kT_ref, v_ref, wo_ref,
               gmoe_ref, wr_ref, br_ref,
               h2_ref, xn2_ref, eidx_ref, gate_ref, sel_ref):
    tm = x_ref.shape[0]
    oT = oT_ref[...].astype(f32)
    onT = (oT * lax.rsqrt(jnp.mean(oT * oT, axis=0, keepdims=True) + RMS_EPS) * gnsa_ref[...]).astype(bf16)
    mix = lax.dot_general(onT, wout_ref[0:NSA_WIDTH, :], (((0,), (0,)), ((), ())),
                          preferred_element_type=f32)
    mix = mix + jnp.dot(conv_ref[...], wout_ref[NSA_WIDTH:D_MODEL, :], preferred_element_type=f32)
    yield
    h1 = x_ref[...] + mix

    hn = _rms(h1, gx_ref[...]).astype(bf16)
    q = (jnp.dot(hn, wq_ref[...], preferred_element_type=f32) * (XATTN_HEAD_DIM ** -0.5)).astype(bf16)
    yield
    head_slices = [slice(h * XATTN_HEAD_DIM, (h + 1) * XATTN_HEAD_DIM) for h in range(XATTN_HEADS)]
    scores = [jnp.dot(q[:, sl], kT_ref[0, sl, :], preferred_element_type=f32) for sl in head_slices]
    yield
    heads = []
    for s, sl in zip(scores, head_slices):
        e = jnp.exp(s - jnp.max(s, axis=-1, keepdims=True))
        p = e * (1.0 / jnp.sum(e, axis=-1, keepdims=True))
        heads.append(jnp.dot(p.astype(bf16), v_ref[0, :, sl], preferred_element_type=f32))
    yield
    o = jnp.concatenate(heads, axis=1).astype(bf16)
    h2 = h1 + jnp.dot(o, wo_ref[...], preferred_element_type=f32)
    yield
    h2_ref[...] = h2

    xn2 = _rms(h2, gmoe_ref[...])
    for s_ in range(ROW_SUB):
        xn2_ref[pl.ds(s_, tm, stride=ROW_SUB), :] = xn2[:, s_ * LANES:(s_ + 1) * LANES]

    logits = jnp.dot(xn2.astype(bf16), wr_ref[...], preferred_element_type=f32) + br_ref[...]
    yield
    lane = lax.broadcasted_iota(i32, (tm, LANES), 1)
    work = logits
    sel = jnp.zeros((tm, LANES), f32)
    eidx = jnp.zeros((tm, LANES), i32)
    vals = []
    for k in range(TOP_K):
        mk = jnp.max(work, axis=-1, keepdims=True)
        ik = jnp.min(jnp.where(work == mk, lane, LANES), axis=-1, keepdims=True)
        hit = lane == ik
        work = jnp.where(hit, -jnp.inf, work)
        sel = jnp.where(hit, 1.0, sel)
        eidx = jnp.where(lane == k, ik, eidx)
        vals.append(mk)
    es = [jnp.exp(v - vals[0]) for v in vals]
    den = es[0]
    for e in es[1:]:
        den = den + e
    gate = jnp.zeros((tm, LANES), f32)
    for k in range(TOP_K):
        gate = jnp.where(lane == k, es[k] / den, gate)
    eidx_ref[...] = eidx
    gate_ref[...] = gate
    sel_ref[...] = sel.astype(bf16)


def _post(x2, oT, conv_n, g_nsa_col, w_out, g_x, w_q, kT, v, w_o, g_moe, w_r, b_r, seq):
    T = x2.shape[0]
    tm = TM_POST
    assert tm == 2 * TQ
    tps = seq // tm
    nqt = seq // TQ
    n_mem = v.shape[1]
    row = lambda i: (i, 0)
    const = lambda i: (0, 0)

    def o_tile(sub):
        return lambda i: (0, (i // tps) * nqt + _nsa_tile_position(2 * (i % tps) + sub, nqt))

    in_specs = [
        pl.BlockSpec((tm, D_MODEL), row),
        pl.BlockSpec((NSA_WIDTH, TQ), o_tile(0)), pl.BlockSpec((NSA_WIDTH, TQ), o_tile(1)),
        pl.BlockSpec((tm, CONV_WIDTH), row),
        pl.BlockSpec((NSA_WIDTH, 1), const),
        pl.BlockSpec((D_MODEL, D_MODEL), const),
        pl.BlockSpec((1, D_MODEL), const),
        pl.BlockSpec((D_MODEL, D_MODEL), const),
        pl.BlockSpec((1, D_MODEL, n_mem), lambda i: (i // tps, 0, 0)),
        pl.BlockSpec((1, n_mem, D_MODEL), lambda i: (i // tps, 0, 0)),
        pl.BlockSpec((D_MODEL, D_MODEL), const),
        pl.BlockSpec((1, D_MODEL), const),
        pl.BlockSpec((D_MODEL, LANES), const),
        pl.BlockSpec((1, LANES), const),
    ]
    out_shape = (jax.ShapeDtypeStruct((T, D_MODEL), f32),
                 jax.ShapeDtypeStruct((T * ROW_SUB, LANES), f32),
                 jax.ShapeDtypeStruct((T, LANES), i32),
                 jax.ShapeDtypeStruct((T, LANES), f32),
                 jax.ShapeDtypeStruct((T, LANES), bf16))
    out_specs = (pl.BlockSpec((tm, D_MODEL), row),
                 pl.BlockSpec((tm * ROW_SUB, LANES), row),
                 pl.BlockSpec((tm, LANES), row), pl.BlockSpec((tm, LANES), row),
                 pl.BlockSpec((tm, LANES), row))
    return pl.pallas_call(
        _post_kernel, grid=(T // tm,), in_specs=in_specs, out_specs=out_specs, out_shape=out_shape,
        compiler_params=_cparams(1), name="post",
    )(x2, oT, oT, conv_n, g_nsa_col, w_out, g_x, w_q, kT, v, w_o, g_moe, w_r, b_r)


def _count_kernel(sel_ref, cum_ref, cnt_ref, carry):
    i = pl.program_id(0)
    tm = sel_ref.shape[0]

    @pl.when(i == 0)
    def _():
        carry[...] = jnp.zeros_like(carry)

    sel = sel_ref[...]
    r = lax.broadcasted_iota(i32, (tm, tm), 0)
    c = lax.broadcasted_iota(i32, (tm, tm), 1)
    strict_lower = jnp.where(c < r, 1.0, 0.0).astype(bf16)
    base = carry[0:1, :]
    cum_ref[...] = jnp.dot(strict_lower, sel, preferred_element_type=f32) + base
    total = base + jnp.sum(sel.astype(f32), axis=0, keepdims=True)
    carry[...] = jnp.broadcast_to(total, carry.shape)
    cnt_ref[...] = jnp.broadcast_to(total, cnt_ref.shape)


def _count(sel):
    T = sel.shape[0]
    tm = TM_ROUTE
    return pl.pallas_call(
        _count_kernel, grid=(T // tm,),
        in_specs=[pl.BlockSpec((tm, LANES), lambda i: (i, 0))],
        out_specs=(pl.BlockSpec((tm, LANES), lambda i: (i, 0)),
                   pl.BlockSpec((SUBLANES, LANES), lambda i: (0, 0))),
        out_shape=(jax.ShapeDtypeStruct((T, LANES), f32), jax.ShapeDtypeStruct((SUBLANES, LANES), f32)),
        scratch_shapes=[pltpu.VMEM((SUBLANES, LANES), f32)],
        compiler_params=_cparams(1), name="route_count",
    )(sel)


def _dest_kernel(cum_ref, eidx_ref, pstart_ref, dest_ref):
    tm = cum_ref.shape[0]
    lane = lax.broadcasted_iota(i32, (tm, LANES), 1)
    row_of = cum_ref[...] + pstart_ref[0:1, :]
    eidx = eidx_ref[...]
    dest = jnp.zeros((tm, LANES), f32)
    for k in range(TOP_K):
        ek = eidx[:, k:k + 1]
        dk = jnp.sum(jnp.where(lane == ek, row_of, 0.0), axis=-1, keepdims=True)
        dest = jnp.where(lane == k, dk, dest)
    dest_ref[...] = dest.astype(i32)


def _dest(cum, eidx, pstart8):
    T = cum.shape[0]
    tm = TM_ROUTE
    row = lambda i: (i, 0)
    return pl.pallas_call(
        _dest_kernel, grid=(T // tm,),
        in_specs=[pl.BlockSpec((tm, LANES), row), pl.BlockSpec((tm, LANES), row),
                  pl.BlockSpec((SUBLANES, LANES), lambda i: (0, 0))],
        out_specs=pl.BlockSpec((tm, LANES), row),
        out_shape=jax.ShapeDtypeStruct((T, LANES), i32),
        compiler_params=_cparams(1), name="route_dest",
    )(cum, eidx, pstart8)


def _invert_kernel(dest_ref, init_ref, slot_ref, sem):
    cp = pltpu.make_async_copy(init_ref, slot_ref, sem)
    cp.start()
    cp.wait()
    chunk = LANES

    def body(j, carry):
        base = j * chunk
        for l in range(chunk):
            slot_ref[dest_ref[base + l]] = base + l
        return carry

    lax.fori_loop(0, dest_ref.shape[0] // chunk, body, 0)


def _invert(dest_flat, n_steps):
    parity = jnp.concatenate([jnp.arange(n_steps, dtype=i32) % 2, jnp.ones((1,), i32)])
    sink = dest_flat.shape[0] + parity[:, None] * BM + jnp.arange(BM, dtype=i32)[None, :]
    return pl.pallas_call(
        _invert_kernel,
        in_specs=[pl.BlockSpec(memory_space=pltpu.SMEM), pl.BlockSpec(memory_space=pltpu.VMEM)],
        out_specs=pl.BlockSpec(memory_space=pltpu.SMEM),
        out_shape=jax.ShapeDtypeStruct(((n_steps + 1) * BM,), i32),
        scratch_shapes=[pltpu.SemaphoreType.DMA],
        compiler_params=pltpu.CompilerParams(vmem_limit_bytes=VMEM_LIMIT), name="route_invert",
    )(dest_flat, sink.reshape(-1))


def _ffn_kernel(blk_e_ref, first_ref, nxt_e_ref, wslot_ref, nused_ref, slot_ref,
                x_hbm, wgu_hbm, bgu_ref, wd_hbm, bd_ref, y_hbm,
                xbuf0, xbuf1, obuf0, obuf1, wgu_f, wd_f, wgu_bf, wd_bf, gsem, ssem, wsem, *, n_tok, sink_row):
    i = pl.program_id(0)
    nused = nused_ref[0]
    n_steps = pl.num_programs(0)
    xbuf = (xbuf0, xbuf1)
    obuf = (obuf0, obuf1)

    def row_window(ref, row):
        return ref.at[pl.ds(pl.multiple_of(row * ROW_SUB, SUBLANES), ROW_SUB)]

    def gather(blk, par, r):
        tok = slot_ref[blk * BM + r] & (n_tok - 1)
        return pltpu.make_async_copy(row_window(x_hbm, tok), xbuf[par].at[pl.ds(r * ROW_SUB, ROW_SUB)],
                                     gsem.at[par])

    def scatter(blk, par, r):
        return pltpu.make_async_copy(obuf[par].at[pl.ds(r * ROW_SUB, ROW_SUB)],
                                     row_window(y_hbm, slot_ref[blk * BM + r]), ssem.at[par])

    def wait_gather(par):
        pltpu.make_async_copy(x_hbm.at[pl.ds(0, BM * ROW_SUB)], xbuf[par], gsem.at[par]).wait()

    def wait_scatter(par):
        pltpu.make_async_copy(obuf[par], y_hbm.at[pl.ds(0, BM * ROW_SUB)], ssem.at[par]).wait()

    def weight_copies(e, ws):
        return (pltpu.make_async_copy(wgu_hbm.at[e], wgu_f.at[ws], wsem.at[ws, 0]),
                pltpu.make_async_copy(wd_hbm.at[e], wd_f.at[ws], wsem.at[ws, 1]))

    @pl.when(i == 0)
    def _():
        obuf0[...] = jnp.zeros(obuf0.shape, f32)
        obuf1[...] = jnp.zeros(obuf1.shape, f32)
        pltpu.make_async_copy(obuf0, y_hbm.at[pl.ds(sink_row * ROW_SUB, BM * ROW_SUB)], ssem.at[0]).start()
        for cp in weight_copies(blk_e_ref[0], 0):
            cp.start(priority=1)
        for r in range(BM):
            gather(0, 0, r).start()

    @pl.when((i < nused) & (first_ref[i] == 1))
    def _():
        ws = wslot_ref[i]
        for cp in weight_copies(blk_e_ref[i], ws):
            cp.wait()

        @pl.when(nxt_e_ref[i] >= 0)
        def _():
            for cp in weight_copies(nxt_e_ref[i], 1 - ws):
                cp.start(priority=1)

        wgu_bf[...] = wgu_f[ws].astype(bf16)
        wd_bf[...] = wd_f[ws].astype(bf16)

    def block(par):
        prev = jnp.where(i == 0, n_steps, i - 1)
        wait_gather(par)
        x = jnp.concatenate([xbuf[par][pl.ds(s_, BM, stride=ROW_SUB), :] for s_ in range(ROW_SUB)],
                            axis=1).astype(bf16)
        for r in range(BM):
            scatter(prev, 1 - par, r).start(priority=1)
        for r in range(BM):
            gather(i + 1, 1 - par, r).start()
        gu = jnp.dot(x, wgu_bf[...], preferred_element_type=f32) + bgu_ref[0]
        gg = jnp.minimum(gu[:, 0:D_FF], SWIGLU_LIMIT)
        uu = jnp.clip(gu[:, D_FF:2 * D_FF], -SWIGLU_LIMIT, SWIGLU_LIMIT)
        hmid = (uu + 1.0) * (gg * jax.nn.sigmoid(SWIGLU_ALPHA * gg))
        out = jnp.dot(hmid.astype(bf16), wd_bf[...], preferred_element_type=f32) + bd_ref[0]
        wait_scatter(par)
        for s_ in range(ROW_SUB):
            obuf[par][pl.ds(s_, BM, stride=ROW_SUB), :] = out[:, s_ * LANES:(s_ + 1) * LANES]

    for par in range(2):
        @pl.when((i < nused) & (i % 2 == par))
        def _():
            block(par)

        @pl.when((i == nused) & (i % 2 == par))
        def _():
            for r in range(BM):
                scatter(i - 1, 1 - par, r).start(priority=1)
            wait_gather(par)
            wait_scatter(par)
            wait_scatter(1 - par)


def _ffn(blk_e, first, nxt_e, wslot, nused, slots, xn2_rows, w_gu, b_gu, w_d, b_d):
    n_steps = blk_e.shape[0]
    n_tok = xn2_rows.shape[0] // ROW_SUB
    assert n_tok & (n_tok - 1) == 0
    n_tok_rows = n_tok * TOP_K
    emap = lambda i, be, *_: (be[i], 0, 0)
    grid_spec = pltpu.PrefetchScalarGridSpec(
        num_scalar_prefetch=6, grid=(n_steps,),
        in_specs=[pl.BlockSpec(memory_space=pl.ANY),
                  pl.BlockSpec(memory_space=pl.ANY),
                  pl.BlockSpec((1, 1, 2 * D_FF), emap),
                  pl.BlockSpec(memory_space=pl.ANY),
                  pl.BlockSpec((1, 1, D_MODEL), emap)],
        out_specs=pl.BlockSpec(memory_space=pl.ANY),
        scratch_shapes=[pltpu.VMEM((BM * ROW_SUB, LANES), f32), pltpu.VMEM((BM * ROW_SUB, LANES), f32),
                        pltpu.VMEM((BM * ROW_SUB, LANES), f32), pltpu.VMEM((BM * ROW_SUB, LANES), f32),
                        pltpu.VMEM((2, D_MODEL, 2 * D_FF), f32), pltpu.VMEM((2, D_FF, D_MODEL), f32),
                        pltpu.VMEM((D_MODEL, 2 * D_FF), bf16), pltpu.VMEM((D_FF, D_MODEL), bf16),
                        pltpu.SemaphoreType.DMA((2,)), pltpu.SemaphoreType.DMA((2,)),
                        pltpu.SemaphoreType.DMA((2, 2))],
    )
    return pl.pallas_call(
        functools.partial(_ffn_kernel, n_tok=n_tok, sink_row=n_tok_rows), grid_spec=grid_spec,
        out_shape=jax.ShapeDtypeStruct(((n_tok_rows + 2 * BM) * ROW_SUB, LANES), f32),
        compiler_params=_cparams(1), name="ffn",
    )(blk_e, first, nxt_e, wslot, nused, slots, xn2_rows, w_gu, b_gu, w_d, b_d)


def _combine_kernel(*refs, final_norm):
    y_refs = refs[:TOP_K]
    gate_ref, h2_ref, gfin_ref, o_ref = refs[TOP_K:]
    tm = h2_ref.shape[0]
    gate = gate_ref[...]
    cols = []
    for s_ in range(ROW_SUB):
        acc = gate[:, 0:1] * y_refs[0][pl.ds(s_, tm, stride=ROW_SUB), :]
        for k in range(1, TOP_K):
            acc = acc + gate[:, k:k + 1] * y_refs[k][pl.ds(s_, tm, stride=ROW_SUB), :]
        cols.append(acc)
    h = h2_ref[...] + jnp.concatenate(cols, axis=1)
    if final_norm:
        h = _rms(h, gfin_ref[...])
    o_ref[...] = h


def _combine(y_rows, gate, h2, g_final, final_norm):
    T = h2.shape[0]
    tm = TM_ROW
    row = lambda i: (i, 0)
    planes = [pl.BlockSpec((tm * ROW_SUB, LANES), functools.partial(lambda i, k: (k * (T // tm) + i, 0), k=k))
              for k in range(TOP_K)]
    return pl.pallas_call(
        functools.partial(_combine_kernel, final_norm=final_norm), grid=(T // tm,),
        in_specs=planes + [pl.BlockSpec((tm, LANES), row), pl.BlockSpec((tm, D_MODEL), row),
                           pl.BlockSpec((1, D_MODEL), lambda i: (0, 0))],
        out_specs=pl.BlockSpec((tm, D_MODEL), row),
        out_shape=jax.ShapeDtypeStruct((T, D_MODEL), f32),
        compiler_params=_cparams(1), name="combine",
    )(*([y_rows] * TOP_K), gate, h2, g_final)


def _prep_inproj_weights(w_in):
    sizes = (NSA_WIDTH,) + (KV_WIDTH,) * 6 + (3 * NSA_HEADS,) + (CONV_WIDTH,) * 3
    offs = [0]
    for s in sizes:
        offs.append(offs[-1] + s)
    seg = lambda n: w_in[:, offs[n]:offs[n + 1]]
    q, kc, vc, ks, vs, kw, vw, gl, ch, cb, cc = (seg(n) for n in range(11))
    w_row = jnp.concatenate([kc, vc, ks, kw, ch, cb, cc], axis=1).astype(bf16)
    gl_g = gl.reshape(D_MODEL, NSA_KV_HEADS, HPG * 3)
    gl_g = jnp.pad(gl_g, ((0, 0), (0, 0), (0, 16 - HPG * 3))).reshape(D_MODEL, NSA_KV_HEADS * 16)
    w_t = jnp.concatenate([q, vs, vw, gl_g], axis=1).T.astype(bf16)
    return w_row, w_t


def _rope_tables(positions):
    half = HEAD_DIM // 2
    inv_freq = ROPE_THETA ** (-jnp.arange(half, dtype=f32) / half)
    ang = positions.reshape(-1).astype(f32)[:, None] * inv_freq
    cos = jnp.cos(ang)
    sin = jnp.sin(ang)
    reps = KV_WIDTH // HEAD_DIM
    cosr = jnp.tile(cos, (1, 2 * reps))
    sinr = jnp.tile(jnp.concatenate([-sin, sin], axis=1), (1, reps))
    return cosr, sinr, cos.T, sin.T


def _cmp_weights(pos, w1, b1):
    w1r = w1.astype(bf16).reshape(2, CMP_STRIDE, HEAD_DIM, CMP_HIDDEN)
    zero = jnp.zeros_like(w1r)
    big = jnp.stack([jnp.concatenate([w1r if g == k else zero for k in range(NSA_KV_HEADS)], axis=-1)
                     for g in range(NSA_KV_HEADS)], axis=2)
    big = big.reshape(2, CMP_STRIDE * KV_WIDTH, NSA_KV_HEADS * CMP_HIDDEN)
    p = jnp.broadcast_to(pos.reshape(2, CMP_STRIDE, 1, HEAD_DIM), (2, CMP_STRIDE, NSA_KV_HEADS, HEAD_DIM))
    p = jnp.pad(p.reshape(2, CMP_STRIDE * KV_WIDTH), ((0, SUBLANES - 2), (0, 0)))
    return big, p, jnp.tile(b1.reshape(1, CMP_HIDDEN), (1, NSA_KV_HEADS))


def _mixer_core(x2, tables, p, batch, seq):
    cosr, sinr, cost, sint = tables
    row1 = lambda v: v.reshape(1, -1)
    w_row, w_t = _prep_inproj_weights(p['w_mix_in'])
    conv_w8 = jnp.pad(p['conv_w'], ((0, SUBLANES - CONV_K), (0, 0)))
    qT, kc, vc, ks, kw, vsT, vwT, gT, conv_n = _inproj(
        x2, row1(p['g_mix_norm']), w_row, w_t, cosr, sinr, cost, sint, conv_w8, row1(p['g_conv_out']), seq)

    w1k, pk, b1k = _cmp_weights(p['cmp_pos_k'], p['cmp_w1_k'], p['cmp_b1_k'])
    w1v, pv, b1v = _cmp_weights(p['cmp_pos_v'], p['cmp_w1_v'], p['cmp_b1_v'])
    kcc, vcT = _compress(kc, vc, pk, pv, w1k, w1v, b1k, b1v,
                         p['cmp_w2_k'].astype(bf16), p['cmp_w2_v'].T.astype(bf16), batch, seq)

    ncp = seq // CMP_STRIDE
    ns = seq // SEL_LEN
    cs = jnp.arange(ncp) * CMP_STRIDE
    js = jnp.arange(ns) * SEL_LEN
    overlap = jnp.clip(jnp.minimum(cs[:, None] + CMP_LEN, js[None, :] + SEL_LEN)
                       - jnp.maximum(cs[:, None], js[None, :]), 0, None).astype(f32) / CMP_LEN
    ovT = overlap.T.astype(bf16)
    oT = _nsa(qT, ks, kw, vsT, vwT, kcc, vcT, gT, ovT, batch, seq)
    return oT, conv_n


def _layer(h, memf, tables, p, final_gain, final_norm):
    batch, seq, _ = h.shape
    T = batch * seq
    x2 = h.reshape(T, D_MODEL)
    row1 = lambda v: v.reshape(1, -1)
    oT, conv_n = _mixer_core(x2, tables, p, batch, seq)

    w_xkv = p['w_xkv']
    kT, v = _memkv(memf, row1(p['g_mem_norm']), w_xkv[:, :D_MODEL].T.astype(bf16),
                   w_xkv[:, D_MODEL:].astype(bf16))
    w_r = jnp.pad(p['w_router'], ((0, 0), (0, LANES - N_EXPERTS))).astype(bf16)
    b_r = jnp.pad(p['b_router'], (0, LANES - N_EXPERTS), constant_values=NEG_INF).reshape(1, LANES)
    h2, xn2_rows, eidx, gate, sel = _post(
        x2, oT, conv_n, p['g_nsa_out'].reshape(NSA_WIDTH, 1), p['w_mix_out'].astype(bf16),
        row1(p['g_xattn_norm']), p['w_xq'].astype(bf16), kT, v, p['w_xo'].astype(bf16),
        row1(p['g_moe_norm']), w_r, b_r, seq)

    cum, cnt = _count(sel)
    counts = cnt[0, :N_EXPERTS].astype(i32)
    padded = (counts + BM - 1) // BM * BM
    pend = jnp.cumsum(padded)
    pstart = pend - padded
    n_steps = (T * TOP_K) // BM + N_EXPERTS + 1
    nused = (pend[-1] // BM).astype(i32)
    step = jnp.arange(n_steps, dtype=i32)
    used = step < nused
    blk_raw = jnp.minimum(jnp.sum((pend[None, :] <= (step * BM)[:, None]).astype(i32), axis=1), N_EXPERTS - 1)
    blk_e = blk_raw[jnp.minimum(step, nused - 1)]
    first = (used & jnp.concatenate([jnp.ones((1,), bool), blk_e[1:] != blk_e[:-1]])).astype(i32)
    wslot = (jnp.cumsum(first) - 1) % 2
    e_ids = jnp.arange(N_EXPERTS, dtype=i32)
    later = (e_ids[None, :] > e_ids[:, None]) & (padded > 0)[None, :]
    nxt_of = jnp.min(jnp.where(later, e_ids[None, :], N_EXPERTS), axis=1)
    nxt_e = jnp.where(nxt_of < N_EXPERTS, nxt_of, -1).astype(i32)[blk_e]
    pstart8 = jnp.broadcast_to(jnp.pad(pstart.astype(f32), (0, LANES - N_EXPERTS))[None, :], (SUBLANES, LANES))

    dest = _dest(cum, eidx, pstart8)
    slots = _invert(dest[:, :TOP_K].T.reshape(-1), n_steps)
    y_rows = _ffn(blk_e, first, nxt_e, wslot.astype(i32), nused.reshape(1), slots, xn2_rows,
                  p['w_gate_up'], p['b_gate_up'].reshape(N_EXPERTS, 1, 2 * D_FF), p['w_down'],
                  p['b_down'].reshape(N_EXPERTS, 1, D_MODEL))
    out = _combine(y_rows, gate, h2, row1(final_gain), final_norm)
    return out.reshape(batch, seq, D_MODEL)


_LAYER_PARAMS = ('g_mix_norm', 'w_mix_in', 'cmp_pos_k', 'cmp_pos_v', 'cmp_w1_k', 'cmp_b1_k', 'cmp_w2_k',
                 'cmp_w1_v', 'cmp_b1_v', 'cmp_w2_v', 'conv_w', 'g_nsa_out', 'g_conv_out', 'w_mix_out',
                 'g_xattn_norm', 'g_mem_norm', 'w_xq', 'w_xkv', 'w_xo', 'g_moe_norm', 'w_router', 'b_router',
                 'w_gate_up', 'b_gate_up', 'w_down', 'b_down')


def kernel(x, mem, positions, g_mix_norm, w_mix_in, cmp_pos_k, cmp_pos_v, cmp_w1_k, cmp_b1_k, cmp_w2_k, cmp_w1_v, cmp_b1_v, cmp_w2_v, conv_w, g_nsa_out, g_conv_out, w_mix_out, g_xattn_norm, g_mem_norm, w_xq, w_xkv, w_xo, g_moe_norm, w_router, b_router, w_gate_up, b_gate_up, w_down, b_down, g_final):
    stacked = dict(zip(_LAYER_PARAMS, (g_mix_norm, w_mix_in, cmp_pos_k, cmp_pos_v, cmp_w1_k, cmp_b1_k, cmp_w2_k,
                                       cmp_w1_v, cmp_b1_v, cmp_w2_v, conv_w, g_nsa_out, g_conv_out, w_mix_out,
                                       g_xattn_norm, g_mem_norm, w_xq, w_xkv, w_xo, g_moe_norm, w_router,
                                       b_router, w_gate_up, b_gate_up, w_down, b_down)))
    depth = g_mix_norm.shape[0]
    tables = _rope_tables(positions)
    h = x
    for l in range(depth):
        p = {k: v[l] for k, v in stacked.items()}
        last = l == depth - 1
        h = _layer(h, mem, tables, p, g_final, final_norm=last)
    return h
```

```python
import functools

import jax
import jax.numpy as jnp
from jax import lax
from jax.experimental import pallas as pl
from jax.experimental.pallas import tpu as pltpu

f32 = jnp.float32
bf16 = jnp.bfloat16
i32 = jnp.int32

D_MODEL = 1024
HEAD_DIM = 64
NSA_HEADS = 8
NSA_KV_HEADS = 2
HPG = NSA_HEADS // NSA_KV_HEADS
NSA_WIDTH = NSA_HEADS * HEAD_DIM
KV_WIDTH = NSA_KV_HEADS * HEAD_DIM
CONV_WIDTH = D_MODEL - NSA_WIDTH
CONV_K = 3
CMP_LEN = 32
CMP_STRIDE = 16
CMP_HIDDEN = 256
SEL_LEN = 64
N_SEL = 16
WINDOW = 512
ROPE_THETA = 10000.0
XATTN_HEADS = 4
XATTN_HEAD_DIM = D_MODEL // XATTN_HEADS
N_EXPERTS = 32
TOP_K = 4
D_FF = D_MODEL
SWIGLU_LIMIT = 7.0
SWIGLU_ALPHA = 1.702
RMS_EPS = 1e-5
NEG_INF = -1e30
FORCED = 1e30

LANES = 128
SUBLANES = 8
VMEM_LIMIT = 56 * 1024 * 1024

TM_IN = 512
TQ = 256
TK = 256
TM_POST = 512
TM_ROUTE = 512
TM_ROW = 256
V_EXT = HEAD_DIM + 16
BM = 256
ROW_SUB =D_MODEL // LANES


def _cparams(n_axes, **kw):
    return pltpu.CompilerParams(dimension_semantics=("arbitrary",) * n_axes,
                                vmem_limit_bytes=VMEM_LIMIT, **kw)


def _rms(t, gain):
    return t * lax.rsqrt(jnp.mean(t * t, axis=-1, keepdims=True) + RMS_EPS) * gain


def _inproj_kernel(x_ref, g_ref, wr_ref, wt_ref, cosr_ref, sinr_ref, cost_ref, sint_ref,
                   convw_ref, gconv_ref,
                   qT_ref, kc_ref, vc_ref, ks_ref, kw_ref, vsT_ref, vwT_ref, gT_ref, conv_ref,
                   ubuf, *, tiles_per_seq):
    i = pl.program_id(0)
    tm = x_ref.shape[0]
    xb = _rms(x_ref[...], g_ref[...]).astype(bf16)

    c0 = 4 * KV_WIDTH
    pr = jnp.dot(xb, wr_ref[:, 0:c0], preferred_element_type=f32)
    pc = jnp.dot(xb, wr_ref[:, c0:c0 + 3 * CONV_WIDTH], preferred_element_type=f32)
    pt = lax.dot_general(wt_ref[...], xb, (((1,), (1,)), ((), ())), preferred_element_type=f32)
    cosr = cosr_ref[...]
    sinr = sinr_ref[...]
    lane = lax.broadcasted_iota(i32, (tm, KV_WIDTH), 1)
    first_half = (lane & (HEAD_DIM - 1)) < HEAD_DIM // 2

    def rope_rows(t):
        rot = jnp.where(first_half, pltpu.roll(t, KV_WIDTH - HEAD_DIM // 2, 1),
                        pltpu.roll(t, HEAD_DIM // 2, 1))
        return t * cosr + rot * sinr

    kc_ref[...] = rope_rows(pr[:, 0:KV_WIDTH])
    vc_ref[...] = pr[:, KV_WIDTH:2 * KV_WIDTH]
    ks_ref[:, 0:KV_WIDTH] = rope_rows(pr[:, 2 * KV_WIDTH:3 * KV_WIDTH]).astype(bf16)
    tok = (i % tiles_per_seq) * tm + lax.broadcasted_iota(i32, (tm, KV_WIDTH), 0)
    ks_ref[:, KV_WIDTH:2 * KV_WIDTH] = jnp.where(lane == tok // SEL_LEN, 1.0, 0.0).astype(bf16)
    kw_ref[...] = rope_rows(pr[:, 3 * KV_WIDTH:4 * KV_WIDTH]).astype(bf16)

    ch = pc[:, 0:CONV_WIDTH]
    cb = pc[:, CONV_WIDTH:2 * CONV_WIDTH]
    cc = pc[:, 2 * CONV_WIDTH:3 * CONV_WIDTH]
    u = cc * ch

    @pl.when(i % tiles_per_seq == 0)
    def _():
        ubuf[0:SUBLANES, :] = jnp.zeros((SUBLANES, CONV_WIDTH), f32)

    @pl.when(i % tiles_per_seq != 0)
    def _():
        ubuf[0:SUBLANES, :] = ubuf[tm:tm + SUBLANES, :]

    ubuf[SUBLANES:SUBLANES + tm, :] = u
    u1 = ubuf[SUBLANES - 1:SUBLANES - 1 + tm, :]
    u2 = ubuf[SUBLANES - 2:SUBLANES - 2 + tm, :]
    w = convw_ref[...]
    y = cb * (w[0:1, :] * u2 + w[1:2, :] * u1 + w[2:3, :] * u)
    conv_ref[...] = _rms(y, gconv_ref[...]).astype(bf16)

    cost = cost_ref[...]
    sint = sint_ref[...]
    half = HEAD_DIM // 2
    scale = HEAD_DIM ** -0.5
    for h in range(NSA_HEADS):
        t1 = pt[h * HEAD_DIM:h * HEAD_DIM + half, :]
        t2 = pt[h * HEAD_DIM + half:(h + 1) * HEAD_DIM, :]
        qT_ref[h * HEAD_DIM:h * HEAD_DIM + half, :] = ((t1 * cost - t2 * sint) * scale).astype(bf16)
        qT_ref[h * HEAD_DIM + half:(h + 1) * HEAD_DIM, :] = ((t2 * cost + t1 * sint) * scale).astype(bf16)
    r0 = NSA_WIDTH
    ones_rows = jnp.where(lax.broadcasted_iota(i32, (V_EXT - HEAD_DIM, tm), 0) == 0, 1.0, 0.0).astype(bf16)
    for vT_ref, base in ((vsT_ref, r0), (vwT_ref, r0 + KV_WIDTH)):
        for g in range(NSA_KV_HEADS):
            vT_ref[g * V_EXT:g * V_EXT + HEAD_DIM, :] = pt[base + g * HEAD_DIM:base + (g + 1) * HEAD_DIM, :].astype(bf16)
            vT_ref[g * V_EXT + HEAD_DIM:(g + 1) * V_EXT, :] = ones_rows
    gT_ref[...] = jax.nn.sigmoid(pt[r0 + 2 * KV_WIDTH:r0 + 2 * KV_WIDTH + 32, :])


def _inproj(x2, g_mix, w_row, w_t, cosr, sinr, cost, sint, conv_w8, g_conv, seq):
    T = x2.shape[0]
    tm = TM_IN
    n_row = w_row.shape[1]
    n_t = w_t.shape[0]
    row = lambda i: (i, 0)
    col = lambda i: (0, i)
    const = lambda i: (0, 0)
    out_shape = (
        jax.ShapeDtypeStruct((NSA_WIDTH, T), bf16),
        jax.ShapeDtypeStruct((T, KV_WIDTH), f32),
        jax.ShapeDtypeStruct((T, KV_WIDTH), f32),
        jax.ShapeDtypeStruct((T, 2 * KV_WIDTH), bf16),
        jax.ShapeDtypeStruct((T, KV_WIDTH), bf16),
        jax.ShapeDtypeStruct((NSA_KV_HEADS * V_EXT, T), bf16),
        jax.ShapeDtypeStruct((NSA_KV_HEADS * V_EXT, T), bf16),
        jax.ShapeDtypeStruct((32, T), f32),
        jax.ShapeDtypeStruct((T, CONV_WIDTH), bf16),
    )
    out_specs = (
        pl.BlockSpec((NSA_WIDTH, tm), col),
        pl.BlockSpec((tm, KV_WIDTH), row), pl.BlockSpec((tm, KV_WIDTH), row),
        pl.BlockSpec((tm, 2 * KV_WIDTH), row), pl.BlockSpec((tm, KV_WIDTH), row),
        pl.BlockSpec((NSA_KV_HEADS * V_EXT, tm), col), pl.BlockSpec((NSA_KV_HEADS * V_EXT, tm), col),
        pl.BlockSpec((32, tm), col),
        pl.BlockSpec((tm, CONV_WIDTH), row),
    )
    in_specs = [
        pl.BlockSpec((tm, D_MODEL), row),
        pl.BlockSpec((1, D_MODEL), const),
        pl.BlockSpec((D_MODEL, n_row), const),
        pl.BlockSpec((n_t, D_MODEL), const),
        pl.BlockSpec((tm, KV_WIDTH), row), pl.BlockSpec((tm, KV_WIDTH), row),
        pl.BlockSpec((HEAD_DIM // 2, tm), col), pl.BlockSpec((HEAD_DIM // 2, tm), col),
        pl.BlockSpec((SUBLANES, CONV_WIDTH), const),
        pl.BlockSpec((1, CONV_WIDTH), const),
    ]
    return pl.pallas_call(
        functools.partial(_inproj_kernel, tiles_per_seq=seq // tm),
        grid=(T // tm,), in_specs=in_specs, out_specs=out_specs, out_shape=out_shape,
        scratch_shapes=[pltpu.VMEM((tm + 2 * SUBLANES, CONV_WIDTH), f32)],
        compiler_params=_cparams(1), name="inproj",
    )(x2, g_mix, w_row, w_t, cosr, sinr, cost, sint, conv_w8, g_conv)


def _compress_kernel(xk_ref, xv_ref, pk_ref, pv_ref, w1k_ref, w1v_ref, b1k_ref, b1v_ref,
                     w2k_ref, w2vT_ref, kcc_ref, vcT_ref):
    ncp = xk_ref.shape[0] // CMP_STRIDE

    def hidden(x_ref, p_ref, w1_ref, b1_ref):
        x = jnp.concatenate([x_ref[pl.ds(l, ncp, stride=CMP_STRIDE), :] for l in range(CMP_STRIDE)], axis=1)
        lo = (x + p_ref[0:1, :]).astype(bf16)
        hi = (x + p_ref[1:2, :]).astype(bf16)
        a = jnp.dot(lo, w1_ref[0], preferred_element_type=f32)
        b = jnp.dot(hi, w1_ref[1], preferred_element_type=f32)
        pre = a + pltpu.roll(b, ncp - 1, 0) + b1_ref[...]
        return jax.nn.gelu(pre).astype(bf16)

    hk = hidden(xk_ref, pk_ref, w1k_ref, b1k_ref)
    hv = hidden(xv_ref, pv_ref, w1v_ref, b1v_ref)
    for g in range(NSA_KV_HEADS):
        sl = slice(g * CMP_HIDDEN, (g + 1) * CMP_HIDDEN)
        kcc_ref[0, g] = jnp.dot(hk[:, sl], w2k_ref[...], preferred_element_type=f32).astype(bf16)
        vcT_ref[0, g] = lax.dot_general(w2vT_ref[...], hv[:, sl], (((1,), (1,)), ((), ())),
                                        preferred_element_type=f32).astype(bf16)


def _compress(kc_rows, vc_rows, pk, pv, w1k, w1v, b1k, b1v, w2k, w2vT, batch, seq):
    ncp = seq // CMP_STRIDE
    wide = CMP_STRIDE * KV_WIDTH
    c2 = lambda b: (0, 0)
    c3 = lambda b: (0, 0, 0)
    in_specs = [
        pl.BlockSpec((seq, KV_WIDTH), lambda b: (b, 0)),
        pl.BlockSpec((seq, KV_WIDTH), lambda b: (b, 0)),
        pl.BlockSpec((SUBLANES, wide), c2), pl.BlockSpec((SUBLANES, wide), c2),
        pl.BlockSpec((2, wide, 2 * CMP_HIDDEN), c3), pl.BlockSpec((2, wide, 2 * CMP_HIDDEN), c3),
        pl.BlockSpec((1, 2 * CMP_HIDDEN), c2), pl.BlockSpec((1, 2 * CMP_HIDDEN), c2),
        pl.BlockSpec((CMP_HIDDEN, HEAD_DIM), c2), pl.BlockSpec((HEAD_DIM, CMP_HIDDEN), c2),
    ]
    out_shape = (jax.ShapeDtypeStruct((batch, NSA_KV_HEADS, ncp, HEAD_DIM), bf16),
                 jax.ShapeDtypeStruct((batch, NSA_KV_HEADS, HEAD_DIM, ncp), bf16))
    out_specs = (pl.BlockSpec((1, NSA_KV_HEADS, ncp, HEAD_DIM), lambda b: (b, 0, 0, 0)),
                 pl.BlockSpec((1, NSA_KV_HEADS, HEAD_DIM, ncp), lambda b: (b, 0, 0, 0)))
    return pl.pallas_call(
        _compress_kernel, grid=(batch,), in_specs=in_specs, out_specs=out_specs,
        out_shape=out_shape, compiler_params=_cparams(1), name="compress",
    )(kc_rows, vc_rows, pk, pv, w1k, w1v, b1k, b1v, w2k, w2vT)


_NQ = HPG * TQ
_COL_BLOCKS = [slice(c * LANES, (c + 1) * LANES) for c in range(_NQ // LANES)]


def _compressed_branch(q4, kcc, vcT, ov, s0):
    ncp = kcc.shape[0]
    ns = ov.shape[0]
    s_lane = s0 + (lax.broadcasted_iota(i32, (1, _NQ), 1) & (TQ - 1))
    sc = jnp.dot(kcc, q4, preferred_element_type=f32)
    yield
    c_end = lax.broadcasted_iota(i32, (ncp, 1), 0) * CMP_STRIDE + (CMP_LEN - 1)
    blocks = []
    for cs in _COL_BLOCKS:
        cmask = c_end <= s_lane[:, cs]
        scm = jnp.where(cmask, sc[:, cs], NEG_INF)
        e_c = jnp.where(cmask, jnp.exp(scm - jnp.max(scm, axis=0, keepdims=True)), 0.0)
        l_c = jnp.sum(e_c, axis=0, keepdims=True)
        blocks.append(e_c * jnp.where(l_c > 0.0, 1.0 / l_c, 0.0))
    p_c = jnp.concatenate(blocks, axis=1)
    o_cmp = jnp.dot(vcT, p_c.astype(bf16), preferred_element_type=f32)
    yield

    j_blk = lax.broadcasted_iota(i32, (ns, 1), 0)
    cur = (s0 + lax.broadcasted_iota(i32, (1, TQ), 1)) // SEL_LEN
    n_live = (s0 + TQ - 1) // SEL_LEN + 1
    if n_live <= N_SEL:
        return o_cmp, jnp.where(j_blk <= cur, 0.0, NEG_INF).astype(bf16)

    ps = p_c[:, 0:TQ]
    for h in range(1, HPG):
        ps = ps + p_c[:, h * TQ:(h + 1) * TQ]
    p_hi = ps.astype(bf16)
    r1 = ps - p_hi.astype(f32)
    p_mid = r1.astype(bf16)
    p_lo = (r1 - p_mid.astype(f32)).astype(bf16)
    imp = (jnp.dot(ov, p_hi, preferred_element_type=f32) + jnp.dot(ov, p_mid, preferred_element_type=f32)
           + jnp.dot(ov, p_lo, preferred_element_type=f32))
    yield
    forced = (j_blk == 0) | (j_blk == cur) | (j_blk == cur - 1)
    imp = jnp.where(forced, FORCED, jnp.where(j_blk > cur, NEG_INF, imp))
    rank = jnp.zeros((ns, TQ), f32)
    for i in range(min(n_live, ns)):
        row = imp[i:i + 1, :]
        tie_before = jnp.where(j_blk > i, 1.0, 0.0)
        rank = rank + jnp.where(row > imp, 1.0, jnp.where(row == imp, tie_before, 0.0))
    sel_bias = jnp.where(rank < float(min(N_SEL, ns)), 0.0, NEG_INF).astype(bf16)
    return o_cmp, sel_bias


def _attend(tiles, q_op):
    m = jnp.full((1, _NQ), NEG_INF, bf16)
    acc = jnp.zeros((V_EXT, _NQ), f32)
    s_next = jnp.dot(tiles[0][0](), q_op, preferred_element_type=f32)
    for t, (_, values_t, bias) in enumerate(tiles):
        sT = s_next
        if t + 1 < len(tiles):
            s_next = jnp.dot(tiles[t + 1][0](), q_op, preferred_element_type=f32)
        yield
        p_blocks, m_blocks, a_blocks = [], [], []
        for c, cs in enumerate(_COL_BLOCKS):
            s = sT[:, cs]
            if bias is not None:
                b0 = (c % (TQ // LANES)) * LANES
                s = s + bias[:, b0:b0 + LANES]
            s = s.astype(bf16)
            m_o = m[:, cs]
            m_n = jnp.maximum(m_o, jnp.max(s, axis=0, keepdims=True))
            p_blocks.append(jnp.exp(s - m_n))
            a_blocks.append(jnp.exp(m_o.astype(f32) - m_n.astype(f32)))
            m_blocks.append(m_n)
        m = jnp.concatenate(m_blocks, axis=1)
        pv = jnp.dot(values_t(), jnp.concatenate(p_blocks, axis=1), preferred_element_type=f32)
        yield
        acc = acc * jnp.concatenate(a_blocks, axis=1) + pv
    return acc[0:HEAD_DIM, :] * (1.0 / acc[HEAD_DIM:HEAD_DIM + 1, :])


def _nsa_pair(pi, nqt, g, qa_ref, qb_ref, ks_ref, kw_ref, vsT_ref, vwT_ref, kcc_ref, vcT_ref, ga_ref, gb_ref,
              ovT_ref, cbias_ref, wbias_ref, o_ref):
    q_tiles = (pi, nqt - 1 - pi)
    ns = ovT_ref.shape[0]
    kcc = kcc_ref[0, 0]
    vcT = vcT_ref[0, 0]
    ov = ovT_ref[...]
    cbias = cbias_ref[...]
    n_back = WINDOW // TK

    def tile(k_ref, vT_ref, kt, bias):
        return (lambda: k_ref[kt * TK:(kt + 1) * TK, :], lambda: vT_ref[:, kt * TK:(kt + 1) * TK], bias)

    def query_tile(slot, q_ref, g_ref, qt):
        q4 = jnp.concatenate([q_ref[h * HEAD_DIM:(h + 1) * HEAD_DIM, :] for h in range(HPG)], axis=1)
        zeros = jnp.zeros_like(q4)
        q_win = jnp.where(g == 0, jnp.concatenate([q4, zeros], axis=0), jnp.concatenate([zeros, q4], axis=0))

        o_win = yield from _attend(
            [tile(kw_ref, vwT_ref, qt - j, cbias if j == 0 else (wbias_ref[...] if j == n_back else None))
             for j in range(min(n_back, qt) + 1)], q_win)

        o_cmp, sel_bias = yield from _compressed_branch(q4, kcc, vcT, ov, qt * TQ)

        q_sel = jnp.concatenate([q_win, jnp.concatenate([sel_bias] * HPG, axis=1),
                                 jnp.zeros((KV_WIDTH - ns, _NQ), bf16)], axis=0)
        o_slc = yield from _attend(
            [tile(ks_ref, vsT_ref, qt, cbias)] + [tile(ks_ref, vsT_ref, kt, None) for kt in range(qt)], q_sel)

        gates = g_ref[...]
        for h in range(HPG):
            sl = slice(h * TQ, (h + 1) * TQ)
            o = (gates[3 * h:3 * h + 1, :] * o_cmp[:, sl] + gates[3 * h + 1:3 * h + 2, :] * o_slc[:, sl]
                 + gates[3 * h + 2:3 * h + 3, :] * o_win[:, sl])
            o_ref[h * HEAD_DIM:(h + 1) * HEAD_DIM, slot * TQ:(slot + 1) * TQ] = o.astype(bf16)

    _interleave([query_tile(slot, q_ref, g_ref, qt)
                 for slot, (q_ref, g_ref, qt) in enumerate(zip((qa_ref, qb_ref), (ga_ref, gb_ref), q_tiles))])


def _nsa_kernel(*refs, nqt):
    for pi in range(nqt // 2):
        @pl.when(pl.program_id(0) == pi)
        def _():
            _nsa_pair(pi, nqt, pl.program_id(2), *refs)


def _nsa_tile_position(qt, nqt):
    return jnp.where(qt < nqt // 2, 2 * qt, 2 * (nqt - 1 - qt) + 1)


def _nsa(qT, ks, kw, vsT, vwT, kcc, vcT, gT, ovT, batch, seq):
    T = batch * seq
    nqt = seq // TQ
    ncp = kcc.shape[2]
    ns = seq // SEL_LEN
    nq = HPG * TQ
    gw = HPG * HEAD_DIM
    assert TQ == TK and WINDOW % TK == 0 and ns <= KV_WIDTH
    assert nqt % 2 == 0 and nqt // 2 >= WINDOW // TK
    kl = jnp.arange(TK)[:, None]
    ql = jnp.arange(TQ)[None, :]
    cbias = jnp.where(kl <= ql, 0.0, NEG_INF).astype(f32)
    wbias = jnp.where(kl > ql, 0.0, NEG_INF).astype(f32)
    amap = lambda p, b, g: (g, b * nqt + p)
    bmap = lambda p, b, g: (g, b * nqt + nqt - 1 - p)
    const = lambda p, b, g: (0, 0)
    in_specs = [
        pl.BlockSpec((gw, TQ), amap), pl.BlockSpec((gw, TQ), bmap),
        pl.BlockSpec((seq, 2 * KV_WIDTH), lambda p, b, g: (b, 0)),
        pl.BlockSpec((seq, KV_WIDTH), lambda p, b, g: (b, 0)),
        pl.BlockSpec((V_EXT, seq), lambda p, b, g: (g, b)),
        pl.BlockSpec((V_EXT, seq), lambda p, b, g: (g, b)),
        pl.BlockSpec((1, 1, ncp, HEAD_DIM), lambda p, b, g: (b, g, 0, 0)),
        pl.BlockSpec((1, 1, HEAD_DIM, ncp), lambda p, b, g: (b, g, 0, 0)),
        pl.BlockSpec((16, TQ), amap), pl.BlockSpec((16, TQ), bmap),
        pl.BlockSpec((ns, ncp), const),
        pl.BlockSpec((TK, TQ), const),
        pl.BlockSpec((TK, TQ), const),
    ]
    return pl.pallas_call(
        functools.partial(_nsa_kernel, nqt=nqt), grid=(nqt // 2, batch, NSA_KV_HEADS), in_specs=in_specs,
        out_specs=pl.BlockSpec((gw, 2 * TQ), lambda p, b, g: (g, b * (nqt // 2) + p)),
        out_shape=jax.ShapeDtypeStruct((NSA_WIDTH, T), bf16),
        compiler_params=_cparams(3), name="nsa",
    )(qT, qT, ks, kw, vsT, vwT, kcc, vcT, gT, gT, ovT, cbias, wbias)


def _memkv_kernel(mem_ref, g_ref, wkT_ref, wv_ref, kT_ref, v_ref):
    mb = _rms(mem_ref[0], g_ref[...]).astype(bf16)
    kT_ref[0] = lax.dot_general(wkT_ref[...], mb, (((1,), (1,)), ((), ())),
                                preferred_element_type=f32).astype(bf16)
    v_ref[0] = jnp.dot(mb, wv_ref[...], preferred_element_type=f32).astype(bf16)


def _memkv(mem, g_mem, wkT, wv):
    batch, n_mem, _ = mem.shape
    c2 = lambda b: (0, 0)
    return pl.pallas_call(
        _memkv_kernel, grid=(batch,),
        in_specs=[pl.BlockSpec((1, n_mem, D_MODEL), lambda b: (b, 0, 0)),
                  pl.BlockSpec((1, D_MODEL), c2),
                  pl.BlockSpec((D_MODEL, D_MODEL), c2), pl.BlockSpec((D_MODEL, D_MODEL), c2)],
        out_specs=(pl.BlockSpec((1, D_MODEL, n_mem), lambda b: (b, 0, 0)),
                   pl.BlockSpec((1, n_mem, D_MODEL), lambda b: (b, 0, 0))),
        out_shape=(jax.ShapeDtypeStruct((batch, D_MODEL, n_mem), bf16),
                   jax.ShapeDtypeStruct((batch, n_mem, D_MODEL), bf16)),
        compiler_params=_cparams(1), name="memkv",
    )(mem, g_mem, wkT, wv)


def _post_kernel(x_ref, oTa_ref, oTb_ref, conv_ref, gnsa_ref, wout_ref, gx_ref, wq_ref, kT_ref, v_ref, wo_ref,
                 gmoe_ref, wr_ref, br_ref,
                 h2_ref, xn2_ref, eidx_ref, gate_ref, sel_ref):
    stages = []
    for sub, oT_ref in enumerate((oTa_ref, oTb_ref)):
        rows = slice(sub * TQ, (sub + 1) * TQ)
        stages.append(_post_rows(
            x_ref.at[rows], oT_ref, conv_ref.at[rows], gnsa_ref, wout_ref, gx_ref, wq_ref, kT_ref, v_ref,
            wo_ref, gmoe_ref, wr_ref, br_ref, h2_ref.at[rows],
            xn2_ref.at[sub * TQ * ROW_SUB:(sub + 1) * TQ * ROW_SUB], eidx_ref.at[rows], gate_ref.at[rows],
            sel_ref.at[rows]))
    _interleave(stages)


_DONE = object()


def _interleave(generators):
    live = list(generators)
    while live:
        live = [g for g in live if next(g, _DONE) is not _DONE]


def _post_rows(x_ref, oT_ref, conv_ref, gnsa_ref, wout_ref, gx_ref, wq_ref, kT_ref, v_ref, wo_ref,
               gmoe_ref, wr_ref, br_ref,
               h2_ref, xn2_ref, eidx_ref, gate_ref, sel_ref):
    tm = x_ref.shape[0]
    oT = oT_ref[...].astype(f32)
    onT = (oT * lax.rsqrt(jnp.mean(oT * oT, axis=0, keepdims=True) + RMS_EPS) * gnsa_ref[...]).astype(bf16)
    mix = lax.dot_general(onT, wout_ref[0:NSA_WIDTH, :], (((0,), (0,)), ((), ())),
                          preferred_element_type=f32)
    mix = mix + jnp.dot(conv_ref[...], wout_ref[NSA_WIDTH:D_MODEL, :], preferred_element_type=f32)
    yield
    h1 = x_ref[...] + mix

    hn = _rms(h1, gx_ref[...]).astype(bf16)
    q = (jnp.dot(hn, wq_ref[...], preferred_element_type=f32) * (XATTN_HEAD_DIM ** -0.5)).astype(bf16)
    yield
    head_slices = [slice(h * XATTN_HEAD_DIM, (h + 1) * XATTN_HEAD_DIM) for h in range(XATTN_HEADS)]
    scores = [jnp.dot(q[:, sl], kT_ref[0, sl, :], preferred_element_type=f32) for sl in head_slices]
    yield
    heads = []
    for s, sl in zip(scores, head_slices):
        e = jnp.exp(s - jnp.max(s, axis=-1, keepdims=True))
        p = e * (1.0 / jnp.sum(e, axis=-1, keepdims=True))
        heads.append(jnp.dot(p.astype(bf16), v_ref[0, :, sl], preferred_element_type=f32))
    yield
    o = jnp.concatenate(heads, axis=1).astype(bf16)
    h2 = h1 + jnp.dot(o, wo_ref[...], preferred_element_type=f32)
    yield
    h2_ref[...] = h2

    xn2 = _rms(h2, gmoe_ref[...])
    for s_ in range(ROW_SUB):
        xn2_ref[pl.ds(s_, tm, stride=ROW_SUB), :] = xn2[:, s_ * LANES:(s_ + 1) * LANES]

    logits = jnp.dot(xn2.astype(bf16), wr_ref[...], preferred_element_type=f32) + br_ref[...]
    yield
    lane = lax.broadcasted_iota(i32, (tm, LANES), 1)
    work = logits
    sel = jnp.zeros((tm, LANES), f32)
    eidx = jnp.zeros((tm, LANES), i32)
    vals = []
    for k in range(TOP_K):
        mk = jnp.max(work, axis=-1, keepdims=True)
        ik = jnp.min(jnp.where(work == mk, lane, LANES), axis=-1, keepdims=True)
        hit = lane == ik
        work = jnp.where(hit, -jnp.inf, work)
        sel = jnp.where(hit, 1.0, sel)
        eidx = jnp.where(lane == k, ik, eidx)
        vals.append(mk)
    es = [jnp.exp(v - vals[0]) for v in vals]
    den = es[0]
    for e in es[1:]:
        den = den + e
    gate = jnp.zeros((tm, LANES), f32)
    for k in range(TOP_K):
        gate = jnp.where(lane == k, es[k] / den, gate)
    eidx_ref[...] = eidx
    gate_ref[...] = gate
    sel_ref[...] = sel.astype(bf16)


def _post(x2, oT, conv_n, g_nsa_col, w_out, g_x, w_q, kT, v, w_o, g_moe, w_r, b_r, seq):
    T = x2.shape[0]
    tm = TM_POST
    assert tm == 2 * TQ
    tps = seq // tm
    nqt = seq // TQ
    n_mem = v.shape[1]
    row = lambda i: (i, 0)
    const = lambda i: (0, 0)

    def o_tile(sub):
        return lambda i: (0, (i // tps) * nqt + _nsa_tile_position(2 * (i % tps) + sub, nqt))

    in_specs = [
        pl.BlockSpec((tm, D_MODEL), row),
        pl.BlockSpec((NSA_WIDTH, TQ), o_tile(0)), pl.BlockSpec((NSA_WIDTH, TQ), o_tile(1)),
        pl.BlockSpec((tm, CONV_WIDTH), row),
        pl.BlockSpec((NSA_WIDTH, 1), const),
        pl.BlockSpec((D_MODEL, D_MODEL), const),
        pl.BlockSpec((1, D_MODEL), const),
        pl.BlockSpec((D_MODEL, D_MODEL), const),
        pl.BlockSpec((1, D_MODEL, n_mem), lambda i: (i // tps, 0, 0)),
        pl.BlockSpec((1, n_mem, D_MODEL), lambda i: (i // tps, 0, 0)),
        pl.BlockSpec((D_MODEL, D_MODEL), const),
        pl.BlockSpec((1, D_MODEL), const),
        pl.BlockSpec((D_MODEL, LANES), const),
        pl.BlockSpec((1, LANES), const),
    ]
    out_shape = (jax.ShapeDtypeStruct((T, D_MODEL), f32),
                 jax.ShapeDtypeStruct((T * ROW_SUB, LANES), f32),
                 jax.ShapeDtypeStruct((T, LANES), i32),
                 jax.ShapeDtypeStruct((T, LANES), f32),
                 jax.ShapeDtypeStruct((T, LANES), bf16))
    out_specs = (pl.BlockSpec((tm, D_MODEL), row),
                 pl.BlockSpec((tm * ROW_SUB, LANES), row),
                 pl.BlockSpec((tm, LANES), row), pl.BlockSpec((tm, LANES), row),
                 pl.BlockSpec((tm, LANES), row))
    return pl.pallas_call(
        _post_kernel, grid=(T // tm,), in_specs=in_specs, out_specs=out_specs, out_shape=out_shape,
        compiler_params=_cparams(1), name="post",
    )(x2, oT, oT, conv_n, g_nsa_col, w_out, g_x, w_q, kT, v, w_o, g_moe, w_r, b_r)


def _count_kernel(sel_ref, cum_ref, cnt_ref, carry):
    i = pl.program_id(0)
    tm = sel_ref.shape[0]

    @pl.when(i == 0)
    def _():
        carry[...] = jnp.zeros_like(carry)

    sel = sel_ref[...]
    r = lax.broadcasted_iota(i32, (tm, tm), 0)
    c = lax.broadcasted_iota(i32, (tm, tm), 1)
    strict_lower = jnp.where(c < r, 1.0, 0.0).astype(bf16)
    base = carry[0:1, :]
    cum_ref[...] = jnp.dot(strict_lower, sel, preferred_element_type=f32) + base
    total = base + jnp.sum(sel.astype(f32), axis=0, keepdims=True)
    carry[...] = jnp.broadcast_to(total, carry.shape)
    cnt_ref[...] = jnp.broadcast_to(total, cnt_ref.shape)


def _count(sel):
    T = sel.shape[0]
    tm = TM_ROUTE
    return pl.pallas_call(
        _count_kernel, grid=(T // tm,),
        in_specs=[pl.BlockSpec((tm, LANES), lambda i: (i, 0))],
        out_specs=(pl.BlockSpec((tm, LANES), lambda i: (i, 0)),
                   pl.BlockSpec((SUBLANES, LANES), lambda i: (0, 0))),
        out_shape=(jax.ShapeDtypeStruct((T, LANES), f32), jax.ShapeDtypeStruct((SUBLANES, LANES), f32)),
        scratch_shapes=[pltpu.VMEM((SUBLANES, LANES), f32)],
        compiler_params=_cparams(1), name="route_count",
    )(sel)


def _dest_kernel(cum_ref, eidx_ref, pstart_ref, dest_ref):
    tm = cum_ref.shape[0]
    lane = lax.broadcasted_iota(i32, (tm, LANES), 1)
    row_of = cum_ref[...] + pstart_ref[0:1, :]
    eidx = eidx_ref[...]
    dest = jnp.zeros((tm, LANES), f32)
    for k in range(TOP_K):
        ek = eidx[:, k:k + 1]
        dk = jnp.sum(jnp.where(lane == ek, row_of, 0.0), axis=-1, keepdims=True)
        dest = jnp.where(lane == k, dk, dest)
    dest_ref[...] = dest.astype(i32)


def _dest(cum, eidx, pstart8):
    T = cum.shape[0]
    tm = TM_ROUTE
    row = lambda i: (i, 0)
    return pl.pallas_call(
        _dest_kernel, grid=(T // tm,),
        in_specs=[pl.BlockSpec((tm, LANES), row), pl.BlockSpec((tm, LANES), row),
                  pl.BlockSpec((SUBLANES, LANES), lambda i: (0, 0))],
        out_specs=pl.BlockSpec((tm, LANES), row),
        out_shape=jax.ShapeDtypeStruct((T, LANES), i32),
        compiler_params=_cparams(1), name="route_dest",
    )(cum, eidx, pstart8)


def _invert_kernel(dest_ref, init_ref, slot_ref, sem):
    cp = pltpu.make_async_copy(init_ref, slot_ref, sem)
    cp.start()
    cp.wait()
    chunk = LANES

    def body(j, carry):
        base = j * chunk
        for l in range(chunk):
            slot_ref[dest_ref[base + l]] = base + l
        return carry

    lax.fori_loop(0, dest_ref.shape[0] // chunk, body, 0)


def _invert(dest_flat, n_steps):
    parity = jnp.concatenate([jnp.arange(n_steps, dtype=i32) % 2, jnp.ones((1,), i32)])
    sink = dest_flat.shape[0] + parity[:, None] * BM + jnp.arange(BM, dtype=i32)[None, :]
    return pl.pallas_call(
        _invert_kernel,
        in_specs=[pl.BlockSpec(memory_space=pltpu.SMEM), pl.BlockSpec(memory_space=pltpu.VMEM)],
        out_specs=pl.BlockSpec(memory_space=pltpu.SMEM),
        out_shape=jax.ShapeDtypeStruct(((n_steps + 1) * BM,), i32),
        scratch_shapes=[pltpu.SemaphoreType.DMA],
        compiler_params=pltpu.CompilerParams(vmem_limit_bytes=VMEM_LIMIT), name="route_invert",
    )(dest_flat, sink.reshape(-1))


def _ffn_kernel(blk_e_ref, first_ref, nxt_e_ref, wslot_ref, nused_ref, slot_ref,
                x_hbm, wgu_hbm, bgu_ref, wd_hbm, bd_ref, y_hbm,
                xbuf0, xbuf1, obuf0, obuf1, wgu_f, wd_f, wgu_bf, wd_bf, gsem, ssem, wsem, *, n_tok, sink_row):
    i = pl.program_id(0)
    nused = nused_ref[0]
    n_steps = pl.num_programs(0)
    xbuf = (xbuf0, xbuf1)
    obuf = (obuf0, obuf1)

    def row_window(ref, row):
        return ref.at[pl.ds(pl.multiple_of(row * ROW_SUB, SUBLANES), ROW_SUB)]

    def gather(blk, par, r):
        tok = slot_ref[blk * BM + r] & (n_tok - 1)
        return pltpu.make_async_copy(row_window(x_hbm, tok), xbuf[par].at[pl.ds(r * ROW_SUB, ROW_SUB)],
                                     gsem.at[par])

    def scatter(blk, par, r):
        return pltpu.make_async_copy(obuf[par].at[pl.ds(r * ROW_SUB, ROW_SUB)],
                                     row_window(y_hbm, slot_ref[blk * BM + r]), ssem.at[par])

    def wait_gather(par):
        pltpu.make_async_copy(x_hbm.at[pl.ds(0, BM * ROW_SUB)], xbuf[par], gsem.at[par]).wait()

    def wait_scatter(par):
        pltpu.make_async_copy(obuf[par], y_hbm.at[pl.ds(0, BM * ROW_SUB)], ssem.at[par]).wait()

    def weight_copies(e, ws):
        return (pltpu.make_async_copy(wgu_hbm.at[e], wgu_f.at[ws], wsem.at[ws, 0]),
                pltpu.make_async_copy(wd_hbm.at[e], wd_f.at[ws], wsem.at[ws, 1]))

    @pl.when(i == 0)
    def _():
        obuf0[...] = jnp.zeros(obuf0.shape, f32)
        obuf1[...] = jnp.zeros(obuf1.shape, f32)
        pltpu.make_async_copy(obuf0, y_hbm.at[pl.ds(sink_row * ROW_SUB, BM * ROW_SUB)], ssem.at[0]).start()
        for cp in weight_copies(blk_e_ref[0], 0):
            cp.start(priority=1)
        for r in range(BM):
            gather(0, 0, r).start()

    @pl.when((i < nused) & (first_ref[i] == 1))
    def _():
        ws = wslot_ref[i]
        for cp in weight_copies(blk_e_ref[i], ws):
            cp.wait()

        @pl.when(nxt_e_ref[i] >= 0)
        def _():
            for cp in weight_copies(nxt_e_ref[i], 1 - ws):
                cp.start(priority=1)

        wgu_bf[...] = wgu_f[ws].astype(bf16)
        wd_bf[...] = wd_f[ws].astype(bf16)

    def block(par):
        prev = jnp.where(i == 0, n_steps, i - 1)
        wait_gather(par)
        x = jnp.concatenate([xbuf[par][pl.ds(s_, BM, stride=ROW_SUB), :] for s_ in range(ROW_SUB)],
                            axis=1).astype(bf16)
        for r in range(BM):
            scatter(prev, 1 - par, r).start(priority=1)
        for r in range(BM):
            gather(i + 1, 1 - par, r).start()
        gu = jnp.dot(x, wgu_bf[...], preferred_element_type=f32) + bgu_ref[0]
        gg = jnp.minimum(gu[:, 0:D_FF], SWIGLU_LIMIT)
        uu = jnp.clip(gu[:, D_FF:2 * D_FF], -SWIGLU_LIMIT, SWIGLU_LIMIT)
        hmid = (uu + 1.0) * (gg * jax.nn.sigmoid(SWIGLU_ALPHA * gg))
        out = jnp.dot(hmid.astype(bf16), wd_bf[...], preferred_element_type=f32) + bd_ref[0]
        wait_scatter(par)
        for s_ in range(ROW_SUB):
            obuf[par][pl.ds(s_, BM, stride=ROW_SUB), :] = out[:, s_ * LANES:(s_ + 1) * LANES]

    for par in range(2):
        @pl.when((i < nused) & (i % 2 == par))
        def _():
            block(par)

        @pl.when((i == nused) & (i % 2 == par))
        def _():
            for r in range(BM):
                scatter(i - 1, 1 - par, r).start(priority=1)
            wait_gather(par)
            wait_scatter(par)
            wait_scatter(1 - par)


def _ffn(blk_e, first, nxt_e, wslot, nused, slots, xn2_rows, w_gu, b_gu, w_d, b_d):
    n_steps = blk_e.shape[0]
    n_tok = xn2_rows.shape[0] // ROW_SUB
    assert n_tok & (n_tok - 1) == 0
    n_tok_rows = n_tok * TOP_K
    emap = lambda i, be, *_: (be[i], 0, 0)
    grid_spec = pltpu.PrefetchScalarGridSpec(
        num_scalar_prefetch=6, grid=(n_steps,),
        in_specs=[pl.BlockSpec(memory_space=pl.ANY),
                  pl.BlockSpec(memory_space=pl.ANY),
                  pl.BlockSpec((1, 1, 2 * D_FF), emap),
                  pl.BlockSpec(memory_space=pl.ANY),
                  pl.BlockSpec((1, 1, D_MODEL), emap)],
        out_specs=pl.BlockSpec(memory_space=pl.ANY),
        scratch_shapes=[pltpu.VMEM((BM * ROW_SUB, LANES), f32), pltpu.VMEM((BM * ROW_SUB, LANES), f32),
                        pltpu.VMEM((BM * ROW_SUB, LANES), f32), pltpu.VMEM((BM * ROW_SUB, LANES), f32),
                        pltpu.VMEM((2, D_MODEL, 2 * D_FF), f32), pltpu.VMEM((2, D_FF, D_MODEL), f32),
                        pltpu.VMEM((D_MODEL, 2 * D_FF), bf16), pltpu.VMEM((D_FF, D_MODEL), bf16),
                        pltpu.SemaphoreType.DMA((2,)), pltpu.SemaphoreType.DMA((2,)),
                        pltpu.SemaphoreType.DMA((2, 2))],
    )
    return pl.pallas_call(
        functools.partial(_ffn_kernel, n_tok=n_tok, sink_row=n_tok_rows), grid_spec=grid_spec,
        out_shape=jax.ShapeDtypeStruct(((n_tok_rows + 2 * BM) * ROW_SUB, LANES), f32),
        compiler_params=_cparams(1), name="ffn",
    )(blk_e, first, nxt_e, wslot, nused, slots, xn2_rows, w_gu, b_gu, w_d, b_d)


def _combine_kernel(*refs, final_norm):
    y_refs = refs[:TOP_K]
    gate_ref, h2_ref, gfin_ref, o_ref = refs[TOP_K:]
    tm = h2_ref.shape[0]
    gate = gate_ref[...]
    cols = []
    for s_ in range(ROW_SUB):
        acc = gate[:, 0:1] * y_refs[0][pl.ds(s_, tm, stride=ROW_SUB), :]
        for k in range(1, TOP_K):
            acc = acc + gate[:, k:k + 1] * y_refs[k][pl.ds(s_, tm, stride=ROW_SUB), :]
        cols.append(acc)
    h = h2_ref[...] + jnp.concatenate(cols, axis=1)
    if final_norm:
        h = _rms(h, gfin_ref[...])
    o_ref[...] = h


def _combine(y_rows, gate, h2, g_final, final_norm):
    T = h2.shape[0]
    tm = TM_ROW
    row = lambda i: (i, 0)
    planes = [pl.BlockSpec((tm * ROW_SUB, LANES), functools.partial(lambda i, k: (k * (T // tm) + i, 0), k=k))
              for k in range(TOP_K)]
    return pl.pallas_call(
        functools.partial(_combine_kernel, final_norm=final_norm), grid=(T // tm,),
        in_specs=planes + [pl.BlockSpec((tm, LANES), row), pl.BlockSpec((tm, D_MODEL), row),
                           pl.BlockSpec((1, D_MODEL), lambda i: (0, 0))],
        out_specs=pl.BlockSpec((tm, D_MODEL), row),
        out_shape=jax.ShapeDtypeStruct((T, D_MODEL), f32),
        compiler_params=_cparams(1), name="combine",
    )(*([y_rows] * TOP_K), gate, h2, g_final)


def _prep_inproj_weights(w_in):
    sizes = (NSA_WIDTH,) + (KV_WIDTH,) * 6 + (3 * NSA_HEADS,) + (CONV_WIDTH,) * 3
    offs = [0]
    for s in sizes:
        offs.append(offs[-1] + s)
    seg = lambda n: w_in[:, offs[n]:offs[n + 1]]
    q, kc, vc, ks, vs, kw, vw, gl, ch, cb, cc = (seg(n) for n in range(11))
    w_row = jnp.concatenate([kc, vc, ks, kw, ch, cb, cc], axis=1).astype(bf16)
    gl_g = gl.reshape(D_MODEL, NSA_KV_HEADS, HPG * 3)
    gl_g = jnp.pad(gl_g, ((0, 0), (0, 0), (0, 16 - HPG * 3))).reshape(D_MODEL, NSA_KV_HEADS * 16)
    w_t = jnp.concatenate([q, vs, vw, gl_g], axis=1).T.astype(bf16)
    return w_row, w_t


def _rope_tables(positions):
    half = HEAD_DIM // 2
    inv_freq = ROPE_THETA ** (-jnp.arange(half, dtype=f32) / half)
    ang = positions.reshape(-1).astype(f32)[:, None] * inv_freq
    cos = jnp.cos(ang)
    sin = jnp.sin(ang)
    reps = KV_WIDTH // HEAD_DIM
    cosr = jnp.tile(cos, (1, 2 * reps))
    sinr = jnp.tile(jnp.concatenate([-sin, sin], axis=1), (1, reps))
    return cosr, sinr, cos.T, sin.T


def _cmp_weights(pos, w1, b1):
    w1r = w1.astype(bf16).reshape(2, CMP_STRIDE, HEAD_DIM, CMP_HIDDEN)
    zero = jnp.zeros_like(w1r)
    big = jnp.stack([jnp.concatenate([w1r if g == k else zero for k in range(NSA_KV_HEADS)], axis=-1)
                     for g in range(NSA_KV_HEADS)], axis=2)
    big = big.reshape(2, CMP_STRIDE * KV_WIDTH, NSA_KV_HEADS * CMP_HIDDEN)
    p = jnp.broadcast_to(pos.reshape(2, CMP_STRIDE, 1, HEAD_DIM), (2, CMP_STRIDE, NSA_KV_HEADS, HEAD_DIM))
    p = jnp.pad(p.reshape(2, CMP_STRIDE * KV_WIDTH), ((0, SUBLANES - 2), (0, 0)))
    return big, p, jnp.tile(b1.reshape(1, CMP_HIDDEN), (1, NSA_KV_HEADS))


def _mixer_core(x2, tables, p, batch, seq):
    cosr, sinr, cost, sint = tables
    row1 = lambda v: v.reshape(1, -1)
    w_row, w_t = _prep_inproj_weights(p['w_mix_in'])
    conv_w8 = jnp.pad(p['conv_w'], ((0, SUBLANES - CONV_K), (0, 0)))
    qT, kc, vc, ks, kw, vsT, vwT, gT, conv_n = _inproj(
        x2, row1(p['g_mix_norm']), w_row, w_t, cosr, sinr, cost, sint, conv_w8, row1(p['g_conv_out']), seq)

    w1k, pk, b1k = _cmp_weights(p['cmp_pos_k'], p['cmp_w1_k'], p['cmp_b1_k'])
    w1v, pv, b1v = _cmp_weights(p['cmp_pos_v'], p['cmp_w1_v'], p['cmp_b1_v'])
    kcc, vcT = _compress(kc, vc, pk, pv, w1k, w1v, b1k, b1v,
                         p['cmp_w2_k'].astype(bf16), p['cmp_w2_v'].T.astype(bf16), batch, seq)

    ncp = seq // CMP_STRIDE
    ns = seq // SEL_LEN
    cs = jnp.arange(ncp) * CMP_STRIDE
    js = jnp.arange(ns) * SEL_LEN
    overlap = jnp.clip(jnp.minimum(cs[:, None] + CMP_LEN, js[None, :] + SEL_LEN)
                       - jnp.maximum(cs[:, None], js[None, :]), 0, None).astype(f32) / CMP_LEN
    ovT = overlap.T.astype(bf16)
    oT = _nsa(qT, ks, kw, vsT, vwT, kcc, vcT, gT, ovT, batch, seq)
    return oT, conv_n


def _layer(h, memf, tables, p, final_gain, final_norm):
    batch, seq, _ = h.shape
    T = batch * seq
    x2 = h.reshape(T, D_MODEL)
    row1 = lambda v: v.reshape(1, -1)
    oT, conv_n = _mixer_core(x2, tables, p, batch, seq)

    w_xkv = p['w_xkv']
    kT, v = _memkv(memf, row1(p['g_mem_norm']), w_xkv[:, :D_MODEL].T.astype(bf16),
                   w_xkv[:, D_MODEL:].astype(bf16))
    w_r = jnp.pad(p['w_router'], ((0, 0), (0, LANES - N_EXPERTS))).astype(bf16)
    b_r = jnp.pad(p['b_router'], (0, LANES - N_EXPERTS), constant_values=NEG_INF).reshape(1, LANES)
    h2, xn2_rows, eidx, gate, sel = _post(
        x2, oT, conv_n, p['g_nsa_out'].reshape(NSA_WIDTH, 1), p['w_mix_out'].astype(bf16),
        row1(p['g_xattn_norm']), p['w_xq'].astype(bf16), kT, v, p['w_xo'].astype(bf16),
        row1(p['g_moe_norm']), w_r, b_r, seq)

    cum, cnt = _count(sel)
    counts = cnt[0, :N_EXPERTS].astype(i32)
    padded = (counts + BM - 1) // BM * BM
    pend = jnp.cumsum(padded)
    pstart = pend - padded
    n_steps = (T * TOP_K) // BM + N_EXPERTS + 1
    nused = (pend[-1] // BM).astype(i32)
    step = jnp.arange(n_steps, dtype=i32)
    used = step < nused
    blk_raw = jnp.minimum(jnp.sum((pend[None, :] <= (step * BM)[:, None]).astype(i32), axis=1), N_EXPERTS - 1)
    e_ids = jnp.arange(N_EXPERTS, dtype=i32)
    last_e = jnp.max(jnp.where(padded > 0, e_ids, 0))
    blk_e = jnp.where(used, blk_raw, last_e)
    first = (used & jnp.concatenate([jnp.ones((1,), bool), blk_e[1:] != blk_e[:-1]])).astype(i32)
    wslot = (jnp.cumsum(first) - 1) % 2
    later = (e_ids[None, :] > e_ids[:, None]) & (padded > 0)[None, :]
    nxt_of = jnp.min(jnp.where(later, e_ids[None, :], N_EXPERTS), axis=1)
    nxt_of = jnp.where(nxt_of < N_EXPERTS, nxt_of, -1).astype(i32)
    nxt_e = jnp.sum(jnp.where(blk_e[:, None] == e_ids[None, :], nxt_of[None, :], 0), axis=1)
    pstart8 = jnp.broadcast_to(jnp.pad(pstart.astype(f32), (0, LANES - N_EXPERTS))[None, :], (SUBLANES, LANES))

    dest = _dest(cum, eidx, pstart8)
    slots = _invert(dest[:, :TOP_K].T.reshape(-1), n_steps)
    y_rows = _ffn(blk_e, first, nxt_e, wslot.astype(i32), nused.reshape(1), slots, xn2_rows,
                  p['w_gate_up'], p['b_gate_up'].reshape(N_EXPERTS, 1, 2 * D_FF), p['w_down'],
                  p['b_down'].reshape(N_EXPERTS, 1, D_MODEL))
    out = _combine(y_rows, gate, h2, row1(final_gain), final_norm)
    return out.reshape(batch, seq, D_MODEL)


_LAYER_PARAMS = ('g_mix_norm', 'w_mix_in', 'cmp_pos_k', 'cmp_pos_v', 'cmp_w1_k', 'cmp_b1_k', 'cmp_w2_k',
                 'cmp_w1_v', 'cmp_b1_v', 'cmp_w2_v', 'conv_w', 'g_nsa_out', 'g_conv_out', 'w_mix_out',
                 'g_xattn_norm', 'g_mem_norm', 'w_xq', 'w_xkv', 'w_xo', 'g_moe_norm', 'w_router', 'b_router',
                 'w_gate_up', 'b_gate_up', 'w_down', 'b_down')


def kernel(x, mem, positions, g_mix_norm, w_mix_in, cmp_pos_k, cmp_pos_v, cmp_w1_k, cmp_b1_k, cmp_w2_k, cmp_w1_v, cmp_b1_v, cmp_w2_v, conv_w, g_nsa_out, g_conv_out, w_mix_out, g_xattn_norm, g_mem_norm, w_xq, w_xkv, w_xo, g_moe_norm, w_router, b_router, w_gate_up, b_gate_up, w_down, b_down, g_final):
    stacked = dict(zip(_LAYER_PARAMS, (g_mix_norm, w_mix_in, cmp_pos_k, cmp_pos_v, cmp_w1_k, cmp_b1_k, cmp_w2_k,
                                       cmp_w1_v, cmp_b1_v, cmp_w2_v, conv_w, g_nsa_out, g_conv_out, w_mix_out,
                                       g_xattn_norm, g_mem_norm, w_xq, w_xkv, w_xo, g_moe_norm, w_router,
                                       b_router, w_gate_up, b_gate_up, w_down, b_down)))
    depth = g_mix_norm.shape[0]
    tables = _rope_tables(positions)
    h = x
    for l in range(depth):
        p = {k: v[l] for k, v in stacked.items()}
        last = l == depth - 1
        h = _layer(h, mem, tables, p, g_final, final_norm=last)
    return h
```

```python
import functools

import jax
import jax.numpy as jnp
from jax import lax
from jax.experimental import pallas as pl
from jax.experimental.pallas import tpu as pltpu

f32 = jnp.float32
bf16 = jnp.bfloat16
i32 = jnp.int32

D_MODEL = 1024
HEAD_DIM = 64
NSA_HEADS = 8
NSA_KV_HEADS = 2
HPG = NSA_HEADS // NSA_KV_HEADS
NSA_WIDTH = NSA_HEADS * HEAD_DIM
KV_WIDTH = NSA_KV_HEADS * HEAD_DIM
CONV_WIDTH = D_MODEL - NSA_WIDTH
CONV_K = 3
CMP_LEN = 32
CMP_STRIDE = 16
CMP_HIDDEN = 256
SEL_LEN = 64
N_SEL = 16
WINDOW = 512
ROPE_THETA = 10000.0
XATTN_HEADS = 4
XATTN_HEAD_DIM = D_MODEL // XATTN_HEADS
N_EXPERTS = 32
TOP_K = 4
D_FF = D_MODEL
SWIGLU_LIMIT = 7.0
SWIGLU_ALPHA = 1.702
RMS_EPS = 1e-5
NEG_INF = -1e30
FORCED = 1e30

LANES = 128
SUBLANES = 8
VMEM_LIMIT = 56 * 1024 * 1024

TM_IN = 512
TQ = 256
TK = 256
TM_POST = 512
TM_ROUTE = 512
TM_ROW = 256
V_EXT = HEAD_DIM + 16
BM = 256
ROW_SUB =D_MODEL // LANES


def _cparams(n_axes, **kw):
    return pltpu.CompilerParams(dimension_semantics=("arbitrary",) * n_axes,
                                vmem_limit_bytes=VMEM_LIMIT, **kw)


def _rms(t, gain):
    return t * lax.rsqrt(jnp.mean(t * t, axis=-1, keepdims=True) + RMS_EPS) * gain


def _inproj_kernel(x_ref, g_ref, wr_ref, wt_ref, cosr_ref, sinr_ref, cost_ref, sint_ref,
                   convw_ref, gconv_ref,
                   qT_ref, kc_ref, vc_ref, ks_ref, kw_ref, vsT_ref, vwT_ref, gT_ref, conv_ref,
                   ubuf, *, tiles_per_seq):
    i = pl.program_id(0)
    tm = x_ref.shape[0]
    xb = _rms(x_ref[...], g_ref[...]).astype(bf16)

    c0 = 4 * KV_WIDTH
    pr = jnp.dot(xb, wr_ref[:, 0:c0], preferred_element_type=f32)
    pc = jnp.dot(xb, wr_ref[:, c0:c0 + 3 * CONV_WIDTH], preferred_element_type=f32)
    pt = lax.dot_general(wt_ref[...], xb, (((1,), (1,)), ((), ())), preferred_element_type=f32)
    cosr = cosr_ref[...]
    sinr = sinr_ref[...]
    lane = lax.broadcasted_iota(i32, (tm, KV_WIDTH), 1)
    first_half = (lane & (HEAD_DIM - 1)) < HEAD_DIM // 2

    def rope_rows(t):
        rot = jnp.where(first_half, pltpu.roll(t, KV_WIDTH - HEAD_DIM // 2, 1),
                        pltpu.roll(t, HEAD_DIM // 2, 1))
        return t * cosr + rot * sinr

    kc_ref[...] = rope_rows(pr[:, 0:KV_WIDTH])
    vc_ref[...] = pr[:, KV_WIDTH:2 * KV_WIDTH]
    ks_ref[:, 0:KV_WIDTH] = rope_rows(pr[:, 2 * KV_WIDTH:3 * KV_WIDTH]).astype(bf16)
    tok = (i % tiles_per_seq) * tm + lax.broadcasted_iota(i32, (tm, KV_WIDTH), 0)
    ks_ref[:, KV_WIDTH:2 * KV_WIDTH] = jnp.where(lane == tok // SEL_LEN, 1.0, 0.0).astype(bf16)
    kw_ref[...] = rope_rows(pr[:, 3 * KV_WIDTH:4 * KV_WIDTH]).astype(bf16)

    ch = pc[:, 0:CONV_WIDTH]
    cb = pc[:, CONV_WIDTH:2 * CONV_WIDTH]
    cc = pc[:, 2 * CONV_WIDTH:3 * CONV_WIDTH]
    u = cc * ch

    @pl.when(i % tiles_per_seq == 0)
    def _():
        ubuf[0:SUBLANES, :] = jnp.zeros((SUBLANES, CONV_WIDTH), f32)

    @pl.when(i % tiles_per_seq != 0)
    def _():
        ubuf[0:SUBLANES, :] = ubuf[tm:tm + SUBLANES, :]

    ubuf[SUBLANES:SUBLANES + tm, :] = u
    u1 = ubuf[SUBLANES - 1:SUBLANES - 1 + tm, :]
    u2 = ubuf[SUBLANES - 2:SUBLANES - 2 + tm, :]
    w = convw_ref[...]
    y = cb * (w[0:1, :] * u2 + w[1:2, :] * u1 + w[2:3, :] * u)
    conv_ref[...] = _rms(y, gconv_ref[...]).astype(bf16)

    cost = cost_ref[...]
    sint = sint_ref[...]
    half = HEAD_DIM // 2
    scale = HEAD_DIM ** -0.5
    for h in range(NSA_HEADS):
        t1 = pt[h * HEAD_DIM:h * HEAD_DIM + half, :]
        t2 = pt[h * HEAD_DIM + half:(h + 1) * HEAD_DIM, :]
        qT_ref[h * HEAD_DIM:h * HEAD_DIM + half, :] = ((t1 * cost - t2 * sint) * scale).astype(bf16)
        qT_ref[h * HEAD_DIM + half:(h + 1) * HEAD_DIM, :] = ((t2 * cost + t1 * sint) * scale).astype(bf16)
    r0 = NSA_WIDTH
    ones_rows = jnp.where(lax.broadcasted_iota(i32, (V_EXT - HEAD_DIM, tm), 0) == 0, 1.0, 0.0).astype(bf16)
    for vT_ref, base in ((vsT_ref, r0), (vwT_ref, r0 + KV_WIDTH)):
        for g in range(NSA_KV_HEADS):
            vT_ref[g * V_EXT:g * V_EXT + HEAD_DIM, :] = pt[base + g * HEAD_DIM:base + (g + 1) * HEAD_DIM, :].astype(bf16)
            vT_ref[g * V_EXT + HEAD_DIM:(g + 1) * V_EXT, :] = ones_rows
    gT_ref[...] = jax.nn.sigmoid(pt[r0 + 2 * KV_WIDTH:r0 + 2 * KV_WIDTH + 32, :])


def _inproj(x2, g_mix, w_row, w_t, cosr, sinr, cost, sint, conv_w8, g_conv, seq):
    T = x2.shape[0]
    tm = TM_IN
    n_row = w_row.shape[1]
    n_t = w_t.shape[0]
    row = lambda i: (i, 0)
    col = lambda i: (0, i)
    const = lambda i: (0, 0)
    out_shape = (
        jax.ShapeDtypeStruct((NSA_WIDTH, T), bf16),
        jax.ShapeDtypeStruct((T, KV_WIDTH), f32),
        jax.ShapeDtypeStruct((T, KV_WIDTH), f32),
        jax.ShapeDtypeStruct((T, 2 * KV_WIDTH), bf16),
        jax.ShapeDtypeStruct((T, KV_WIDTH), bf16),
        jax.ShapeDtypeStruct((NSA_KV_HEADS * V_EXT, T), bf16),
        jax.ShapeDtypeStruct((NSA_KV_HEADS * V_EXT, T), bf16),
        jax.ShapeDtypeStruct((32, T), f32),
        jax.ShapeDtypeStruct((T, CONV_WIDTH), bf16),
    )
    out_specs = (
        pl.BlockSpec((NSA_WIDTH, tm), col),
        pl.BlockSpec((tm, KV_WIDTH), row), pl.BlockSpec((tm, KV_WIDTH), row),
        pl.BlockSpec((tm, 2 * KV_WIDTH), row), pl.BlockSpec((tm, KV_WIDTH), row),
        pl.BlockSpec((NSA_KV_HEADS * V_EXT, tm), col), pl.BlockSpec((NSA_KV_HEADS * V_EXT, tm), col),
        pl.BlockSpec((32, tm), col),
        pl.BlockSpec((tm, CONV_WIDTH), row),
    )
    in_specs = [
        pl.BlockSpec((tm, D_MODEL), row),
        pl.BlockSpec((1, D_MODEL), const),
        pl.BlockSpec((D_MODEL, n_row), const),
        pl.BlockSpec((n_t, D_MODEL), const),
        pl.BlockSpec((tm, KV_WIDTH), row), pl.BlockSpec((tm, KV_WIDTH), row),
        pl.BlockSpec((HEAD_DIM // 2, tm), col), pl.BlockSpec((HEAD_DIM // 2, tm), col),
        pl.BlockSpec((SUBLANES, CONV_WIDTH), const),
        pl.BlockSpec((1, CONV_WIDTH), const),
    ]
    return pl.pallas_call(
        functools.partial(_inproj_kernel, tiles_per_seq=seq // tm),
        grid=(T // tm,), in_specs=in_specs, out_specs=out_specs, out_shape=out_shape,
        scratch_shapes=[pltpu.VMEM((tm + 2 * SUBLANES, CONV_WIDTH), f32)],
        compiler_params=_cparams(1), name="inproj",
    )(x2, g_mix, w_row, w_t, cosr, sinr, cost, sint, conv_w8, g_conv)


def _compress_kernel(xk_ref, xv_ref, pk_ref, pv_ref, w1k_ref, w1v_ref, b1k_ref, b1v_ref,
                     w2k_ref, w2vT_ref, kcc_ref, vcT_ref):
    ncp = xk_ref.shape[0] // CMP_STRIDE

    def hidden(x_ref, p_ref, w1_ref, b1_ref):
        x = jnp.concatenate([x_ref[pl.ds(l, ncp, stride=CMP_STRIDE), :] for l in range(CMP_STRIDE)], axis=1)
        lo = (x + p_ref[0:1, :]).astype(bf16)
        hi = (x + p_ref[1:2, :]).astype(bf16)
        a = jnp.dot(lo, w1_ref[0], preferred_element_type=f32)
        b = jnp.dot(hi, w1_ref[1], preferred_element_type=f32)
        pre = a + pltpu.roll(b, ncp - 1, 0) + b1_ref[...]
        return jax.nn.gelu(pre).astype(bf16)

    hk = hidden(xk_ref, pk_ref, w1k_ref, b1k_ref)
    hv = hidden(xv_ref, pv_ref, w1v_ref, b1v_ref)
    for g in range(NSA_KV_HEADS):
        sl = slice(g * CMP_HIDDEN, (g + 1) * CMP_HIDDEN)
        kcc_ref[0, g] = jnp.dot(hk[:, sl], w2k_ref[...], preferred_element_type=f32).astype(bf16)
        vcT_ref[0, g] = lax.dot_general(w2vT_ref[...], hv[:, sl], (((1,), (1,)), ((), ())),
                                        preferred_element_type=f32).astype(bf16)


def _compress(kc_rows, vc_rows, pk, pv, w1k, w1v, b1k, b1v, w2k, w2vT, batch, seq):
    ncp = seq // CMP_STRIDE
    wide = CMP_STRIDE * KV_WIDTH
    c2 = lambda b: (0, 0)
    c3 = lambda b: (0, 0, 0)
    in_specs = [
        pl.BlockSpec((seq, KV_WIDTH), lambda b: (b, 0)),
        pl.BlockSpec((seq, KV_WIDTH), lambda b: (b, 0)),
        pl.BlockSpec((SUBLANES, wide), c2), pl.BlockSpec((SUBLANES, wide), c2),
        pl.BlockSpec((2, wide, 2 * CMP_HIDDEN), c3), pl.BlockSpec((2, wide, 2 * CMP_HIDDEN), c3),
        pl.BlockSpec((1, 2 * CMP_HIDDEN), c2), pl.BlockSpec((1, 2 * CMP_HIDDEN), c2),
        pl.BlockSpec((CMP_HIDDEN, HEAD_DIM), c2), pl.BlockSpec((HEAD_DIM, CMP_HIDDEN), c2),
    ]
    out_shape = (jax.ShapeDtypeStruct((batch, NSA_KV_HEADS, ncp, HEAD_DIM), bf16),
                 jax.ShapeDtypeStruct((batch, NSA_KV_HEADS, HEAD_DIM, ncp), bf16))
    out_specs = (pl.BlockSpec((1, NSA_KV_HEADS, ncp, HEAD_DIM), lambda b: (b, 0, 0, 0)),
                 pl.BlockSpec((1, NSA_KV_HEADS, HEAD_DIM, ncp), lambda b: (b, 0, 0, 0)))
    return pl.pallas_call(
        _compress_kernel, grid=(batch,), in_specs=in_specs, out_specs=out_specs,
        out_shape=out_shape, compiler_params=_cparams(1), name="compress",
    )(kc_rows, vc_rows, pk, pv, w1k, w1v, b1k, b1v, w2k, w2vT)


_NQ = HPG * TQ
_COL_BLOCKS = [slice(c * LANES, (c + 1) * LANES) for c in range(_NQ // LANES)]


def _compressed_branch(q4, kcc, vcT, ov, s0):
    ncp = kcc.shape[0]
    ns = ov.shape[0]
    s_lane = s0 + (lax.broadcasted_iota(i32, (1, _NQ), 1) & (TQ - 1))
    sc = jnp.dot(kcc, q4, preferred_element_type=f32)
    yield
    c_end = lax.broadcasted_iota(i32, (ncp, 1), 0) * CMP_STRIDE + (CMP_LEN - 1)
    blocks = []
    for cs in _COL_BLOCKS:
        cmask = c_end <= s_lane[:, cs]
        scm = jnp.where(cmask, sc[:, cs], NEG_INF)
        e_c = jnp.where(cmask, jnp.exp(scm - jnp.max(scm, axis=0, keepdims=True)), 0.0)
        l_c = jnp.sum(e_c, axis=0, keepdims=True)
        blocks.append(e_c * jnp.where(l_c > 0.0, 1.0 / l_c, 0.0))
    p_c = jnp.concatenate(blocks, axis=1)
    o_cmp = jnp.dot(vcT, p_c.astype(bf16), preferred_element_type=f32)
    yield

    j_blk = lax.broadcasted_iota(i32, (ns, 1), 0)
    cur = (s0 + lax.broadcasted_iota(i32, (1, TQ), 1)) // SEL_LEN
    n_live = (s0 + TQ - 1) // SEL_LEN + 1
    if n_live <= N_SEL:
        return o_cmp, jnp.where(j_blk <= cur, 0.0, NEG_INF).astype(bf16)

    ps = p_c[:, 0:TQ]
    for h in range(1, HPG):
        ps = ps + p_c[:, h * TQ:(h + 1) * TQ]
    p_hi = ps.astype(bf16)
    r1 = ps - p_hi.astype(f32)
    p_mid = r1.astype(bf16)
    p_lo = (r1 - p_mid.astype(f32)).astype(bf16)
    imp = (jnp.dot(ov, p_hi, preferred_element_type=f32) + jnp.dot(ov, p_mid, preferred_element_type=f32)
           + jnp.dot(ov, p_lo, preferred_element_type=f32))
    yield
    forced = (j_blk == 0) | (j_blk == cur) | (j_blk == cur - 1)
    imp = jnp.where(forced, FORCED, jnp.where(j_blk > cur, NEG_INF, imp))
    rank = jnp.zeros((ns, TQ), f32)
    for i in range(min(n_live, ns)):
        row = imp[i:i + 1, :]
        tie_before = jnp.where(j_blk > i, 1.0, 0.0)
        rank = rank + jnp.where(row > imp, 1.0, jnp.where(row == imp, tie_before, 0.0))
    sel_bias = jnp.where(rank < float(min(N_SEL, ns)), 0.0, NEG_INF).astype(bf16)
    return o_cmp, sel_bias


def _attend(tiles, q_op):
    m = jnp.full((1, _NQ), NEG_INF, bf16)
    acc = jnp.zeros((V_EXT, _NQ), f32)
    s_next = jnp.dot(tiles[0][0](), q_op, preferred_element_type=f32)
    for t, (_, values_t, bias) in enumerate(tiles):
        sT = s_next
        if t + 1 < len(tiles):
            s_next = jnp.dot(tiles[t + 1][0](), q_op, preferred_element_type=f32)
        yield
        p_blocks, m_blocks, a_blocks = [], [], []
        for c, cs in enumerate(_COL_BLOCKS):
            s = sT[:, cs]
            if bias is not None:
                b0 = (c % (TQ // LANES)) * LANES
                s = s + bias[:, b0:b0 + LANES]
            s = s.astype(bf16)
            m_o = m[:, cs]
            m_n = jnp.maximum(m_o, jnp.max(s, axis=0, keepdims=True))
            p_blocks.append(jnp.exp(s - m_n))
            a_blocks.append(jnp.exp(m_o.astype(f32) - m_n.astype(f32)))
            m_blocks.append(m_n)
        m = jnp.concatenate(m_blocks, axis=1)
        pv = jnp.dot(values_t(), jnp.concatenate(p_blocks, axis=1), preferred_element_type=f32)
        yield
        acc = acc * jnp.concatenate(a_blocks, axis=1) + pv
    return acc[0:HEAD_DIM, :] * (1.0 / acc[HEAD_DIM:HEAD_DIM + 1, :])


def _nsa_pair(pi, nqt, g, qa_ref, qb_ref, ks_ref, kw_ref, vsT_ref, vwT_ref, kcc_ref, vcT_ref, ga_ref, gb_ref,
              ovT_ref, cbias_ref, wbias_ref, o_ref):
    q_tiles = (pi, nqt - 1 - pi)
    ns = ovT_ref.shape[0]
    kcc = kcc_ref[0, 0]
    vcT = vcT_ref[0, 0]
    ov = ovT_ref[...]
    cbias = cbias_ref[...]
    n_back = WINDOW // TK

    def tile(k_ref, vT_ref, kt, bias):
        return (lambda: k_ref[kt * TK:(kt + 1) * TK, :], lambda: vT_ref[:, kt * TK:(kt + 1) * TK], bias)

    def query_tile(slot, q_ref, g_ref, qt):
        q4 = jnp.concatenate([q_ref[h * HEAD_DIM:(h + 1) * HEAD_DIM, :] for h in range(HPG)], axis=1)
        zeros = jnp.zeros_like(q4)
        q_win = jnp.where(g == 0, jnp.concatenate([q4, zeros], axis=0), jnp.concatenate([zeros, q4], axis=0))

        o_win = yield from _attend(
            [tile(kw_ref, vwT_ref, qt - j, cbias if j == 0 else (wbias_ref[...] if j == n_back else None))
             for j in range(min(n_back, qt) + 1)], q_win)

        o_cmp, sel_bias = yield from _compressed_branch(q4, kcc, vcT, ov, qt * TQ)

        q_sel = jnp.concatenate([q_win, jnp.concatenate([sel_bias] * HPG, axis=1),
                                 jnp.zeros((KV_WIDTH - ns, _NQ), bf16)], axis=0)
        o_slc = yield from _attend(
            [tile(ks_ref, vsT_ref, qt, cbias)] + [tile(ks_ref, vsT_ref, kt, None) for kt in range(qt)], q_sel)

        gates = g_ref[...]
        for h in range(HPG):
            sl = slice(h * TQ, (h + 1) * TQ)
            o = (gates[3 * h:3 * h + 1, :] * o_cmp[:, sl] + gates[3 * h + 1:3 * h + 2, :] * o_slc[:, sl]
                 + gates[3 * h + 2:3 * h + 3, :] * o_win[:, sl])
            o_ref[h * HEAD_DIM:(h + 1) * HEAD_DIM, slot * TQ:(slot + 1) * TQ] = o.astype(bf16)

    _interleave([query_tile(slot, q_ref, g_ref, qt)
                 for slot, (q_ref, g_ref, qt) in enumerate(zip((qa_ref, qb_ref), (ga_ref, gb_ref), q_tiles))])


def _nsa_kernel(*refs, nqt):
    for pi in range(nqt // 2):
        @pl.when(pl.program_id(0) == pi)
        def _():
            _nsa_pair(pi, nqt, pl.program_id(2), *refs)


def _nsa_tile_position(qt, nqt):
    return jnp.where(qt < nqt // 2, 2 * qt, 2 * (nqt - 1 - qt) + 1)


def _nsa(qT, ks, kw, vsT, vwT, kcc, vcT, gT, ovT, batch, seq):
    T = batch * seq
    nqt = seq // TQ
    ncp = kcc.shape[2]
    ns = seq // SEL_LEN
    nq = HPG * TQ
    gw = HPG * HEAD_DIM
    assert TQ == TK and WINDOW % TK == 0 and ns <= KV_WIDTH
    assert nqt % 2 == 0 and nqt // 2 >= WINDOW // TK
    kl = jnp.arange(TK)[:, None]
    ql = jnp.arange(TQ)[None, :]
    cbias = jnp.where(kl <= ql, 0.0, NEG_INF).astype(f32)
    wbias = jnp.where(kl > ql, 0.0, NEG_INF).astype(f32)
    amap = lambda p, b, g: (g, b * nqt + p)
    bmap = lambda p, b, g: (g, b * nqt + nqt - 1 - p)
    const = lambda p, b, g: (0, 0)
    in_specs = [
        pl.BlockSpec((gw, TQ), amap), pl.BlockSpec((gw, TQ), bmap),
        pl.BlockSpec((seq, 2 * KV_WIDTH), lambda p, b, g: (b, 0)),
        pl.BlockSpec((seq, KV_WIDTH), lambda p, b, g: (b, 0)),
        pl.BlockSpec((V_EXT, seq), lambda p, b, g: (g, b)),
        pl.BlockSpec((V_EXT, seq), lambda p, b, g: (g, b)),
        pl.BlockSpec((1, 1, ncp, HEAD_DIM), lambda p, b, g: (b, g, 0, 0)),
        pl.BlockSpec((1, 1, HEAD_DIM, ncp), lambda p, b, g: (b, g, 0, 0)),
        pl.BlockSpec((16, TQ), amap), pl.BlockSpec((16, TQ), bmap),
        pl.BlockSpec((ns, ncp), const),
        pl.BlockSpec((TK, TQ), const),
        pl.BlockSpec((TK, TQ), const),
    ]
    return pl.pallas_call(
        functools.partial(_nsa_kernel, nqt=nqt), grid=(nqt // 2, batch, NSA_KV_HEADS), in_specs=in_specs,
        out_specs=pl.BlockSpec((gw, 2 * TQ), lambda p, b, g: (g, b * (nqt // 2) + p)),
        out_shape=jax.ShapeDtypeStruct((NSA_WIDTH, T), bf16),
        compiler_params=_cparams(3), name="nsa",
    )(qT, qT, ks, kw, vsT, vwT, kcc, vcT, gT, gT, ovT, cbias, wbias)


def _memkv_kernel(mem_ref, g_ref, wkT_ref, wv_ref, kT_ref, v_ref):
    mb = _rms(mem_ref[0], g_ref[...]).astype(bf16)
    kT_ref[0] = lax.dot_general(wkT_ref[...], mb, (((1,), (1,)), ((), ())),
                                preferred_element_type=f32).astype(bf16)
    v_ref[0] = jnp.dot(mb, wv_ref[...], preferred_element_type=f32).astype(bf16)


def _memkv(mem, g_mem, wkT, wv):
    batch, n_mem, _ = mem.shape
    c2 = lambda b: (0, 0)
    return pl.pallas_call(
        _memkv_kernel, grid=(batch,),
        in_specs=[pl.BlockSpec((1, n_mem, D_MODEL), lambda b: (b, 0, 0)),
                  pl.BlockSpec((1, D_MODEL), c2),
                  pl.BlockSpec((D_MODEL, D_MODEL), c2), pl.BlockSpec((D_MODEL, D_MODEL), c2)],
        out_specs=(pl.BlockSpec((1, D_MODEL, n_mem), lambda b: (b, 0, 0)),
                   pl.BlockSpec((1, n_mem, D_MODEL), lambda b: (b, 0, 0))),
        out_shape=(jax.ShapeDtypeStruct((batch, D_MODEL, n_mem), bf16),
                   jax.ShapeDtypeStruct((batch, n_mem, D_MODEL), bf16)),
        compiler_params=_cparams(1), name="memkv",
    )(mem, g_mem, wkT, wv)


def _post_kernel(x_ref, oTa_ref, oTb_ref, conv_ref, gnsa_ref, wout_ref, gx_ref, wq_ref, kT_ref, v_ref, wo_ref,
                 gmoe_ref, wr_ref, br_ref,
                 h2_ref, xn2_ref, eidx_ref, gate_ref, sel_ref):
    stages = []
    for sub, oT_ref in enumerate((oTa_ref, oTb_ref)):
        rows = slice(sub * TQ, (sub + 1) * TQ)
        stages.append(_post_rows(
            x_ref.at[rows], oT_ref, conv_ref.at[rows], gnsa_ref, wout_ref, gx_ref, wq_ref, kT_ref, v_ref,
            wo_ref, gmoe_ref, wr_ref, br_ref, h2_ref.at[rows],
            xn2_ref.at[sub * TQ * ROW_SUB:(sub + 1) * TQ * ROW_SUB], eidx_ref.at[rows], gate_ref.at[rows],
            sel_ref.at[rows]))
    _interleave(stages)


_DONE = object()


def _interleave(generators):
    live = list(generators)
    while live:
        live = [g for g in live if next(g, _DONE) is not _DONE]


def _post_rows(x_ref, oT_ref, conv_ref, gnsa_ref, wout_ref, gx_ref, wq_ref, kT_ref, v_ref, wo_ref,
               gmoe_ref, wr_ref, br_ref,
               h2_ref, xn2_ref, eidx_ref, gate_ref, sel_ref):
    tm = x_ref.shape[0]
    oT = oT_ref[...].astype(f32)
    onT = (oT * lax.rsqrt(jnp.mean(oT * oT, axis=0, keepdims=True) + RMS_EPS) * gnsa_ref[...]).astype(bf16)
    mix = lax.dot_general(onT, wout_ref[0:NSA_WIDTH, :], (((0,), (0,)), ((), ())),
                          preferred_element_type=f32)
    mix = mix + jnp.dot(conv_ref[...], wout_ref[NSA_WIDTH:D_MODEL, :], preferred_element_type=f32)
    yield
    h1 = x_ref[...] + mix

    hn = _rms(h1, gx_ref[...]).astype(bf16)
    q = (jnp.dot(hn, wq_ref[...], preferred_element_type=f32) * (XATTN_HEAD_DIM ** -0.5)).astype(bf16)
    yield
    head_slices = [slice(h * XATTN_HEAD_DIM, (h + 1) * XATTN_HEAD_DIM) for h in range(XATTN_HEADS)]
    scores = [jnp.dot(q[:, sl], kT_ref[0, sl, :], preferred_element_type=f32) for sl in head_slices]
    yield
    heads = []
    for s, sl in zip(scores, head_slices):
        e = jnp.exp(s - jnp.max(s, axis=-1, keepdims=True))
        p = e * (1.0 / jnp.sum(e, axis=-1, keepdims=True))
        heads.append(jnp.dot(p.astype(bf16), v_ref[0, :, sl], preferred_element_type=f32))
    yield
    o = jnp.concatenate(heads, axis=1).astype(bf16)
    h2 = h1 + jnp.dot(o, wo_ref[...], preferred_element_type=f32)
    yield
    h2_ref[...] = h2

    xn2 = _rms(h2, gmoe_ref[...])
    for s_ in range(ROW_SUB):
        xn2_ref[pl.ds(s_, tm, stride=ROW_SUB), :] = xn2[:, s_ * LANES:(s_ + 1) * LANES]

    logits = jnp.dot(xn2.astype(bf16), wr_ref[...], preferred_element_type=f32) + br_ref[...]
    yield
    lane = lax.broadcasted_iota(i32, (tm, LANES), 1)
    work = logits
    sel = jnp.zeros((tm, LANES), f32)
    eidx = jnp.zeros((tm, LANES), i32)
    vals = []
    for k in range(TOP_K):
        mk = jnp.max(work, axis=-1, keepdims=True)
        ik = jnp.min(jnp.where(work == mk, lane, LANES), axis=-1, keepdims=True)
        hit = lane == ik
        work = jnp.where(hit, -jnp.inf, work)
        sel = jnp.where(hit, 1.0, sel)
        eidx = jnp.where(lane == k, ik, eidx)
        vals.append(mk)
    es = [jnp.exp(v - vals[0]) for v in vals]
    den = es[0]
    for e in es[1:]:
        den = den + e
    gate = jnp.zeros((tm, LANES), f32)
    for k in range(TOP_K):
        gate = jnp.where(lane == k, es[k] / den, gate)
    eidx_ref[...] = eidx
    gate_ref[...] = gate
    sel_ref[...] = sel.astype(bf16)


def _post(x2, oT, conv_n, g_nsa_col, w_out, g_x, w_q, kT, v, w_o, g_moe, w_r, b_r, seq):
    T = x2.shape[0]
    tm = TM_POST
    assert tm == 2 * TQ
    tps = seq // tm
    nqt = seq // TQ
    n_mem = v.shape[1]
    row = lambda i: (i, 0)
    const = lambda i: (0, 0)

    def o_tile(sub):
        return lambda i: (0, (i // tps) * nqt + _nsa_tile_position(2 * (i % tps) + sub, nqt))

    in_specs = [
        pl.BlockSpec((tm, D_MODEL), row),
        pl.BlockSpec((NSA_WIDTH, TQ), o_tile(0)), pl.BlockSpec((NSA_WIDTH, TQ), o_tile(1)),
        pl.BlockSpec((tm, CONV_WIDTH), row),
        pl.BlockSpec((NSA_WIDTH, 1), const),
        pl.BlockSpec((D_MODEL, D_MODEL), const),
        pl.BlockSpec((1, D_MODEL), const),
        pl.BlockSpec((D_MODEL, D_MODEL), const),
        pl.BlockSpec((1, D_MODEL, n_mem), lambda i: (i // tps, 0, 0)),
        pl.BlockSpec((1, n_mem, D_MODEL), lambda i: (i // tps, 0, 0)),
        pl.BlockSpec((D_MODEL, D_MODEL), const),
        pl.BlockSpec((1, D_MODEL), const),
        pl.BlockSpec((D_MODEL, LANES), const),
        pl.BlockSpec((1, LANES), const),
    ]
    out_shape = (jax.ShapeDtypeStruct((T, D_MODEL), f32),
                 jax.ShapeDtypeStruct((T * ROW_SUB, LANES), f32),
                 jax.ShapeDtypeStruct((T, LANES), i32),
                 jax.ShapeDtypeStruct((T, LANES), f32),
                 jax.ShapeDtypeStruct((T, LANES), bf16))
    out_specs = (pl.BlockSpec((tm, D_MODEL), row),
                 pl.BlockSpec((tm * ROW_SUB, LANES), row),
                 pl.BlockSpec((tm, LANES), row), pl.BlockSpec((tm, LANES), row),
                 pl.BlockSpec((tm, LANES), row))
    return pl.pallas_call(
        _post_kernel, grid=(T // tm,), in_specs=in_specs, out_specs=out_specs, out_shape=out_shape,
        compiler_params=_cparams(1), name="post",
    )(x2, oT, oT, conv_n, g_nsa_col, w_out, g_x, w_q, kT, v, w_o, g_moe, w_r, b_r)


def _count_kernel(sel_ref, cum_ref, cnt_ref, carry):
    i = pl.program_id(0)
    tm = sel_ref.shape[0]

    @pl.when(i == 0)
    def _():
        carry[...] = jnp.zeros_like(carry)

    sel = sel_ref[...]
    r = lax.broadcasted_iota(i32, (tm, tm), 0)
    c = lax.broadcasted_iota(i32, (tm, tm), 1)
    strict_lower = jnp.where(c < r, 1.0, 0.0).astype(bf16)
    base = carry[0:1, :]
    cum_ref[...] = jnp.dot(strict_lower, sel, preferred_element_type=f32) + base
    total = base + jnp.sum(sel.astype(f32), axis=0, keepdims=True)
    carry[...] = jnp.broadcast_to(total, carry.shape)
    cnt_ref[...] = jnp.broadcast_to(total, cnt_ref.shape)


def _count(sel):
    T = sel.shape[0]
    tm = TM_ROUTE
    return pl.pallas_call(
        _count_kernel, grid=(T // tm,),
        in_specs=[pl.BlockSpec((tm, LANES), lambda i: (i, 0))],
        out_specs=(pl.BlockSpec((tm, LANES), lambda i: (i, 0)),
                   pl.BlockSpec((SUBLANES, LANES), lambda i: (0, 0))),
        out_shape=(jax.ShapeDtypeStruct((T, LANES), f32), jax.ShapeDtypeStruct((SUBLANES, LANES), f32)),
        scratch_shapes=[pltpu.VMEM((SUBLANES, LANES), f32)],
        compiler_params=_cparams(1), name="route_count",
    )(sel)


def _dest_kernel(cum_ref, eidx_ref, pstart_ref, dest_ref):
    tm = cum_ref.shape[0]
    lane = lax.broadcasted_iota(i32, (tm, LANES), 1)
    row_of = cum_ref[...] + pstart_ref[0:1, :]
    eidx = eidx_ref[...]
    dest = jnp.zeros((tm, LANES), f32)
    for k in range(TOP_K):
        ek = eidx[:, k:k + 1]
        dk = jnp.sum(jnp.where(lane == ek, row_of, 0.0), axis=-1, keepdims=True)
        dest = jnp.where(lane == k, dk, dest)
    dest_ref[...] = dest.astype(i32)


def _dest(cum, eidx, pstart8):
    T = cum.shape[0]
    tm = TM_ROUTE
    row = lambda i: (i, 0)
    return pl.pallas_call(
        _dest_kernel, grid=(T // tm,),
        in_specs=[pl.BlockSpec((tm, LANES), row), pl.BlockSpec((tm, LANES), row),
                  pl.BlockSpec((SUBLANES, LANES), lambda i: (0, 0))],
        out_specs=pl.BlockSpec((tm, LANES), row),
        out_shape=jax.ShapeDtypeStruct((T, LANES), i32),
        compiler_params=_cparams(1), name="route_dest",
    )(cum, eidx, pstart8)


def _invert_kernel(dest_ref, init_ref, slot_ref, sem):
    cp = pltpu.make_async_copy(init_ref, slot_ref, sem)
    cp.start()
    cp.wait()
    chunk = LANES

    def body(j, carry):
        base = j * chunk
        for l in range(chunk):
            slot_ref[dest_ref[base + l]] = base + l
        return carry

    lax.fori_loop(0, dest_ref.shape[0] // chunk, body, 0)


def _invert(dest_flat, n_steps):
    parity = jnp.concatenate([jnp.arange(n_steps, dtype=i32) % 2, jnp.ones((1,), i32)])
    sink = dest_flat.shape[0] + parity[:, None] * BM + jnp.arange(BM, dtype=i32)[None, :]
    return pl.pallas_call(
        _invert_kernel,
        in_specs=[pl.BlockSpec(memory_space=pltpu.SMEM), pl.BlockSpec(memory_space=pltpu.VMEM)],
        out_specs=pl.BlockSpec(memory_space=pltpu.SMEM),
        out_shape=jax.ShapeDtypeStruct(((n_steps + 1) * BM,), i32),
        scratch_shapes=[pltpu.SemaphoreType.DMA],
        compiler_params=pltpu.CompilerParams(vmem_limit_bytes=VMEM_LIMIT), name="route_invert",
    )(dest_flat, sink.reshape(-1))


def _ffn_kernel(blk_e_ref, first_ref, nxt_e_ref, wslot_ref, nused_ref, slot_ref,
                x_hbm, wgu_hbm, bgu_ref, wd_hbm, bd_ref, y_hbm,
                xbuf0, xbuf1, obuf0, obuf1, wgu_f, wd_f, wgu_bf, wd_bf, gsem, ssem, wsem, *, n_tok, sink_row):
    i = pl.program_id(0)
    nused = nused_ref[0]
    n_steps = pl.num_programs(0)
    xbuf = (xbuf0, xbuf1)
    obuf = (obuf0, obuf1)

    def row_window(ref, row):
        return ref.at[pl.ds(pl.multiple_of(row * ROW_SUB, SUBLANES), ROW_SUB)]

    def gather(blk, par, r):
        tok = slot_ref[blk * BM + r] & (n_tok - 1)
        return pltpu.make_async_copy(row_window(x_hbm, tok), xbuf[par].at[pl.ds(r * ROW_SUB, ROW_SUB)],
                                     gsem.at[par])

    def scatter(blk, par, r):
        return pltpu.make_async_copy(obuf[par].at[pl.ds(r * ROW_SUB, ROW_SUB)],
                                     row_window(y_hbm, slot_ref[blk * BM + r]), ssem.at[par])

    def wait_gather(par):
        pltpu.make_async_copy(x_hbm.at[pl.ds(0, BM * ROW_SUB)], xbuf[par], gsem.at[par]).wait()

    def wait_scatter(par):
        pltpu.make_async_copy(obuf[par], y_hbm.at[pl.ds(0, BM * ROW_SUB)], ssem.at[par]).wait()

    def weight_copies(e, ws):
        return (pltpu.make_async_copy(wgu_hbm.at[e], wgu_f.at[ws], wsem.at[ws, 0]),
                pltpu.make_async_copy(wd_hbm.at[e], wd_f.at[ws], wsem.at[ws, 1]))

    @pl.when(i == 0)
    def _():
        obuf0[...] = jnp.zeros(obuf0.shape, f32)
        obuf1[...] = jnp.zeros(obuf1.shape, f32)
        pltpu.make_async_copy(obuf0, y_hbm.at[pl.ds(sink_row * ROW_SUB, BM * ROW_SUB)], ssem.at[0]).start()
        for cp in weight_copies(blk_e_ref[0], 0):
            cp.start(priority=1)
        for r in range(BM):
            gather(0, 0, r).start()

    @pl.when((i < nused) & (first_ref[i] == 1))
    def _():
        ws = wslot_ref[i]
        for cp in weight_copies(blk_e_ref[i], ws):
            cp.wait()

        @pl.when(nxt_e_ref[i] >= 0)
        def _():
            for cp in weight_copies(nxt_e_ref[i], 1 - ws):
                cp.start(priority=1)

        wgu_bf[...] = wgu_f[ws].astype(bf16)
        wd_bf[...] = wd_f[ws].astype(bf16)

    def block(par, gather_first):
        prev = jnp.where(i == 0, n_steps, i - 1)
        wait_gather(par)

        def issue_gather():
            for r in range(BM):
                gather(i + 1, 1 - par, r).start()

        if gather_first:
            pl.when(i < nused)(issue_gather)
        x = jnp.concatenate([xbuf[par][pl.ds(s_, BM, stride=ROW_SUB), :] for s_ in range(ROW_SUB)],
                            axis=1).astype(bf16)
        for r in range(BM):
            scatter(prev, 1 - par, r).start(priority=1)
        if not gather_first:
            issue_gather()
        gu = jnp.dot(x, wgu_bf[...], preferred_element_type=f32) + bgu_ref[0]
        gg = jnp.minimum(gu[:, 0:D_FF], SWIGLU_LIMIT)
        uu = jnp.clip(gu[:, D_FF:2 * D_FF], -SWIGLU_LIMIT, SWIGLU_LIMIT)
        hmid = (uu + 1.0) * (gg * jax.nn.sigmoid(SWIGLU_ALPHA * gg))
        out = jnp.dot(hmid.astype(bf16), wd_bf[...], preferred_element_type=f32) + bd_ref[0]
        wait_scatter(par)
        for s_ in range(ROW_SUB):
            obuf[par][pl.ds(s_, BM, stride=ROW_SUB), :] = out[:, s_ * LANES:(s_ + 1) * LANES]

    for par in range(2):
        @pl.when((i < nused) & (i % 2 == par))
        def _():
            block(par, gather_first=(par == 1))

        @pl.when((i == nused) & (i % 2 == par))
        def _():
            for r in range(BM):
                scatter(i - 1, 1 - par, r).start(priority=1)
            wait_gather(par)
            wait_scatter(par)
            wait_scatter(1 - par)


def _ffn(blk_e, first, nxt_e, wslot, nused, slots, xn2_rows, w_gu, b_gu, w_d, b_d):
    n_steps = blk_e.shape[0]
    n_tok = xn2_rows.shape[0] // ROW_SUB
    assert n_tok & (n_tok - 1) == 0
    n_tok_rows = n_tok * TOP_K
    emap = lambda i, be, *_: (be[i], 0, 0)
    grid_spec = pltpu.PrefetchScalarGridSpec(
        num_scalar_prefetch=6, grid=(n_steps,),
        in_specs=[pl.BlockSpec(memory_space=pl.ANY),
                  pl.BlockSpec(memory_space=pl.ANY),
                  pl.BlockSpec((1, 1, 2 * D_FF), emap),
                  pl.BlockSpec(memory_space=pl.ANY),
                  pl.BlockSpec((1, 1, D_MODEL), emap)],
        out_specs=pl.BlockSpec(memory_space=pl.ANY),
        scratch_shapes=[pltpu.VMEM((BM * ROW_SUB, LANES), f32), pltpu.VMEM((BM * ROW_SUB, LANES), f32),
                        pltpu.VMEM((BM * ROW_SUB, LANES), f32), pltpu.VMEM((BM * ROW_SUB, LANES), f32),
                        pltpu.VMEM((2, D_MODEL, 2 * D_FF), f32), pltpu.VMEM((2, D_FF, D_MODEL), f32),
                        pltpu.VMEM((D_MODEL, 2 * D_FF), bf16), pltpu.VMEM((D_FF, D_MODEL), bf16),
                        pltpu.SemaphoreType.DMA((2,)), pltpu.SemaphoreType.DMA((2,)),
                        pltpu.SemaphoreType.DMA((2, 2))],
    )
    return pl.pallas_call(
        functools.partial(_ffn_kernel, n_tok=n_tok, sink_row=n_tok_rows), grid_spec=grid_spec,
        out_shape=jax.ShapeDtypeStruct(((n_tok_rows + 2 * BM) * ROW_SUB, LANES), f32),
        compiler_params=_cparams(1), name="ffn",
    )(blk_e, first, nxt_e, wslot, nused, slots, xn2_rows, w_gu, b_gu, w_d, b_d)


def _combine_kernel(*refs, final_norm):
    y_refs = refs[:TOP_K]
    gate_ref, h2_ref, gfin_ref, o_ref = refs[TOP_K:]
    tm = h2_ref.shape[0]
    gate = gate_ref[...]
    cols = []
    for s_ in range(ROW_SUB):
        acc = gate[:, 0:1] * y_refs[0][pl.ds(s_, tm, stride=ROW_SUB), :]
        for k in range(1, TOP_K):
            acc = acc + gate[:, k:k + 1] * y_refs[k][pl.ds(s_, tm, stride=ROW_SUB), :]
        cols.append(acc)
    h = h2_ref[...] + jnp.concatenate(cols, axis=1)
    if final_norm:
        h = _rms(h, gfin_ref[...])
    o_ref[...] = h


def _combine(y_rows, gate, h2, g_final, final_norm):
    T = h2.shape[0]
    tm = TM_ROW
    row = lambda i: (i, 0)
    planes = [pl.BlockSpec((tm * ROW_SUB, LANES), functools.partial(lambda i, k: (k * (T // tm) + i, 0), k=k))
              for k in range(TOP_K)]
    return pl.pallas_call(
        functools.partial(_combine_kernel, final_norm=final_norm), grid=(T // tm,),
        in_specs=planes + [pl.BlockSpec((tm, LANES), row), pl.BlockSpec((tm, D_MODEL), row),
                           pl.BlockSpec((1, D_MODEL), lambda i: (0, 0))],
        out_specs=pl.BlockSpec((tm, D_MODEL), row),
        out_shape=jax.ShapeDtypeStruct((T, D_MODEL), f32),
        compiler_params=_cparams(1), name="combine",
    )(*([y_rows] * TOP_K), gate, h2, g_final)


def _prep_inproj_weights(w_in):
    sizes = (NSA_WIDTH,) + (KV_WIDTH,) * 6 + (3 * NSA_HEADS,) + (CONV_WIDTH,) * 3
    offs = [0]
    for s in sizes:
        offs.append(offs[-1] + s)
    seg = lambda n: w_in[:, offs[n]:offs[n + 1]]
    q, kc, vc, ks, vs, kw, vw, gl, ch, cb, cc = (seg(n) for n in range(11))
    w_row = jnp.concatenate([kc, vc, ks, kw, ch, cb, cc], axis=1).astype(bf16)
    gl_g = gl.reshape(D_MODEL, NSA_KV_HEADS, HPG * 3)
    gl_g = jnp.pad(gl_g, ((0, 0), (0, 0), (0, 16 - HPG * 3))).reshape(D_MODEL, NSA_KV_HEADS * 16)
    w_t = jnp.concatenate([q, vs, vw, gl_g], axis=1).T.astype(bf16)
    return w_row, w_t


def _rope_tables(positions):
    half = HEAD_DIM // 2
    inv_freq = ROPE_THETA ** (-jnp.arange(half, dtype=f32) / half)
    ang = positions.reshape(-1).astype(f32)[:, None] * inv_freq
    cos = jnp.cos(ang)
    sin = jnp.sin(ang)
    reps = KV_WIDTH // HEAD_DIM
    cosr = jnp.tile(cos, (1, 2 * reps))
    sinr = jnp.tile(jnp.concatenate([-sin, sin], axis=1), (1, reps))
    return cosr, sinr, cos.T, sin.T


def _cmp_weights(pos, w1, b1):
    w1r = w1.astype(bf16).reshape(2, CMP_STRIDE, HEAD_DIM, CMP_HIDDEN)
    zero = jnp.zeros_like(w1r)
    big = jnp.stack([jnp.concatenate([w1r if g == k else zero for k in range(NSA_KV_HEADS)], axis=-1)
                     for g in range(NSA_KV_HEADS)], axis=2)
    big = big.reshape(2, CMP_STRIDE * KV_WIDTH, NSA_KV_HEADS * CMP_HIDDEN)
    p = jnp.broadcast_to(pos.reshape(2, CMP_STRIDE, 1, HEAD_DIM), (2, CMP_STRIDE, NSA_KV_HEADS, HEAD_DIM))
    p = jnp.pad(p.reshape(2, CMP_STRIDE * KV_WIDTH), ((0, SUBLANES - 2), (0, 0)))
    return big, p, jnp.tile(b1.reshape(1, CMP_HIDDEN), (1, NSA_KV_HEADS))


def _mixer_core(x2, tables, p, batch, seq):
    cosr, sinr, cost, sint = tables
    row1 = lambda v: v.reshape(1, -1)
    w_row, w_t = _prep_inproj_weights(p['w_mix_in'])
    conv_w8 = jnp.pad(p['conv_w'], ((0, SUBLANES - CONV_K), (0, 0)))
    qT, kc, vc, ks, kw, vsT, vwT, gT, conv_n = _inproj(
        x2, row1(p['g_mix_norm']), w_row, w_t, cosr, sinr, cost, sint, conv_w8, row1(p['g_conv_out']), seq)

    w1k, pk, b1k = _cmp_weights(p['cmp_pos_k'], p['cmp_w1_k'], p['cmp_b1_k'])
    w1v, pv, b1v = _cmp_weights(p['cmp_pos_v'], p['cmp_w1_v'], p['cmp_b1_v'])
    kcc, vcT = _compress(kc, vc, pk, pv, w1k, w1v, b1k, b1v,
                         p['cmp_w2_k'].astype(bf16), p['cmp_w2_v'].T.astype(bf16), batch, seq)

    ncp = seq // CMP_STRIDE
    ns = seq // SEL_LEN
    cs = jnp.arange(ncp) * CMP_STRIDE
    js = jnp.arange(ns) * SEL_LEN
    overlap = jnp.clip(jnp.minimum(cs[:, None] + CMP_LEN, js[None, :] + SEL_LEN)
                       - jnp.maximum(cs[:, None], js[None, :]), 0, None).astype(f32) / CMP_LEN
    ovT = overlap.T.astype(bf16)
    oT = _nsa(qT, ks, kw, vsT, vwT, kcc, vcT, gT, ovT, batch, seq)
    return oT, conv_n


def _layer(h, memf, tables, p, final_gain, final_norm):
    batch, seq, _ = h.shape
    T = batch * seq
    x2 = h.reshape(T, D_MODEL)
    row1 = lambda v: v.reshape(1, -1)
    oT, conv_n = _mixer_core(x2, tables, p, batch, seq)

    w_xkv = p['w_xkv']
    kT, v = _memkv(memf, row1(p['g_mem_norm']), w_xkv[:, :D_MODEL].T.astype(bf16),
                   w_xkv[:, D_MODEL:].astype(bf16))
    w_r = jnp.pad(p['w_router'], ((0, 0), (0, LANES - N_EXPERTS))).astype(bf16)
    b_r = jnp.pad(p['b_router'], (0, LANES - N_EXPERTS), constant_values=NEG_INF).reshape(1, LANES)
    h2, xn2_rows, eidx, gate, sel = _post(
        x2, oT, conv_n, p['g_nsa_out'].reshape(NSA_WIDTH, 1), p['w_mix_out'].astype(bf16),
        row1(p['g_xattn_norm']), p['w_xq'].astype(bf16), kT, v, p['w_xo'].astype(bf16),
        row1(p['g_moe_norm']), w_r, b_r, seq)

    cum, cnt = _count(sel)
    counts = cnt[0, :N_EXPERTS].astype(i32)
    padded = (counts + BM - 1) // BM * BM
    pend = jnp.cumsum(padded)
    pstart = pend - padded
    n_steps = (T * TOP_K) // BM + N_EXPERTS + 1
    nused = (pend[-1] // BM).astype(i32)
    step = jnp.arange(n_steps, dtype=i32)
    used = step < nused
    blk_raw = jnp.minimum(jnp.sum((pend[None, :] <= (step * BM)[:, None]).astype(i32), axis=1), N_EXPERTS - 1)
    e_ids = jnp.arange(N_EXPERTS, dtype=i32)
    last_e = jnp.max(jnp.where(padded > 0, e_ids, 0))
    blk_e = jnp.where(used, blk_raw, last_e)
    first = (used & jnp.concatenate([jnp.ones((1,), bool), blk_e[1:] != blk_e[:-1]])).astype(i32)
    wslot = (jnp.cumsum(first) - 1) % 2
    later = (e_ids[None, :] > e_ids[:, None]) & (padded > 0)[None, :]
    nxt_of = jnp.min(jnp.where(later, e_ids[None, :], N_EXPERTS), axis=1)
    nxt_of = jnp.where(nxt_of < N_EXPERTS, nxt_of, -1).astype(i32)
    nxt_e = jnp.sum(jnp.where(blk_e[:, None] == e_ids[None, :], nxt_of[None, :], 0), axis=1)
    pstart8 = jnp.broadcast_to(jnp.pad(pstart.astype(f32), (0, LANES - N_EXPERTS))[None, :], (SUBLANES, LANES))

    dest = _dest(cum, eidx, pstart8)
    slots = _invert(dest[:, :TOP_K].T.reshape(-1), n_steps)
    y_rows = _ffn(blk_e, first, nxt_e, wslot.astype(i32), nused.reshape(1), slots, xn2_rows,
                  p['w_gate_up'], p['b_gate_up'].reshape(N_EXPERTS, 1, 2 * D_FF), p['w_down'],
                  p['b_down'].reshape(N_EXPERTS, 1, D_MODEL))
    out = _combine(y_rows, gate, h2, row1(final_gain), final_norm)
    return out.reshape(batch, seq, D_MODEL)


_LAYER_PARAMS = ('g_mix_norm', 'w_mix_in', 'cmp_pos_k', 'cmp_pos_v', 'cmp_w1_k', 'cmp_b1_k', 'cmp_w2_k',
                 'cmp_w1_v', 'cmp_b1_v', 'cmp_w2_v', 'conv_w', 'g_nsa_out', 'g_conv_out', 'w_mix_out',
                 'g_xattn_norm', 'g_mem_norm', 'w_xq', 'w_xkv', 'w_xo', 'g_moe_norm', 'w_router', 'b_router',
                 'w_gate_up', 'b_gate_up', 'w_down', 'b_down')


def kernel(x, mem, positions, g_mix_norm, w_mix_in, cmp_pos_k, cmp_pos_v, cmp_w1_k, cmp_b1_k, cmp_w2_k, cmp_w1_v, cmp_b1_v, cmp_w2_v, conv_w, g_nsa_out, g_conv_out, w_mix_out, g_xattn_norm, g_mem_norm, w_xq, w_xkv, w_xo, g_moe_norm, w_router, b_router, w_gate_up, b_gate_up, w_down, b_down, g_final):
    stacked = dict(zip(_LAYER_PARAMS, (g_mix_norm, w_mix_in, cmp_pos_k, cmp_pos_v, cmp_w1_k, cmp_b1_k, cmp_w2_k,
                                       cmp_w1_v, cmp_b1_v, cmp_w2_v, conv_w, g_nsa_out, g_conv_out, w_mix_out,
                                       g_xattn_norm, g_mem_norm, w_xq, w_xkv, w_xo, g_moe_norm, w_router,
                                       b_router, w_gate_up, b_gate_up, w_down, b_down)))
    depth = g_mix_norm.shape[0]
    tables = _rope_tables(positions)
    h = x
    for l in range(depth):
        p = {k: v[l] for k, v in stacked.items()}
        last = l == depth - 1
        h = _layer(h, mem, tables, p, g_final, final_norm=last)
    return h
```

```python
import functools

import jax
import jax.numpy as jnp
from jax import lax
from jax.experimental import pallas as pl
from jax.experimental.pallas import tpu as pltpu

f32 = jnp.float32
bf16 = jnp.bfloat16
i32 = jnp.int32

D_MODEL = 1024
HEAD_DIM = 64
NSA_HEADS = 8
NSA_KV_HEADS = 2
HPG = NSA_HEADS // NSA_KV_HEADS
NSA_WIDTH = NSA_HEADS * HEAD_DIM
KV_WIDTH = NSA_KV_HEADS * HEAD_DIM
CONV_WIDTH = D_MODEL - NSA_WIDTH
CONV_K = 3
CMP_LEN = 32
CMP_STRIDE = 16
CMP_HIDDEN = 256
SEL_LEN = 64
N_SEL = 16
WINDOW = 512
ROPE_THETA = 10000.0
XATTN_HEADS = 4
XATTN_HEAD_DIM = D_MODEL // XATTN_HEADS
N_EXPERTS = 32
TOP_K = 4
D_FF = D_MODEL
SWIGLU_LIMIT = 7.0
SWIGLU_ALPHA = 1.702
RMS_EPS = 1e-5
NEG_INF = -1e30
FORCED = 1e30

LANES = 128
SUBLANES = 8
VMEM_LIMIT = 56 * 1024 * 1024

TM_IN = 512
TQ = 256
TK = 256
TM_POST = 512
TM_ROUTE = 512
TM_ROW = 256
V_EXT = HEAD_DIM + 16
BM = 256
ROW_SUB =D_MODEL // LANES


def _cparams(n_axes, **kw):
    return pltpu.CompilerParams(dimension_semantics=("arbitrary",) * n_axes,
                                vmem_limit_bytes=VMEM_LIMIT, **kw)


def _rms(t, gain):
    return t * lax.rsqrt(jnp.mean(t * t, axis=-1, keepdims=True) + RMS_EPS) * gain


def _inproj_kernel(x_ref, g_ref, wr_ref, wt_ref, cosr_ref, sinr_ref, cost_ref, sint_ref,
                   convw_ref, gconv_ref,
                   qT_ref, kc_ref, vc_ref, ks_ref, kw_ref, vsT_ref, vwT_ref, gT_ref, conv_ref,
                   ubuf, *, tiles_per_seq):
    i = pl.program_id(0)
    tm = x_ref.shape[0]

    @pl.when(i % tiles_per_seq == 0)
    def _():
        ubuf[0:SUBLANES, :] = jnp.zeros((SUBLANES, CONV_WIDTH), f32)

    @pl.when(i % tiles_per_seq != 0)
    def _():
        ubuf[0:SUBLANES, :] = ubuf[tm:tm + SUBLANES, :]

    stages = []
    for sub in range(tm // TQ):
        rows = slice(sub * TQ, (sub + 1) * TQ)
        stages.append(_inproj_rows(
            (i % tiles_per_seq) * tm + sub * TQ, SUBLANES + sub * TQ,
            x_ref.at[rows], g_ref, wr_ref, wt_ref, cosr_ref.at[rows], sinr_ref.at[rows],
            cost_ref.at[:, rows], sint_ref.at[:, rows], convw_ref, gconv_ref,
            qT_ref.at[:, rows], kc_ref.at[rows], vc_ref.at[rows], ks_ref.at[rows], kw_ref.at[rows],
            vsT_ref.at[:, rows], vwT_ref.at[:, rows], gT_ref.at[:, rows], conv_ref.at[rows], ubuf))
    _interleave(stages)


def _inproj_rows(tok0, u0, x_ref, g_ref, wr_ref, wt_ref, cosr_ref, sinr_ref, cost_ref, sint_ref,
                 convw_ref, gconv_ref,
                 qT_ref, kc_ref, vc_ref, ks_ref, kw_ref, vsT_ref, vwT_ref, gT_ref, conv_ref, ubuf):
    tm = x_ref.shape[0]
    xb = _rms(x_ref[...], g_ref[...]).astype(bf16)
    yield

    c0 = 4 * KV_WIDTH
    pr = jnp.dot(xb, wr_ref[:, 0:c0], preferred_element_type=f32)
    pc = jnp.dot(xb, wr_ref[:, c0:c0 + 3 * CONV_WIDTH], preferred_element_type=f32)
    pt = lax.dot_general(wt_ref[...], xb, (((1,), (1,)), ((), ())), preferred_element_type=f32)
    yield
    cosr = cosr_ref[...]
    sinr = sinr_ref[...]
    lane = lax.broadcasted_iota(i32, (tm, KV_WIDTH), 1)
    first_half = (lane & (HEAD_DIM - 1)) < HEAD_DIM // 2

    def rope_rows(t):
        rot = jnp.where(first_half, pltpu.roll(t, KV_WIDTH - HEAD_DIM // 2, 1),
                        pltpu.roll(t, HEAD_DIM // 2, 1))
        return t * cosr + rot * sinr

    kc_ref[...] = rope_rows(pr[:, 0:KV_WIDTH])
    vc_ref[...] = pr[:, KV_WIDTH:2 * KV_WIDTH]
    ks_ref[:, 0:KV_WIDTH] = rope_rows(pr[:, 2 * KV_WIDTH:3 * KV_WIDTH]).astype(bf16)
    tok = tok0 + lax.broadcasted_iota(i32, (tm, KV_WIDTH), 0)
    ks_ref[:, KV_WIDTH:2 * KV_WIDTH] = jnp.where(lane == tok // SEL_LEN, 1.0, 0.0).astype(bf16)
    kw_ref[...] = rope_rows(pr[:, 3 * KV_WIDTH:4 * KV_WIDTH]).astype(bf16)

    ch = pc[:, 0:CONV_WIDTH]
    cb = pc[:, CONV_WIDTH:2 * CONV_WIDTH]
    cc = pc[:, 2 * CONV_WIDTH:3 * CONV_WIDTH]
    u = cc * ch
    ubuf[u0:u0 + tm, :] = u
    u1 = ubuf[u0 - 1:u0 - 1 + tm, :]
    u2 = ubuf[u0 - 2:u0 - 2 + tm, :]
    w = convw_ref[...]
    y = cb * (w[0:1, :] * u2 + w[1:2, :] * u1 + w[2:3, :] * u)
    conv_ref[...] = _rms(y, gconv_ref[...]).astype(bf16)
    yield

    cost = cost_ref[...]
    sint = sint_ref[...]
    half = HEAD_DIM // 2
    scale = HEAD_DIM ** -0.5
    for h in range(NSA_HEADS):
        t1 = pt[h * HEAD_DIM:h * HEAD_DIM + half, :]
        t2 = pt[h * HEAD_DIM + half:(h + 1) * HEAD_DIM, :]
        qT_ref[h * HEAD_DIM:h * HEAD_DIM + half, :] = ((t1 * cost - t2 * sint) * scale).astype(bf16)
        qT_ref[h * HEAD_DIM + half:(h + 1) * HEAD_DIM, :] = ((t2 * cost + t1 * sint) * scale).astype(bf16)
    r0 = NSA_WIDTH
    ones_rows = jnp.where(lax.broadcasted_iota(i32, (V_EXT - HEAD_DIM, tm), 0) == 0, 1.0, 0.0).astype(bf16)
    for vT_ref, base in ((vsT_ref, r0), (vwT_ref, r0 + KV_WIDTH)):
        for g in range(NSA_KV_HEADS):
            vT_ref[g * V_EXT:g * V_EXT + HEAD_DIM, :] = pt[base + g * HEAD_DIM:base + (g + 1) * HEAD_DIM, :].astype(bf16)
            vT_ref[g * V_EXT + HEAD_DIM:(g + 1) * V_EXT, :] = ones_rows
    gT_ref[...] = jax.nn.sigmoid(pt[r0 + 2 * KV_WIDTH:r0 + 2 * KV_WIDTH + 32, :])


def _inproj(x2, g_mix, w_row, w_t, cosr, sinr, cost, sint, conv_w8, g_conv, seq):
    T = x2.shape[0]
    tm = TM_IN
    n_row = w_row.shape[1]
    n_t = w_t.shape[0]
    row = lambda i: (i, 0)
    col = lambda i: (0, i)
    const = lambda i: (0, 0)
    out_shape = (
        jax.ShapeDtypeStruct((NSA_WIDTH, T), bf16),
        jax.ShapeDtypeStruct((T, KV_WIDTH), f32),
        jax.ShapeDtypeStruct((T, KV_WIDTH), f32),
        jax.ShapeDtypeStruct((T, 2 * KV_WIDTH), bf16),
        jax.ShapeDtypeStruct((T, KV_WIDTH), bf16),
        jax.ShapeDtypeStruct((NSA_KV_HEADS * V_EXT, T), bf16),
        jax.ShapeDtypeStruct((NSA_KV_HEADS * V_EXT, T), bf16),
        jax.ShapeDtypeStruct((32, T), f32),
        jax.ShapeDtypeStruct((T, CONV_WIDTH), bf16),
    )
    out_specs = (
        pl.BlockSpec((NSA_WIDTH, tm), col),
        pl.BlockSpec((tm, KV_WIDTH), row), pl.BlockSpec((tm, KV_WIDTH), row),
        pl.BlockSpec((tm, 2 * KV_WIDTH), row), pl.BlockSpec((tm, KV_WIDTH), row),
        pl.BlockSpec((NSA_KV_HEADS * V_EXT, tm), col), pl.BlockSpec((NSA_KV_HEADS * V_EXT, tm), col),
        pl.BlockSpec((32, tm), col),
        pl.BlockSpec((tm, CONV_WIDTH), row),
    )
    in_specs = [
        pl.BlockSpec((tm, D_MODEL), row),
        pl.BlockSpec((1, D_MODEL), const),
        pl.BlockSpec((D_MODEL, n_row), const),
        pl.BlockSpec((n_t, D_MODEL), const),
        pl.BlockSpec((tm, KV_WIDTH), row), pl.BlockSpec((tm, KV_WIDTH), row),
        pl.BlockSpec((HEAD_DIM // 2, tm), col), pl.BlockSpec((HEAD_DIM // 2, tm), col),
        pl.BlockSpec((SUBLANES, CONV_WIDTH), const),
        pl.BlockSpec((1, CONV_WIDTH), const),
    ]
    return pl.pallas_call(
        functools.partial(_inproj_kernel, tiles_per_seq=seq // tm),
        grid=(T // tm,), in_specs=in_specs, out_specs=out_specs, out_shape=out_shape,
        scratch_shapes=[pltpu.VMEM((tm + 2 * SUBLANES, CONV_WIDTH), f32)],
        compiler_params=_cparams(1), name="inproj",
    )(x2, g_mix, w_row, w_t, cosr, sinr, cost, sint, conv_w8, g_conv)


def _compress_kernel(xk_ref, xv_ref, pk_ref, pv_ref, w1k_ref, w1v_ref, b1k_ref, b1v_ref,
                     w2k_ref, w2vT_ref, kcc_ref, vcT_ref):
    ncp = xk_ref.shape[0] // CMP_STRIDE

    def hidden(x_ref, p_ref, w1_ref, b1_ref):
        x = jnp.concatenate([x_ref[pl.ds(l, ncp, stride=CMP_STRIDE), :] for l in range(CMP_STRIDE)], axis=1)
        lo = (x + p_ref[0:1, :]).astype(bf16)
        hi = (x + p_ref[1:2, :]).astype(bf16)
        a = jnp.dot(lo, w1_ref[0], preferred_element_type=f32)
        b = jnp.dot(hi, w1_ref[1], preferred_element_type=f32)
        pre = a + pltpu.roll(b, ncp - 1, 0) + b1_ref[...]
        return jax.nn.gelu(pre).astype(bf16)

    hk = hidden(xk_ref, pk_ref, w1k_ref, b1k_ref)
    hv = hidden(xv_ref, pv_ref, w1v_ref, b1v_ref)
    for g in range(NSA_KV_HEADS):
        sl = slice(g * CMP_HIDDEN, (g + 1) * CMP_HIDDEN)
        kcc_ref[0, g] = jnp.dot(hk[:, sl], w2k_ref[...], preferred_element_type=f32).astype(bf16)
        vcT_ref[0, g] = lax.dot_general(w2vT_ref[...], hv[:, sl], (((1,), (1,)), ((), ())),
                                        preferred_element_type=f32).astype(bf16)


def _compress(kc_rows, vc_rows, pk, pv, w1k, w1v, b1k, b1v, w2k, w2vT, batch, seq):
    ncp = seq // CMP_STRIDE
    wide = CMP_STRIDE * KV_WIDTH
    c2 = lambda b: (0, 0)
    c3 = lambda b: (0, 0, 0)
    in_specs = [
        pl.BlockSpec((seq, KV_WIDTH), lambda b: (b, 0)),
        pl.BlockSpec((seq, KV_WIDTH), lambda b: (b, 0)),
        pl.BlockSpec((SUBLANES, wide), c2), pl.BlockSpec((SUBLANES, wide), c2),
        pl.BlockSpec((2, wide, 2 * CMP_HIDDEN), c3), pl.BlockSpec((2, wide, 2 * CMP_HIDDEN), c3),
        pl.BlockSpec((1, 2 * CMP_HIDDEN), c2), pl.BlockSpec((1, 2 * CMP_HIDDEN), c2),
        pl.BlockSpec((CMP_HIDDEN, HEAD_DIM), c2), pl.BlockSpec((HEAD_DIM, CMP_HIDDEN), c2),
    ]
    out_shape = (jax.ShapeDtypeStruct((batch, NSA_KV_HEADS, ncp, HEAD_DIM), bf16),
                 jax.ShapeDtypeStruct((batch, NSA_KV_HEADS, HEAD_DIM, ncp), bf16))
    out_specs = (pl.BlockSpec((1, NSA_KV_HEADS, ncp, HEAD_DIM), lambda b: (b, 0, 0, 0)),
                 pl.BlockSpec((1, NSA_KV_HEADS, HEAD_DIM, ncp), lambda b: (b, 0, 0, 0)))
    return pl.pallas_call(
        _compress_kernel, grid=(batch,), in_specs=in_specs, out_specs=out_specs,
        out_shape=out_shape, compiler_params=_cparams(1), name="compress",
    )(kc_rows, vc_rows, pk, pv, w1k, w1v, b1k, b1v, w2k, w2vT)


_NQ = HPG * TQ
_COL_BLOCKS = [slice(c * LANES, (c + 1) * LANES) for c in range(_NQ // LANES)]


def _compressed_branch(q4, kcc, vcT, ov, s0):
    ncp = kcc.shape[0]
    ns = ov.shape[0]
    s_lane = s0 + (lax.broadcasted_iota(i32, (1, _NQ), 1) & (TQ - 1))
    sc = jnp.dot(kcc, q4, preferred_element_type=f32)
    yield
    c_end = lax.broadcasted_iota(i32, (ncp, 1), 0) * CMP_STRIDE + (CMP_LEN - 1)
    blocks = []
    for cs in _COL_BLOCKS:
        cmask = c_end <= s_lane[:, cs]
        scm = jnp.where(cmask, sc[:, cs], NEG_INF)
        e_c = jnp.where(cmask, jnp.exp(scm - jnp.max(scm, axis=0, keepdims=True)), 0.0)
        l_c = jnp.sum(e_c, axis=0, keepdims=True)
        blocks.append(e_c * jnp.where(l_c > 0.0, 1.0 / l_c, 0.0))
    p_c = jnp.concatenate(blocks, axis=1)
    o_cmp = jnp.dot(vcT, p_c.astype(bf16), preferred_element_type=f32)
    yield

    j_blk = lax.broadcasted_iota(i32, (ns, 1), 0)
    cur = (s0 + lax.broadcasted_iota(i32, (1, TQ), 1)) // SEL_LEN
    n_live = (s0 + TQ - 1) // SEL_LEN + 1
    if n_live <= N_SEL:
        return o_cmp, jnp.where(j_blk <= cur, 0.0, NEG_INF).astype(bf16)

    ps = p_c[:, 0:TQ]
    for h in range(1, HPG):
        ps = ps + p_c[:, h * TQ:(h + 1) * TQ]
    p_hi = ps.astype(bf16)
    r1 = ps - p_hi.astype(f32)
    p_mid = r1.astype(bf16)
    p_lo = (r1 - p_mid.astype(f32)).astype(bf16)
    imp = (jnp.dot(ov, p_hi, preferred_element_type=f32) + jnp.dot(ov, p_mid, preferred_element_type=f32)
           + jnp.dot(ov, p_lo, preferred_element_type=f32))
    yield
    forced = (j_blk == 0) | (j_blk == cur) | (j_blk == cur - 1)
    imp = jnp.where(forced, FORCED, jnp.where(j_blk > cur, NEG_INF, imp))
    rank = jnp.zeros((ns, TQ), f32)
    for i in range(min(n_live, ns)):
        row = imp[i:i + 1, :]
        tie_before = jnp.where(j_blk > i, 1.0, 0.0)
        rank = rank + jnp.where(row > imp, 1.0, jnp.where(row == imp, tie_before, 0.0))
    sel_bias = jnp.where(rank < float(min(N_SEL, ns)), 0.0, NEG_INF).astype(bf16)
    return o_cmp, sel_bias


def _attend(tiles, q_op):
    m = jnp.full((1, _NQ), NEG_INF, bf16)
    acc = jnp.zeros((V_EXT, _NQ), f32)
    s_next = jnp.dot(tiles[0][0](), q_op, preferred_element_type=f32)
    for t, (_, values_t, bias) in enumerate(tiles):
        sT = s_next
        if t + 1 < len(tiles):
            s_next = jnp.dot(tiles[t + 1][0](), q_op, preferred_element_type=f32)
        yield
        p_blocks, m_blocks, a_blocks = [], [], []
        for c, cs in enumerate(_COL_BLOCKS):
            s = sT[:, cs]
            if bias is not None:
                b0 = (c % (TQ // LANES)) * LANES
                s = s + bias[:, b0:b0 + LANES]
            s = s.astype(bf16)
            m_o = m[:, cs]
            m_n = jnp.maximum(m_o, jnp.max(s, axis=0, keepdims=True))
            p_blocks.append(jnp.exp(s - m_n))
            a_blocks.append(jnp.exp(m_o.astype(f32) - m_n.astype(f32)))
            m_blocks.append(m_n)
        m = jnp.concatenate(m_blocks, axis=1)
        pv = jnp.dot(values_t(), jnp.concatenate(p_blocks, axis=1), preferred_element_type=f32)
        yield
        acc = acc * jnp.concatenate(a_blocks, axis=1) + pv
    return acc[0:HEAD_DIM, :] * (1.0 / acc[HEAD_DIM:HEAD_DIM + 1, :])


def _nsa_pair(pi, nqt, g, qa_ref, qb_ref, ks_ref, kw_ref, vsT_ref, vwT_ref, kcc_ref, vcT_ref, ga_ref, gb_ref,
              ovT_ref, cbias_ref, wbias_ref, o_ref):
    q_tiles = (pi, nqt - 1 - pi)
    ns = ovT_ref.shape[0]
    kcc = kcc_ref[0, 0]
    vcT = vcT_ref[0, 0]
    ov = ovT_ref[...]
    cbias = cbias_ref[...]
    n_back = WINDOW // TK

    def tile(k_ref, vT_ref, kt, bias):
        return (lambda: k_ref[kt * TK:(kt + 1) * TK, :], lambda: vT_ref[:, kt * TK:(kt + 1) * TK], bias)

    def query_tile(slot, q_ref, g_ref, qt):
        q4 = jnp.concatenate([q_ref[h * HEAD_DIM:(h + 1) * HEAD_DIM, :] for h in range(HPG)], axis=1)
        zeros = jnp.zeros_like(q4)
        q_win = jnp.where(g == 0, jnp.concatenate([q4, zeros], axis=0), jnp.concatenate([zeros, q4], axis=0))

        o_win = yield from _attend(
            [tile(kw_ref, vwT_ref, qt - j, cbias if j == 0 else (wbias_ref[...] if j == n_back else None))
             for j in range(min(n_back, qt) + 1)], q_win)

        o_cmp, sel_bias = yield from _compressed_branch(q4, kcc, vcT, ov, qt * TQ)

        q_sel = jnp.concatenate([q_win, jnp.concatenate([sel_bias] * HPG, axis=1),
                                 jnp.zeros((KV_WIDTH - ns, _NQ), bf16)], axis=0)
        o_slc = yield from _attend(
            [tile(ks_ref, vsT_ref, qt, cbias)] + [tile(ks_ref, vsT_ref, kt, None) for kt in range(qt)], q_sel)

        gates = g_ref[...]
        for h in range(HPG):
            sl = slice(h * TQ, (h + 1) * TQ)
            o = (gates[3 * h:3 * h + 1, :] * o_cmp[:, sl] + gates[3 * h + 1:3 * h + 2, :] * o_slc[:, sl]
                 + gates[3 * h + 2:3 * h + 3, :] * o_win[:, sl])
            o_ref[h * HEAD_DIM:(h + 1) * HEAD_DIM, slot * TQ:(slot + 1) * TQ] = o.astype(bf16)

    _interleave([query_tile(slot, q_ref, g_ref, qt)
                 for slot, (q_ref, g_ref, qt) in enumerate(zip((qa_ref, qb_ref), (ga_ref, gb_ref), q_tiles))])


def _nsa_kernel(*refs, nqt):
    for pi in range(nqt // 2):
        @pl.when(pl.program_id(0) == pi)
        def _():
            _nsa_pair(pi, nqt, pl.program_id(2), *refs)


def _nsa_tile_position(qt, nqt):
    return jnp.where(qt < nqt // 2, 2 * qt, 2 * (nqt - 1 - qt) + 1)


def _nsa(qT, ks, kw, vsT, vwT, kcc, vcT, gT, ovT, batch, seq):
    T = batch * seq
    nqt = seq // TQ
    ncp = kcc.shape[2]
    ns = seq // SEL_LEN
    nq = HPG * TQ
    gw = HPG * HEAD_DIM
    assert TQ == TK and WINDOW % TK == 0 and ns <= KV_WIDTH
    assert nqt % 2 == 0 and nqt // 2 >= WINDOW // TK
    kl = jnp.arange(TK)[:, None]
    ql = jnp.arange(TQ)[None, :]
    cbias = jnp.where(kl <= ql, 0.0, NEG_INF).astype(f32)
    wbias = jnp.where(kl > ql, 0.0, NEG_INF).astype(f32)
    amap = lambda p, b, g: (g, b * nqt + p)
    bmap = lambda p, b, g: (g, b * nqt + nqt - 1 - p)
    const = lambda p, b, g: (0, 0)
    in_specs = [
        pl.BlockSpec((gw, TQ), amap), pl.BlockSpec((gw, TQ), bmap),
        pl.BlockSpec((seq, 2 * KV_WIDTH), lambda p, b, g: (b, 0)),
        pl.BlockSpec((seq, KV_WIDTH), lambda p, b, g: (b, 0)),
        pl.BlockSpec((V_EXT, seq), lambda p, b, g: (g, b)),
        pl.BlockSpec((V_EXT, seq), lambda p, b, g: (g, b)),
        pl.BlockSpec((1, 1, ncp, HEAD_DIM), lambda p, b, g: (b, g, 0, 0)),
        pl.BlockSpec((1, 1, HEAD_DIM, ncp), lambda p, b, g: (b, g, 0, 0)),
        pl.BlockSpec((16, TQ), amap), pl.BlockSpec((16, TQ), bmap),
        pl.BlockSpec((ns, ncp), const),
        pl.BlockSpec((TK, TQ), const),
        pl.BlockSpec((TK, TQ), const),
    ]
    return pl.pallas_call(
        functools.partial(_nsa_kernel, nqt=nqt), grid=(nqt // 2, batch, NSA_KV_HEADS), in_specs=in_specs,
        out_specs=pl.BlockSpec((gw, 2 * TQ), lambda p, b, g: (g, b * (nqt // 2) + p)),
        out_shape=jax.ShapeDtypeStruct((NSA_WIDTH, T), bf16),
        compiler_params=_cparams(3), name="nsa",
    )(qT, qT, ks, kw, vsT, vwT, kcc, vcT, gT, gT, ovT, cbias, wbias)


def _memkv_kernel(mem_ref, g_ref, wkT_ref, wv_ref, kT_ref, v_ref):
    mb = _rms(mem_ref[0], g_ref[...]).astype(bf16)
    kT_ref[0] = lax.dot_general(wkT_ref[...], mb, (((1,), (1,)), ((), ())),
                                preferred_element_type=f32).astype(bf16)
    v_ref[0] = jnp.dot(mb, wv_ref[...], preferred_element_type=f32).astype(bf16)


def _memkv(mem, g_mem, wkT, wv):
    batch, n_mem, _ = mem.shape
    c2 = lambda b: (0, 0)
    return pl.pallas_call(
        _memkv_kernel, grid=(batch,),
        in_specs=[pl.BlockSpec((1, n_mem, D_MODEL), lambda b: (b, 0, 0)),
                  pl.BlockSpec((1, D_MODEL), c2),
                  pl.BlockSpec((D_MODEL, D_MODEL), c2), pl.BlockSpec((D_MODEL, D_MODEL), c2)],
        out_specs=(pl.BlockSpec((1, D_MODEL, n_mem), lambda b: (b, 0, 0)),
                   pl.BlockSpec((1, n_mem, D_MODEL), lambda b: (b, 0, 0))),
        out_shape=(jax.ShapeDtypeStruct((batch, D_MODEL, n_mem), bf16),
                   jax.ShapeDtypeStruct((batch, n_mem, D_MODEL), bf16)),
        compiler_params=_cparams(1), name="memkv",
    )(mem, g_mem, wkT, wv)


def _post_kernel(x_ref, oTa_ref, oTb_ref, conv_ref, gnsa_ref, wout_ref, gx_ref, wq_ref, kT_ref, v_ref, wo_ref,
                 gmoe_ref, wr_ref, br_ref,
                 h2_ref, xn2_ref, eidx_ref, gate_ref, sel_ref):
    stages = []
    for sub, oT_ref in enumerate((oTa_ref, oTb_ref)):
        rows = slice(sub * TQ, (sub + 1) * TQ)
        stages.append(_post_rows(
            x_ref.at[rows], oT_ref, conv_ref.at[rows], gnsa_ref, wout_ref, gx_ref, wq_ref, kT_ref, v_ref,
            wo_ref, gmoe_ref, wr_ref, br_ref, h2_ref.at[rows],
            xn2_ref.at[sub * TQ * ROW_SUB:(sub + 1) * TQ * ROW_SUB], eidx_ref.at[rows], gate_ref.at[rows],
            sel_ref.at[rows]))
    _interleave(stages)


_DONE = object()


def _interleave(generators):
    live = list(generators)
    while live:
        live = [g for g in live if next(g, _DONE) is not _DONE]


def _post_rows(x_ref, oT_ref, conv_ref, gnsa_ref, wout_ref, gx_ref, wq_ref, kT_ref, v_ref, wo_ref,
               gmoe_ref, wr_ref, br_ref,
               h2_ref, xn2_ref, eidx_ref, gate_ref, sel_ref):
    tm = x_ref.shape[0]
    oT = oT_ref[...].astype(f32)
    onT = (oT * lax.rsqrt(jnp.mean(oT * oT, axis=0, keepdims=True) + RMS_EPS) * gnsa_ref[...]).astype(bf16)
    mix = lax.dot_general(onT, wout_ref[0:NSA_WIDTH, :], (((0,), (0,)), ((), ())),
                          preferred_element_type=f32)
    mix = mix + jnp.dot(conv_ref[...], wout_ref[NSA_WIDTH:D_MODEL, :], preferred_element_type=f32)
    yield
    h1 = x_ref[...] + mix

    hn = _rms(h1, gx_ref[...]).astype(bf16)
    q = (jnp.dot(hn, wq_ref[...], preferred_element_type=f32) * (XATTN_HEAD_DIM ** -0.5)).astype(bf16)
    yield
    head_slices = [slice(h * XATTN_HEAD_DIM, (h + 1) * XATTN_HEAD_DIM) for h in range(XATTN_HEADS)]
    scores = [jnp.dot(q[:, sl], kT_ref[0, sl, :], preferred_element_type=f32) for sl in head_slices]
    yield
    heads = []
    for s, sl in zip(scores, head_slices):
        e = jnp.exp(s - jnp.max(s, axis=-1, keepdims=True))
        p = e * (1.0 / jnp.sum(e, axis=-1, keepdims=True))
        heads.append(jnp.dot(p.astype(bf16), v_ref[0, :, sl], preferred_element_type=f32))
    yield
    o = jnp.concatenate(heads, axis=1).astype(bf16)
    h2 = h1 + jnp.dot(o, wo_ref[...], preferred_element_type=f32)
    yield
    h2_ref[...] = h2

    xn2 = _rms(h2, gmoe_ref[...])
    for s_ in range(ROW_SUB):
        xn2_ref[pl.ds(s_, tm, stride=ROW_SUB), :] = xn2[:, s_ * LANES:(s_ + 1) * LANES]

    logits = jnp.dot(xn2.astype(bf16), wr_ref[...], preferred_element_type=f32) + br_ref[...]
    yield
    lane = lax.broadcasted_iota(i32, (tm, LANES), 1)
    work = logits
    sel = jnp.zeros((tm, LANES), f32)
    eidx = jnp.zeros((tm, LANES), i32)
    vals = []
    for k in range(TOP_K):
        mk = jnp.max(work, axis=-1, keepdims=True)
        ik = jnp.min(jnp.where(work == mk, lane, LANES), axis=-1, keepdims=True)
        hit = lane == ik
        work = jnp.where(hit, -jnp.inf, work)
        sel = jnp.where(hit, 1.0, sel)
        eidx = jnp.where(lane == k, ik, eidx)
        vals.append(mk)
    es = [jnp.exp(v - vals[0]) for v in vals]
    den = es[0]
    for e in es[1:]:
        den = den + e
    gate = jnp.zeros((tm, LANES), f32)
    for k in range(TOP_K):
        gate = jnp.where(lane == k, es[k] / den, gate)
    eidx_ref[...] = eidx
    gate_ref[...] = gate
    sel_ref[...] = sel.astype(bf16)


def _post(x2, oT, conv_n, g_nsa_col, w_out, g_x, w_q, kT, v, w_o, g_moe, w_r, b_r, seq):
    T = x2.shape[0]
    tm = TM_POST
    assert tm == 2 * TQ
    tps = seq // tm
    nqt = seq // TQ
    n_mem = v.shape[1]
    row = lambda i: (i, 0)
    const = lambda i: (0, 0)

    def o_tile(sub):
        return lambda i: (0, (i // tps) * nqt + _nsa_tile_position(2 * (i % tps) + sub, nqt))

    in_specs = [
        pl.BlockSpec((tm, D_MODEL), row),
        pl.BlockSpec((NSA_WIDTH, TQ), o_tile(0)), pl.BlockSpec((NSA_WIDTH, TQ), o_tile(1)),
        pl.BlockSpec((tm, CONV_WIDTH), row),
        pl.BlockSpec((NSA_WIDTH, 1), const),
        pl.BlockSpec((D_MODEL, D_MODEL), const),
        pl.BlockSpec((1, D_MODEL), const),
        pl.BlockSpec((D_MODEL, D_MODEL), const),
        pl.BlockSpec((1, D_MODEL, n_mem), lambda i: (i // tps, 0, 0)),
        pl.BlockSpec((1, n_mem, D_MODEL), lambda i: (i // tps, 0, 0)),
        pl.BlockSpec((D_MODEL, D_MODEL), const),
        pl.BlockSpec((1, D_MODEL), const),
        pl.BlockSpec((D_MODEL, LANES), const),
        pl.BlockSpec((1, LANES), const),
    ]
    out_shape = (jax.ShapeDtypeStruct((T, D_MODEL), f32),
                 jax.ShapeDtypeStruct((T * ROW_SUB, LANES), f32),
                 jax.ShapeDtypeStruct((T, LANES), i32),
                 jax.ShapeDtypeStruct((T, LANES), f32),
                 jax.ShapeDtypeStruct((T, LANES), bf16))
    out_specs = (pl.BlockSpec((tm, D_MODEL), row),
                 pl.BlockSpec((tm * ROW_SUB, LANES), row),
                 pl.BlockSpec((tm, LANES), row), pl.BlockSpec((tm, LANES), row),
                 pl.BlockSpec((tm, LANES), row))
    return pl.pallas_call(
        _post_kernel, grid=(T // tm,), in_specs=in_specs, out_specs=out_specs, out_shape=out_shape,
        compiler_params=_cparams(1), name="post",
    )(x2, oT, oT, conv_n, g_nsa_col, w_out, g_x, w_q, kT, v, w_o, g_moe, w_r, b_r)


def _count_kernel(sel_ref, cum_ref, cnt_ref, carry):
    i = pl.program_id(0)
    tm = sel_ref.shape[0]

    @pl.when(i == 0)
    def _():
        carry[...] = jnp.zeros_like(carry)

    sel = sel_ref[...]
    r = lax.broadcasted_iota(i32, (tm, tm), 0)
    c = lax.broadcasted_iota(i32, (tm, tm), 1)
    strict_lower = jnp.where(c < r, 1.0, 0.0).astype(bf16)
    base = carry[0:1, :]
    cum_ref[...] = jnp.dot(strict_lower, sel, preferred_element_type=f32) + base
    total = base + jnp.sum(sel.astype(f32), axis=0, keepdims=True)
    carry[...] = jnp.broadcast_to(total, carry.shape)
    cnt_ref[...] = jnp.broadcast_to(total, cnt_ref.shape)


def _count(sel):
    T = sel.shape[0]
    tm = TM_ROUTE
    return pl.pallas_call(
        _count_kernel, grid=(T // tm,),
        in_specs=[pl.BlockSpec((tm, LANES), lambda i: (i, 0))],
        out_specs=(pl.BlockSpec((tm, LANES), lambda i: (i, 0)),
                   pl.BlockSpec((SUBLANES, LANES), lambda i: (0, 0))),
        out_shape=(jax.ShapeDtypeStruct((T, LANES), f32), jax.ShapeDtypeStruct((SUBLANES, LANES), f32)),
        scratch_shapes=[pltpu.VMEM((SUBLANES, LANES), f32)],
        compiler_params=_cparams(1), name="route_count",
    )(sel)


def _dest_kernel(cum_ref, eidx_ref, pstart_ref, dest_ref):
    tm = cum_ref.shape[0]
    lane = lax.broadcasted_iota(i32, (tm, LANES), 1)
    row_of = cum_ref[...] + pstart_ref[0:1, :]
    eidx = eidx_ref[...]
    dest = jnp.zeros((tm, LANES), f32)
    for k in range(TOP_K):
        ek = eidx[:, k:k + 1]
        dk = jnp.sum(jnp.where(lane == ek, row_of, 0.0), axis=-1, keepdims=True)
        dest = jnp.where(lane == k, dk, dest)
    dest_ref[...] = dest.astype(i32)


def _dest(cum, eidx, pstart8):
    T = cum.shape[0]
    tm = TM_ROUTE
    row = lambda i: (i, 0)
    return pl.pallas_call(
        _dest_kernel, grid=(T // tm,),
        in_specs=[pl.BlockSpec((tm, LANES), row), pl.BlockSpec((tm, LANES), row),
                  pl.BlockSpec((SUBLANES, LANES), lambda i: (0, 0))],
        out_specs=pl.BlockSpec((tm, LANES), row),
        out_shape=jax.ShapeDtypeStruct((T, LANES), i32),
        compiler_params=_cparams(1), name="route_dest",
    )(cum, eidx, pstart8)


def _invert_kernel(dest_ref, init_ref, slot_ref, sem):
    cp = pltpu.make_async_copy(init_ref, slot_ref, sem)
    cp.start()
    cp.wait()
    chunk = LANES

    def body(j, carry):
        base = j * chunk
        for l in range(chunk):
            slot_ref[dest_ref[base + l]] = base + l
        return carry

    lax.fori_loop(0, dest_ref.shape[0] // chunk, body, 0)


def _invert(dest_flat, n_steps):
    parity = jnp.concatenate([jnp.arange(n_steps, dtype=i32) % 2, jnp.ones((1,), i32)])
    sink = dest_flat.shape[0] + parity[:, None] * BM + jnp.arange(BM, dtype=i32)[None, :]
    return pl.pallas_call(
        _invert_kernel,
        in_specs=[pl.BlockSpec(memory_space=pltpu.SMEM), pl.BlockSpec(memory_space=pltpu.VMEM)],
        out_specs=pl.BlockSpec(memory_space=pltpu.SMEM),
        out_shape=jax.ShapeDtypeStruct(((n_steps + 1) * BM,), i32),
        scratch_shapes=[pltpu.SemaphoreType.DMA],
        compiler_params=pltpu.CompilerParams(vmem_limit_bytes=VMEM_LIMIT), name="route_invert",
    )(dest_flat, sink.reshape(-1))


def _ffn_kernel(blk_e_ref, first_ref, nxt_e_ref, wslot_ref, nused_ref, slot_ref,
                x_hbm, wgu_hbm, bgu_ref, wd_hbm, bd_ref, y_hbm,
                xbuf0, xbuf1, obuf0, obuf1, wgu_f, wd_f, wgu_bf, wd_bf, gsem, ssem, wsem, *, n_tok, sink_row):
    i = pl.program_id(0)
    nused = nused_ref[0]
    n_steps = pl.num_programs(0)
    xbuf = (xbuf0, xbuf1)
    obuf = (obuf0, obuf1)

    def row_window(ref, row):
        return ref.at[pl.ds(pl.multiple_of(row * ROW_SUB, SUBLANES), ROW_SUB)]

    def gather(blk, par, r):
        tok = slot_ref[blk * BM + r] & (n_tok - 1)
        return pltpu.make_async_copy(row_window(x_hbm, tok), xbuf[par].at[pl.ds(r * ROW_SUB, ROW_SUB)],
                                     gsem.at[par])

    def scatter(blk, par, r):
        return pltpu.make_async_copy(obuf[par].at[pl.ds(r * ROW_SUB, ROW_SUB)],
                                     row_window(y_hbm, slot_ref[blk * BM + r]), ssem.at[par])

    def wait_gather(par):
        pltpu.make_async_copy(x_hbm.at[pl.ds(0, BM * ROW_SUB)], xbuf[par], gsem.at[par]).wait()

    def wait_scatter(par):
        pltpu.make_async_copy(obuf[par], y_hbm.at[pl.ds(0, BM * ROW_SUB)], ssem.at[par]).wait()

    def weight_copies(e, ws):
        return (pltpu.make_async_copy(wgu_hbm.at[e], wgu_f.at[ws], wsem.at[ws, 0]),
                pltpu.make_async_copy(wd_hbm.at[e], wd_f.at[ws], wsem.at[ws, 1]))

    @pl.when(i == 0)
    def _():
        obuf0[...] = jnp.zeros(obuf0.shape, f32)
        obuf1[...] = jnp.zeros(obuf1.shape, f32)
        pltpu.make_async_copy(obuf0, y_hbm.at[pl.ds(sink_row * ROW_SUB, BM * ROW_SUB)], ssem.at[0]).start()
        for cp in weight_copies(blk_e_ref[0], 0):
            cp.start(priority=1)
        for r in range(BM):
            gather(0, 0, r).start()

    @pl.when((i < nused) & (first_ref[i] == 1))
    def _():
        ws = wslot_ref[i]
        for cp in weight_copies(blk_e_ref[i], ws):
            cp.wait()

        @pl.when(nxt_e_ref[i] >= 0)
        def _():
            for cp in weight_copies(nxt_e_ref[i], 1 - ws):
                cp.start(priority=1)

        wgu_bf[...] = wgu_f[ws].astype(bf16)
        wd_bf[...] = wd_f[ws].astype(bf16)

    def block(par, gather_first):
        prev = jnp.where(i == 0, n_steps, i - 1)
        wait_gather(par)

        def issue_gather():
            for r in range(BM):
                gather(i + 1, 1 - par, r).start()

        if gather_first:
            pl.when(i < nused)(issue_gather)
        x = jnp.concatenate([xbuf[par][pl.ds(s_, BM, stride=ROW_SUB), :] for s_ in range(ROW_SUB)],
                            axis=1).astype(bf16)
        for r in range(BM):
            scatter(prev, 1 - par, r).start(priority=1)
        if not gather_first:
            issue_gather()
        gu = jnp.dot(x, wgu_bf[...], preferred_element_type=f32) + bgu_ref[0]
        gg = jnp.minimum(gu[:, 0:D_FF], SWIGLU_LIMIT)
        uu = jnp.clip(gu[:, D_FF:2 * D_FF], -SWIGLU_LIMIT, SWIGLU_LIMIT)
        hmid = (uu + 1.0) * (gg * jax.nn.sigmoid(SWIGLU_ALPHA * gg))
        out = jnp.dot(hmid.astype(bf16), wd_bf[...], preferred_element_type=f32) + bd_ref[0]
        wait_scatter(par)
        for s_ in range(ROW_SUB):
            obuf[par][pl.ds(s_, BM, stride=ROW_SUB), :] = out[:, s_ * LANES:(s_ + 1) * LANES]

    for par in range(2):
        @pl.when((i < nused) & (i % 2 == par))
        def _():
            block(par, gather_first=(par == 1))

        @pl.when((i == nused) & (i % 2 == par))
        def _():
            for r in range(BM):
                scatter(i - 1, 1 - par, r).start(priority=1)
            wait_gather(par)
            wait_scatter(par)
            wait_scatter(1 - par)


def _ffn(blk_e, first, nxt_e, wslot, nused, slots, xn2_rows, w_gu, b_gu, w_d, b_d):
    n_steps = blk_e.shape[0]
    n_tok = xn2_rows.shape[0] // ROW_SUB
    assert n_tok & (n_tok - 1) == 0
    n_tok_rows = n_tok * TOP_K
    emap = lambda i, be, *_: (be[i], 0, 0)
    grid_spec = pltpu.PrefetchScalarGridSpec(
        num_scalar_prefetch=6, grid=(n_steps,),
        in_specs=[pl.BlockSpec(memory_space=pl.ANY),
                  pl.BlockSpec(memory_space=pl.ANY),
                  pl.BlockSpec((1, 1, 2 * D_FF), emap),
                  pl.BlockSpec(memory_space=pl.ANY),
                  pl.BlockSpec((1, 1, D_MODEL), emap)],
        out_specs=pl.BlockSpec(memory_space=pl.ANY),
        scratch_shapes=[pltpu.VMEM((BM * ROW_SUB, LANES), f32), pltpu.VMEM((BM * ROW_SUB, LANES), f32),
                        pltpu.VMEM((BM * ROW_SUB, LANES), f32), pltpu.VMEM((BM * ROW_SUB, LANES), f32),
                        pltpu.VMEM((2, D_MODEL, 2 * D_FF), f32), pltpu.VMEM((2, D_FF, D_MODEL), f32),
                        pltpu.VMEM((D_MODEL, 2 * D_FF), bf16), pltpu.VMEM((D_FF, D_MODEL), bf16),
                        pltpu.SemaphoreType.DMA((2,)), pltpu.SemaphoreType.DMA((2,)),
                        pltpu.SemaphoreType.DMA((2, 2))],
    )
    return pl.pallas_call(
        functools.partial(_ffn_kernel, n_tok=n_tok, sink_row=n_tok_rows), grid_spec=grid_spec,
        out_shape=jax.ShapeDtypeStruct(((n_tok_rows + 2 * BM) * ROW_SUB, LANES), f32),
        compiler_params=_cparams(1), name="ffn",
    )(blk_e, first, nxt_e, wslot, nused, slots, xn2_rows, w_gu, b_gu, w_d, b_d)


def _combine_kernel(*refs, final_norm):
    y_refs = refs[:TOP_K]
    gate_ref, h2_ref, gfin_ref, o_ref = refs[TOP_K:]
    tm = h2_ref.shape[0]
    gate = gate_ref[...]
    cols = []
    for s_ in range(ROW_SUB):
        acc = gate[:, 0:1] * y_refs[0][pl.ds(s_, tm, stride=ROW_SUB), :]
        for k in range(1, TOP_K):
            acc = acc + gate[:, k:k + 1] * y_refs[k][pl.ds(s_, tm, stride=ROW_SUB), :]
        cols.append(acc)
    h = h2_ref[...] + jnp.concatenate(cols, axis=1)
    if final_norm:
        h = _rms(h, gfin_ref[...])
    o_ref[...] = h


def _combine(y_rows, gate, h2, g_final, final_norm):
    T = h2.shape[0]
    tm = TM_ROW
    row = lambda i: (i, 0)
    planes = [pl.BlockSpec((tm * ROW_SUB, LANES), functools.partial(lambda i, k: (k * (T // tm) + i, 0), k=k))
              for k in range(TOP_K)]
    return pl.pallas_call(
        functools.partial(_combine_kernel, final_norm=final_norm), grid=(T // tm,),
        in_specs=planes + [pl.BlockSpec((tm, LANES), row), pl.BlockSpec((tm, D_MODEL), row),
                           pl.BlockSpec((1, D_MODEL), lambda i: (0, 0))],
        out_specs=pl.BlockSpec((tm, D_MODEL), row),
        out_shape=jax.ShapeDtypeStruct((T, D_MODEL), f32),
        compiler_params=_cparams(1), name="combine",
    )(*([y_rows] * TOP_K), gate, h2, g_final)


def _prep_inproj_weights(w_in):
    sizes = (NSA_WIDTH,) + (KV_WIDTH,) * 6 + (3 * NSA_HEADS,) + (CONV_WIDTH,) * 3
    offs = [0]
    for s in sizes:
        offs.append(offs[-1] + s)
    seg = lambda n: w_in[:, offs[n]:offs[n + 1]]
    q, kc, vc, ks, vs, kw, vw, gl, ch, cb, cc = (seg(n) for n in range(11))
    w_row = jnp.concatenate([kc, vc, ks, kw, ch, cb, cc], axis=1).astype(bf16)
    gl_g = gl.reshape(D_MODEL, NSA_KV_HEADS, HPG * 3)
    gl_g = jnp.pad(gl_g, ((0, 0), (0, 0), (0, 16 - HPG * 3))).reshape(D_MODEL, NSA_KV_HEADS * 16)
    w_t = jnp.concatenate([q, vs, vw, gl_g], axis=1).T.astype(bf16)
    return w_row, w_t


def _rope_tables(positions):
    half = HEAD_DIM // 2
    inv_freq = ROPE_THETA ** (-jnp.arange(half, dtype=f32) / half)
    ang = positions.reshape(-1).astype(f32)[:, None] * inv_freq
    cos = jnp.cos(ang)
    sin = jnp.sin(ang)
    reps = KV_WIDTH // HEAD_DIM
    cosr = jnp.tile(cos, (1, 2 * reps))
    sinr = jnp.tile(jnp.concatenate([-sin, sin], axis=1), (1, reps))
    return cosr, sinr, cos.T, sin.T


def _cmp_weights(pos, w1, b1):
    w1r = w1.astype(bf16).reshape(2, CMP_STRIDE, HEAD_DIM, CMP_HIDDEN)
    zero = jnp.zeros_like(w1r)
    big = jnp.stack([jnp.concatenate([w1r if g == k else zero for k in range(NSA_KV_HEADS)], axis=-1)
                     for g in range(NSA_KV_HEADS)], axis=2)
    big = big.reshape(2, CMP_STRIDE * KV_WIDTH, NSA_KV_HEADS * CMP_HIDDEN)
    p = jnp.broadcast_to(pos.reshape(2, CMP_STRIDE, 1, HEAD_DIM), (2, CMP_STRIDE, NSA_KV_HEADS, HEAD_DIM))
    p = jnp.pad(p.reshape(2, CMP_STRIDE * KV_WIDTH), ((0, SUBLANES - 2), (0, 0)))
    return big, p, jnp.tile(b1.reshape(1, CMP_HIDDEN), (1, NSA_KV_HEADS))


def _mixer_core(x2, tables, p, batch, seq):
    cosr, sinr, cost, sint = tables
    row1 = lambda v: v.reshape(1, -1)
    w_row, w_t = _prep_inproj_weights(p['w_mix_in'])
    conv_w8 = jnp.pad(p['conv_w'], ((0, SUBLANES - CONV_K), (0, 0)))
    qT, kc, vc, ks, kw, vsT, vwT, gT, conv_n = _inproj(
        x2, row1(p['g_mix_norm']), w_row, w_t, cosr, sinr, cost, sint, conv_w8, row1(p['g_conv_out']), seq)

    w1k, pk, b1k = _cmp_weights(p['cmp_pos_k'], p['cmp_w1_k'], p['cmp_b1_k'])
    w1v, pv, b1v = _cmp_weights(p['cmp_pos_v'], p['cmp_w1_v'], p['cmp_b1_v'])
    kcc, vcT = _compress(kc, vc, pk, pv, w1k, w1v, b1k, b1v,
                         p['cmp_w2_k'].astype(bf16), p['cmp_w2_v'].T.astype(bf16), batch, seq)

    ncp = seq // CMP_STRIDE
    ns = seq // SEL_LEN
    cs = jnp.arange(ncp) * CMP_STRIDE
    js = jnp.arange(ns) * SEL_LEN
    overlap = jnp.clip(jnp.minimum(cs[:, None] + CMP_LEN, js[None, :] + SEL_LEN)
                       - jnp.maximum(cs[:, None], js[None, :]), 0, None).astype(f32) / CMP_LEN
    ovT = overlap.T.astype(bf16)
    oT = _nsa(qT, ks, kw, vsT, vwT, kcc, vcT, gT, ovT, batch, seq)
    return oT, conv_n


def _layer(h, memf, tables, p, final_gain, final_norm):
    batch, seq, _ = h.shape
    T = batch * seq
    x2 = h.reshape(T, D_MODEL)
    row1 = lambda v: v.reshape(1, -1)
    oT, conv_n = _mixer_core(x2, tables, p, batch, seq)

    w_xkv = p['w_xkv']
    kT, v = _memkv(memf, row1(p['g_mem_norm']), w_xkv[:, :D_MODEL].T.astype(bf16),
                   w_xkv[:, D_MODEL:].astype(bf16))
    w_r = jnp.pad(p['w_router'], ((0, 0), (0, LANES - N_EXPERTS))).astype(bf16)
    b_r = jnp.pad(p['b_router'], (0, LANES - N_EXPERTS), constant_values=NEG_INF).reshape(1, LANES)
    h2, xn2_rows, eidx, gate, sel = _post(
        x2, oT, conv_n, p['g_nsa_out'].reshape(NSA_WIDTH, 1), p['w_mix_out'].astype(bf16),
        row1(p['g_xattn_norm']), p['w_xq'].astype(bf16), kT, v, p['w_xo'].astype(bf16),
        row1(p['g_moe_norm']), w_r, b_r, seq)

    cum, cnt = _count(sel)
    counts = cnt[0, :N_EXPERTS].astype(i32)
    padded = (counts + BM - 1) // BM * BM
    pend = jnp.cumsum(padded)
    pstart = pend - padded
    n_steps = (T * TOP_K) // BM + N_EXPERTS + 1
    nused = (pend[-1] // BM).astype(i32)
    step = jnp.arange(n_steps, dtype=i32)
    used = step < nused
    blk_raw = jnp.minimum(jnp.sum((pend[None, :] <= (step * BM)[:, None]).astype(i32), axis=1), N_EXPERTS - 1)
    e_ids = jnp.arange(N_EXPERTS, dtype=i32)
    last_e = jnp.max(jnp.where(padded > 0, e_ids, 0))
    blk_e = jnp.where(used, blk_raw, last_e)
    first = (used & jnp.concatenate([jnp.ones((1,), bool), blk_e[1:] != blk_e[:-1]])).astype(i32)
    wslot = (jnp.cumsum(first) - 1) % 2
    later = (e_ids[None, :] > e_ids[:, None]) & (padded > 0)[None, :]
    nxt_of = jnp.min(jnp.where(later, e_ids[None, :], N_EXPERTS), axis=1)
    nxt_of = jnp.where(nxt_of < N_EXPERTS, nxt_of, -1).astype(i32)
    nxt_e = jnp.sum(jnp.where(blk_e[:, None] == e_ids[None, :], nxt_of[None, :], 0), axis=1)
    pstart8 = jnp.broadcast_to(jnp.pad(pstart.astype(f32), (0, LANES - N_EXPERTS))[None, :], (SUBLANES, LANES))

    dest = _dest(cum, eidx, pstart8)
    slots = _invert(dest[:, :TOP_K].T.reshape(-1), n_steps)
    y_rows = _ffn(blk_e, first, nxt_e, wslot.astype(i32), nused.reshape(1), slots, xn2_rows,
                  p['w_gate_up'], p['b_gate_up'].reshape(N_EXPERTS, 1, 2 * D_FF), p['w_down'],
                  p['b_down'].reshape(N_EXPERTS, 1, D_MODEL))
    out = _combine(y_rows, gate, h2, row1(final_gain), final_norm)
    return out.reshape(batch, seq, D_MODEL)


_LAYER_PARAMS = ('g_mix_norm', 'w_mix_in', 'cmp_pos_k', 'cmp_pos_v', 'cmp_w1_k', 'cmp_b1_k', 'cmp_w2_k',
                 'cmp_w1_v', 'cmp_b1_v', 'cmp_w2_v', 'conv_w', 'g_nsa_out', 'g_conv_out', 'w_mix_out',
                 'g_xattn_norm', 'g_mem_norm', 'w_xq', 'w_xkv', 'w_xo', 'g_moe_norm', 'w_router', 'b_router',
                 'w_gate_up', 'b_gate_up', 'w_down', 'b_down')


def kernel(x, mem, positions, g_mix_norm, w_mix_in, cmp_pos_k, cmp_pos_v, cmp_w1_k, cmp_b1_k, cmp_w2_k, cmp_w1_v, cmp_b1_v, cmp_w2_v, conv_w, g_nsa_out, g_conv_out, w_mix_out, g_xattn_norm, g_mem_norm, w_xq, w_xkv, w_xo, g_moe_norm, w_router, b_router, w_gate_up, b_gate_up, w_down, b_down, g_final):
    stacked = dict(zip(_LAYER_PARAMS, (g_mix_norm, w_mix_in, cmp_pos_k, cmp_pos_v, cmp_w1_k, cmp_b1_k, cmp_w2_k,
                                       cmp_w1_v, cmp_b1_v, cmp_w2_v, conv_w, g_nsa_out, g_conv_out, w_mix_out,
                                       g_xattn_norm, g_mem_norm, w_xq, w_xkv, w_xo, g_moe_norm, w_router,
                                       b_router, w_gate_up, b_gate_up, w_down, b_down)))
    depth = g_mix_norm.shape[0]
    tables = _rope_tables(positions)
    h = x
    for l in range(depth):
        p = {k: v[l] for k, v in stacked.items()}
        last = l == depth - 1
        h = _layer(h, mem, tables, p, g_final, final_norm=last)
    return h
```

```python
import functools

import jax
import jax.numpy as jnp
from jax import lax
from jax.experimental import pallas as pl
from jax.experimental.pallas import tpu as pltpu

f32 = jnp.float32
bf16 = jnp.bfloat16
i32 = jnp.int32

D_MODEL = 1024
HEAD_DIM = 64
NSA_HEADS = 8
NSA_KV_HEADS = 2
HPG = NSA_HEADS // NSA_KV_HEADS
NSA_WIDTH = NSA_HEADS * HEAD_DIM
KV_WIDTH = NSA_KV_HEADS * HEAD_DIM
CONV_WIDTH = D_MODEL - NSA_WIDTH
CONV_K = 3
CMP_LEN = 32
CMP_STRIDE = 16
CMP_HIDDEN = 256
SEL_LEN = 64
N_SEL = 16
WINDOW = 512
ROPE_THETA = 10000.0
XATTN_HEADS = 4
XATTN_HEAD_DIM = D_MODEL // XATTN_HEADS
N_EXPERTS = 32
TOP_K = 4
D_FF = D_MODEL
SWIGLU_LIMIT = 7.0
SWIGLU_ALPHA = 1.702
RMS_EPS = 1e-5
NEG_INF = -1e30
FORCED = 1e30

LANES = 128
SUBLANES = 8
VMEM_LIMIT = 56 * 1024 * 1024

TM_IN = 512
TQ = 256
TK = 256
TM_POST = 512
TM_ROUTE = 512
TM_ROW = 512
V_EXT = HEAD_DIM + 16
BM = 256
ROW_SUB =D_MODEL // LANES


def _cparams(n_axes, **kw):
    return pltpu.CompilerParams(dimension_semantics=("arbitrary",) * n_axes,
                                vmem_limit_bytes=VMEM_LIMIT, **kw)


def _rms(t, gain):
    return t * lax.rsqrt(jnp.mean(t * t, axis=-1, keepdims=True) + RMS_EPS) * gain


def _inproj_kernel(x_ref, g_ref, wr_ref, wt_ref, cosr_ref, sinr_ref, cost_ref, sint_ref,
                   convw_ref, gconv_ref,
                   qT_ref, kc_ref, vc_ref, ks_ref, kw_ref, vsT_ref, vwT_ref, gT_ref, conv_ref,
                   ubuf, *, tiles_per_seq):
    i = pl.program_id(0)
    tm = x_ref.shape[0]

    @pl.when(i % tiles_per_seq == 0)
    def _():
        ubuf[0:SUBLANES, :] = jnp.zeros((SUBLANES, CONV_WIDTH), f32)

    @pl.when(i % tiles_per_seq != 0)
    def _():
        ubuf[0:SUBLANES, :] = ubuf[tm:tm + SUBLANES, :]

    stages = []
    for sub in range(tm // TQ):
        rows = slice(sub * TQ, (sub + 1) * TQ)
        stages.append(_inproj_rows(
            (i % tiles_per_seq) * tm + sub * TQ, SUBLANES + sub * TQ,
            x_ref.at[rows], g_ref, wr_ref, wt_ref, cosr_ref.at[rows], sinr_ref.at[rows],
            cost_ref.at[:, rows], sint_ref.at[:, rows], convw_ref, gconv_ref,
            qT_ref.at[:, rows], kc_ref.at[rows], vc_ref.at[rows], ks_ref.at[rows], kw_ref.at[rows],
            vsT_ref.at[:, rows], vwT_ref.at[:, rows], gT_ref.at[:, rows], conv_ref.at[rows], ubuf))
    _interleave(stages)


def _inproj_rows(tok0, u0, x_ref, g_ref, wr_ref, wt_ref, cosr_ref, sinr_ref, cost_ref, sint_ref,
                 convw_ref, gconv_ref,
                 qT_ref, kc_ref, vc_ref, ks_ref, kw_ref, vsT_ref, vwT_ref, gT_ref, conv_ref, ubuf):
    tm = x_ref.shape[0]
    xb = _rms(x_ref[...], g_ref[...]).astype(bf16)
    yield

    c0 = 4 * KV_WIDTH
    pr = jnp.dot(xb, wr_ref[:, 0:c0], preferred_element_type=f32)
    pc = jnp.dot(xb, wr_ref[:, c0:c0 + 3 * CONV_WIDTH], preferred_element_type=f32)
    pt = lax.dot_general(wt_ref[...], xb, (((1,), (1,)), ((), ())), preferred_element_type=f32)
    yield
    cosr = cosr_ref[...]
    sinr = sinr_ref[...]
    lane = lax.broadcasted_iota(i32, (tm, KV_WIDTH), 1)
    first_half = (lane & (HEAD_DIM - 1)) < HEAD_DIM // 2

    def rope_rows(t):
        rot = jnp.where(first_half, pltpu.roll(t, KV_WIDTH - HEAD_DIM // 2, 1),
                        pltpu.roll(t, HEAD_DIM // 2, 1))
        return t * cosr + rot * sinr

    kc_ref[...] = rope_rows(pr[:, 0:KV_WIDTH])
    vc_ref[...] = pr[:, KV_WIDTH:2 * KV_WIDTH]
    ks_ref[:, 0:KV_WIDTH] = rope_rows(pr[:, 2 * KV_WIDTH:3 * KV_WIDTH]).astype(bf16)
    tok = tok0 + lax.broadcasted_iota(i32, (tm, KV_WIDTH), 0)
    ks_ref[:, KV_WIDTH:2 * KV_WIDTH] = jnp.where(lane == tok // SEL_LEN, 1.0, 0.0).astype(bf16)
    kw_ref[...] = rope_rows(pr[:, 3 * KV_WIDTH:4 * KV_WIDTH]).astype(bf16)

    ch = pc[:, 0:CONV_WIDTH]
    cb = pc[:, CONV_WIDTH:2 * CONV_WIDTH]
    cc = pc[:, 2 * CONV_WIDTH:3 * CONV_WIDTH]
    u = cc * ch
    ubuf[u0:u0 + tm, :] = u
    u1 = ubuf[u0 - 1:u0 - 1 + tm, :]
    u2 = ubuf[u0 - 2:u0 - 2 + tm, :]
    w = convw_ref[...]
    y = cb * (w[0:1, :] * u2 + w[1:2, :] * u1 + w[2:3, :] * u)
    conv_ref[...] = _rms(y, gconv_ref[...]).astype(bf16)
    yield

    cost = cost_ref[...]
    sint = sint_ref[...]
    half = HEAD_DIM // 2
    scale = HEAD_DIM ** -0.5
    for h in range(NSA_HEADS):
        t1 = pt[h * HEAD_DIM:h * HEAD_DIM + half, :]
        t2 = pt[h * HEAD_DIM + half:(h + 1) * HEAD_DIM, :]
        qT_ref[h * HEAD_DIM:h * HEAD_DIM + half, :] = ((t1 * cost - t2 * sint) * scale).astype(bf16)
        qT_ref[h * HEAD_DIM + half:(h + 1) * HEAD_DIM, :] = ((t2 * cost + t1 * sint) * scale).astype(bf16)
    r0 = NSA_WIDTH
    ones_rows = jnp.where(lax.broadcasted_iota(i32, (V_EXT - HEAD_DIM, tm), 0) == 0, 1.0, 0.0).astype(bf16)
    for vT_ref, base in ((vsT_ref, r0), (vwT_ref, r0 + KV_WIDTH)):
        for g in range(NSA_KV_HEADS):
            vT_ref[g * V_EXT:g * V_EXT + HEAD_DIM, :] = pt[base + g * HEAD_DIM:base + (g + 1) * HEAD_DIM, :].astype(bf16)
            vT_ref[g * V_EXT + HEAD_DIM:(g + 1) * V_EXT, :] = ones_rows
    gT_ref[...] = jax.nn.sigmoid(pt[r0 + 2 * KV_WIDTH:r0 + 2 * KV_WIDTH + 32, :])


def _inproj(x2, g_mix, w_row, w_t, cosr, sinr, cost, sint, conv_w8, g_conv, seq):
    T = x2.shape[0]
    tm = TM_IN
    n_row = w_row.shape[1]
    n_t = w_t.shape[0]
    row = lambda i: (i, 0)
    col = lambda i: (0, i)
    const = lambda i: (0, 0)
    out_shape = (
        jax.ShapeDtypeStruct((NSA_WIDTH, T), bf16),
        jax.ShapeDtypeStruct((T, KV_WIDTH), f32),
        jax.ShapeDtypeStruct((T, KV_WIDTH), f32),
        jax.ShapeDtypeStruct((T, 2 * KV_WIDTH), bf16),
        jax.ShapeDtypeStruct((T, KV_WIDTH), bf16),
        jax.ShapeDtypeStruct((NSA_KV_HEADS * V_EXT, T), bf16),
        jax.ShapeDtypeStruct((NSA_KV_HEADS * V_EXT, T), bf16),
        jax.ShapeDtypeStruct((32, T), f32),
        jax.ShapeDtypeStruct((T, CONV_WIDTH), bf16),
    )
    out_specs = (
        pl.BlockSpec((NSA_WIDTH, tm), col),
        pl.BlockSpec((tm, KV_WIDTH), row), pl.BlockSpec((tm, KV_WIDTH), row),
        pl.BlockSpec((tm, 2 * KV_WIDTH), row), pl.BlockSpec((tm, KV_WIDTH), row),
        pl.BlockSpec((NSA_KV_HEADS * V_EXT, tm), col), pl.BlockSpec((NSA_KV_HEADS * V_EXT, tm), col),
        pl.BlockSpec((32, tm), col),
        pl.BlockSpec((tm, CONV_WIDTH), row),
    )
    in_specs = [
        pl.BlockSpec((tm, D_MODEL), row),
        pl.BlockSpec((1, D_MODEL), const),
        pl.BlockSpec((D_MODEL, n_row), const),
        pl.BlockSpec((n_t, D_MODEL), const),
        pl.BlockSpec((tm, KV_WIDTH), row), pl.BlockSpec((tm, KV_WIDTH), row),
        pl.BlockSpec((HEAD_DIM // 2, tm), col), pl.BlockSpec((HEAD_DIM // 2, tm), col),
        pl.BlockSpec((SUBLANES, CONV_WIDTH), const),
        pl.BlockSpec((1, CONV_WIDTH), const),
    ]
    return pl.pallas_call(
        functools.partial(_inproj_kernel, tiles_per_seq=seq // tm),
        grid=(T // tm,), in_specs=in_specs, out_specs=out_specs, out_shape=out_shape,
        scratch_shapes=[pltpu.VMEM((tm + 2 * SUBLANES, CONV_WIDTH), f32)],
        compiler_params=_cparams(1), name="inproj",
    )(x2, g_mix, w_row, w_t, cosr, sinr, cost, sint, conv_w8, g_conv)


def _compress_kernel(xk_ref, xv_ref, pk_ref, pv_ref, w1k_ref, w1v_ref, b1k_ref, b1v_ref,
                     w2k_ref, w2vT_ref, kcc_ref, vcT_ref):
    ncp = xk_ref.shape[0] // CMP_STRIDE

    def hidden(x_ref, p_ref, w1_ref, b1_ref):
        x = jnp.concatenate([x_ref[pl.ds(l, ncp, stride=CMP_STRIDE), :] for l in range(CMP_STRIDE)], axis=1)
        lo = (x + p_ref[0:1, :]).astype(bf16)
        hi = (x + p_ref[1:2, :]).astype(bf16)
        a = jnp.dot(lo, w1_ref[0], preferred_element_type=f32)
        b = jnp.dot(hi, w1_ref[1], preferred_element_type=f32)
        pre = a + pltpu.roll(b, ncp - 1, 0) + b1_ref[...]
        return jax.nn.gelu(pre).astype(bf16)

    hk = hidden(xk_ref, pk_ref, w1k_ref, b1k_ref)
    hv = hidden(xv_ref, pv_ref, w1v_ref, b1v_ref)
    for g in range(NSA_KV_HEADS):
        sl = slice(g * CMP_HIDDEN, (g + 1) * CMP_HIDDEN)
        kcc_ref[0, g] = jnp.dot(hk[:, sl], w2k_ref[...], preferred_element_type=f32).astype(bf16)
        vcT_ref[0, g] = lax.dot_general(w2vT_ref[...], hv[:, sl], (((1,), (1,)), ((), ())),
                                        preferred_element_type=f32).astype(bf16)


def _compress(kc_rows, vc_rows, pk, pv, w1k, w1v, b1k, b1v, w2k, w2vT, batch, seq):
    ncp = seq // CMP_STRIDE
    wide = CMP_STRIDE * KV_WIDTH
    c2 = lambda b: (0, 0)
    c3 = lambda b: (0, 0, 0)
    in_specs = [
        pl.BlockSpec((seq, KV_WIDTH), lambda b: (b, 0)),
        pl.BlockSpec((seq, KV_WIDTH), lambda b: (b, 0)),
        pl.BlockSpec((SUBLANES, wide), c2), pl.BlockSpec((SUBLANES, wide), c2),
        pl.BlockSpec((2, wide, 2 * CMP_HIDDEN), c3), pl.BlockSpec((2, wide, 2 * CMP_HIDDEN), c3),
        pl.BlockSpec((1, 2 * CMP_HIDDEN), c2), pl.BlockSpec((1, 2 * CMP_HIDDEN), c2),
        pl.BlockSpec((CMP_HIDDEN, HEAD_DIM), c2), pl.BlockSpec((HEAD_DIM, CMP_HIDDEN), c2),
    ]
    out_shape = (jax.ShapeDtypeStruct((batch, NSA_KV_HEADS, ncp, HEAD_DIM), bf16),
                 jax.ShapeDtypeStruct((batch, NSA_KV_HEADS, HEAD_DIM, ncp), bf16))
    out_specs = (pl.BlockSpec((1, NSA_KV_HEADS, ncp, HEAD_DIM), lambda b: (b, 0, 0, 0)),
                 pl.BlockSpec((1, NSA_KV_HEADS, HEAD_DIM, ncp), lambda b: (b, 0, 0, 0)))
    return pl.pallas_call(
        _compress_kernel, grid=(batch,), in_specs=in_specs, out_specs=out_specs,
        out_shape=out_shape, compiler_params=_cparams(1), name="compress",
    )(kc_rows, vc_rows, pk, pv, w1k, w1v, b1k, b1v, w2k, w2vT)


_NQ = HPG * TQ
_COL_BLOCKS = [slice(c * LANES, (c + 1) * LANES) for c in range(_NQ // LANES)]


def _compressed_branch(q4, kcc, vcT, ov, s0):
    ncp = kcc.shape[0]
    ns = ov.shape[0]
    s_lane = s0 + (lax.broadcasted_iota(i32, (1, _NQ), 1) & (TQ - 1))
    sc = jnp.dot(kcc, q4, preferred_element_type=f32)
    yield
    c_end = lax.broadcasted_iota(i32, (ncp, 1), 0) * CMP_STRIDE + (CMP_LEN - 1)
    blocks = []
    for cs in _COL_BLOCKS:
        cmask = c_end <= s_lane[:, cs]
        scm = jnp.where(cmask, sc[:, cs], NEG_INF)
        e_c = jnp.where(cmask, jnp.exp(scm - jnp.max(scm, axis=0, keepdims=True)), 0.0)
        l_c = jnp.sum(e_c, axis=0, keepdims=True)
        blocks.append(e_c * jnp.where(l_c > 0.0, 1.0 / l_c, 0.0))
    p_c = jnp.concatenate(blocks, axis=1)
    o_cmp = jnp.dot(vcT, p_c.astype(bf16), preferred_element_type=f32)
    yield

    j_blk = lax.broadcasted_iota(i32, (ns, 1), 0)
    cur = (s0 + lax.broadcasted_iota(i32, (1, TQ), 1)) // SEL_LEN
    n_live = (s0 + TQ - 1) // SEL_LEN + 1
    if n_live <= N_SEL:
        return o_cmp, jnp.where(j_blk <= cur, 0.0, NEG_INF).astype(bf16)

    ps = p_c[:, 0:TQ]
    for h in range(1, HPG):
        ps = ps + p_c[:, h * TQ:(h + 1) * TQ]
    p_hi = ps.astype(bf16)
    r1 = ps - p_hi.astype(f32)
    p_mid = r1.astype(bf16)
    p_lo = (r1 - p_mid.astype(f32)).astype(bf16)
    imp = (jnp.dot(ov, p_hi, preferred_element_type=f32) + jnp.dot(ov, p_mid, preferred_element_type=f32)
           + jnp.dot(ov, p_lo, preferred_element_type=f32))
    yield
    forced = (j_blk == 0) | (j_blk == cur) | (j_blk == cur - 1)
    imp = jnp.where(forced, FORCED, jnp.where(j_blk > cur, NEG_INF, imp))
    rank = jnp.zeros((ns, TQ), f32)
    for i in range(min(n_live, ns)):
        row = imp[i:i + 1, :]
        tie_before = jnp.where(j_blk > i, 1.0, 0.0)
        rank = rank + jnp.where(row > imp, 1.0, jnp.where(row == imp, tie_before, 0.0))
    sel_bias = jnp.where(rank < float(min(N_SEL, ns)), 0.0, NEG_INF).astype(bf16)
    return o_cmp, sel_bias


def _attend(tiles, q_op):
    m = jnp.full((1, _NQ), NEG_INF, bf16)
    acc = jnp.zeros((V_EXT, _NQ), f32)
    s_next = jnp.dot(tiles[0][0](), q_op, preferred_element_type=f32)
    for t, (_, values_t, bias) in enumerate(tiles):
        sT = s_next
        if t + 1 < len(tiles):
            s_next = jnp.dot(tiles[t + 1][0](), q_op, preferred_element_type=f32)
        yield
        p_blocks, m_blocks, a_blocks = [], [], []
        for c, cs in enumerate(_COL_BLOCKS):
            s = sT[:, cs]
            if bias is not None:
                b0 = (c % (TQ // LANES)) * LANES
                s = s + bias[:, b0:b0 + LANES]
            s = s.astype(bf16)
            m_o = m[:, cs]
            m_n = jnp.maximum(m_o, jnp.max(s, axis=0, keepdims=True))
            p_blocks.append(jnp.exp(s - m_n))
            a_blocks.append(jnp.exp(m_o.astype(f32) - m_n.astype(f32)))
            m_blocks.append(m_n)
        m = jnp.concatenate(m_blocks, axis=1)
        pv = jnp.dot(values_t(), jnp.concatenate(p_blocks, axis=1), preferred_element_type=f32)
        yield
        acc = acc * jnp.concatenate(a_blocks, axis=1) + pv
    return acc[0:HEAD_DIM, :] * (1.0 / acc[HEAD_DIM:HEAD_DIM + 1, :])


def _nsa_pair(pi, nqt, g, qa_ref, qb_ref, ks_ref, kw_ref, vsT_ref, vwT_ref, kcc_ref, vcT_ref, ga_ref, gb_ref,
              ovT_ref, cbias_ref, wbias_ref, o_ref):
    q_tiles = (pi, nqt - 1 - pi)
    ns = ovT_ref.shape[0]
    kcc = kcc_ref[0, 0]
    vcT = vcT_ref[0, 0]
    ov = ovT_ref[...]
    cbias = cbias_ref[...]
    n_back = WINDOW // TK

    def tile(k_ref, vT_ref, kt, bias):
        return (lambda: k_ref[kt * TK:(kt + 1) * TK, :], lambda: vT_ref[:, kt * TK:(kt + 1) * TK], bias)

    def query_tile(slot, q_ref, g_ref, qt):
        q4 = jnp.concatenate([q_ref[h * HEAD_DIM:(h + 1) * HEAD_DIM, :] for h in range(HPG)], axis=1)
        zeros = jnp.zeros_like(q4)
        q_win = jnp.where(g == 0, jnp.concatenate([q4, zeros], axis=0), jnp.concatenate([zeros, q4], axis=0))

        o_win = yield from _attend(
            [tile(kw_ref, vwT_ref, qt - j, cbias if j == 0 else (wbias_ref[...] if j == n_back else None))
             for j in range(min(n_back, qt) + 1)], q_win)

        o_cmp, sel_bias = yield from _compressed_branch(q4, kcc, vcT, ov, qt * TQ)

        q_sel = jnp.concatenate([q_win, jnp.concatenate([sel_bias] * HPG, axis=1),
                                 jnp.zeros((KV_WIDTH - ns, _NQ), bf16)], axis=0)
        o_slc = yield from _attend(
            [tile(ks_ref, vsT_ref, qt, cbias)] + [tile(ks_ref, vsT_ref, kt, None) for kt in range(qt)], q_sel)

        gates = g_ref[...]
        for h in range(HPG):
            sl = slice(h * TQ, (h + 1) * TQ)
            o = (gates[3 * h:3 * h + 1, :] * o_cmp[:, sl] + gates[3 * h + 1:3 * h + 2, :] * o_slc[:, sl]
                 + gates[3 * h + 2:3 * h + 3, :] * o_win[:, sl])
            o_ref[h * HEAD_DIM:(h + 1) * HEAD_DIM, slot * TQ:(slot + 1) * TQ] = o.astype(bf16)

    _interleave([query_tile(slot, q_ref, g_ref, qt)
                 for slot, (q_ref, g_ref, qt) in enumerate(zip((qa_ref, qb_ref), (ga_ref, gb_ref), q_tiles))])


def _nsa_kernel(*refs, nqt):
    for pi in range(nqt // 2):
        @pl.when(pl.program_id(0) == pi)
        def _():
            _nsa_pair(pi, nqt, pl.program_id(2), *refs)


def _nsa_tile_position(qt, nqt):
    return jnp.where(qt < nqt // 2, 2 * qt, 2 * (nqt - 1 - qt) + 1)


def _nsa(qT, ks, kw, vsT, vwT, kcc, vcT, gT, ovT, batch, seq):
    T = batch * seq
    nqt = seq // TQ
    ncp = kcc.shape[2]
    ns = seq // SEL_LEN
    nq = HPG * TQ
    gw = HPG * HEAD_DIM
    assert TQ == TK and WINDOW % TK == 0 and ns <= KV_WIDTH
    assert nqt % 2 == 0 and nqt // 2 >= WINDOW // TK
    kl = jnp.arange(TK)[:, None]
    ql = jnp.arange(TQ)[None, :]
    cbias = jnp.where(kl <= ql, 0.0, NEG_INF).astype(f32)
    wbias = jnp.where(kl > ql, 0.0, NEG_INF).astype(f32)
    amap = lambda p, b, g: (g, b * nqt + p)
    bmap = lambda p, b, g: (g, b * nqt + nqt - 1 - p)
    const = lambda p, b, g: (0, 0)
    in_specs = [
        pl.BlockSpec((gw, TQ), amap), pl.BlockSpec((gw, TQ), bmap),
        pl.BlockSpec((seq, 2 * KV_WIDTH), lambda p, b, g: (b, 0)),
        pl.BlockSpec((seq, KV_WIDTH), lambda p, b, g: (b, 0)),
        pl.BlockSpec((V_EXT, seq), lambda p, b, g: (g, b)),
        pl.BlockSpec((V_EXT, seq), lambda p, b, g: (g, b)),
        pl.BlockSpec((1, 1, ncp, HEAD_DIM), lambda p, b, g: (b, g, 0, 0)),
        pl.BlockSpec((1, 1, HEAD_DIM, ncp), lambda p, b, g: (b, g, 0, 0)),
        pl.BlockSpec((16, TQ), amap), pl.BlockSpec((16, TQ), bmap),
        pl.BlockSpec((ns, ncp), const),
        pl.BlockSpec((TK, TQ), const),
        pl.BlockSpec((TK, TQ), const),
    ]
    return pl.pallas_call(
        functools.partial(_nsa_kernel, nqt=nqt), grid=(nqt // 2, batch, NSA_KV_HEADS), in_specs=in_specs,
        out_specs=pl.BlockSpec((gw, 2 * TQ), lambda p, b, g: (g, b * (nqt // 2) + p)),
        out_shape=jax.ShapeDtypeStruct((NSA_WIDTH, T), bf16),
        compiler_params=_cparams(3), name="nsa",
    )(qT, qT, ks, kw, vsT, vwT, kcc, vcT, gT, gT, ovT, cbias, wbias)


def _memkv_kernel(mem_ref, g_ref, wkT_ref, wv_ref, kT_ref, v_ref):
    mb = _rms(mem_ref[0], g_ref[...]).astype(bf16)
    kT_ref[0] = lax.dot_general(wkT_ref[...], mb, (((1,), (1,)), ((), ())),
                                preferred_element_type=f32).astype(bf16)
    v_ref[0] = jnp.dot(mb, wv_ref[...], preferred_element_type=f32).astype(bf16)


def _memkv(mem, g_mem, wkT, wv):
    batch, n_mem, _ = mem.shape
    c2 = lambda b: (0, 0)
    return pl.pallas_call(
        _memkv_kernel, grid=(batch,),
        in_specs=[pl.BlockSpec((1, n_mem, D_MODEL), lambda b: (b, 0, 0)),
                  pl.BlockSpec((1, D_MODEL), c2),
                  pl.BlockSpec((D_MODEL, D_MODEL), c2), pl.BlockSpec((D_MODEL, D_MODEL), c2)],
        out_specs=(pl.BlockSpec((1, D_MODEL, n_mem), lambda b: (b, 0, 0)),
                   pl.BlockSpec((1, n_mem, D_MODEL), lambda b: (b, 0, 0))),
        out_shape=(jax.ShapeDtypeStruct((batch, D_MODEL, n_mem), bf16),
                   jax.ShapeDtypeStruct((batch, n_mem, D_MODEL), bf16)),
        compiler_params=_cparams(1), name="memkv",
    )(mem, g_mem, wkT, wv)


def _post_kernel(x_ref, *refs):
    n_sub = x_ref.shape[0] // TQ
    oT_refs = refs[:n_sub]
    (conv_ref, gnsa_ref, wout_ref, gx_ref, wq_ref, kT_ref, v_ref, wo_ref, gmoe_ref, wr_ref, br_ref,
     h2_ref, xn2_ref, eidx_ref, gate_ref, sel_ref) = refs[n_sub:]
    stages = []
    for sub, oT_ref in enumerate(oT_refs):
        rows = slice(sub * TQ, (sub + 1) * TQ)
        stages.append(_post_rows(
            x_ref.at[rows], oT_ref, conv_ref.at[rows], gnsa_ref, wout_ref, gx_ref, wq_ref, kT_ref, v_ref,
            wo_ref, gmoe_ref, wr_ref, br_ref, h2_ref.at[rows],
            xn2_ref.at[sub * TQ * ROW_SUB:(sub + 1) * TQ * ROW_SUB], eidx_ref.at[rows], gate_ref.at[rows],
            sel_ref.at[rows]))
    _interleave(stages)


_DONE = object()


def _interleave(generators):
    live = list(generators)
    while live:
        live = [g for g in live if next(g, _DONE) is not _DONE]


def _post_rows(x_ref, oT_ref, conv_ref, gnsa_ref, wout_ref, gx_ref, wq_ref, kT_ref, v_ref, wo_ref,
               gmoe_ref, wr_ref, br_ref,
               h2_ref, xn2_ref, eidx_ref, gate_ref, sel_ref):
    tm = x_ref.shape[0]
    oT = oT_ref[...].astype(f32)
    onT = (oT * lax.rsqrt(jnp.mean(oT * oT, axis=0, keepdims=True) + RMS_EPS) * gnsa_ref[...]).astype(bf16)
    mix = lax.dot_general(onT, wout_ref[0:NSA_WIDTH, :], (((0,), (0,)), ((), ())),
                          preferred_element_type=f32)
    mix = mix + jnp.dot(conv_ref[...], wout_ref[NSA_WIDTH:D_MODEL, :], preferred_element_type=f32)
    yield
    h1 = x_ref[...] + mix

    hn = _rms(h1, gx_ref[...]).astype(bf16)
    q = (jnp.dot(hn, wq_ref[...], preferred_element_type=f32) * (XATTN_HEAD_DIM ** -0.5)).astype(bf16)
    yield
    head_slices = [slice(h * XATTN_HEAD_DIM, (h + 1) * XATTN_HEAD_DIM) for h in range(XATTN_HEADS)]
    scores = [jnp.dot(q[:, sl], kT_ref[0, sl, :], preferred_element_type=f32) for sl in head_slices]
    yield
    heads = []
    for s, sl in zip(scores, head_slices):
        e = jnp.exp(s - jnp.max(s, axis=-1, keepdims=True))
        p = e * (1.0 / jnp.sum(e, axis=-1, keepdims=True))
        heads.append(jnp.dot(p.astype(bf16), v_ref[0, :, sl], preferred_element_type=f32))
    yield
    o = jnp.concatenate(heads, axis=1).astype(bf16)
    h2 = h1 + jnp.dot(o, wo_ref[...], preferred_element_type=f32)
    yield
    h2_ref[...] = h2

    xn2 = _rms(h2, gmoe_ref[...])
    for s_ in range(ROW_SUB):
        xn2_ref[pl.ds(s_, tm, stride=ROW_SUB), :] = xn2[:, s_ * LANES:(s_ + 1) * LANES]

    logits = jnp.dot(xn2.astype(bf16), wr_ref[...], preferred_element_type=f32) + br_ref[...]
    yield
    lane = lax.broadcasted_iota(i32, (tm, LANES), 1)
    work = logits
    sel = jnp.zeros((tm, LANES), f32)
    eidx = jnp.zeros((tm, LANES), i32)
    vals = []
    for k in range(TOP_K):
        mk = jnp.max(work, axis=-1, keepdims=True)
        ik = jnp.min(jnp.where(work == mk, lane, LANES), axis=-1, keepdims=True)
        hit = lane == ik
        work = jnp.where(hit, -jnp.inf, work)
        sel = jnp.where(hit, 1.0, sel)
        eidx = jnp.where(lane == k, ik, eidx)
        vals.append(mk)
    es = [jnp.exp(v - vals[0]) for v in vals]
    den = es[0]
    for e in es[1:]:
        den = den + e
    gate = jnp.zeros((tm, LANES), f32)
    for k in range(TOP_K):
        gate = jnp.where(lane == k, es[k] / den, gate)
    eidx_ref[...] = eidx
    gate_ref[...] = gate
    sel_ref[...] = sel.astype(bf16)


def _post(x2, oT, conv_n, g_nsa_col, w_out, g_x, w_q, kT, v, w_o, g_moe, w_r, b_r, seq):
    T = x2.shape[0]
    tm = TM_POST
    n_sub = tm // TQ
    assert tm % TQ == 0 and seq % tm == 0
    tps = seq // tm
    nqt = seq // TQ
    n_mem = v.shape[1]
    row = lambda i: (i, 0)
    const = lambda i: (0, 0)

    def o_tile(sub):
        return lambda i: (0, (i // tps) * nqt + _nsa_tile_position(n_sub * (i % tps) + sub, nqt))

    in_specs = [
        pl.BlockSpec((tm, D_MODEL), row),
        *[pl.BlockSpec((NSA_WIDTH, TQ), o_tile(sub)) for sub in range(n_sub)],
        pl.BlockSpec((tm, CONV_WIDTH), row),
        pl.BlockSpec((NSA_WIDTH, 1), const),
        pl.BlockSpec((D_MODEL, D_MODEL), const),
        pl.BlockSpec((1, D_MODEL), const),
        pl.BlockSpec((D_MODEL, D_MODEL), const),
        pl.BlockSpec((1, D_MODEL, n_mem), lambda i: (i // tps, 0, 0)),
        pl.BlockSpec((1, n_mem, D_MODEL), lambda i: (i // tps, 0, 0)),
        pl.BlockSpec((D_MODEL, D_MODEL), const),
        pl.BlockSpec((1, D_MODEL), const),
        pl.BlockSpec((D_MODEL, LANES), const),
        pl.BlockSpec((1, LANES), const),
    ]
    out_shape = (jax.ShapeDtypeStruct((T, D_MODEL), f32),
                 jax.ShapeDtypeStruct((T * ROW_SUB, LANES), f32),
                 jax.ShapeDtypeStruct((T, LANES), i32),
                 jax.ShapeDtypeStruct((T, LANES), f32),
                 jax.ShapeDtypeStruct((T, LANES), bf16))
    out_specs = (pl.BlockSpec((tm, D_MODEL), row),
                 pl.BlockSpec((tm * ROW_SUB, LANES), row),
                 pl.BlockSpec((tm, LANES), row), pl.BlockSpec((tm, LANES), row),
                 pl.BlockSpec((tm, LANES), row))
    return pl.pallas_call(
        _post_kernel, grid=(T // tm,), in_specs=in_specs, out_specs=out_specs, out_shape=out_shape,
        compiler_params=_cparams(1), name="post",
    )(x2, *([oT] * n_sub), conv_n, g_nsa_col, w_out, g_x, w_q, kT, v, w_o, g_moe, w_r, b_r)


def _count_kernel(sel_ref, cum_ref, cnt_ref, carry):
    i = pl.program_id(0)
    tm = sel_ref.shape[0]

    @pl.when(i == 0)
    def _():
        carry[...] = jnp.zeros_like(carry)

    sel = sel_ref[...]
    r = lax.broadcasted_iota(i32, (tm, tm), 0)
    c = lax.broadcasted_iota(i32, (tm, tm), 1)
    strict_lower = jnp.where(c < r, 1.0, 0.0).astype(bf16)
    base = carry[0:1, :]
    cum_ref[...] = jnp.dot(strict_lower, sel, preferred_element_type=f32) + base
    total = base + jnp.sum(sel.astype(f32), axis=0, keepdims=True)
    carry[...] = jnp.broadcast_to(total, carry.shape)
    cnt_ref[...] = jnp.broadcast_to(total, cnt_ref.shape)


def _count(sel):
    T = sel.shape[0]
    tm = TM_ROUTE
    return pl.pallas_call(
        _count_kernel, grid=(T // tm,),
        in_specs=[pl.BlockSpec((tm, LANES), lambda i: (i, 0))],
        out_specs=(pl.BlockSpec((tm, LANES), lambda i: (i, 0)),
                   pl.BlockSpec((SUBLANES, LANES), lambda i: (0, 0))),
        out_shape=(jax.ShapeDtypeStruct((T, LANES), f32), jax.ShapeDtypeStruct((SUBLANES, LANES), f32)),
        scratch_shapes=[pltpu.VMEM((SUBLANES, LANES), f32)],
        compiler_params=_cparams(1), name="route_count",
    )(sel)


def _dest_kernel(cum_ref, eidx_ref, pstart_ref, dest_ref):
    tm = cum_ref.shape[0]
    lane = lax.broadcasted_iota(i32, (tm, LANES), 1)
    row_of = cum_ref[...] + pstart_ref[0:1, :]
    eidx = eidx_ref[...]
    dest = jnp.zeros((tm, LANES), f32)
    for k in range(TOP_K):
        ek = eidx[:, k:k + 1]
        dk = jnp.sum(jnp.where(lane == ek, row_of, 0.0), axis=-1, keepdims=True)
        dest = jnp.where(lane == k, dk, dest)
    dest_ref[...] = dest.astype(i32)


def _dest(cum, eidx, pstart8):
    T = cum.shape[0]
    tm = TM_ROUTE
    row = lambda i: (i, 0)
    return pl.pallas_call(
        _dest_kernel, grid=(T // tm,),
        in_specs=[pl.BlockSpec((tm, LANES), row), pl.BlockSpec((tm, LANES), row),
                  pl.BlockSpec((SUBLANES, LANES), lambda i: (0, 0))],
        out_specs=pl.BlockSpec((tm, LANES), row),
        out_shape=jax.ShapeDtypeStruct((T, LANES), i32),
        compiler_params=_cparams(1), name="route_dest",
    )(cum, eidx, pstart8)


def _invert_kernel(dest_ref, init_ref, slot_ref, sem):
    cp = pltpu.make_async_copy(init_ref, slot_ref, sem)
    cp.start()
    cp.wait()
    chunk = LANES

    def body(j, carry):
        base = j * chunk
        for l in range(chunk):
            slot_ref[dest_ref[base + l]] = base + l
        return carry

    lax.fori_loop(0, dest_ref.shape[0] // chunk, body, 0)


def _invert(dest_flat, n_steps):
    parity = jnp.concatenate([jnp.arange(n_steps, dtype=i32) % 2, jnp.ones((1,), i32)])
    sink = dest_flat.shape[0] + parity[:, None] * BM + jnp.arange(BM, dtype=i32)[None, :]
    return pl.pallas_call(
        _invert_kernel,
        in_specs=[pl.BlockSpec(memory_space=pltpu.SMEM), pl.BlockSpec(memory_space=pltpu.VMEM)],
        out_specs=pl.BlockSpec(memory_space=pltpu.SMEM),
        out_shape=jax.ShapeDtypeStruct(((n_steps + 1) * BM,), i32),
        scratch_shapes=[pltpu.SemaphoreType.DMA],
        compiler_params=pltpu.CompilerParams(vmem_limit_bytes=VMEM_LIMIT), name="route_invert",
    )(dest_flat, sink.reshape(-1))


def _ffn_kernel(blk_e_ref, first_ref, nxt_e_ref, wslot_ref, nused_ref, slot_ref,
                x_hbm, wgu_hbm, bgu_ref, wd_hbm, bd_ref, y_hbm,
                xbuf0, xbuf1, obuf0, obuf1, wgu_f, wd_f, wgu_bf, wd_bf, gsem, ssem, wsem, *, n_tok, sink_row):
    i = pl.program_id(0)
    nused = nused_ref[0]
    n_steps = pl.num_programs(0)
    xbuf = (xbuf0, xbuf1)
    obuf = (obuf0, obuf1)

    def row_window(ref, row):
        return ref.at[pl.ds(pl.multiple_of(row * ROW_SUB, SUBLANES), ROW_SUB)]

    def gather(blk, par, r):
        tok = slot_ref[blk * BM + r] & (n_tok - 1)
        return pltpu.make_async_copy(row_window(x_hbm, tok), xbuf[par].at[pl.ds(r * ROW_SUB, ROW_SUB)],
                                     gsem.at[par])

    def scatter(blk, par, r):
        return pltpu.make_async_copy(obuf[par].at[pl.ds(r * ROW_SUB, ROW_SUB)],
                                     row_window(y_hbm, slot_ref[blk * BM + r]), ssem.at[par])

    def wait_gather(par):
        pltpu.make_async_copy(x_hbm.at[pl.ds(0, BM * ROW_SUB)], xbuf[par], gsem.at[par]).wait()

    def wait_scatter(par):
        pltpu.make_async_copy(obuf[par], y_hbm.at[pl.ds(0, BM * ROW_SUB)], ssem.at[par]).wait()

    def weight_copies(e, ws):
        return (pltpu.make_async_copy(wgu_hbm.at[e], wgu_f.at[ws], wsem.at[ws, 0]),
                pltpu.make_async_copy(wd_hbm.at[e], wd_f.at[ws], wsem.at[ws, 1]))

    @pl.when(i == 0)
    def _():
        obuf0[...] = jnp.zeros(obuf0.shape, f32)
        obuf1[...] = jnp.zeros(obuf1.shape, f32)
        pltpu.make_async_copy(obuf0, y_hbm.at[pl.ds(sink_row * ROW_SUB, BM * ROW_SUB)], ssem.at[0]).start()
        for cp in weight_copies(blk_e_ref[0], 0):
            cp.start(priority=1)
        for r in range(BM):
            gather(0, 0, r).start()

    @pl.when((i < nused) & (first_ref[i] == 1))
    def _():
        ws = wslot_ref[i]
        for cp in weight_copies(blk_e_ref[i], ws):
            cp.wait()

        @pl.when(nxt_e_ref[i] >= 0)
        def _():
            for cp in weight_copies(nxt_e_ref[i], 1 - ws):
                cp.start(priority=1)

        wgu_bf[...] = wgu_f[ws].astype(bf16)
        wd_bf[...] = wd_f[ws].astype(bf16)

    def block(par, issues_gathers):
        prev = jnp.where(i == 0, n_steps, i - 1)
        wait_gather(par)
        x = jnp.concatenate([xbuf[par][pl.ds(s_, BM, stride=ROW_SUB), :] for s_ in range(ROW_SUB)],
                            axis=1).astype(bf16)
        if issues_gathers:
            for r in range(BM):
                gather(i + 1, 1 - par, r).start()
        for r in range(BM):
            scatter(prev, 1 - par, r).start(priority=1)
        if issues_gathers:
            for r in range(BM):
                gather(i + 2, par, r).start()
        gu = jnp.dot(x, wgu_bf[...], preferred_element_type=f32) + bgu_ref[0]
        gg = jnp.minimum(gu[:, 0:D_FF], SWIGLU_LIMIT)
        uu = jnp.clip(gu[:, D_FF:2 * D_FF], -SWIGLU_LIMIT, SWIGLU_LIMIT)
        hmid = (uu + 1.0) * (gg * jax.nn.sigmoid(SWIGLU_ALPHA * gg))
        out = jnp.dot(hmid.astype(bf16), wd_bf[...], preferred_element_type=f32) + bd_ref[0]
        wait_scatter(par)
        for s_ in range(ROW_SUB):
            obuf[par][pl.ds(s_, BM, stride=ROW_SUB), :] = out[:, s_ * LANES:(s_ + 1) * LANES]

    for par in range(2):
        @pl.when((i < nused) & (i % 2 == par))
        def _():
            block(par, issues_gathers=(par == 0))

        @pl.when((i == nused) & (i % 2 == par))
        def _():
            for r in range(BM):
                scatter(i - 1, 1 - par, r).start(priority=1)
            wait_gather(0)
            if par == 1:
                wait_gather(1)
            wait_scatter(par)
            wait_scatter(1 - par)


def _ffn(blk_e, first, nxt_e, wslot, nused, slots, xn2_rows, w_gu, b_gu, w_d, b_d):
    n_steps = blk_e.shape[0]
    n_tok = xn2_rows.shape[0] // ROW_SUB
    assert n_tok & (n_tok - 1) == 0
    n_tok_rows = n_tok * TOP_K
    emap = lambda i, be, *_: (be[i], 0, 0)
    grid_spec = pltpu.PrefetchScalarGridSpec(
        num_scalar_prefetch=6, grid=(n_steps,),
        in_specs=[pl.BlockSpec(memory_space=pl.ANY),
                  pl.BlockSpec(memory_space=pl.ANY),
                  pl.BlockSpec((1, 1, 2 * D_FF), emap),
                  pl.BlockSpec(memory_space=pl.ANY),
                  pl.BlockSpec((1, 1, D_MODEL), emap)],
        out_specs=pl.BlockSpec(memory_space=pl.ANY),
        scratch_shapes=[pltpu.VMEM((BM * ROW_SUB, LANES), f32), pltpu.VMEM((BM * ROW_SUB, LANES), f32),
                        pltpu.VMEM((BM * ROW_SUB, LANES), f32), pltpu.VMEM((BM * ROW_SUB, LANES), f32),
                        pltpu.VMEM((2, D_MODEL, 2 * D_FF), f32), pltpu.VMEM((2, D_FF, D_MODEL), f32),
                        pltpu.VMEM((D_MODEL, 2 * D_FF), bf16), pltpu.VMEM((D_FF, D_MODEL), bf16),
                        pltpu.SemaphoreType.DMA((2,)), pltpu.SemaphoreType.DMA((2,)),
                        pltpu.SemaphoreType.DMA((2, 2))],
    )
    return pl.pallas_call(
        functools.partial(_ffn_kernel, n_tok=n_tok, sink_row=n_tok_rows), grid_spec=grid_spec,
        out_shape=jax.ShapeDtypeStruct(((n_tok_rows + 2 * BM) * ROW_SUB, LANES), f32),
        compiler_params=_cparams(1), name="ffn",
    )(blk_e, first, nxt_e, wslot, nused, slots, xn2_rows, w_gu, b_gu, w_d, b_d)


def _combine_kernel(*refs, final_norm):
    y_refs = refs[:TOP_K]
    gate_ref, h2_ref, gfin_ref, o_ref = refs[TOP_K:]
    tm = h2_ref.shape[0]
    gate = gate_ref[...]
    cols = []
    for s_ in range(ROW_SUB):
        acc = gate[:, 0:1] * y_refs[0][pl.ds(s_, tm, stride=ROW_SUB), :]
        for k in range(1, TOP_K):
            acc = acc + gate[:, k:k + 1] * y_refs[k][pl.ds(s_, tm, stride=ROW_SUB), :]
        cols.append(acc)
    h = h2_ref[...] + jnp.concatenate(cols, axis=1)
    if final_norm:
        h = _rms(h, gfin_ref[...])
    o_ref[...] = h


def _combine(y_rows, gate, h2, g_final, final_norm):
    T = h2.shape[0]
    tm = TM_ROW
    row = lambda i: (i, 0)
    planes = [pl.BlockSpec((tm * ROW_SUB, LANES), functools.partial(lambda i, k: (k * (T // tm) + i, 0), k=k))
              for k in range(TOP_K)]
    return pl.pallas_call(
        functools.partial(_combine_kernel, final_norm=final_norm), grid=(T // tm,),
        in_specs=planes + [pl.BlockSpec((tm, LANES), row), pl.BlockSpec((tm, D_MODEL), row),
                           pl.BlockSpec((1, D_MODEL), lambda i: (0, 0))],
        out_specs=pl.BlockSpec((tm, D_MODEL), row),
        out_shape=jax.ShapeDtypeStruct((T, D_MODEL), f32),
        compiler_params=_cparams(1), name="combine",
    )(*([y_rows] * TOP_K), gate, h2, g_final)


def _prep_inproj_weights(w_in):
    sizes = (NSA_WIDTH,) + (KV_WIDTH,) * 6 + (3 * NSA_HEADS,) + (CONV_WIDTH,) * 3
    offs = [0]
    for s in sizes:
        offs.append(offs[-1] + s)
    seg = lambda n: w_in[:, offs[n]:offs[n + 1]]
    q, kc, vc, ks, vs, kw, vw, gl, ch, cb, cc = (seg(n) for n in range(11))
    w_row = jnp.concatenate([kc, vc, ks, kw, ch, cb, cc], axis=1).astype(bf16)
    gl_g = gl.reshape(D_MODEL, NSA_KV_HEADS, HPG * 3)
    gl_g = jnp.pad(gl_g, ((0, 0), (0, 0), (0, 16 - HPG * 3))).reshape(D_MODEL, NSA_KV_HEADS * 16)
    w_t = jnp.concatenate([q, vs, vw, gl_g], axis=1).T.astype(bf16)
    return w_row, w_t


def _rope_tables(positions):
    half = HEAD_DIM // 2
    inv_freq = ROPE_THETA ** (-jnp.arange(half, dtype=f32) / half)
    ang = positions.reshape(-1).astype(f32)[:, None] * inv_freq
    cos = jnp.cos(ang)
    sin = jnp.sin(ang)
    reps = KV_WIDTH // HEAD_DIM
    cosr = jnp.tile(cos, (1, 2 * reps))
    sinr = jnp.tile(jnp.concatenate([-sin, sin], axis=1), (1, reps))
    return cosr, sinr, cos.T, sin.T


def _cmp_weights(pos, w1, b1):
    w1r = w1.astype(bf16).reshape(2, CMP_STRIDE, HEAD_DIM, CMP_HIDDEN)
    zero = jnp.zeros_like(w1r)
    big = jnp.stack([jnp.concatenate([w1r if g == k else zero for k in range(NSA_KV_HEADS)], axis=-1)
                     for g in range(NSA_KV_HEADS)], axis=2)
    big = big.reshape(2, CMP_STRIDE * KV_WIDTH, NSA_KV_HEADS * CMP_HIDDEN)
    p = jnp.broadcast_to(pos.reshape(2, CMP_STRIDE, 1, HEAD_DIM), (2, CMP_STRIDE, NSA_KV_HEADS, HEAD_DIM))
    p = jnp.pad(p.reshape(2, CMP_STRIDE * KV_WIDTH), ((0, SUBLANES - 2), (0, 0)))
    return big, p, jnp.tile(b1.reshape(1, CMP_HIDDEN), (1, NSA_KV_HEADS))


def _mixer_core(x2, tables, p, batch, seq):
    cosr, sinr, cost, sint = tables
    row1 = lambda v: v.reshape(1, -1)
    w_row, w_t = _prep_inproj_weights(p['w_mix_in'])
    conv_w8 = jnp.pad(p['conv_w'], ((0, SUBLANES - CONV_K), (0, 0)))
    qT, kc, vc, ks, kw, vsT, vwT, gT, conv_n = _inproj(
        x2, row1(p['g_mix_norm']), w_row, w_t, cosr, sinr, cost, sint, conv_w8, row1(p['g_conv_out']), seq)

    w1k, pk, b1k = _cmp_weights(p['cmp_pos_k'], p['cmp_w1_k'], p['cmp_b1_k'])
    w1v, pv, b1v = _cmp_weights(p['cmp_pos_v'], p['cmp_w1_v'], p['cmp_b1_v'])
    kcc, vcT = _compress(kc, vc, pk, pv, w1k, w1v, b1k, b1v,
                         p['cmp_w2_k'].astype(bf16), p['cmp_w2_v'].T.astype(bf16), batch, seq)

    ncp = seq // CMP_STRIDE
    ns = seq // SEL_LEN
    cs = jnp.arange(ncp) * CMP_STRIDE
    js = jnp.arange(ns) * SEL_LEN
    overlap = jnp.clip(jnp.minimum(cs[:, None] + CMP_LEN, js[None, :] + SEL_LEN)
                       - jnp.maximum(cs[:, None], js[None, :]), 0, None).astype(f32) / CMP_LEN
    ovT = overlap.T.astype(bf16)
    oT = _nsa(qT, ks, kw, vsT, vwT, kcc, vcT, gT, ovT, batch, seq)
    return oT, conv_n


def _layer(h, memf, tables, p, final_gain, final_norm):
    batch, seq, _ = h.shape
    T = batch * seq
    x2 = h.reshape(T, D_MODEL)
    row1 = lambda v: v.reshape(1, -1)
    oT, conv_n = _mixer_core(x2, tables, p, batch, seq)

    w_xkv = p['w_xkv']
    kT, v = _memkv(memf, row1(p['g_mem_norm']), w_xkv[:, :D_MODEL].T.astype(bf16),
                   w_xkv[:, D_MODEL:].astype(bf16))
    w_r = jnp.pad(p['w_router'], ((0, 0), (0, LANES - N_EXPERTS))).astype(bf16)
    b_r = jnp.pad(p['b_router'], (0, LANES - N_EXPERTS), constant_values=NEG_INF).reshape(1, LANES)
    h2, xn2_rows, eidx, gate, sel = _post(
        x2, oT, conv_n, p['g_nsa_out'].reshape(NSA_WIDTH, 1), p['w_mix_out'].astype(bf16),
        row1(p['g_xattn_norm']), p['w_xq'].astype(bf16), kT, v, p['w_xo'].astype(bf16),
        row1(p['g_moe_norm']), w_r, b_r, seq)

    cum, cnt = _count(sel)
    counts = cnt[0, :N_EXPERTS].astype(i32)
    padded = (counts + BM - 1) // BM * BM
    pend = jnp.cumsum(padded)
    pstart = pend - padded
    n_steps = (T * TOP_K) // BM + N_EXPERTS + 1
    nused = (pend[-1] // BM).astype(i32)
    step = jnp.arange(n_steps, dtype=i32)
    used = step < nused
    blk_raw = jnp.minimum(jnp.sum((pend[None, :] <= (step * BM)[:, None]).astype(i32), axis=1), N_EXPERTS - 1)
    e_ids = jnp.arange(N_EXPERTS, dtype=i32)
    last_e = jnp.max(jnp.where(padded > 0, e_ids, 0))
    blk_e = jnp.where(used, blk_raw, last_e)
    first = (used & jnp.concatenate([jnp.ones((1,), bool), blk_e[1:] != blk_e[:-1]])).astype(i32)
    wslot = (jnp.cumsum(first) - 1) % 2
    later = (e_ids[None, :] > e_ids[:, None]) & (padded > 0)[None, :]
    nxt_of = jnp.min(jnp.where(later, e_ids[None, :], N_EXPERTS), axis=1)
    nxt_of = jnp.where(nxt_of < N_EXPERTS, nxt_of, -1).astype(i32)
    nxt_e = jnp.sum(jnp.where(blk_e[:, None] == e_ids[None, :], nxt_of[None, :], 0), axis=1)
    pstart8 = jnp.broadcast_to(jnp.pad(pstart.astype(f32), (0, LANES - N_EXPERTS))[None, :], (SUBLANES, LANES))

    dest = _dest(cum, eidx, pstart8)
    slots = _invert(dest[:, :TOP_K].T.reshape(-1), n_steps)
    y_rows = _ffn(blk_e, first, nxt_e, wslot.astype(i32), nused.reshape(1), slots, xn2_rows,
                  p['w_gate_up'], p['b_gate_up'].reshape(N_EXPERTS, 1, 2 * D_FF), p['w_down'],
                  p['b_down'].reshape(N_EXPERTS, 1, D_MODEL))
    out = _combine(y_rows, gate, h2, row1(final_gain), final_norm)
    return out.reshape(batch, seq, D_MODEL)


_LAYER_PARAMS = ('g_mix_norm', 'w_mix_in', 'cmp_pos_k', 'cmp_pos_v', 'cmp_w1_k', 'cmp_b1_k', 'cmp_w2_k',
                 'cmp_w1_v', 'cmp_b1_v', 'cmp_w2_v', 'conv_w', 'g_nsa_out', 'g_conv_out', 'w_mix_out',
                 'g_xattn_norm', 'g_mem_norm', 'w_xq', 'w_xkv', 'w_xo', 'g_moe_norm', 'w_router', 'b_router',
                 'w_gate_up', 'b_gate_up', 'w_down', 'b_down')


def kernel(x, mem, positions, g_mix_norm, w_mix_in, cmp_pos_k, cmp_pos_v, cmp_w1_k, cmp_b1_k, cmp_w2_k, cmp_w1_v, cmp_b1_v, cmp_w2_v, conv_w, g_nsa_out, g_conv_out, w_mix_out, g_xattn_norm, g_mem_norm, w_xq, w_xkv, w_xo, g_moe_norm, w_router, b_router, w_gate_up, b_gate_up, w_down, b_down, g_final):
    stacked = dict(zip(_LAYER_PARAMS, (g_mix_norm, w_mix_in, cmp_pos_k, cmp_pos_v, cmp_w1_k, cmp_b1_k, cmp_w2_k,
                                       cmp_w1_v, cmp_b1_v, cmp_w2_v, conv_w, g_nsa_out, g_conv_out, w_mix_out,
                                       g_xattn_norm, g_mem_norm, w_xq, w_xkv, w_xo, g_moe_norm, w_router,
                                       b_router, w_gate_up, b_gate_up, w_down, b_down)))
    depth = g_mix_norm.shape[0]
    tables = _rope_tables(positions)
    h = x
    for l in range(depth):
        p = {k: v[l] for k, v in stacked.items()}
        last = l == depth - 1
        h = _layer(h, mem, tables, p, g_final, final_norm=last)
    return h
```

```python
import functools

import jax
import jax.numpy as jnp
from jax import lax
from jax.experimental import pallas as pl
from jax.experimental.pallas import tpu as pltpu

f32 = jnp.float32
bf16 = jnp.bfloat16
i32 = jnp.int32

D_MODEL = 1024
HEAD_DIM = 64
NSA_HEADS = 8
NSA_KV_HEADS = 2
HPG = NSA_HEADS // NSA_KV_HEADS
NSA_WIDTH = NSA_HEADS * HEAD_DIM
KV_WIDTH = NSA_KV_HEADS * HEAD_DIM
CONV_WIDTH = D_MODEL - NSA_WIDTH
CONV_K = 3
CMP_LEN = 32
CMP_STRIDE = 16
CMP_HIDDEN = 256
SEL_LEN = 64
N_SEL = 16
WINDOW = 512
ROPE_THETA = 10000.0
XATTN_HEADS = 4
XATTN_HEAD_DIM = D_MODEL // XATTN_HEADS
N_EXPERTS = 32
TOP_K = 4
D_FF = D_MODEL
SWIGLU_LIMIT = 7.0
SWIGLU_ALPHA = 1.702
RMS_EPS = 1e-5
NEG_INF = -1e30
FORCED = 1e30

LANES = 128
SUBLANES = 8
VMEM_LIMIT = 56 * 1024 * 1024

TM_IN = 512
TQ = 256
TK = 256
TM_POST = 512
TM_ROUTE = 512
TM_ROW = 512
V_EXT = HEAD_DIM + 16
BM = 256
ROW_SUB =D_MODEL // LANES


def _cparams(n_axes, **kw):
    return pltpu.CompilerParams(dimension_semantics=("arbitrary",) * n_axes,
                                vmem_limit_bytes=VMEM_LIMIT, **kw)


def _rms(t, gain):
    return t * lax.rsqrt(jnp.mean(t * t, axis=-1, keepdims=True) + RMS_EPS) * gain


def _inproj_kernel(x_ref, g_ref, wr_ref, wt_ref, cosr_ref, sinr_ref, cost_ref, sint_ref,
                   convw_ref, gconv_ref,
                   qT_ref, kc_ref, vc_ref, ks_ref, kw_ref, vsT_ref, vwT_ref, gT_ref, conv_ref,
                   ubuf, *, tiles_per_seq):
    i = pl.program_id(0)
    tm = x_ref.shape[0]

    @pl.when(i % tiles_per_seq == 0)
    def _():
        ubuf[0:SUBLANES, :] = jnp.zeros((SUBLANES, CONV_WIDTH), f32)

    @pl.when(i % tiles_per_seq != 0)
    def _():
        ubuf[0:SUBLANES, :] = ubuf[tm:tm + SUBLANES, :]

    stages = []
    for sub in range(tm // TQ):
        rows = slice(sub * TQ, (sub + 1) * TQ)
        stages.append(_inproj_rows(
            (i % tiles_per_seq) * tm + sub * TQ, SUBLANES + sub * TQ,
            x_ref.at[rows], g_ref, wr_ref, wt_ref, cosr_ref.at[rows], sinr_ref.at[rows],
            cost_ref.at[:, rows], sint_ref.at[:, rows], convw_ref, gconv_ref,
            qT_ref.at[:, rows], kc_ref.at[rows], vc_ref.at[rows], ks_ref.at[rows], kw_ref.at[rows],
            vsT_ref.at[:, rows], vwT_ref.at[:, rows], gT_ref.at[:, rows], conv_ref.at[rows], ubuf))
    _interleave(stages)


def _inproj_rows(tok0, u0, x_ref, g_ref, wr_ref, wt_ref, cosr_ref, sinr_ref, cost_ref, sint_ref,
                 convw_ref, gconv_ref,
                 qT_ref, kc_ref, vc_ref, ks_ref, kw_ref, vsT_ref, vwT_ref, gT_ref, conv_ref, ubuf):
    tm = x_ref.shape[0]
    xb = _rms(x_ref[...], g_ref[...]).astype(bf16)
    yield

    c0 = 4 * KV_WIDTH
    pr = jnp.dot(xb, wr_ref[:, 0:c0], preferred_element_type=f32)
    pc = jnp.dot(xb, wr_ref[:, c0:c0 + 3 * CONV_WIDTH], preferred_element_type=f32)
    pt = lax.dot_general(wt_ref[...], xb, (((1,), (1,)), ((), ())), preferred_element_type=f32)
    yield
    cosr = cosr_ref[...]
    sinr = sinr_ref[...]
    lane = lax.broadcasted_iota(i32, (tm, KV_WIDTH), 1)
    first_half = (lane & (HEAD_DIM - 1)) < HEAD_DIM // 2

    def rope_rows(t):
        rot = jnp.where(first_half, pltpu.roll(t, KV_WIDTH - HEAD_DIM // 2, 1),
                        pltpu.roll(t, HEAD_DIM // 2, 1))
        return t * cosr + rot * sinr

    kc_ref[...] = rope_rows(pr[:, 0:KV_WIDTH])
    vc_ref[...] = pr[:, KV_WIDTH:2 * KV_WIDTH]
    ks_ref[:, 0:KV_WIDTH] = rope_rows(pr[:, 2 * KV_WIDTH:3 * KV_WIDTH]).astype(bf16)
    tok = tok0 + lax.broadcasted_iota(i32, (tm, KV_WIDTH), 0)
    ks_ref[:, KV_WIDTH:2 * KV_WIDTH] = jnp.where(lane == tok // SEL_LEN, 1.0, 0.0).astype(bf16)
    kw_ref[...] = rope_rows(pr[:, 3 * KV_WIDTH:4 * KV_WIDTH]).astype(bf16)

    ch = pc[:, 0:CONV_WIDTH]
    cb = pc[:, CONV_WIDTH:2 * CONV_WIDTH]
    cc = pc[:, 2 * CONV_WIDTH:3 * CONV_WIDTH]
    u = cc * ch
    ubuf[u0:u0 + tm, :] = u
    u1 = ubuf[u0 - 1:u0 - 1 + tm, :]
    u2 = ubuf[u0 - 2:u0 - 2 + tm, :]
    w = convw_ref[...]
    y = cb * (w[0:1, :] * u2 + w[1:2, :] * u1 + w[2:3, :] * u)
    conv_ref[...] = _rms(y, gconv_ref[...]).astype(bf16)
    yield

    cost = cost_ref[...]
    sint = sint_ref[...]
    half = HEAD_DIM // 2
    scale = HEAD_DIM ** -0.5
    for h in range(NSA_HEADS):
        t1 = pt[h * HEAD_DIM:h * HEAD_DIM + half, :]
        t2 = pt[h * HEAD_DIM + half:(h + 1) * HEAD_DIM, :]
        qT_ref[h * HEAD_DIM:h * HEAD_DIM + half, :] = ((t1 * cost - t2 * sint) * scale).astype(bf16)
        qT_ref[h * HEAD_DIM + half:(h + 1) * HEAD_DIM, :] = ((t2 * cost + t1 * sint) * scale).astype(bf16)
    r0 = NSA_WIDTH
    ones_rows = jnp.where(lax.broadcasted_iota(i32, (V_EXT - HEAD_DIM, tm), 0) == 0, 1.0, 0.0).astype(bf16)
    for vT_ref, base in ((vsT_ref, r0), (vwT_ref, r0 + KV_WIDTH)):
        for g in range(NSA_KV_HEADS):
            vT_ref[g * V_EXT:g * V_EXT + HEAD_DIM, :] = pt[base + g * HEAD_DIM:base + (g + 1) * HEAD_DIM, :].astype(bf16)
            vT_ref[g * V_EXT + HEAD_DIM:(g + 1) * V_EXT, :] = ones_rows
    gT_ref[...] = jax.nn.sigmoid(pt[r0 + 2 * KV_WIDTH:r0 + 2 * KV_WIDTH + 32, :])


def _inproj(x2, g_mix, w_row, w_t, cosr, sinr, cost, sint, conv_w8, g_conv, seq):
    T = x2.shape[0]
    tm = TM_IN
    n_row = w_row.shape[1]
    n_t = w_t.shape[0]
    row = lambda i: (i, 0)
    col = lambda i: (0, i)
    const = lambda i: (0, 0)
    out_shape = (
        jax.ShapeDtypeStruct((NSA_WIDTH, T), bf16),
        jax.ShapeDtypeStruct((T, KV_WIDTH), f32),
        jax.ShapeDtypeStruct((T, KV_WIDTH), f32),
        jax.ShapeDtypeStruct((T, 2 * KV_WIDTH), bf16),
        jax.ShapeDtypeStruct((T, KV_WIDTH), bf16),
        jax.ShapeDtypeStruct((NSA_KV_HEADS * V_EXT, T), bf16),
        jax.ShapeDtypeStruct((NSA_KV_HEADS * V_EXT, T), bf16),
        jax.ShapeDtypeStruct((32, T), f32),
        jax.ShapeDtypeStruct((T, CONV_WIDTH), bf16),
    )
    out_specs = (
        pl.BlockSpec((NSA_WIDTH, tm), col),
        pl.BlockSpec((tm, KV_WIDTH), row), pl.BlockSpec((tm, KV_WIDTH), row),
        pl.BlockSpec((tm, 2 * KV_WIDTH), row), pl.BlockSpec((tm, KV_WIDTH), row),
        pl.BlockSpec((NSA_KV_HEADS * V_EXT, tm), col), pl.BlockSpec((NSA_KV_HEADS * V_EXT, tm), col),
        pl.BlockSpec((32, tm), col),
        pl.BlockSpec((tm, CONV_WIDTH), row),
    )
    in_specs = [
        pl.BlockSpec((tm, D_MODEL), row),
        pl.BlockSpec((1, D_MODEL), const),
        pl.BlockSpec((D_MODEL, n_row), const),
        pl.BlockSpec((n_t, D_MODEL), const),
        pl.BlockSpec((tm, KV_WIDTH), row), pl.BlockSpec((tm, KV_WIDTH), row),
        pl.BlockSpec((HEAD_DIM // 2, tm), col), pl.BlockSpec((HEAD_DIM // 2, tm), col),
        pl.BlockSpec((SUBLANES, CONV_WIDTH), const),
        pl.BlockSpec((1, CONV_WIDTH), const),
    ]
    return pl.pallas_call(
        functools.partial(_inproj_kernel, tiles_per_seq=seq // tm),
        grid=(T // tm,), in_specs=in_specs, out_specs=out_specs, out_shape=out_shape,
        scratch_shapes=[pltpu.VMEM((tm + 2 * SUBLANES, CONV_WIDTH), f32)],
        compiler_params=_cparams(1), name="inproj",
    )(x2, g_mix, w_row, w_t, cosr, sinr, cost, sint, conv_w8, g_conv)


def _compress_kernel(xk_ref, xv_ref, pk_ref, pv_ref, w1k_ref, w1v_ref, b1k_ref, b1v_ref,
                     w2k_ref, w2vT_ref, kcc_ref, vcT_ref):
    ncp = xk_ref.shape[0] // CMP_STRIDE

    def hidden(x_ref, p_ref, w1_ref, b1_ref):
        x = jnp.concatenate([x_ref[pl.ds(l, ncp, stride=CMP_STRIDE), :] for l in range(CMP_STRIDE)], axis=1)
        lo = (x + p_ref[0:1, :]).astype(bf16)
        hi = (x + p_ref[1:2, :]).astype(bf16)
        a = jnp.dot(lo, w1_ref[0], preferred_element_type=f32)
        b = jnp.dot(hi, w1_ref[1], preferred_element_type=f32)
        pre = a + pltpu.roll(b, ncp - 1, 0) + b1_ref[...]
        return jax.nn.gelu(pre).astype(bf16)

    hk = hidden(xk_ref, pk_ref, w1k_ref, b1k_ref)
    hv = hidden(xv_ref, pv_ref, w1v_ref, b1v_ref)
    for g in range(NSA_KV_HEADS):
        sl = slice(g * CMP_HIDDEN, (g + 1) * CMP_HIDDEN)
        kcc_ref[0, g] = jnp.dot(hk[:, sl], w2k_ref[...], preferred_element_type=f32).astype(bf16)
        vcT_ref[0, g] = lax.dot_general(w2vT_ref[...], hv[:, sl], (((1,), (1,)), ((), ())),
                                        preferred_element_type=f32).astype(bf16)


def _compress(kc_rows, vc_rows, pk, pv, w1k, w1v, b1k, b1v, w2k, w2vT, batch, seq):
    ncp = seq // CMP_STRIDE
    wide = CMP_STRIDE * KV_WIDTH
    c2 = lambda b: (0, 0)
    c3 = lambda b: (0, 0, 0)
    in_specs = [
        pl.BlockSpec((seq, KV_WIDTH), lambda b: (b, 0)),
        pl.BlockSpec((seq, KV_WIDTH), lambda b: (b, 0)),
        pl.BlockSpec((SUBLANES, wide), c2), pl.BlockSpec((SUBLANES, wide), c2),
        pl.BlockSpec((2, wide, 2 * CMP_HIDDEN), c3), pl.BlockSpec((2, wide, 2 * CMP_HIDDEN), c3),
        pl.BlockSpec((1, 2 * CMP_HIDDEN), c2), pl.BlockSpec((1, 2 * CMP_HIDDEN), c2),
        pl.BlockSpec((CMP_HIDDEN, HEAD_DIM), c2), pl.BlockSpec((HEAD_DIM, CMP_HIDDEN), c2),
    ]
    out_shape = (jax.ShapeDtypeStruct((batch, NSA_KV_HEADS, ncp, HEAD_DIM), bf16),
                 jax.ShapeDtypeStruct((batch, NSA_KV_HEADS, HEAD_DIM, ncp), bf16))
    out_specs = (pl.BlockSpec((1, NSA_KV_HEADS, ncp, HEAD_DIM), lambda b: (b, 0, 0, 0)),
                 pl.BlockSpec((1, NSA_KV_HEADS, HEAD_DIM, ncp), lambda b: (b, 0, 0, 0)))
    return pl.pallas_call(
        _compress_kernel, grid=(batch,), in_specs=in_specs, out_specs=out_specs,
        out_shape=out_shape, compiler_params=_cparams(1), name="compress",
    )(kc_rows, vc_rows, pk, pv, w1k, w1v, b1k, b1v, w2k, w2vT)


_NQ = HPG * TQ
_COL_BLOCKS = [slice(c * LANES, (c + 1) * LANES) for c in range(_NQ // LANES)]


def _compressed_branch(q4, kcc, vcT, ov, s0):
    ncp = kcc.shape[0]
    ns = ov.shape[0]
    s_lane = s0 + (lax.broadcasted_iota(i32, (1, _NQ), 1) & (TQ - 1))
    sc = jnp.dot(kcc, q4, preferred_element_type=f32)
    yield
    c_end = lax.broadcasted_iota(i32, (ncp, 1), 0) * CMP_STRIDE + (CMP_LEN - 1)
    blocks = []
    for cs in _COL_BLOCKS:
        cmask = c_end <= s_lane[:, cs]
        scm = jnp.where(cmask, sc[:, cs], NEG_INF)
        e_c = jnp.where(cmask, jnp.exp(scm - jnp.max(scm, axis=0, keepdims=True)), 0.0)
        l_c = jnp.sum(e_c, axis=0, keepdims=True)
        blocks.append(e_c * jnp.where(l_c > 0.0, 1.0 / l_c, 0.0))
    p_c = jnp.concatenate(blocks, axis=1)
    o_cmp = jnp.dot(vcT, p_c.astype(bf16), preferred_element_type=f32)
    yield

    j_blk = lax.broadcasted_iota(i32, (ns, 1), 0)
    cur = (s0 + lax.broadcasted_iota(i32, (1, TQ), 1)) // SEL_LEN
    n_live = (s0 + TQ - 1) // SEL_LEN + 1
    if n_live <= N_SEL:
        return o_cmp, jnp.where(j_blk <= cur, 0.0, NEG_INF).astype(bf16)

    ps = p_c[:, 0:TQ]
    for h in range(1, HPG):
        ps = ps + p_c[:, h * TQ:(h + 1) * TQ]
    p_hi = ps.astype(bf16)
    r1 = ps - p_hi.astype(f32)
    p_mid = r1.astype(bf16)
    p_lo = (r1 - p_mid.astype(f32)).astype(bf16)
    imp = (jnp.dot(ov, p_hi, preferred_element_type=f32) + jnp.dot(ov, p_mid, preferred_element_type=f32)
           + jnp.dot(ov, p_lo, preferred_element_type=f32))
    yield
    forced = (j_blk == 0) | (j_blk == cur) | (j_blk == cur - 1)
    imp = jnp.where(forced, FORCED, jnp.where(j_blk > cur, NEG_INF, imp))
    rank = jnp.zeros((ns, TQ), f32)
    for i in range(min(n_live, ns)):
        row = imp[i:i + 1, :]
        tie_before = jnp.where(j_blk > i, 1.0, 0.0)
        rank = rank + jnp.where(row > imp, 1.0, jnp.where(row == imp, tie_before, 0.0))
    sel_bias = jnp.where(rank < float(min(N_SEL, ns)), 0.0, NEG_INF).astype(bf16)
    return o_cmp, sel_bias


def _attend(tiles, q_op):
    m = jnp.full((1, _NQ), NEG_INF, bf16)
    acc = jnp.zeros((V_EXT, _NQ), f32)
    s_next = jnp.dot(tiles[0][0](), q_op, preferred_element_type=f32)
    for t, (_, values_t, bias) in enumerate(tiles):
        sT = s_next
        if t + 1 < len(tiles):
            s_next = jnp.dot(tiles[t + 1][0](), q_op, preferred_element_type=f32)
        yield
        p_blocks, m_blocks, a_blocks = [], [], []
        for c, cs in enumerate(_COL_BLOCKS):
            s = sT[:, cs]
            if bias is not None:
                b0 = (c % (TQ // LANES)) * LANES
                s = s + bias[:, b0:b0 + LANES]
            s = s.astype(bf16)
            m_o = m[:, cs]
            m_n = jnp.maximum(m_o, jnp.max(s, axis=0, keepdims=True))
            p_blocks.append(jnp.exp(s - m_n))
            a_blocks.append(jnp.exp(m_o.astype(f32) - m_n.astype(f32)))
            m_blocks.append(m_n)
        m = jnp.concatenate(m_blocks, axis=1)
        pv = jnp.dot(values_t(), jnp.concatenate(p_blocks, axis=1), preferred_element_type=f32)
        yield
        acc = acc * jnp.concatenate(a_blocks, axis=1) + pv
    return acc[0:HEAD_DIM, :] * (1.0 / acc[HEAD_DIM:HEAD_DIM + 1, :])


def _nsa_pair(pi, nqt, g, qa_ref, qb_ref, ks_ref, kw_ref, vsT_ref, vwT_ref, kcc_ref, vcT_ref, ga_ref, gb_ref,
              ovT_ref, cbias_ref, wbias_ref, o_ref):
    q_tiles = (pi, nqt - 1 - pi)
    ns = ovT_ref.shape[0]
    kcc = kcc_ref[0, 0]
    vcT = vcT_ref[0, 0]
    ov = ovT_ref[...]
    cbias = cbias_ref[...]
    n_back = WINDOW // TK

    def tile(k_ref, vT_ref, kt, bias):
        return (lambda: k_ref[kt * TK:(kt + 1) * TK, :], lambda: vT_ref[:, kt * TK:(kt + 1) * TK], bias)

    def query_tile(slot, q_ref, g_ref, qt):
        q4 = jnp.concatenate([q_ref[h * HEAD_DIM:(h + 1) * HEAD_DIM, :] for h in range(HPG)], axis=1)
        zeros = jnp.zeros_like(q4)
        q_win = jnp.where(g == 0, jnp.concatenate([q4, zeros], axis=0), jnp.concatenate([zeros, q4], axis=0))

        o_win = yield from _attend(
            [tile(kw_ref, vwT_ref, qt - j, cbias if j == 0 else (wbias_ref[...] if j == n_back else None))
             for j in range(min(n_back, qt) + 1)], q_win)

        o_cmp, sel_bias = yield from _compressed_branch(q4, kcc, vcT, ov, qt * TQ)

        q_sel = jnp.concatenate([q_win, jnp.concatenate([sel_bias] * HPG, axis=1),
                                 jnp.zeros((KV_WIDTH - ns, _NQ), bf16)], axis=0)
        o_slc = yield from _attend(
            [tile(ks_ref, vsT_ref, qt, cbias)] + [tile(ks_ref, vsT_ref, kt, None) for kt in range(qt)], q_sel)

        gates = g_ref[...]
        for h in range(HPG):
            sl = slice(h * TQ, (h + 1) * TQ)
            o = (gates[3 * h:3 * h + 1, :] * o_cmp[:, sl] + gates[3 * h + 1:3 * h + 2, :] * o_slc[:, sl]
                 + gates[3 * h + 2:3 * h + 3, :] * o_win[:, sl])
            o_ref[h * HEAD_DIM:(h + 1) * HEAD_DIM, slot * TQ:(slot + 1) * TQ] = o.astype(bf16)

    _interleave([query_tile(slot, q_ref, g_ref, qt)
                 for slot, (q_ref, g_ref, qt) in enumerate(zip((qa_ref, qb_ref), (ga_ref, gb_ref), q_tiles))])


def _nsa_kernel(*refs, nqt):
    for pi in range(nqt // 2):
        @pl.when(pl.program_id(0) == pi)
        def _():
            _nsa_pair(pi, nqt, pl.program_id(2), *refs)


def _nsa_tile_position(qt, nqt):
    return jnp.where(qt < nqt // 2, 2 * qt, 2 * (nqt - 1 - qt) + 1)


def _nsa(qT, ks, kw, vsT, vwT, kcc, vcT, gT, ovT, batch, seq):
    T = batch * seq
    nqt = seq // TQ
    ncp = kcc.shape[2]
    ns = seq // SEL_LEN
    nq = HPG * TQ
    gw = HPG * HEAD_DIM
    assert TQ == TK and WINDOW % TK == 0 and ns <= KV_WIDTH
    assert nqt % 2 == 0 and nqt // 2 >= WINDOW // TK
    kl = jnp.arange(TK)[:, None]
    ql = jnp.arange(TQ)[None, :]
    cbias = jnp.where(kl <= ql, 0.0, NEG_INF).astype(f32)
    wbias = jnp.where(kl > ql, 0.0, NEG_INF).astype(f32)
    amap = lambda p, b, g: (g, b * nqt + p)
    bmap = lambda p, b, g: (g, b * nqt + nqt - 1 - p)
    const = lambda p, b, g: (0, 0)
    in_specs = [
        pl.BlockSpec((gw, TQ), amap), pl.BlockSpec((gw, TQ), bmap),
        pl.BlockSpec((seq, 2 * KV_WIDTH), lambda p, b, g: (b, 0)),
        pl.BlockSpec((seq, KV_WIDTH), lambda p, b, g: (b, 0)),
        pl.BlockSpec((V_EXT, seq), lambda p, b, g: (g, b)),
        pl.BlockSpec((V_EXT, seq), lambda p, b, g: (g, b)),
        pl.BlockSpec((1, 1, ncp, HEAD_DIM), lambda p, b, g: (b, g, 0, 0)),
        pl.BlockSpec((1, 1, HEAD_DIM, ncp), lambda p, b, g: (b, g, 0, 0)),
        pl.BlockSpec((16, TQ), amap), pl.BlockSpec((16, TQ), bmap),
        pl.BlockSpec((ns, ncp), const),
        pl.BlockSpec((TK, TQ), const),
        pl.BlockSpec((TK, TQ), const),
    ]
    return pl.pallas_call(
        functools.partial(_nsa_kernel, nqt=nqt), grid=(nqt // 2, batch, NSA_KV_HEADS), in_specs=in_specs,
        out_specs=pl.BlockSpec((gw, 2 * TQ), lambda p, b, g: (g, b * (nqt // 2) + p)),
        out_shape=jax.ShapeDtypeStruct((NSA_WIDTH, T), bf16),
        compiler_params=_cparams(3), name="nsa",
    )(qT, qT, ks, kw, vsT, vwT, kcc, vcT, gT, gT, ovT, cbias, wbias)


def _memkv_kernel(mem_ref, g_ref, wkT_ref, wv_ref, kT_ref, v_ref):
    mb = _rms(mem_ref[0], g_ref[...]).astype(bf16)
    kT_ref[0] = lax.dot_general(wkT_ref[...], mb, (((1,), (1,)), ((), ())),
                                preferred_element_type=f32).astype(bf16)
    v_ref[0] = jnp.dot(mb, wv_ref[...], preferred_element_type=f32).astype(bf16)


def _memkv(mem, g_mem, wkT, wv):
    batch, n_mem, _ = mem.shape
    c2 = lambda b: (0, 0)
    return pl.pallas_call(
        _memkv_kernel, grid=(batch,),
        in_specs=[pl.BlockSpec((1, n_mem, D_MODEL), lambda b: (b, 0, 0)),
                  pl.BlockSpec((1, D_MODEL), c2),
                  pl.BlockSpec((D_MODEL, D_MODEL), c2), pl.BlockSpec((D_MODEL, D_MODEL), c2)],
        out_specs=(pl.BlockSpec((1, D_MODEL, n_mem), lambda b: (b, 0, 0)),
                   pl.BlockSpec((1, n_mem, D_MODEL), lambda b: (b, 0, 0))),
        out_shape=(jax.ShapeDtypeStruct((batch, D_MODEL, n_mem), bf16),
                   jax.ShapeDtypeStruct((batch, n_mem, D_MODEL), bf16)),
        compiler_params=_cparams(1), name="memkv",
    )(mem, g_mem, wkT, wv)


def _post_kernel(x_ref, *refs):
    n_sub = x_ref.shape[0] // TQ
    oT_refs = refs[:n_sub]
    (conv_ref, gnsa_ref, wout_ref, gx_ref, wq_ref, kT_ref, v_ref, wo_ref, gmoe_ref, wr_ref, br_ref,
     h2_ref, xn2_ref, eidx_ref, gate_ref, sel_ref) = refs[n_sub:]
    stages = []
    for sub, oT_ref in enumerate(oT_refs):
        rows = slice(sub * TQ, (sub + 1) * TQ)
        stages.append(_post_rows(
            x_ref.at[rows], oT_ref, conv_ref.at[rows], gnsa_ref, wout_ref, gx_ref, wq_ref, kT_ref, v_ref,
            wo_ref, gmoe_ref, wr_ref, br_ref, h2_ref.at[rows],
            xn2_ref.at[sub * TQ * ROW_SUB:(sub + 1) * TQ * ROW_SUB], eidx_ref.at[rows], gate_ref.at[rows],
            sel_ref.at[rows]))
    _interleave(stages)


_DONE = object()


def _interleave(generators):
    live = list(generators)
    while live:
        live = [g for g in live if next(g, _DONE) is not _DONE]


def _post_rows(x_ref, oT_ref, conv_ref, gnsa_ref, wout_ref, gx_ref, wq_ref, kT_ref, v_ref, wo_ref,
               gmoe_ref, wr_ref, br_ref,
               h2_ref, xn2_ref, eidx_ref, gate_ref, sel_ref):
    tm = x_ref.shape[0]
    oT = oT_ref[...].astype(f32)
    onT = (oT * lax.rsqrt(jnp.mean(oT * oT, axis=0, keepdims=True) + RMS_EPS) * gnsa_ref[...]).astype(bf16)
    mix = lax.dot_general(onT, wout_ref[0:NSA_WIDTH, :], (((0,), (0,)), ((), ())),
                          preferred_element_type=f32)
    mix = mix + jnp.dot(conv_ref[...], wout_ref[NSA_WIDTH:D_MODEL, :], preferred_element_type=f32)
    yield
    h1 = x_ref[...] + mix

    hn = _rms(h1, gx_ref[...]).astype(bf16)
    q = (jnp.dot(hn, wq_ref[...], preferred_element_type=f32) * (XATTN_HEAD_DIM ** -0.5)).astype(bf16)
    yield
    head_slices = [slice(h * XATTN_HEAD_DIM, (h + 1) * XATTN_HEAD_DIM) for h in range(XATTN_HEADS)]
    scores = [jnp.dot(q[:, sl], kT_ref[0, sl, :], preferred_element_type=f32) for sl in head_slices]
    yield
    heads = []
    for s, sl in zip(scores, head_slices):
        e = jnp.exp(s - jnp.max(s, axis=-1, keepdims=True))
        p = e * (1.0 / jnp.sum(e, axis=-1, keepdims=True))
        heads.append(jnp.dot(p.astype(bf16), v_ref[0, :, sl], preferred_element_type=f32))
    yield
    o = jnp.concatenate(heads, axis=1).astype(bf16)
    h2 = h1 + jnp.dot(o, wo_ref[...], preferred_element_type=f32)
    yield
    h2_ref[...] = h2

    xn2 = _rms(h2, gmoe_ref[...])
    for s_ in range(ROW_SUB):
        xn2_ref[pl.ds(s_, tm, stride=ROW_SUB), :] = xn2[:, s_ * LANES:(s_ + 1) * LANES]

    logits = jnp.dot(xn2.astype(bf16), wr_ref[...], preferred_element_type=f32) + br_ref[...]
    yield
    lane = lax.broadcasted_iota(i32, (tm, LANES), 1)
    work = logits
    sel = jnp.zeros((tm, LANES), f32)
    eidx = jnp.zeros((tm, LANES), i32)
    vals = []
    for k in range(TOP_K):
        mk = jnp.max(work, axis=-1, keepdims=True)
        ik = jnp.min(jnp.where(work == mk, lane, LANES), axis=-1, keepdims=True)
        hit = lane == ik
        work = jnp.where(hit, -jnp.inf, work)
        sel = jnp.where(hit, 1.0, sel)
        eidx = jnp.where(lane == k, ik, eidx)
        vals.append(mk)
    es = [jnp.exp(v - vals[0]) for v in vals]
    den = es[0]
    for e in es[1:]:
        den = den + e
    gate = jnp.zeros((tm, LANES), f32)
    for k in range(TOP_K):
        gate = jnp.where(lane == k, es[k] / den, gate)
    eidx_ref[...] = eidx
    gate_ref[...] = gate
    sel_ref[...] = sel.astype(bf16)


def _post(x2, oT, conv_n, g_nsa_col, w_out, g_x, w_q, kT, v, w_o, g_moe, w_r, b_r, seq):
    T = x2.shape[0]
    tm = TM_POST
    n_sub = tm // TQ
    assert tm % TQ == 0 and seq % tm == 0
    tps = seq // tm
    nqt = seq // TQ
    n_mem = v.shape[1]
    row = lambda i: (i, 0)
    const = lambda i: (0, 0)

    def o_tile(sub):
        return lambda i: (0, (i // tps) * nqt + _nsa_tile_position(n_sub * (i % tps) + sub, nqt))

    in_specs = [
        pl.BlockSpec((tm, D_MODEL), row),
        *[pl.BlockSpec((NSA_WIDTH, TQ), o_tile(sub)) for sub in range(n_sub)],
        pl.BlockSpec((tm, CONV_WIDTH), row),
        pl.BlockSpec((NSA_WIDTH, 1), const),
        pl.BlockSpec((D_MODEL, D_MODEL), const),
        pl.BlockSpec((1, D_MODEL), const),
        pl.BlockSpec((D_MODEL, D_MODEL), const),
        pl.BlockSpec((1, D_MODEL, n_mem), lambda i: (i // tps, 0, 0)),
        pl.BlockSpec((1, n_mem, D_MODEL), lambda i: (i // tps, 0, 0)),
        pl.BlockSpec((D_MODEL, D_MODEL), const),
        pl.BlockSpec((1, D_MODEL), const),
        pl.BlockSpec((D_MODEL, LANES), const),
        pl.BlockSpec((1, LANES), const),
    ]
    out_shape = (jax.ShapeDtypeStruct((T, D_MODEL), f32),
                 jax.ShapeDtypeStruct((T * ROW_SUB, LANES), f32),
                 jax.ShapeDtypeStruct((T, LANES), i32),
                 jax.ShapeDtypeStruct((T, LANES), f32),
                 jax.ShapeDtypeStruct((T, LANES), bf16))
    out_specs = (pl.BlockSpec((tm, D_MODEL), row),
                 pl.BlockSpec((tm * ROW_SUB, LANES), row),
                 pl.BlockSpec((tm, LANES), row), pl.BlockSpec((tm, LANES), row),
                 pl.BlockSpec((tm, LANES), row))
    return pl.pallas_call(
        _post_kernel, grid=(T // tm,), in_specs=in_specs, out_specs=out_specs, out_shape=out_shape,
        compiler_params=_cparams(1), name="post",
    )(x2, *([oT] * n_sub), conv_n, g_nsa_col, w_out, g_x, w_q, kT, v, w_o, g_moe, w_r, b_r)


def _count_kernel(sel_ref, cum_ref, cnt_ref, carry):
    i = pl.program_id(0)
    tm = sel_ref.shape[0]

    @pl.when(i == 0)
    def _():
        carry[...] = jnp.zeros_like(carry)

    sel = sel_ref[...]
    r = lax.broadcasted_iota(i32, (tm, tm), 0)
    c = lax.broadcasted_iota(i32, (tm, tm), 1)
    strict_lower = jnp.where(c < r, 1.0, 0.0).astype(bf16)
    base = carry[0:1, :]
    cum_ref[...] = jnp.dot(strict_lower, sel, preferred_element_type=f32) + base
    total = base + jnp.sum(sel.astype(f32), axis=0, keepdims=True)
    carry[...] = jnp.broadcast_to(total, carry.shape)
    cnt_ref[...] = jnp.broadcast_to(total, cnt_ref.shape)


def _count(sel):
    T = sel.shape[0]
    tm = TM_ROUTE
    return pl.pallas_call(
        _count_kernel, grid=(T // tm,),
        in_specs=[pl.BlockSpec((tm, LANES), lambda i: (i, 0))],
        out_specs=(pl.BlockSpec((tm, LANES), lambda i: (i, 0)),
                   pl.BlockSpec((SUBLANES, LANES), lambda i: (0, 0))),
        out_shape=(jax.ShapeDtypeStruct((T, LANES), f32), jax.ShapeDtypeStruct((SUBLANES, LANES), f32)),
        scratch_shapes=[pltpu.VMEM((SUBLANES, LANES), f32)],
        compiler_params=_cparams(1), name="route_count",
    )(sel)


def _dest_kernel(cum_ref, eidx_ref, pstart_ref, dest_ref):
    tm = cum_ref.shape[0]
    lane = lax.broadcasted_iota(i32, (tm, LANES), 1)
    row_of = cum_ref[...] + pstart_ref[0:1, :]
    eidx = eidx_ref[...]
    dest = jnp.zeros((tm, LANES), f32)
    for k in range(TOP_K):
        ek = eidx[:, k:k + 1]
        dk = jnp.sum(jnp.where(lane == ek, row_of, 0.0), axis=-1, keepdims=True)
        dest = jnp.where(lane == k, dk, dest)
    dest_ref[...] = dest.astype(i32)


def _dest(cum, eidx, pstart8):
    T = cum.shape[0]
    tm = TM_ROUTE
    row = lambda i: (i, 0)
    return pl.pallas_call(
        _dest_kernel, grid=(T // tm,),
        in_specs=[pl.BlockSpec((tm, LANES), row), pl.BlockSpec((tm, LANES), row),
                  pl.BlockSpec((SUBLANES, LANES), lambda i: (0, 0))],
        out_specs=pl.BlockSpec((tm, LANES), row),
        out_shape=jax.ShapeDtypeStruct((T, LANES), i32),
        compiler_params=_cparams(1), name="route_dest",
    )(cum, eidx, pstart8)


def _invert_kernel(dest_ref, init_ref, slot_ref, sem):
    cp = pltpu.make_async_copy(init_ref, slot_ref, sem)
    cp.start()
    cp.wait()
    chunk = LANES

    def body(j, carry):
        base = j * chunk
        for l in range(chunk):
            slot_ref[dest_ref[base + l]] = base + l
        return carry

    lax.fori_loop(0, dest_ref.shape[0] // chunk, body, 0)


def _invert(dest_flat, n_steps):
    parity = jnp.concatenate([jnp.arange(n_steps, dtype=i32) % 2, jnp.ones((1,), i32)])
    sink = dest_flat.shape[0] + parity[:, None] * BM + jnp.arange(BM, dtype=i32)[None, :]
    return pl.pallas_call(
        _invert_kernel,
        in_specs=[pl.BlockSpec(memory_space=pltpu.SMEM), pl.BlockSpec(memory_space=pltpu.VMEM)],
        out_specs=pl.BlockSpec(memory_space=pltpu.SMEM),
        out_shape=jax.ShapeDtypeStruct(((n_steps + 1) * BM,), i32),
        scratch_shapes=[pltpu.SemaphoreType.DMA],
        compiler_params=pltpu.CompilerParams(vmem_limit_bytes=VMEM_LIMIT), name="route_invert",
    )(dest_flat, sink.reshape(-1))


def _ffn_kernel(blk_e_ref, first_ref, nxt_e_ref, wslot_ref, nused_ref, slot_ref,
                x_hbm, wgu_hbm, bgu_ref, wd_hbm, bd_ref, y_hbm,
                xbuf0, xbuf1, obuf0, obuf1, wgu_f, wd_f, wgu_bf, wd_bf, gsem, ssem, wsem, *, n_tok, sink_row):
    i = pl.program_id(0)
    nused = nused_ref[0]
    n_steps = pl.num_programs(0)
    xbuf = (xbuf0, xbuf1)
    obuf = (obuf0, obuf1)

    def row_window(ref, row):
        return ref.at[pl.ds(pl.multiple_of(row * ROW_SUB, SUBLANES), ROW_SUB)]

    def gather(blk, par, r):
        tok = slot_ref[blk * BM + r] & (n_tok - 1)
        return pltpu.make_async_copy(row_window(x_hbm, tok), xbuf[par].at[pl.ds(r * ROW_SUB, ROW_SUB)],
                                     gsem.at[par])

    def scatter(blk, par, r):
        return pltpu.make_async_copy(obuf[par].at[pl.ds(r * ROW_SUB, ROW_SUB)],
                                     row_window(y_hbm, slot_ref[blk * BM + r]), ssem.at[par])

    def wait_gather(par):
        pltpu.make_async_copy(x_hbm.at[pl.ds(0, BM * ROW_SUB)], xbuf[par], gsem.at[par]).wait()

    def wait_scatter(par):
        pltpu.make_async_copy(obuf[par], y_hbm.at[pl.ds(0, BM * ROW_SUB)], ssem.at[par]).wait()

    def weight_copies(e, ws):
        return (pltpu.make_async_copy(wgu_hbm.at[e], wgu_f.at[ws], wsem.at[ws, 0]),
                pltpu.make_async_copy(wd_hbm.at[e], wd_f.at[ws], wsem.at[ws, 1]))

    @pl.when(i == 0)
    def _():
        obuf0[...] = jnp.zeros(obuf0.shape, f32)
        obuf1[...] = jnp.zeros(obuf1.shape, f32)
        pltpu.make_async_copy(obuf0, y_hbm.at[pl.ds(sink_row * ROW_SUB, BM * ROW_SUB)], ssem.at[0]).start()
        for cp in weight_copies(blk_e_ref[0], 0):
            cp.start(priority=1)
        for r in range(BM):
            gather(0, 0, r).start()

    @pl.when((i < nused) & (first_ref[i] == 1))
    def _():
        ws = wslot_ref[i]
        for cp in weight_copies(blk_e_ref[i], ws):
            cp.wait()

        @pl.when(nxt_e_ref[i] >= 0)
        def _():
            for cp in weight_copies(nxt_e_ref[i], 1 - ws):
                cp.start(priority=1)

        wgu_bf[...] = wgu_f[ws].astype(bf16)
        wd_bf[...] = wd_f[ws].astype(bf16)

    def block(par, issues_gathers):
        prev = jnp.where(i == 0, n_steps, i - 1)
        wait_gather(par)
        x = jnp.concatenate([xbuf[par][pl.ds(s_, BM, stride=ROW_SUB), :] for s_ in range(ROW_SUB)],
                            axis=1).astype(bf16)
        if issues_gathers:
            for r in range(BM):
                gather(i + 1, 1 - par, r).start()
        for r in range(BM):
            scatter(prev, 1 - par, r).start(priority=1 - par)
        if issues_gathers:
            for r in range(BM):
                gather(i + 2, par, r).start()
        gu = jnp.dot(x, wgu_bf[...], preferred_element_type=f32) + bgu_ref[0]
        gg = jnp.minimum(gu[:, 0:D_FF], SWIGLU_LIMIT)
        uu = jnp.clip(gu[:, D_FF:2 * D_FF], -SWIGLU_LIMIT, SWIGLU_LIMIT)
        hmid = (uu + 1.0) * (gg * jax.nn.sigmoid(SWIGLU_ALPHA * gg))
        out = jnp.dot(hmid.astype(bf16), wd_bf[...], preferred_element_type=f32) + bd_ref[0]
        wait_scatter(par)
        for s_ in range(ROW_SUB):
            obuf[par][pl.ds(s_, BM, stride=ROW_SUB), :] = out[:, s_ * LANES:(s_ + 1) * LANES]

    for par in range(2):
        @pl.when((i < nused) & (i % 2 == par))
        def _():
            block(par, issues_gathers=(par == 0))

        @pl.when((i == nused) & (i % 2 == par))
        def _():
            for r in range(BM):
                scatter(i - 1, 1 - par, r).start(priority=1)
            wait_gather(0)
            if par == 1:
                wait_gather(1)
            wait_scatter(par)
            wait_scatter(1 - par)


def _ffn(blk_e, first, nxt_e, wslot, nused, slots, xn2_rows, w_gu, b_gu, w_d, b_d):
    n_steps = blk_e.shape[0]
    n_tok = xn2_rows.shape[0] // ROW_SUB
    assert n_tok & (n_tok - 1) == 0
    n_tok_rows = n_tok * TOP_K
    emap = lambda i, be, *_: (be[i], 0, 0)
    grid_spec = pltpu.PrefetchScalarGridSpec(
        num_scalar_prefetch=6, grid=(n_steps,),
        in_specs=[pl.BlockSpec(memory_space=pl.ANY),
                  pl.BlockSpec(memory_space=pl.ANY),
                  pl.BlockSpec((1, 1, 2 * D_FF), emap),
                  pl.BlockSpec(memory_space=pl.ANY),
                  pl.BlockSpec((1, 1, D_MODEL), emap)],
        out_specs=pl.BlockSpec(memory_space=pl.ANY),
        scratch_shapes=[pltpu.VMEM((BM * ROW_SUB, LANES), f32), pltpu.VMEM((BM * ROW_SUB, LANES), f32),
                        pltpu.VMEM((BM * ROW_SUB, LANES), f32), pltpu.VMEM((BM * ROW_SUB, LANES), f32),
                        pltpu.VMEM((2, D_MODEL, 2 * D_FF), f32), pltpu.VMEM((2, D_FF, D_MODEL), f32),
                        pltpu.VMEM((D_MODEL, 2 * D_FF), bf16), pltpu.VMEM((D_FF, D_MODEL), bf16),
                        pltpu.SemaphoreType.DMA((2,)), pltpu.SemaphoreType.DMA((2,)),
                        pltpu.SemaphoreType.DMA((2, 2))],
    )
    return pl.pallas_call(
        functools.partial(_ffn_kernel, n_tok=n_tok, sink_row=n_tok_rows), grid_spec=grid_spec,
        out_shape=jax.ShapeDtypeStruct(((n_tok_rows + 2 * BM) * ROW_SUB, LANES), f32),
        compiler_params=_cparams(1), name="ffn",
    )(blk_e, first, nxt_e, wslot, nused, slots, xn2_rows, w_gu, b_gu, w_d, b_d)


def _combine_kernel(*refs, final_norm):
    y_refs = refs[:TOP_K]
    gate_ref, h2_ref, gfin_ref, o_ref = refs[TOP_K:]
    tm = h2_ref.shape[0]
    gate = gate_ref[...]
    cols = []
    for s_ in range(ROW_SUB):
        acc = gate[:, 0:1] * y_refs[0][pl.ds(s_, tm, stride=ROW_SUB), :]
        for k in range(1, TOP_K):
            acc = acc + gate[:, k:k + 1] * y_refs[k][pl.ds(s_, tm, stride=ROW_SUB), :]
        cols.append(acc)
    h = h2_ref[...] + jnp.concatenate(cols, axis=1)
    if final_norm:
        h = _rms(h, gfin_ref[...])
    o_ref[...] = h


def _combine(y_rows, gate, h2, g_final, final_norm):
    T = h2.shape[0]
    tm = TM_ROW
    row = lambda i: (i, 0)
    planes = [pl.BlockSpec((tm * ROW_SUB, LANES), functools.partial(lambda i, k: (k * (T // tm) + i, 0), k=k))
              for k in range(TOP_K)]
    return pl.pallas_call(
        functools.partial(_combine_kernel, final_norm=final_norm), grid=(T // tm,),
        in_specs=planes + [pl.BlockSpec((tm, LANES), row), pl.BlockSpec((tm, D_MODEL), row),
                           pl.BlockSpec((1, D_MODEL), lambda i: (0, 0))],
        out_specs=pl.BlockSpec((tm, D_MODEL), row),
        out_shape=jax.ShapeDtypeStruct((T, D_MODEL), f32),
        compiler_params=_cparams(1), name="combine",
    )(*([y_rows] * TOP_K), gate, h2, g_final)


def _prep_inproj_weights(w_in):
    sizes = (NSA_WIDTH,) + (KV_WIDTH,) * 6 + (3 * NSA_HEADS,) + (CONV_WIDTH,) * 3
    offs = [0]
    for s in sizes:
        offs.append(offs[-1] + s)
    seg = lambda n: w_in[:, offs[n]:offs[n + 1]]
    q, kc, vc, ks, vs, kw, vw, gl, ch, cb, cc = (seg(n) for n in range(11))
    w_row = jnp.concatenate([kc, vc, ks, kw, ch, cb, cc], axis=1).astype(bf16)
    gl_g = gl.reshape(D_MODEL, NSA_KV_HEADS, HPG * 3)
    gl_g = jnp.pad(gl_g, ((0, 0), (0, 0), (0, 16 - HPG * 3))).reshape(D_MODEL, NSA_KV_HEADS * 16)
    w_t = jnp.concatenate([q, vs, vw, gl_g], axis=1).T.astype(bf16)
    return w_row, w_t


def _rope_tables(positions):
    half = HEAD_DIM // 2
    inv_freq = ROPE_THETA ** (-jnp.arange(half, dtype=f32) / half)
    ang = positions.reshape(-1).astype(f32)[:, None] * inv_freq
    cos = jnp.cos(ang)
    sin = jnp.sin(ang)
    reps = KV_WIDTH // HEAD_DIM
    cosr = jnp.tile(cos, (1, 2 * reps))
    sinr = jnp.tile(jnp.concatenate([-sin, sin], axis=1), (1, reps))
    return cosr, sinr, cos.T, sin.T


def _cmp_weights(pos, w1, b1):
    w1r = w1.astype(bf16).reshape(2, CMP_STRIDE, HEAD_DIM, CMP_HIDDEN)
    zero = jnp.zeros_like(w1r)
    big = jnp.stack([jnp.concatenate([w1r if g == k else zero for k in range(NSA_KV_HEADS)], axis=-1)
                     for g in range(NSA_KV_HEADS)], axis=2)
    big = big.reshape(2, CMP_STRIDE * KV_WIDTH, NSA_KV_HEADS * CMP_HIDDEN)
    p = jnp.broadcast_to(pos.reshape(2, CMP_STRIDE, 1, HEAD_DIM), (2, CMP_STRIDE, NSA_KV_HEADS, HEAD_DIM))
    p = jnp.pad(p.reshape(2, CMP_STRIDE * KV_WIDTH), ((0, SUBLANES - 2), (0, 0)))
    return big, p, jnp.tile(b1.reshape(1, CMP_HIDDEN), (1, NSA_KV_HEADS))


def _mixer_core(x2, tables, p, batch, seq):
    cosr, sinr, cost, sint = tables
    row1 = lambda v: v.reshape(1, -1)
    w_row, w_t = _prep_inproj_weights(p['w_mix_in'])
    conv_w8 = jnp.pad(p['conv_w'], ((0, SUBLANES - CONV_K), (0, 0)))
    qT, kc, vc, ks, kw, vsT, vwT, gT, conv_n = _inproj(
        x2, row1(p['g_mix_norm']), w_row, w_t, cosr, sinr, cost, sint, conv_w8, row1(p['g_conv_out']), seq)

    w1k, pk, b1k = _cmp_weights(p['cmp_pos_k'], p['cmp_w1_k'], p['cmp_b1_k'])
    w1v, pv, b1v = _cmp_weights(p['cmp_pos_v'], p['cmp_w1_v'], p['cmp_b1_v'])
    kcc, vcT = _compress(kc, vc, pk, pv, w1k, w1v, b1k, b1v,
                         p['cmp_w2_k'].astype(bf16), p['cmp_w2_v'].T.astype(bf16), batch, seq)

    ncp = seq // CMP_STRIDE
    ns = seq // SEL_LEN
    cs = jnp.arange(ncp) * CMP_STRIDE
    js = jnp.arange(ns) * SEL_LEN
    overlap = jnp.clip(jnp.minimum(cs[:, None] + CMP_LEN, js[None, :] + SEL_LEN)
                       - jnp.maximum(cs[:, None], js[None, :]), 0, None).astype(f32) / CMP_LEN
    ovT = overlap.T.astype(bf16)
    oT = _nsa(qT, ks, kw, vsT, vwT, kcc, vcT, gT, ovT, batch, seq)
    return oT, conv_n


def _layer(h, memf, tables, p, final_gain, final_norm):
    batch, seq, _ = h.shape
    T = batch * seq
    x2 = h.reshape(T, D_MODEL)
    row1 = lambda v: v.reshape(1, -1)
    oT, conv_n = _mixer_core(x2, tables, p, batch, seq)

    w_xkv = p['w_xkv']
    kT, v = _memkv(memf, row1(p['g_mem_norm']), w_xkv[:, :D_MODEL].T.astype(bf16),
                   w_xkv[:, D_MODEL:].astype(bf16))
    w_r = jnp.pad(p['w_router'], ((0, 0), (0, LANES - N_EXPERTS))).astype(bf16)
    b_r = jnp.pad(p['b_router'], (0, LANES - N_EXPERTS), constant_values=NEG_INF).reshape(1, LANES)
    h2, xn2_rows, eidx, gate, sel = _post(
        x2, oT, conv_n, p['g_nsa_out'].reshape(NSA_WIDTH, 1), p['w_mix_out'].astype(bf16),
        row1(p['g_xattn_norm']), p['w_xq'].astype(bf16), kT, v, p['w_xo'].astype(bf16),
        row1(p['g_moe_norm']), w_r, b_r, seq)

    cum, cnt = _count(sel)
    counts = cnt[0, :N_EXPERTS].astype(i32)
    padded = (counts + BM - 1) // BM * BM
    pend = jnp.cumsum(padded)
    pstart = pend - padded
    n_steps = (T * TOP_K) // BM + N_EXPERTS + 1
    nused = (pend[-1] // BM).astype(i32)
    step = jnp.arange(n_steps, dtype=i32)
    used = step < nused
    blk_raw = jnp.minimum(jnp.sum((pend[None, :] <= (step * BM)[:, None]).astype(i32), axis=1), N_EXPERTS - 1)
    e_ids = jnp.arange(N_EXPERTS, dtype=i32)
    last_e = jnp.max(jnp.where(padded > 0, e_ids, 0))
    blk_e = jnp.where(used, blk_raw, last_e)
    first = (used & jnp.concatenate([jnp.ones((1,), bool), blk_e[1:] != blk_e[:-1]])).astype(i32)
    wslot = (jnp.cumsum(first) - 1) % 2
    later = (e_ids[None, :] > e_ids[:, None]) & (padded > 0)[None, :]
    nxt_of = jnp.min(jnp.where(later, e_ids[None, :], N_EXPERTS), axis=1)
    nxt_of = jnp.where(nxt_of < N_EXPERTS, nxt_of, -1).astype(i32)
    nxt_e = jnp.sum(jnp.where(blk_e[:, None] == e_ids[None, :], nxt_of[None, :], 0), axis=1)
    pstart8 = jnp.broadcast_to(jnp.pad(pstart.astype(f32), (0, LANES - N_EXPERTS))[None, :], (SUBLANES, LANES))

    dest = _dest(cum, eidx, pstart8)
    slots = _invert(dest[:, :TOP_K].T.reshape(-1), n_steps)
    y_rows = _ffn(blk_e, first, nxt_e, wslot.astype(i32), nused.reshape(1), slots, xn2_rows,
                  p['w_gate_up'], p['b_gate_up'].reshape(N_EXPERTS, 1, 2 * D_FF), p['w_down'],
                  p['b_down'].reshape(N_EXPERTS, 1, D_MODEL))
    out = _combine(y_rows, gate, h2, row1(final_gain), final_norm)
    return out.reshape(batch, seq, D_MODEL)


_LAYER_PARAMS = ('g_mix_norm', 'w_mix_in', 'cmp_pos_k', 'cmp_pos_v', 'cmp_w1_k', 'cmp_b1_k', 'cmp_w2_k',
                 'cmp_w1_v', 'cmp_b1_v', 'cmp_w2_v', 'conv_w', 'g_nsa_out', 'g_conv_out', 'w_mix_out',
                 'g_xattn_norm', 'g_mem_norm', 'w_xq', 'w_xkv', 'w_xo', 'g_moe_norm', 'w_router', 'b_router',
                 'w_gate_up', 'b_gate_up', 'w_down', 'b_down')


def kernel(x, mem, positions, g_mix_norm, w_mix_in, cmp_pos_k, cmp_pos_v, cmp_w1_k, cmp_b1_k, cmp_w2_k, cmp_w1_v, cmp_b1_v, cmp_w2_v, conv_w, g_nsa_out, g_conv_out, w_mix_out, g_xattn_norm, g_mem_norm, w_xq, w_xkv, w_xo, g_moe_norm, w_router, b_router, w_gate_up, b_gate_up, w_down, b_down, g_final):
    stacked = dict(zip(_LAYER_PARAMS, (g_mix_norm, w_mix_in, cmp_pos_k, cmp_pos_v, cmp_w1_k, cmp_b1_k, cmp_w2_k,
                                       cmp_w1_v, cmp_b1_v, cmp_w2_v, conv_w, g_nsa_out, g_conv_out, w_mix_out,
                                       g_xattn_norm, g_mem_norm, w_xq, w_xkv, w_xo, g_moe_norm, w_router,
                                       b_router, w_gate_up, b_gate_up, w_down, b_down)))
    depth = g_mix_norm.shape[0]
    tables = _rope_tables(positions)
    h = x
    for l in range(depth):
        p = {k: v[l] for k, v in stacked.items()}
        last = l == depth - 1
        h = _layer(h, mem, tables, p, g_final, final_norm=last)
    return h
```
